```python
import math
import jax
import jax.numpy as jnp
from jax import lax
import numpy as np

D_MODEL = 2048
BATCH = 2
SEQ = 4096
DEPTH = 4
DEC_BATCH = 8
DEC_SEQ = 1
PAST_LEN = 16384
PAGE_SIZE = 128

N_MIXERS = 3
N_GDN = (DEPTH + 2) // N_MIXERS
N_GLA = (DEPTH + 1) // N_MIXERS
N_DIFF = DEPTH // N_MIXERS

GDN_HEAD_DIM = 128
GDN_QK_HEADS = D_MODEL // 128
GDN_V_HEADS = 2 * GDN_QK_HEADS
GDN_QK_DIM = GDN_QK_HEADS * GDN_HEAD_DIM
GDN_V_DIM = GDN_V_HEADS * GDN_HEAD_DIM
GDN_CONV_DIM = 2 * GDN_QK_DIM + GDN_V_DIM
GDN_CONV_W = 4
GDN_CHUNK = 64

GLA_HEADS = 4
GLA_DK = D_MODEL // 2 // GLA_HEADS
GLA_DV = D_MODEL // GLA_HEADS
GLA_GATE_RANK = 16
GLA_GATE_NORMALIZER = 16.0
GLA_CHUNK = 64

DIFF_HEAD_DIM = 128
DIFF_HEADS = D_MODEL // (2 * DIFF_HEAD_DIM)
ATTN_QBLOCK = 128

N_BUCKETS = 32
MAX_DISTANCE = 128

D_FF = ((8 * D_MODEL + 3 * 256 - 1) // (3 * 256)) * 256

NORM_EPS = 1e-6

kernel_name = 'hybrid_gdn_gla_diffattn_decode_step'


def _rms_norm(x, w):
    xf = x.astype(jnp.float32)
    y = xf * lax.rsqrt(jnp.mean(xf * xf, axis=-1, keepdims=True) + NORM_EPS)
    return (y * w.astype(jnp.float32)).astype(x.dtype)


def _l2norm(x):
    xf = x.astype(jnp.float32)
    return xf * lax.rsqrt(jnp.sum(xf * xf, axis=-1, keepdims=True) + NORM_EPS)


def _pad_time(x, pad):
    if pad == 0:
        return x
    return jnp.pad(x, [(0, 0), (0, pad)] + [(0, 0)] * (x.ndim - 2))


def _to_chunks(x, c):
    b, t, h = x.shape[:3]
    x = x.reshape((b, t // c, c, h) + x.shape[3:])
    return jnp.moveaxis(x, (1, 3), (0, 2))


def _from_chunks(o, t):
    o = jnp.moveaxis(o, (0, 2), (1, 3))
    b, n, c, h, d = o.shape
    return o.reshape(b, n * c, h, d)[:, :t]


def _causal_conv(x, buf, w):
    t = x.shape[1]
    xp = jnp.concatenate([buf.astype(x.dtype), x], axis=1)
    y = xp[:, 0:t] * w[0]
    for i in range(1, GDN_CONV_W):
        y = y + xp[:, i:i + t] * w[i]
    return y, xp[:, t:]


def _gated_delta_rule(q, k, v, beta, g, s0):
    t = q.shape[1]
    c = min(GDN_CHUNK, t)
    pad = (-t) % c
    xs = tuple(_to_chunks(_pad_time(a, pad), c) for a in (q, k, v, beta, g))
    dv = v.shape[-1]
    tril = jnp.tril(jnp.ones((c, c), dtype=bool))
    strict = jnp.tril(jnp.ones((c, c), dtype=bool), -1)

    def step(s, inp):
        qc, kc, vc, bc, gc = inp
        gcum = jnp.cumsum(gc, axis=-1)
        decay = jnp.exp(jnp.where(tril, gcum[..., :, None] - gcum[..., None, :], -jnp.inf))
        kb = kc * bc[..., None]
        m = jnp.where(strict, jnp.einsum('bhid,bhjd->bhij', kb, kc) * decay, 0.0)
        rhs = jnp.concatenate([vc * bc[..., None], kb * jnp.exp(gcum)[..., None]], axis=-1)
        sol = lax.linalg.triangular_solve(m, rhs, left_side=True, lower=True, unit_diagonal=True)
        u, w = sol[..., :dv], sol[..., dv:]
        v_new = u - jnp.einsum('bhcd,bhde->bhce', w, s)
        attn = jnp.einsum('bhid,bhjd->bhij', qc, kc) * decay
        o = (jnp.einsum('bhcd,bhde->bhce', qc * jnp.exp(gcum)[..., None], s)
             + jnp.einsum('bhij,bhje->bhie', attn, v_new))
        g_last = gcum[..., -1]
        s = (s * jnp.exp(g_last)[..., None, None]
             + jnp.einsum('bhcd,bhce->bhde', kc * jnp.exp(g_last[..., None] - gcum)[..., None], v_new))
        return s, o

    s, o = lax.scan(step, s0, xs)
    return _from_chunks(o, t), s


def _gla_rule(q, k, v, gk, s0):
    t = q.shape[1]
    c = min(GLA_CHUNK, t)
    pad = (-t) % c
    xs = tuple(_to_chunks(_pad_time(a, pad), c) for a in (q, k, v, gk))
    tril = jnp.tril(jnp.ones((c, c), dtype=bool))

    def step(s, inp):
        qc, kc, vc, gc = inp
        bcum = jnp.cumsum(gc, axis=2)
        dec = jnp.exp(jnp.where(tril[:, :, None],
                                bcum[:, :, :, None, :] - bcum[:, :, None, :, :], -jnp.inf))
        attn = jnp.einsum('bhid,bhjd,bhijd->bhij', qc, kc, dec)
        o = (jnp.einsum('bhcd,bhde->bhce', qc * jnp.exp(bcum), s)
             + jnp.einsum('bhij,bhje->bhie', attn, vc))
        b_last = bcum[:, :, -1]
        s = (s * jnp.exp(b_last)[..., None]
             + jnp.einsum('bhcd,bhce->bhde', kc * jnp.exp(b_last[:, :, None] - bcum), vc))
        return s, o

    s, o = lax.scan(step, s0, xs)
    return _from_chunks(o, t), s


def _t5_bias(q_pos, k_pos, table):
    n = jnp.maximum(q_pos[:, None] - k_pos[None, :], 0)
    max_exact = N_BUCKETS // 2
    nf = jnp.maximum(n, max_exact).astype(jnp.float32)
    large = max_exact + (jnp.log(nf / max_exact) / math.log(MAX_DISTANCE / max_exact)
                         * (N_BUCKETS - max_exact)).astype(jnp.int32)
    bucket = jnp.where(n < max_exact, n, jnp.minimum(large, N_BUCKETS - 1))
    return jnp.moveaxis(table[bucket], -1, 0).astype(jnp.float32)


def _diff_attend(q, q_pos, ks, vs, kposes, lam, table):
    logits = []
    for k_, kp in zip(ks, kposes):
        lg = (jnp.einsum('bqnd,bknd->bnqk', q, k_.astype(jnp.float32)) * DIFF_HEAD_DIM ** -0.5
              + _t5_bias(q_pos, kp, table))
        logits.append(jnp.where(kp[None, :] <= q_pos[:, None], lg, -jnp.inf))
    p = jax.nn.softmax(jnp.concatenate(logits, axis=-1), axis=-1)
    b, _, nq, nk = p.shape
    p = p.reshape(b, DIFF_HEADS, 2, nq, nk)
    d = p[:, :, 0] - lam * p[:, :, 1]
    out = jnp.zeros((b, nq, DIFF_HEADS, 2 * DIFF_HEAD_DIM), jnp.float32)
    off = 0
    for v_ in vs:
        n_k = v_.shape[1]
        out = out + jnp.einsum('bhqk,bkhe->bqhe', d[..., off:off + n_k], v_.astype(jnp.float32))
        off += n_k
    return out


def _diff_blocks(q, q_pos, ks, vs, kposes, lam, table):
    b, t = q.shape[:2]
    c = min(ATTN_QBLOCK, t)
    pad = (-t) % c
    n = (t + pad) // c
    qp = _pad_time(q, pad).reshape(b, n, c, 2 * DIFF_HEADS, DIFF_HEAD_DIM)
    qpos = jnp.concatenate([q_pos, q_pos[-1] + 1 + jnp.arange(pad, dtype=jnp.int32)]).reshape(n, c)

    def blk(args):
        qb, pb = args
        return _diff_attend(qb, pb, ks, vs, kposes, lam, table)

    out = lax.map(blk, (jnp.moveaxis(qp, 1, 0), qpos))
    return jnp.moveaxis(out, 0, 1).reshape(b, n * c, DIFF_HEADS, 2 * DIFF_HEAD_DIM)[:, :t]


def _gdn_mixer(h, s0, buf0, w_in, w_ba, conv_w, a_log, dt_bias, norm_w, w_out):
    b, t, _ = h.shape
    proj = h @ w_in
    qkv, buf = _causal_conv(proj[..., :GDN_CONV_DIM], buf0, conv_w)
    qkv = jax.nn.silu(qkv)
    z = proj[..., GDN_CONV_DIM:].reshape(b, t, GDN_V_HEADS, GDN_HEAD_DIM)
    q = _l2norm(qkv[..., :GDN_QK_DIM].reshape(b, t, GDN_QK_HEADS, GDN_HEAD_DIM)) * GDN_HEAD_DIM ** -0.5
    k = _l2norm(qkv[..., GDN_QK_DIM:2 * GDN_QK_DIM].reshape(b, t, GDN_QK_HEADS, GDN_HEAD_DIM))
    v = qkv[..., 2 * GDN_QK_DIM:].reshape(b, t, GDN_V_HEADS, GDN_HEAD_DIM).astype(jnp.float32)
    rep = GDN_V_HEADS // GDN_QK_HEADS
    q = jnp.repeat(q, rep, axis=2)
    k = jnp.repeat(k, rep, axis=2)
    ba = (h @ w_ba).astype(jnp.float32)
    beta = jax.nn.sigmoid(ba[..., :GDN_V_HEADS])
    g = -jnp.exp(a_log.astype(jnp.float32)) * jax.nn.softplus(ba[..., GDN_V_HEADS:] + dt_bias.astype(jnp.float32))
    o, s = _gated_delta_rule(q, k, v, beta, g, s0.astype(jnp.float32))
    o = _rms_norm(o, norm_w) * jax.nn.silu(z.astype(jnp.float32))
    return o.reshape(b, t, GDN_V_DIM).astype(h.dtype) @ w_out, s, buf


def _gla_mixer(h, s0, w_qkvg, w_gk1, w_gk2, b_gk, norm_w, w_out):
    b, t, _ = h.shape
    kd = GLA_HEADS * GLA_DK
    qkvg = h @ w_qkvg
    q = qkvg[..., :kd].reshape(b, t, GLA_HEADS, GLA_DK).astype(jnp.float32) * GLA_DK ** -0.5
    k = qkvg[..., kd:2 * kd].reshape(b, t, GLA_HEADS, GLA_DK).astype(jnp.float32)
    v = qkvg[..., 2 * kd:2 * kd + D_MODEL].reshape(b, t, GLA_HEADS, GLA_DV).astype(jnp.float32)
    gate = qkvg[..., 2 * kd + D_MODEL:]
    gk = jax.nn.log_sigmoid(((h @ w_gk1) @ w_gk2 + b_gk).astype(jnp.float32)) / GLA_GATE_NORMALIZER
    o, s = _gla_rule(q, k, v, gk.reshape(b, t, GLA_HEADS, GLA_DK), s0.astype(jnp.float32))
    o = _rms_norm(o, norm_w).reshape(b, t, D_MODEL).astype(h.dtype) * jax.nn.silu(gate)
    return o @ w_out, s


def _diff_mixer(h, pos, k_past, v_past, w_qkv, lambdas, subln, w_out, table, layer_idx):
    b, t, _ = h.shape
    qkv = h @ w_qkv
    q = qkv[..., :D_MODEL].reshape(b, t, 2 * DIFF_HEADS, DIFF_HEAD_DIM)
    k = qkv[..., D_MODEL:2 * D_MODEL].reshape(b, t, 2 * DIFF_HEADS, DIFF_HEAD_DIM)
    v = qkv[..., 2 * D_MODEL:].reshape(b, t, DIFF_HEADS, 2 * DIFF_HEAD_DIM)
    lam_init = 0.8 - 0.6 * math.exp(-0.3 * layer_idx)
    lf = lambdas.astype(jnp.float32)
    lam = jnp.exp(jnp.sum(lf[0] * lf[1])) - jnp.exp(jnp.sum(lf[2] * lf[3])) + lam_init
    if k_past is None:
        ks, vs, kposes = [k], [v], [pos]
    else:
        ks, vs = [k_past, k], [v_past, v]
        kposes = [jnp.arange(k_past.shape[1], dtype=jnp.int32), pos]
    o = _diff_blocks(q.astype(jnp.float32), pos, ks, vs, kposes, lam, table)
    o = _rms_norm(o, subln) * (1.0 - lam_init)
    return o.reshape(b, t, D_MODEL).astype(h.dtype) @ w_out, k, v


def _swiglu(h, w_up, w_down):
    gu = h @ w_up
    return (jax.nn.silu(gu[..., :D_FF]) * gu[..., D_FF:]) @ w_down


def _trunk(x, pos, gdn_s0, gdn_c0, gla_s0, cache_k, cache_v, page_table, p):
    gdn_s, gdn_c, gla_s, att_k, att_v = [], [], [], [], []
    for i in range(DEPTH):
        j = i // N_MIXERS
        kind = i % N_MIXERS
        h = _rms_norm(x, p['norm_w'][i, 0])
        if kind == 0:
            m, s_new, c_new = _gdn_mixer(h, gdn_s0[j], gdn_c0[j], p['gdn_w_in'][j], p['gdn_w_ba'][j],
                                         p['gdn_conv_w'][j], p['gdn_a_log'][j], p['gdn_dt_bias'][j],
                                         p['gdn_norm_w'][j], p['gdn_w_out'][j])
            gdn_s.append(s_new)
            gdn_c.append(c_new)
        elif kind == 1:
            m, s_new = _gla_mixer(h, gla_s0[j], p['gla_w_qkvg'][j], p['gla_w_gk1'][j], p['gla_w_gk2'][j],
                                  p['gla_b_gk'][j], p['gla_norm_w'][j], p['gla_w_out'][j])
            gla_s.append(s_new)
        else:
            if cache_k is None:
                k_past, v_past = None, None
            else:
                nb, n_pages = page_table.shape
                past = n_pages * PAGE_SIZE
                k_past = cache_k[j, page_table].reshape(nb, past, 2 * DIFF_HEADS, DIFF_HEAD_DIM)
                v_past = cache_v[j, page_table].reshape(nb, past, DIFF_HEADS, 2 * DIFF_HEAD_DIM)
            m, k_new, v_new = _diff_mixer(h, pos, k_past, v_past, p['diff_w_qkv'][j], p['diff_lambda'][j],
                                          p['diff_subln'][j], p['diff_w_out'][j], p['rel_bias'], i)
            att_k.append(k_new)
            att_v.append(v_new)
        x = x + _rms_norm(m, p['norm_w'][i, 1])
        h = _rms_norm(x, p['norm_w'][i, 2])
        x = x + _rms_norm(_swiglu(h, p['ffn_w_up'][i], p['ffn_w_down'][i]), p['norm_w'][i, 3])
    return x, jnp.stack(gdn_s), jnp.stack(gdn_c), jnp.stack(gla_s), jnp.stack(att_k), jnp.stack(att_v)


def setup_inputs(seed: int = 0) -> dict:
    key = jax.random.key(seed)
    ks = jax.random.split(key, 32)
    f32 = jnp.float32

    def nrm(k, shape, scale):
        return jax.random.normal(k, shape, f32) * scale

    n_pages = PAST_LEN // PAGE_SIZE
    n_used = DEC_BATCH * n_pages
    n_phys = n_used + (n_used + 3) // 4
    page_table = jax.random.permutation(ks[0], n_phys)[:n_used].reshape(DEC_BATCH, n_pages).astype(jnp.int32)

    dt = jnp.exp(jax.random.uniform(ks[1], (N_GDN, GDN_V_HEADS), f32, math.log(1e-3), math.log(1e-1)))
    kd = GLA_HEADS * GLA_DK
    return {
        'x_prompt': nrm(ks[2], (BATCH, SEQ, D_MODEL), 1.0),
        'x_sample': nrm(ks[3], (DEC_BATCH, DEC_SEQ, D_MODEL), 1.0),
        'state_gdn': nrm(ks[4], (N_GDN, DEC_BATCH, GDN_V_HEADS, GDN_HEAD_DIM, GDN_HEAD_DIM), 0.1),
        'state_gdn_conv': nrm(ks[5], (N_GDN, DEC_BATCH, GDN_CONV_W - 1, GDN_CONV_DIM), 1.0),
        'state_gla': nrm(ks[6], (N_GLA, DEC_BATCH, GLA_HEADS, GLA_DK, GLA_DV), 0.5),
        'cache_k': nrm(ks[7], (N_DIFF, n_phys, PAGE_SIZE, 2 * DIFF_HEADS, DIFF_HEAD_DIM), 1.0),
        'cache_v': nrm(ks[8], (N_DIFF, n_phys, PAGE_SIZE, DIFF_HEADS, 2 * DIFF_HEAD_DIM), 1.0),
        'page_table': page_table,
        'norm_w': 1.0 + nrm(ks[9], (DEPTH, 4, D_MODEL), 0.02),
        'ffn_w_up': nrm(ks[10], (DEPTH, D_MODEL, 2 * D_FF), D_MODEL ** -0.5),
        'ffn_w_down': nrm(ks[11], (DEPTH, D_FF, D_MODEL), D_FF ** -0.5),
        'rel_bias': nrm(ks[12], (N_BUCKETS, 2 * DIFF_HEADS), 0.2),
        'gdn_w_in': nrm(ks[13], (N_GDN, D_MODEL, GDN_CONV_DIM + GDN_V_DIM), D_MODEL ** -0.5),
        'gdn_w_ba': nrm(ks[14], (N_GDN, D_MODEL, 2 * GDN_V_HEADS), D_MODEL ** -0.5),
        'gdn_conv_w': nrm(ks[15], (N_GDN, GDN_CONV_W, GDN_CONV_DIM), GDN_CONV_W ** -0.5),
        'gdn_a_log': jnp.log(jax.random.uniform(ks[16], (N_GDN, GDN_V_HEADS), f32, 1.0, 16.0)),
        'gdn_dt_bias': dt + jnp.log(-jnp.expm1(-dt)),
        'gdn_norm_w': 1.0 + nrm(ks[17], (N_GDN, GDN_HEAD_DIM), 0.02),
        'gdn_w_out': nrm(ks[18], (N_GDN, GDN_V_DIM, D_MODEL), GDN_V_DIM ** -0.5),
        'gla_w_qkvg': nrm(ks[19], (N_GLA, D_MODEL, 2 * kd + 2 * D_MODEL), D_MODEL ** -0.5),
        'gla_w_gk1': nrm(ks[20], (N_GLA, D_MODEL, GLA_GATE_RANK), D_MODEL ** -0.5),
        'gla_w_gk2': nrm(ks[21], (N_GLA, GLA_GATE_RANK, kd), GLA_GATE_RANK ** -0.5),
        'gla_b_gk': nrm(ks[22], (N_GLA, kd), 0.1),
        'gla_norm_w': 1.0 + nrm(ks[23], (N_GLA, GLA_DV), 0.02),
        'gla_w_out': nrm(ks[24], (N_GLA, D_MODEL, D_MODEL), D_MODEL ** -0.5),
        'diff_w_qkv': nrm(ks[25], (N_DIFF, D_MODEL, 3 * D_MODEL), D_MODEL ** -0.5),
        'diff_lambda': nrm(ks[26], (N_DIFF, 4, DIFF_HEAD_DIM), 0.1),
        'diff_subln': 1.0 + nrm(ks[27], (N_DIFF, 2 * DIFF_HEAD_DIM), 0.02),
        'diff_w_out': nrm(ks[28], (N_DIFF, D_MODEL, D_MODEL), D_MODEL ** -0.5),
    }


def reference(x_prompt, x_sample, state_gdn, state_gdn_conv, state_gla, cache_k, cache_v, page_table,
              norm_w, ffn_w_up, ffn_w_down, rel_bias,
              gdn_w_in, gdn_w_ba, gdn_conv_w, gdn_a_log, gdn_dt_bias, gdn_norm_w, gdn_w_out,
              gla_w_qkvg, gla_w_gk1, gla_w_gk2, gla_b_gk, gla_norm_w, gla_w_out,
              diff_w_qkv, diff_lambda, diff_subln, diff_w_out):
    p = dict(norm_w=norm_w, ffn_w_up=ffn_w_up, ffn_w_down=ffn_w_down, rel_bias=rel_bias,
             gdn_w_in=gdn_w_in, gdn_w_ba=gdn_w_ba, gdn_conv_w=gdn_conv_w, gdn_a_log=gdn_a_log,
             gdn_dt_bias=gdn_dt_bias, gdn_norm_w=gdn_norm_w, gdn_w_out=gdn_w_out,
             gla_w_qkvg=gla_w_qkvg, gla_w_gk1=gla_w_gk1, gla_w_gk2=gla_w_gk2, gla_b_gk=gla_b_gk,
             gla_norm_w=gla_norm_w, gla_w_out=gla_w_out,
             diff_w_qkv=diff_w_qkv, diff_lambda=diff_lambda, diff_subln=diff_subln, diff_w_out=diff_w_out)
    bp, tp = x_prompt.shape[:2]
    ts = x_sample.shape[1]
    past = page_table.shape[1] * PAGE_SIZE

    pos_p = jnp.arange(tp, dtype=jnp.int32)
    y_prompt, p_gdn_state, p_gdn_conv, p_gla_state, p_k, p_v = _trunk(
        x_prompt, pos_p,
        jnp.zeros((N_GDN, bp, GDN_V_HEADS, GDN_HEAD_DIM, GDN_HEAD_DIM), jnp.float32),
        jnp.zeros((N_GDN, bp, GDN_CONV_W - 1, GDN_CONV_DIM), x_prompt.dtype),
        jnp.zeros((N_GLA, bp, GLA_HEADS, GLA_DK, GLA_DV), jnp.float32),
        None, None, None, p)

    pos_s = past + jnp.arange(ts, dtype=jnp.int32)
    y_sample, s_gdn_state, s_gdn_conv, s_gla_state, s_k, s_v = _trunk(
        x_sample, pos_s, state_gdn, state_gdn_conv, state_gla, cache_k, cache_v, page_table, p)

    return (y_prompt, y_sample, p_gdn_state, p_gdn_conv, p_gla_state, p_k, p_v,
            s_gdn_state, s_gdn_conv, s_gla_state, s_k, s_v)
```

```python
import functools
import math

import numpy as np
import jax
import jax.numpy as jnp
from jax import lax
from jax.experimental import pallas as pl
from jax.experimental.pallas import tpu as pltpu

F32 = jnp.float32
BF16 = jnp.bfloat16

D_MODEL = 2048
DEPTH = 4
PAGE_SIZE = 128
N_MIXERS = 3
GDN_HEAD_DIM = 128
GDN_QK_HEADS = D_MODEL // 128
GDN_V_HEADS = 2 * GDN_QK_HEADS
GDN_QK_DIM = GDN_QK_HEADS * GDN_HEAD_DIM
GDN_V_DIM = GDN_V_HEADS * GDN_HEAD_DIM
GDN_CONV_DIM = 2 * GDN_QK_DIM + GDN_V_DIM
GDN_CONV_W = 4
GLA_HEADS = 4
GLA_DK = D_MODEL // 2 // GLA_HEADS
GLA_DV = D_MODEL // GLA_HEADS
GLA_GATE_NORMALIZER = 16.0
DIFF_HEAD_DIM = 128
DIFF_HEADS = D_MODEL // (2 * DIFF_HEAD_DIM)
N_BUCKETS = 32
MAX_DISTANCE = 128
D_FF = ((8 * D_MODEL + 3 * 256 - 1) // (3 * 256)) * 256
NORM_EPS = 1e-6

LANES = 128
SUBLANES = 8
CHUNK = 64
SUB = 16
VMEM_LIMIT_BYTES = 48 * 1024 * 1024
NEG = -1e30


def _params(*sem):
    return pltpu.CompilerParams(dimension_semantics=sem, vmem_limit_bytes=VMEM_LIMIT_BYTES)


def _dot(a, b):
    return jnp.dot(a.astype(BF16), b.astype(BF16), preferred_element_type=F32)


def _dot_nt(a, b):
    return lax.dot_general(a.astype(BF16), b.astype(BF16), (((1,), (1,)), ((), ())),
                           preferred_element_type=F32)


def _dot_tn(a, b):
    return lax.dot_general(a.astype(BF16), b.astype(BF16), (((0,), (0,)), ((), ())),
                           preferred_element_type=F32)


def _split3(x):
    hi = x.astype(BF16)
    r = x - hi.astype(F32)
    mid = r.astype(BF16)
    lo = (r - mid.astype(F32)).astype(BF16)
    return hi, mid, lo


def _sigmoid(x):
    return 1.0 / (1.0 + jnp.exp(-x))


def _softplus(x):
    return jnp.maximum(x, 0.0) + jnp.log(1.0 + jnp.exp(-jnp.abs(x)))


def _rms(x, w):
    return x * lax.rsqrt(jnp.mean(x * x, axis=-1, keepdims=True) + NORM_EPS) * w


def _chunk_tril(r):
    i = lax.broadcasted_iota(jnp.int32, (r, r), 0)
    j = lax.broadcasted_iota(jnp.int32, (r, r), 1)
    return jnp.where((j <= i) & ((i // CHUNK) == (j // CHUNK)), 1.0, 0.0).astype(BF16)


def _rmsnorm_kernel(x_ref, w_ref, o_ref):
    o_ref[...] = _rms(x_ref[...], w_ref[...]).astype(o_ref.dtype)


def _rmsnorm(x, w, out_dtype, tm):
    m, d = x.shape
    tm = min(tm, m)
    return pl.pallas_call(
        _rmsnorm_kernel, grid=(m // tm,),
        in_specs=[pl.BlockSpec((tm, d), lambda i: (i, 0)), pl.BlockSpec((1, d), lambda i: (0, 0))],
        out_specs=pl.BlockSpec((tm, d), lambda i: (i, 0)),
        out_shape=jax.ShapeDtypeStruct((m, d), out_dtype),
        compiler_params=_params("parallel"), name="rmsnorm")(x, w)


def _mm_kernel(a_ref, w_ref, o_ref):
    o_ref[...] = _dot(a_ref[...], w_ref[...]).astype(o_ref.dtype)


def _matmul(a, w, layer, out_dtype, tm, tn):
    m, k = a.shape
    n = w.shape[2]
    tm, tn = min(tm, m), min(tn, n)
    assert m % tm == 0 and n % tn == 0
    return pl.pallas_call(
        _mm_kernel, grid=(m // tm, n // tn),
        in_specs=[pl.BlockSpec((tm, k), lambda i, j: (i, 0)),
                  pl.BlockSpec((None, k, tn), lambda i, j: (layer, 0, j))],
        out_specs=pl.BlockSpec((tm, tn), lambda i, j: (i, j)),
        out_shape=jax.ShapeDtypeStruct((m, n), out_dtype),
        compiler_params=_params("parallel", "parallel"), name="matmul")(a, w)


def _swiglu_kernel(a_ref, wg_ref, wu_ref, o_ref):
    a = a_ref[...].astype(BF16)
    g = _dot(a, wg_ref[...])
    u = _dot(a, wu_ref[...])
    o_ref[...] = (g * _sigmoid(g) * u).astype(o_ref.dtype)


def _swiglu_up(a, w_up, layer, out_dtype, tm, tn):
    m, k = a.shape
    f = w_up.shape[2] // 2
    tm, tn = min(tm, m), min(tn, f)
    assert m % tm == 0 and f % tn == 0
    nf = f // tn
    return pl.pallas_call(
        _swiglu_kernel, grid=(m // tm, nf),
        in_specs=[pl.BlockSpec((tm, k), lambda i, j: (i, 0)),
                  pl.BlockSpec((None, k, tn), lambda i, j: (layer, 0, j)),
                  pl.BlockSpec((None, k, tn), lambda i, j: (layer, 0, j + nf))],
        out_specs=pl.BlockSpec((tm, tn), lambda i, j: (i, j)),
        out_shape=jax.ShapeDtypeStruct((m, f), out_dtype),
        compiler_params=_params("parallel", "parallel"), name="swiglu_up")(a, w_up, w_up)


def _mm_res_kernel(a_ref, w_ref, x_ref, nw1_ref, nw2_ref, xo_ref, *rest, nk, emit_h):
    if emit_h:
        h_ref, acc_ref = rest
    else:
        (acc_ref,) = rest
    kk = pl.program_id(1)

    @pl.when(kk == 0)
    def _():
        acc_ref[...] = jnp.zeros_like(acc_ref)

    acc_ref[...] += _dot(a_ref[...], w_ref[...])

    @pl.when(kk == nk - 1)
    def _():
        xn = x_ref[...] + _rms(acc_ref[...], nw1_ref[...])
        xo_ref[...] = xn
        if emit_h:
            h_ref[...] = _rms(xn, nw2_ref[...]).astype(h_ref.dtype)


def _matmul_residual(a, w, layer, x, nw_post, nw_next, h_dtype, tm, tk):
    m, k = a.shape
    n = w.shape[2]
    tm, tk = min(tm, m), min(tk, k)
    assert m % tm == 0 and k % tk == 0
    nk = k // tk
    emit_h = nw_next is not None
    if not emit_h:
        nw_next = nw_post
    out_shape = [jax.ShapeDtypeStruct((m, n), F32)]
    out_specs = [pl.BlockSpec((tm, n), lambda i, kk: (i, 0))]
    if emit_h:
        out_shape.append(jax.ShapeDtypeStruct((m, n), h_dtype))
        out_specs.append(pl.BlockSpec((tm, n), lambda i, kk: (i, 0)))
    res = pl.pallas_call(
        functools.partial(_mm_res_kernel, nk=nk, emit_h=emit_h), grid=(m // tm, nk),
        in_specs=[pl.BlockSpec((tm, tk), lambda i, kk: (i, kk)),
                  pl.BlockSpec((None, tk, n), lambda i, kk: (layer, kk, 0)),
                  pl.BlockSpec((tm, n), lambda i, kk: (i, 0)),
                  pl.BlockSpec((1, n), lambda i, kk: (0, 0)),
                  pl.BlockSpec((1, n), lambda i, kk: (0, 0))],
        out_specs=out_specs, out_shape=out_shape,
        scratch_shapes=[pltpu.VMEM((tm, n), F32)],
        compiler_params=_params("parallel", "arbitrary"), name="matmul_residual")(
            a, w, x, nw_post, nw_next)
    return (res[0], res[1]) if emit_h else (res[0], None)


def _gdn_gates_kernel(a_ref, w_ref, alog_ref, dtb_ref, beta_ref, g_ref):
    ba = _dot(a_ref[...], w_ref[...])
    nh = beta_ref.shape[-1]
    beta_ref[...] = _sigmoid(ba[:, :nh])
    g_ref[...] = -jnp.exp(alog_ref[...]) * _softplus(ba[:, nh:] + dtb_ref[...])


def _gdn_gates(a, w_ba, layer, a_log, dt_bias, tm):
    m, k = a.shape
    nh = GDN_V_HEADS
    tm = min(tm, m)
    return pl.pallas_call(
        _gdn_gates_kernel, grid=(m // tm,),
        in_specs=[pl.BlockSpec((tm, k), lambda i: (i, 0)),
                  pl.BlockSpec((None, k, 2 * nh), lambda i: (layer, 0, 0)),
                  pl.BlockSpec((1, nh), lambda i: (0, 0)),
                  pl.BlockSpec((1, nh), lambda i: (0, 0))],
        out_specs=[pl.BlockSpec((tm, nh), lambda i: (i, 0))] * 2,
        out_shape=[jax.ShapeDtypeStruct((m, nh), F32)] * 2,
        compiler_params=_params("parallel"), name="gdn_gates")(a, w_ba, a_log, dt_bias)


def _gdn_prep_kernel(x_ref, prev_ref, buf_ref, cw_ref, o_ref, *, n_qk_tiles):
    i = pl.program_id(1)
    j = pl.program_id(2)
    x = x_ref[...]
    tt, tc = x.shape
    halo = jnp.where(i == 0, buf_ref[...], prev_ref[...])
    row8 = lax.broadcasted_iota(jnp.int32, (SUBLANES, tc), 0)
    y = x * cw_ref[GDN_CONV_W - 1:GDN_CONV_W, :]
    for s in range(1, GDN_CONV_W):
        xs = pltpu.roll(x, s, 0)
        hs = pltpu.roll(halo, s, 0)
        top = jnp.where(row8 < s, hs, xs[:SUBLANES])
        shifted = top if tt == SUBLANES else jnp.concatenate([top, xs[SUBLANES:]], axis=0)
        y = y + shifted * cw_ref[GDN_CONV_W - 1 - s:GDN_CONV_W - s, :]
    y = y * _sigmoid(y)
    pieces = []
    for hh in range(tc // GDN_HEAD_DIM):
        yh = y[:, hh * GDN_HEAD_DIM:(hh + 1) * GDN_HEAD_DIM]
        pieces.append(yh * lax.rsqrt(jnp.sum(yh * yh, axis=-1, keepdims=True) + NORM_EPS))
    yn = jnp.concatenate(pieces, axis=1)
    o_ref[...] = jnp.where(j < n_qk_tiles, yn * GDN_HEAD_DIM ** -0.5,
                           jnp.where(j < 2 * n_qk_tiles, yn, y))


def _gdn_prep(proj3, buf8, conv_w, tt, tc):
    b, t, _ = proj3.shape
    tt = min(tt, t)
    assert t % tt == 0 and tt % SUBLANES == 0 and GDN_QK_DIM % tc == 0
    hb = tt // SUBLANES
    return pl.pallas_call(
        functools.partial(_gdn_prep_kernel, n_qk_tiles=GDN_QK_DIM // tc),
        grid=(b, t // tt, GDN_CONV_DIM // tc),
        in_specs=[pl.BlockSpec((None, tt, tc), lambda bb, i, j: (bb, i, j)),
                  pl.BlockSpec((None, SUBLANES, tc), lambda bb, i, j: (bb, jnp.maximum(i * hb - 1, 0), j)),
                  pl.BlockSpec((None, SUBLANES, tc), lambda bb, i, j: (bb, 0, j)),
                  pl.BlockSpec((GDN_CONV_W, tc), lambda bb, i, j: (0, j))],
        out_specs=pl.BlockSpec((None, tt, tc), lambda bb, i, j: (bb, i, j)),
        out_shape=jax.ShapeDtypeStruct((b, t, GDN_CONV_DIM), F32),
        compiler_params=_params("parallel", "parallel", "parallel"), name="gdn_prep")(
            proj3, proj3, buf8, conv_w)


def _gdn_delta_kernel(q_ref, k_ref, v_ref, z_ref, gc_ref, bc_ref, gr_ref, s0_ref, nw_ref,
                      o_ref, so_ref, s_scr, *, cb, t_valid, n_blocks):
    n = pl.program_id(2)
    r = cb * CHUNK
    dh = GDN_HEAD_DIM

    @pl.when(n == 0)
    def _():
        s_scr[...] = s0_ref[...]

    q = q_ref[...]
    k = k_ref[...]
    v = v_ref[...]
    gcol = gc_ref[...]
    bcol = bc_ref[...]
    grow = gr_ref[...]
    if t_valid is not None:
        okc = (n * r + lax.broadcasted_iota(jnp.int32, (r, 1), 0)) < t_valid
        okr = (n * r + lax.broadcasted_iota(jnp.int32, (1, r), 1)) < t_valid
        k = jnp.where(okc, k, 0.0)
        v = jnp.where(okc, v, 0.0)
        gcol = jnp.where(okc, gcol, 0.0)
        bcol = jnp.where(okc, bcol, 0.0)
        grow = jnp.where(okr, grow, 0.0)

    lm = _chunk_tril(r)
    gcum_col = sum(_dot(lm, p) for p in _split3(gcol))
    gcum_row = sum(_dot_nt(p, lm) for p in _split3(grow))

    ri = lax.broadcasted_iota(jnp.int32, (CHUNK, CHUNK), 0)
    ci = lax.broadcasted_iota(jnp.int32, (CHUNK, CHUNK), 1)
    eye = jnp.where(ri == ci, 1.0, 0.0)
    lvl_masks = []
    size = 1
    while size < CHUNK:
        lvl_masks.append(((ri // (2 * size)) == (ci // (2 * size)))
                         & ((ri // size) % 2 == 1) & ((ci // size) % 2 == 0))
        size *= 2

    saved = []
    for c in range(cb):
        sl = slice(c * CHUNK, (c + 1) * CHUNK)
        kc = k[sl]
        qc = q[sl]
        qk = _dot_nt(qc, kc)
        per_head = []
        for hh in range(2):
            gc = gcum_col[sl, hh:hh + 1]
            gr = gcum_row[hh:hh + 1, sl]
            bc = bcol[sl, hh:hh + 1]
            decay = jnp.exp(jnp.where(ri >= ci, gc - gr, NEG))
            kb = kc * bc
            mm_ = _dot_nt(kb, kc) * decay
            tinv = eye - jnp.where(lvl_masks[0], mm_, 0.0)
            for msk in lvl_masks[1:]:
                tinv = tinv - _dot(tinv, _dot(jnp.where(msk, mm_, 0.0), tinv))
            eg = jnp.exp(gc)
            rhs = jnp.concatenate([v[sl, hh * dh:(hh + 1) * dh] * bc, kb * eg], axis=1)
            sol = _dot(tinv, rhs)
            glast = gc[CHUNK - 1:CHUNK, :]
            per_head.append((sol[:, :dh], sol[:, dh:], qk * decay, qc * eg,
                             kc * jnp.exp(glast - gc), jnp.exp(glast)))
        saved.append(per_head)

    nw = nw_ref[...]
    for hh in range(2):
        s = s_scr[hh]
        for c in range(cb):
            sl = slice(c * CHUNK, (c + 1) * CHUNK)
            u, w, attn, qg, kd, eglast = saved[c][hh]
            ws = _dot(jnp.concatenate([w, qg], axis=0), s)
            v_new = u - ws[:CHUNK]
            o = ws[CHUNK:] + _dot(attn, v_new)
            s = s * eglast + _dot_tn(kd, v_new)
            zc = z_ref[sl, hh * dh:(hh + 1) * dh]
            o_ref[sl, hh * dh:(hh + 1) * dh] = (_rms(o, nw) * (zc * _sigmoid(zc))).astype(o_ref.dtype)
        s_scr[hh] = s

    @pl.when(n == n_blocks - 1)
    def _():
        so_ref[...] = s_scr[...]


def _gdn_delta(qkv3, proj3, gcol, bcol, grow, s0, norm_w, out_dtype, cb, t_valid):
    b, t, _ = qkv3.shape
    r = cb * CHUNK
    assert t % r == 0
    nb = t // r
    dh = GDN_HEAD_DIM
    koff = GDN_QK_DIM // dh
    voff = 2 * GDN_QK_DIM // (2 * dh)
    zoff = GDN_CONV_DIM // (2 * dh)
    return pl.pallas_call(
        functools.partial(_gdn_delta_kernel, cb=cb, t_valid=t_valid, n_blocks=nb),
        grid=(b, GDN_QK_HEADS, nb),
        in_specs=[pl.BlockSpec((None, r, dh), lambda bb, h, n: (bb, n, h)),
                  pl.BlockSpec((None, r, dh), lambda bb, h, n: (bb, n, koff + h)),
                  pl.BlockSpec((None, r, 2 * dh), lambda bb, h, n: (bb, n, voff + h)),
                  pl.BlockSpec((None, r, 2 * dh), lambda bb, h, n: (bb, n, zoff + h)),
                  pl.BlockSpec((None, None, r, 2), lambda bb, h, n: (bb, h, n, 0)),
                  pl.BlockSpec((None, None, r, 2), lambda bb, h, n: (bb, h, n, 0)),
                  pl.BlockSpec((None, None, 2, r), lambda bb, h, n: (bb, h, 0, n)),
                  pl.BlockSpec((None, 2, dh, dh), lambda bb, h, n: (bb, h, 0, 0)),
                  pl.BlockSpec((1, dh), lambda bb, h, n: (0, 0))],
        out_specs=[pl.BlockSpec((None, r, 2 * dh), lambda bb, h, n: (bb, n, h)),
                   pl.BlockSpec((None, 2, dh, dh), lambda bb, h, n: (bb, h, 0, 0))],
        out_shape=[jax.ShapeDtypeStruct((b, t, GDN_V_DIM), out_dtype),
                   jax.ShapeDtypeStruct(s0.shape, F32)],
        scratch_shapes=[pltpu.VMEM((2, dh, dh), F32)],
        compiler_params=_params("parallel", "parallel", "arbitrary"), name="gdn_delta")(
            qkv3, qkv3, qkv3, proj3, gcol, bcol, grow, s0, norm_w)


def _gla_kernel(q_ref, k_ref, v_ref, gate_ref, gk_ref, bgk_ref, s0_ref, nw_ref,
                o_ref, so_ref, s_scr, *, cb, t_valid, n_blocks):
    n = pl.program_id(2)
    r = cb * CHUNK
    dk, dv = GLA_DK, GLA_DV

    @pl.when(n == 0)
    def _():
        s_scr[...] = s0_ref[...]

    xg = gk_ref[...] + bgk_ref[...]
    g = (jnp.minimum(xg, 0.0) - jnp.log(1.0 + jnp.exp(-jnp.abs(xg)))) * (1.0 / GLA_GATE_NORMALIZER)
    q = q_ref[...] * GLA_DK ** -0.5
    k = k_ref[...]
    v = v_ref[...]
    if t_valid is not None:
        okc = (n * r + lax.broadcasted_iota(jnp.int32, (r, 1), 0)) < t_valid
        g = jnp.where(okc, g, 0.0)
        k = jnp.where(okc, k, 0.0)
        v = jnp.where(okc, v, 0.0)

    gparts = _split3(g)
    bcum = sum(_dot(_chunk_tril(r), p) for p in gparts)
    ones = jnp.ones((CHUNK, LANES), BF16)
    row = lax.broadcasted_iota(jnp.int32, (CHUNK, 1), 0)
    jrow = lax.broadcasted_iota(jnp.int32, (SUB, 1), 0)
    lane = lax.broadcasted_iota(jnp.int32, (SUB, CHUNK), 1)
    nw = nw_ref[...]
    s = s_scr[...]
    for c in range(cb):
        sl = slice(c * CHUNK, (c + 1) * CHUNK)
        qc, kc, vc, bc = q[sl], k[sl], v[sl], bcum[sl]
        blast = bc[CHUNK - 1:CHUNK, :]
        attn_t = jnp.zeros((CHUNK, CHUNK), F32)
        for bi in range(1, CHUNK // SUB):
            bref = bc[bi * SUB:bi * SUB + 1, :]
            k_i = jnp.where(row < bi * SUB, kc * jnp.exp(jnp.minimum(bref - bc, 0.0)), 0.0)
            q_i = jnp.where((row >= bi * SUB) & (row < (bi + 1) * SUB),
                            qc * jnp.exp(jnp.minimum(bc - bref, 0.0)), 0.0)
            attn_t = attn_t + _dot_nt(k_i, q_i)
        diag = []
        for bi in range(CHUNK // SUB):
            sb = slice(bi * SUB, (bi + 1) * SUB)
            qb, kb, bb = qc[sb], kc[sb], bc[sb]
            d = jnp.zeros((SUB, CHUNK), F32)
            for il in range(SUB):
                e = jnp.exp(jnp.minimum(bb[il:il + 1, :] - bb, 0.0))
                col = jnp.sum(kb * e * qb[il:il + 1, :], axis=-1, keepdims=True)
                col = jnp.where(jrow <= il, col, 0.0)
                d = jnp.where(lane == bi * SUB + il, col, d)
            diag.append(d)
        attn_t = attn_t + jnp.concatenate(diag, axis=0)
        o = _dot(qc * jnp.exp(bc), s) + _dot_tn(attn_t, vc)
        bl_col = sum(_dot_tn(p[sl], ones) for p in gparts)
        decay_col = jnp.concatenate([jnp.exp(bl_col)] * (dv // LANES), axis=1)
        s = s * decay_col + _dot_tn(kc * jnp.exp(blast - bc), vc)
        gt = gate_ref[sl, :]
        o_ref[sl, :] = (_rms(o, nw) * (gt * _sigmoid(gt))).astype(o_ref.dtype)
    s_scr[...] = s

    @pl.when(n == n_blocks - 1)
    def _():
        so_ref[...] = s_scr[...]


def _gla_chunks(qkvg3, gk3, b_gk, s0, norm_w, out_dtype, cb, t_valid):
    b, t, _ = qkvg3.shape
    r = cb * CHUNK
    assert t % r == 0
    nb = t // r
    dk, dv = GLA_DK, GLA_DV
    koff = GLA_HEADS
    voff = 2 * GLA_HEADS * dk // dv
    goff = voff + GLA_HEADS
    return pl.pallas_call(
        functools.partial(_gla_kernel, cb=cb, t_valid=t_valid, n_blocks=nb),
        grid=(b, GLA_HEADS, nb),
        in_specs=[pl.BlockSpec((None, r, dk), lambda bb, h, n: (bb, n, h)),
                  pl.BlockSpec((None, r, dk), lambda bb, h, n: (bb, n, koff + h)),
                  pl.BlockSpec((None, r, dv), lambda bb, h, n: (bb, n, voff + h)),
                  pl.BlockSpec((None, r, dv), lambda bb, h, n: (bb, n, goff + h)),
                  pl.BlockSpec((None, r, dk), lambda bb, h, n: (bb, n, h)),
                  pl.BlockSpec((1, dk), lambda bb, h, n: (0, h)),
                  pl.BlockSpec((None, None, dk, dv), lambda bb, h, n: (bb, h, 0, 0)),
                  pl.BlockSpec((1, dv), lambda bb, h, n: (0, 0))],
        out_specs=[pl.BlockSpec((None, r, dv), lambda bb, h, n: (bb, n, h)),
                   pl.BlockSpec((None, None, dk, dv), lambda bb, h, n: (bb, h, 0, 0))],
        out_shape=[jax.ShapeDtypeStruct((b, t, GLA_HEADS * dv), out_dtype),
                   jax.ShapeDtypeStruct(s0.shape, F32)],
        scratch_shapes=[pltpu.VMEM((dk, dv), F32)],
        compiler_params=_params("parallel", "parallel", "arbitrary"), name="gla_chunks")(
            qkvg3, qkvg3, qkvg3, qkvg3, gk3, b_gk, s0, norm_w)


def _t5_bucket(n):
    n = np.asarray(n)
    max_exact = N_BUCKETS // 2
    nf = np.maximum(n, max_exact).astype(np.float32)
    large = max_exact + (np.log(nf / max_exact) / math.log(MAX_DISTANCE / max_exact)
                         * (N_BUCKETS - max_exact)).astype(np.int32)
    return np.where(n < max_exact, n, np.minimum(large, N_BUCKETS - 1)).astype(np.int32)


def _lambda(lam_ref, lam_init):
    lf = lam_ref[...]
    s1 = jnp.sum(lf[0:1] * lf[1:2], axis=-1, keepdims=True)
    s2 = jnp.sum(lf[2:3] * lf[3:4], axis=-1, keepdims=True)
    return jnp.exp(s1) - jnp.exp(s2) + lam_init


def _flash_kernel(q_ref, k_ref, v_ref, bd_ref, bs_ref, rb_ref, lam_ref, sub_ref, o_ref,
                  m_scr, l_scr, acc_scr, bias_scr, *, lam_init):
    h = pl.program_id(1)
    qi = pl.program_id(2)
    ki = pl.program_id(3)
    tb = q_ref.shape[0]
    dh = DIFF_HEAD_DIM

    @pl.when((qi == 0) & (ki == 0))
    def _():
        ri = lax.broadcasted_iota(jnp.int32, (tb, tb), 0)
        ci = lax.broadcasted_iota(jnp.int32, (tb, tb), 1)
        for mm in range(2):
            bd = jnp.zeros((tb, tb), F32)
            bs = jnp.zeros((tb, tb), F32)
            for bk in range(N_BUCKETS):
                val = rb_ref[bk, 2 * h + mm]
                bd = jnp.where(bd_ref[...] == bk, val, bd)
                bs = jnp.where(bs_ref[...] == bk, val, bs)
            bias_scr[mm, 0] = jnp.where(ci > ri, NEG, bd)
            bias_scr[mm, 1] = bs

    @pl.when(ki == 0)
    def _():
        m_scr[...] = jnp.full_like(m_scr, NEG)
        l_scr[...] = jnp.zeros_like(l_scr)
        acc_scr[...] = jnp.zeros_like(acc_scr)

    @pl.when(ki <= qi)
    def _():
        v = v_ref[...].astype(BF16)
        for mm in range(2):
            s = _dot_nt(q_ref[:, mm * dh:(mm + 1) * dh], k_ref[:, mm * dh:(mm + 1) * dh]) * dh ** -0.5
            far = rb_ref[N_BUCKETS - 1, 2 * h + mm]
            s = s + jnp.where(ki == qi, bias_scr[mm, 0], jnp.where(ki == qi - 1, bias_scr[mm, 1], far))
            m_prev = m_scr[mm]
            m_new = jnp.maximum(m_prev, jnp.max(s, axis=-1, keepdims=True))
            alpha = jnp.exp(m_prev - m_new)
            p = jnp.exp(s - m_new)
            l_scr[mm] = alpha * l_scr[mm] + jnp.sum(p, axis=-1, keepdims=True)
            acc_scr[mm] = alpha * acc_scr[mm] + _dot(p, v)
            m_scr[mm] = m_new

    @pl.when(ki == qi)
    def _():
        lam = _lambda(lam_ref, lam_init)
        out = acc_scr[0] / l_scr[0] - lam * (acc_scr[1] / l_scr[1])
        o_ref[...] = (_rms(out, sub_ref[...]) * (1.0 - lam_init)).astype(o_ref.dtype)


def _flash_diff(qkv3, rel_bias, lambdas, subln, lam_init, out_dtype, tb):
    b, t, _ = qkv3.shape
    tb = min(tb, t)
    assert t % tb == 0 and tb >= MAX_DISTANCE
    nq = t // tb
    dd = 2 * DIFF_HEAD_DIM
    i = np.arange(tb)[:, None]
    j = np.arange(tb)[None, :]
    bkt_diag = jnp.asarray(_t5_bucket(np.maximum(i - j, 0)))
    bkt_sub = jnp.asarray(_t5_bucket(tb + i - j))
    koff = D_MODEL // dd
    return pl.pallas_call(
        functools.partial(_flash_kernel, lam_init=lam_init),
        grid=(b, DIFF_HEADS, nq, nq),
        in_specs=[pl.BlockSpec((None, tb, dd), lambda bb, h, qi, ki: (bb, qi, h)),
                  pl.BlockSpec((None, tb, dd), lambda bb, h, qi, ki: (bb, jnp.minimum(ki, qi), koff + h)),
                  pl.BlockSpec((None, tb, dd), lambda bb, h, qi, ki: (bb, jnp.minimum(ki, qi), 2 * koff + h)),
                  pl.BlockSpec((tb, tb), lambda bb, h, qi, ki: (0, 0)),
                  pl.BlockSpec((tb, tb), lambda bb, h, qi, ki: (0, 0)),
                  pl.BlockSpec(memory_space=pltpu.SMEM),
                  pl.BlockSpec((4, DIFF_HEAD_DIM), lambda bb, h, qi, ki: (0, 0)),
                  pl.BlockSpec((1, dd), lambda bb, h, qi, ki: (0, 0))],
        out_specs=pl.BlockSpec((None, tb, dd), lambda bb, h, qi, ki: (bb, qi, h)),
        out_shape=jax.ShapeDtypeStruct((b, t, D_MODEL), out_dtype),
        scratch_shapes=[pltpu.VMEM((2, tb, 1), F32), pltpu.VMEM((2, tb, 1), F32),
                        pltpu.VMEM((2, tb, dd), F32), pltpu.VMEM((2, 2, tb, tb), F32)],
        compiler_params=_params("parallel", "parallel", "arbitrary", "arbitrary"), name="flash_diff")(
            qkv3, qkv3, qkv3, bkt_diag, bkt_sub, rel_bias, lambdas, subln)


def _decode_kernel(pt_ref, q_ref, kp_ref, vp_ref, kn_ref, vn_ref, bkt_ref, tbl_ref, lam_ref, sub_ref,
                   o_ref, m_scr, l_scr, acc_scr, bias_scr, *, n_pages, lam_init):
    p = pl.program_id(1)
    nh = DIFF_HEADS
    scale = DIFF_HEAD_DIM ** -0.5

    @pl.when(p == 0)
    def _():
        m_scr[...] = jnp.full_like(m_scr, NEG)
        l_scr[...] = jnp.zeros_like(l_scr)
        acc_scr[...] = jnp.zeros_like(acc_scr)
        bkt = bkt_ref[...]
        for par in range(2):
            bias = jnp.zeros((PAGE_SIZE, nh, 1), F32)
            for bk in range(N_BUCKETS):
                bias = jnp.where(bkt == bk, tbl_ref[bk, par][None], bias)
            bias_scr[par] = bias

    vpage = vp_ref[...]
    for par in range(2):
        qv = q_ref[par]
        kk = kp_ref[:, pl.ds(par, nh, stride=2), :]
        s = jnp.sum(kk * qv[None], axis=-1, keepdims=True) * scale
        s = s + jnp.where(p == n_pages - 1, bias_scr[par], tbl_ref[N_BUCKETS - 1, par][None])
        m_prev = m_scr[par]
        m_new = jnp.maximum(m_prev, jnp.max(s, axis=0))
        alpha = jnp.exp(m_prev - m_new)
        pe = jnp.exp(s - m_new[None])
        l_scr[par] = alpha * l_scr[par] + jnp.sum(pe, axis=0)
        acc_scr[par] = alpha * acc_scr[par] + jnp.sum(pe * vpage, axis=0)
        m_scr[par] = m_new

    @pl.when(p == n_pages - 1)
    def _():
        outs = []
        for par in range(2):
            qv = q_ref[par]
            s = jnp.sum(kn_ref[par] * qv, axis=-1, keepdims=True) * scale + tbl_ref[0, par]
            m_prev = m_scr[par]
            m_new = jnp.maximum(m_prev, s)
            alpha = jnp.exp(m_prev - m_new)
            pe = jnp.exp(s - m_new)
            l = alpha * l_scr[par] + pe
            acc = alpha * acc_scr[par] + pe * vn_ref[...]
            outs.append(acc / l)
        out = outs[0] - _lambda(lam_ref, lam_init) * outs[1]
        o_ref[...] = _rms(out, sub_ref[...]) * (1.0 - lam_init)


def _decode_diff(q_eo, kn_eo, v_new, cache_k, cache_v, layer, page_table, rel_bias, lambdas, subln, lam_init):
    b, n_pages = page_table.shape
    nh, dh = DIFF_HEADS, DIFF_HEAD_DIM
    past = n_pages * PAGE_SIZE
    bkt_last = jnp.asarray(_t5_bucket(PAGE_SIZE - np.arange(PAGE_SIZE)).reshape(PAGE_SIZE, 1, 1))
    assert past - (n_pages - 1) * PAGE_SIZE >= PAGE_SIZE and PAGE_SIZE >= MAX_DISTANCE
    tbl = jnp.transpose(rel_bias.reshape(N_BUCKETS, nh, 2), (0, 2, 1))[..., None]
    grid_spec = pltpu.PrefetchScalarGridSpec(
        num_scalar_prefetch=1, grid=(b, n_pages),
        in_specs=[pl.BlockSpec((None, 2, nh, dh), lambda bb, p, pt: (bb, 0, 0, 0)),
                  pl.BlockSpec((None, None, PAGE_SIZE, 2 * nh, dh), lambda bb, p, pt: (layer, pt[bb, p], 0, 0, 0)),
                  pl.BlockSpec((None, None, PAGE_SIZE, nh, 2 * dh), lambda bb, p, pt: (layer, pt[bb, p], 0, 0, 0)),
                  pl.BlockSpec((None, 2, nh, dh), lambda bb, p, pt: (bb, 0, 0, 0)),
                  pl.BlockSpec((None, nh, 2 * dh), lambda bb, p, pt: (bb, 0, 0)),
                  pl.BlockSpec((PAGE_SIZE, 1, 1), lambda bb, p, pt: (0, 0, 0)),
                  pl.BlockSpec((N_BUCKETS, 2, nh, 1), lambda bb, p, pt: (0, 0, 0, 0)),
                  pl.BlockSpec((4, dh), lambda bb, p, pt: (0, 0)),
                  pl.BlockSpec((1, 2 * dh), lambda bb, p, pt: (0, 0))],
        out_specs=pl.BlockSpec((None, nh, 2 * dh), lambda bb, p, pt: (bb, 0, 0)),
        scratch_shapes=[pltpu.VMEM((2, nh, 1), F32), pltpu.VMEM((2, nh, 1), F32),
                        pltpu.VMEM((2, nh, 2 * dh), F32), pltpu.VMEM((2, PAGE_SIZE, nh, 1), F32)])
    return pl.pallas_call(
        functools.partial(_decode_kernel, n_pages=n_pages, lam_init=lam_init),
        grid_spec=grid_spec,
        out_shape=jax.ShapeDtypeStruct((b, nh, 2 * dh), F32),
        compiler_params=_params("parallel", "arbitrary"), name="decode_diff")(
            page_table, q_eo, cache_k, cache_v, kn_eo, v_new, bkt_last, tbl, lambdas, subln)


class _Tiles:
    def __init__(self, prompt):
        self.prompt = prompt
        self.act_dtype = BF16 if prompt else F32
        self.tm = 1024 if prompt else SUBLANES
        self.tn = 512 if prompt else 1024
        self.tn_ff = 512
        self.tm_res = 512 if prompt else SUBLANES
        self.tk_res = 512
        self.tt_prep = 256 if prompt else SUBLANES
        self.tc_prep = 1024
        self.cb = 4 if prompt else 1
        self.tb_attn = 512


def _group_trunk(x3, tiles, gdn_s0, gdn_c0, gla_s0, cache_k, cache_v, page_table, p):
    b, t, d = x3.shape
    m = b * t
    x = x3.reshape(m, d)
    if not tiles.prompt:
        assert t == 1 and m % SUBLANES == 0
    t_pad = t if tiles.prompt else CHUNK
    t_valid = None if tiles.prompt else t
    nw = lambda i, jn: p['norm_w'][i, jn].reshape(1, d)
    adt = tiles.act_dtype

    def pad_time(a3, to):
        return a3 if a3.shape[1] == to else jnp.pad(a3, ((0, 0), (0, to - a3.shape[1]), (0, 0)))

    gdn_s, gdn_c, gla_s, att_k, att_v = [], [], [], [], []
    h = _rmsnorm(x, nw(0, 0), adt, tiles.tm_res)
    for i in range(DEPTH):
        j = i // N_MIXERS
        kind = i % N_MIXERS
        if kind == 0:
            proj = _matmul(h, p['gdn_w_in'], j, F32, tiles.tm, tiles.tn)
            beta, g = _gdn_gates(h, p['gdn_w_ba'], j, p['gdn_a_log'][j].reshape(1, -1),
                                 p['gdn_dt_bias'][j].reshape(1, -1), tiles.tm)
            proj3 = proj.reshape(b, t, -1)
            buf0 = gdn_c0[j]
            nbuf = GDN_CONV_W - 1
            if t >= nbuf:
                new_buf = proj3[:, t - nbuf:, :GDN_CONV_DIM]
            else:
                new_buf = jnp.concatenate([buf0[:, t:], proj3[..., :GDN_CONV_DIM]], axis=1)
            buf8 = jnp.pad(buf0, ((0, 0), (SUBLANES - (GDN_CONV_W - 1), 0), (0, 0)))
            t8 = max(t, SUBLANES)
            proj3p = pad_time(proj3, t8)
            qkv3 = pad_time(_gdn_prep(proj3p, buf8, p['gdn_conv_w'][j], tiles.tt_prep, tiles.tc_prep), t_pad)
            proj3p = pad_time(proj3p, t_pad)

            def heads(a):
                a = pad_time(a.reshape(b, t, GDN_QK_HEADS * 2), t_pad)
                return a.reshape(b, t_pad, GDN_QK_HEADS, 2).transpose(0, 2, 1, 3)
            gcol, bcol = heads(g), heads(beta)
            o3, s_new = _gdn_delta(qkv3, proj3p, gcol, bcol, gcol.transpose(0, 1, 3, 2), gdn_s0[j],
                                   p['gdn_norm_w'][j].reshape(1, -1), adt, tiles.cb, t_valid)
            mix_in, w_out = o3[:, :t].reshape(m, -1), p['gdn_w_out']
            gdn_s.append(s_new)
            gdn_c.append(new_buf)
        elif kind == 1:
            qkvg = _matmul(h, p['gla_w_qkvg'], j, F32, tiles.tm, tiles.tn)
            low = _matmul(h, p['gla_w_gk1'], j, F32, tiles.tm, tiles.tn)
            gk = _matmul(low, p['gla_w_gk2'], j, F32, tiles.tm, tiles.tn)
            o3, s_new = _gla_chunks(pad_time(qkvg.reshape(b, t, -1), t_pad), pad_time(gk.reshape(b, t, -1), t_pad),
                                    p['gla_b_gk'][j].reshape(1, -1), gla_s0[j],
                                    p['gla_norm_w'][j].reshape(1, -1), adt, tiles.cb, t_valid)
            mix_in, w_out = o3[:, :t].reshape(m, -1), p['gla_w_out']
            gla_s.append(s_new)
        else:
            lam_init = 0.8 - 0.6 * math.exp(-0.3 * i)
            qkv = _matmul(h, p['diff_w_qkv'], j, F32, tiles.tm, tiles.tn)
            nh, dh = DIFF_HEADS, DIFF_HEAD_DIM
            k_new = qkv[:, D_MODEL:2 * D_MODEL].reshape(b, t, 2 * nh, dh)
            v_new = qkv[:, 2 * D_MODEL:].reshape(b, t, nh, 2 * dh)
            subln = p['diff_subln'][j].reshape(1, -1)
            if tiles.prompt:
                o3 = _flash_diff(qkv.reshape(b, t, -1), p['rel_bias'], p['diff_lambda'][j], subln,
                                 lam_init, adt, tiles.tb_attn)
                mix_in = o3.reshape(m, -1)
            else:
                eo = lambda a: a.reshape(b, nh, 2, dh).transpose(0, 2, 1, 3)
                o = _decode_diff(eo(qkv[:, :D_MODEL]), eo(qkv[:, D_MODEL:2 * D_MODEL]), v_new.reshape(b, nh, 2 * dh),
                                 cache_k, cache_v, j, page_table, p['rel_bias'], p['diff_lambda'][j],
                                 subln, lam_init)
                mix_in = o.reshape(m, -1)
            w_out = p['diff_w_out']
            att_k.append(k_new)
            att_v.append(v_new)
        x, h = _matmul_residual(mix_in, w_out, j, x, nw(i, 1), nw(i, 2), adt, tiles.tm_res, tiles.tk_res)
        act = _swiglu_up(h, p['ffn_w_up'], i, adt, tiles.tm, tiles.tn_ff)
        nxt = nw(i + 1, 0) if i + 1 < DEPTH else None
        x, h = _matmul_residual(act, p['ffn_w_down'], i, x, nw(i, 3), nxt, adt, tiles.tm_res, tiles.tk_res)
    return (x.reshape(b, t, d), jnp.stack(gdn_s), jnp.stack(gdn_c), jnp.stack(gla_s),
            jnp.stack(att_k), jnp.stack(att_v))


def kernel(x_prompt, x_sample, state_gdn, state_gdn_conv, state_gla, cache_k, cache_v, page_table,
           norm_w, ffn_w_up, ffn_w_down, rel_bias,
           gdn_w_in, gdn_w_ba, gdn_conv_w, gdn_a_log, gdn_dt_bias, gdn_norm_w, gdn_w_out,
           gla_w_qkvg, gla_w_gk1, gla_w_gk2, gla_b_gk, gla_norm_w, gla_w_out,
           diff_w_qkv, diff_lambda, diff_subln, diff_w_out):
    p = dict(norm_w=norm_w, ffn_w_up=ffn_w_up, ffn_w_down=ffn_w_down, rel_bias=rel_bias,
             gdn_w_in=gdn_w_in, gdn_w_ba=gdn_w_ba, gdn_conv_w=gdn_conv_w, gdn_a_log=gdn_a_log,
             gdn_dt_bias=gdn_dt_bias, gdn_norm_w=gdn_norm_w, gdn_w_out=gdn_w_out,
             gla_w_qkvg=gla_w_qkvg, gla_w_gk1=gla_w_gk1, gla_w_gk2=gla_w_gk2, gla_b_gk=gla_b_gk,
             gla_norm_w=gla_norm_w, gla_w_out=gla_w_out,
             diff_w_qkv=diff_w_qkv, diff_lambda=diff_lambda, diff_subln=diff_subln, diff_w_out=diff_w_out)
    bp = x_prompt.shape[0]
    n_gdn, n_gla = state_gdn.shape[0], state_gla.shape[0]
    zeros_gdn = jnp.zeros((n_gdn, bp) + state_gdn.shape[2:], F32)
    zeros_conv = jnp.zeros((n_gdn, bp) + state_gdn_conv.shape[2:], F32)
    zeros_gla = jnp.zeros((n_gla, bp) + state_gla.shape[2:], F32)
    outs_p = _group_trunk(x_prompt, _Tiles(True), zeros_gdn, zeros_conv, zeros_gla, None, None, None, p)
    outs_s = _group_trunk(x_sample, _Tiles(False), state_gdn, state_gdn_conv, state_gla,
                          cache_k, cache_v, page_table, p)
    return (outs_p[0], outs_s[0]) + outs_p[1:] + outs_s[1:]
```

```python
import functools
import math

import numpy as np
import jax
import jax.numpy as jnp
from jax import lax
from jax.experimental import pallas as pl
from jax.experimental.pallas import tpu as pltpu

F32 = jnp.float32
BF16 = jnp.bfloat16

D_MODEL = 2048
DEPTH = 4
PAGE_SIZE = 128
N_MIXERS = 3
GDN_HEAD_DIM = 128
GDN_QK_HEADS = D_MODEL // 128
GDN_V_HEADS = 2 * GDN_QK_HEADS
GDN_QK_DIM = GDN_QK_HEADS * GDN_HEAD_DIM
GDN_V_DIM = GDN_V_HEADS * GDN_HEAD_DIM
GDN_CONV_DIM = 2 * GDN_QK_DIM + GDN_V_DIM
GDN_CONV_W = 4
GLA_HEADS = 4
GLA_DK = D_MODEL // 2 // GLA_HEADS
GLA_DV = D_MODEL // GLA_HEADS
GLA_GATE_NORMALIZER = 16.0
DIFF_HEAD_DIM = 128
DIFF_HEADS = D_MODEL // (2 * DIFF_HEAD_DIM)
N_BUCKETS = 32
MAX_DISTANCE = 128
D_FF = ((8 * D_MODEL + 3 * 256 - 1) // (3 * 256)) * 256
NORM_EPS = 1e-6

LANES = 128
SUBLANES = 8
CHUNK = 64
SUB = 16
VMEM_LIMIT_BYTES = 48 * 1024 * 1024
VMEM_LIMIT_BIG_BYTES = 56 * 1024 * 1024
NEG = -1e30


def _params(*sem, vmem=VMEM_LIMIT_BYTES):
    return pltpu.CompilerParams(dimension_semantics=sem, vmem_limit_bytes=vmem)


def _dot(a, b):
    return jnp.dot(a.astype(BF16), b.astype(BF16), preferred_element_type=F32)


def _dot_nt(a, b):
    return lax.dot_general(a.astype(BF16), b.astype(BF16), (((1,), (1,)), ((), ())),
                           preferred_element_type=F32)


def _dot_tn(a, b):
    return lax.dot_general(a.astype(BF16), b.astype(BF16), (((0,), (0,)), ((), ())),
                           preferred_element_type=F32)


def _split3(x):
    hi = x.astype(BF16)
    r = x - hi.astype(F32)
    mid = r.astype(BF16)
    lo = (r - mid.astype(F32)).astype(BF16)
    return hi, mid, lo


def _sigmoid(x):
    return 1.0 / (1.0 + jnp.exp(-x))


def _softplus(x):
    return jnp.maximum(x, 0.0) + jnp.log(1.0 + jnp.exp(-jnp.abs(x)))


def _rms(x, w):
    return x * lax.rsqrt(jnp.mean(x * x, axis=-1, keepdims=True) + NORM_EPS) * w


def _chunk_tril(r):
    i = lax.broadcasted_iota(jnp.int32, (r, r), 0)
    j = lax.broadcasted_iota(jnp.int32, (r, r), 1)
    return jnp.where((j <= i) & ((i // CHUNK) == (j // CHUNK)), 1.0, 0.0).astype(BF16)


def _rmsnorm_kernel(x_ref, w_ref, o_ref):
    o_ref[...] = _rms(x_ref[...], w_ref[...]).astype(o_ref.dtype)


def _rmsnorm(x, w, out_dtype, tm):
    m, d = x.shape
    tm = min(tm, m)
    return pl.pallas_call(
        _rmsnorm_kernel, grid=(m // tm,),
        in_specs=[pl.BlockSpec((tm, d), lambda i: (i, 0)), pl.BlockSpec((1, d), lambda i: (0, 0))],
        out_specs=pl.BlockSpec((tm, d), lambda i: (i, 0)),
        out_shape=jax.ShapeDtypeStruct((m, d), out_dtype),
        compiler_params=_params("parallel"), name="rmsnorm")(x, w)


def _mm_kernel(a_ref, w_ref, o_ref):
    o_ref[...] = _dot(a_ref[...], w_ref[...]).astype(o_ref.dtype)


def _matmul(a, w, layer, out_dtype, tm, tn):
    m, k = a.shape
    n = w.shape[2]
    tm, tn = min(tm, m), min(tn, n)
    assert m % tm == 0 and n % tn == 0
    return pl.pallas_call(
        _mm_kernel, grid=(m // tm, n // tn),
        in_specs=[pl.BlockSpec((tm, k), lambda i, j: (i, 0)),
                  pl.BlockSpec((None, k, tn), lambda i, j: (layer, 0, j))],
        out_specs=pl.BlockSpec((tm, tn), lambda i, j: (i, j)),
        out_shape=jax.ShapeDtypeStruct((m, n), out_dtype),
        compiler_params=_params("parallel", "parallel"), name="matmul")(a, w)


def _swiglu_kernel(a_ref, wg_ref, wu_ref, o_ref):
    a = a_ref[...].astype(BF16)
    g = _dot(a, wg_ref[...])
    u = _dot(a, wu_ref[...])
    o_ref[...] = (g * _sigmoid(g) * u).astype(o_ref.dtype)


def _swiglu_up(a, w_up, layer, out_dtype, tm, tn):
    m, k = a.shape
    f = w_up.shape[2] // 2
    tm, tn = min(tm, m), min(tn, f)
    assert m % tm == 0 and f % tn == 0
    nf = f // tn
    return pl.pallas_call(
        _swiglu_kernel, grid=(m // tm, nf),
        in_specs=[pl.BlockSpec((tm, k), lambda i, j: (i, 0)),
                  pl.BlockSpec((None, k, tn), lambda i, j: (layer, 0, j)),
                  pl.BlockSpec((None, k, tn), lambda i, j: (layer, 0, j + nf))],
        out_specs=pl.BlockSpec((tm, tn), lambda i, j: (i, j)),
        out_shape=jax.ShapeDtypeStruct((m, f), out_dtype),
        compiler_params=_params("parallel", "parallel"), name="swiglu_up")(a, w_up, w_up)


def _mm_res_kernel(a_ref, w_ref, x_ref, nw1_ref, nw2_ref, xo_ref, *h_ref, nk, emit_h):
    kk = pl.program_id(1)
    part = _dot(a_ref[...], w_ref[...])

    @pl.when(kk == 0)
    def _():
        xo_ref[...] = part

    @pl.when(kk > 0)
    def _():
        xo_ref[...] += part

    @pl.when(kk == nk - 1)
    def _():
        xn = x_ref[...] + _rms(xo_ref[...], nw1_ref[...])
        xo_ref[...] = xn
        if emit_h:
            h_ref[0][...] = _rms(xn, nw2_ref[...]).astype(h_ref[0].dtype)


def _matmul_residual(a, w, layer, x, nw_post, nw_next, h_dtype, tm, tk):
    m, k = a.shape
    n = w.shape[2]
    tm, tk = min(tm, m), min(tk, k)
    assert m % tm == 0 and k % tk == 0
    nk = k // tk
    emit_h = nw_next is not None
    if not emit_h:
        nw_next = nw_post
    out_shape = [jax.ShapeDtypeStruct((m, n), F32)]
    out_specs = [pl.BlockSpec((tm, n), lambda i, kk: (i, 0))]
    if emit_h:
        out_shape.append(jax.ShapeDtypeStruct((m, n), h_dtype))
        out_specs.append(pl.BlockSpec((tm, n), lambda i, kk: (i, 0)))
    res = pl.pallas_call(
        functools.partial(_mm_res_kernel, nk=nk, emit_h=emit_h), grid=(m // tm, nk),
        in_specs=[pl.BlockSpec((tm, tk), lambda i, kk: (i, kk)),
                  pl.BlockSpec((None, tk, n), lambda i, kk: (layer, kk, 0)),
                  pl.BlockSpec((tm, n), lambda i, kk: (i, 0), pipeline_mode=pl.Buffered(1)),
                  pl.BlockSpec((1, n), lambda i, kk: (0, 0)),
                  pl.BlockSpec((1, n), lambda i, kk: (0, 0))],
        out_specs=out_specs, out_shape=out_shape,
        compiler_params=_params("parallel", "arbitrary", vmem=VMEM_LIMIT_BIG_BYTES), name="matmul_residual")(
            a, w, x, nw_post, nw_next)
    return (res[0], res[1]) if emit_h else (res[0], None)


def _gdn_gates_kernel(a_ref, w_ref, alog_ref, dtb_ref, beta_ref, g_ref):
    ba = _dot(a_ref[...], w_ref[...])
    nh = beta_ref.shape[-1]
    beta_ref[...] = _sigmoid(ba[:, :nh])
    g_ref[...] = -jnp.exp(alog_ref[...]) * _softplus(ba[:, nh:] + dtb_ref[...])


def _gdn_gates(a, w_ba, layer, a_log, dt_bias, tm):
    m, k = a.shape
    nh = GDN_V_HEADS
    tm = min(tm, m)
    return pl.pallas_call(
        _gdn_gates_kernel, grid=(m // tm,),
        in_specs=[pl.BlockSpec((tm, k), lambda i: (i, 0)),
                  pl.BlockSpec((None, k, 2 * nh), lambda i: (layer, 0, 0)),
                  pl.BlockSpec((1, nh), lambda i: (0, 0)),
                  pl.BlockSpec((1, nh), lambda i: (0, 0))],
        out_specs=[pl.BlockSpec((tm, nh), lambda i: (i, 0))] * 2,
        out_shape=[jax.ShapeDtypeStruct((m, nh), F32)] * 2,
        compiler_params=_params("parallel"), name="gdn_gates")(a, w_ba, a_log, dt_bias)


def _gdn_prep_kernel(x_ref, prev_ref, buf_ref, cw_ref, o_ref, *, n_qk_tiles):
    i = pl.program_id(1)
    j = pl.program_id(2)
    x = x_ref[...]
    tt, tc = x.shape
    halo = jnp.where(i == 0, buf_ref[...], prev_ref[...])
    row8 = lax.broadcasted_iota(jnp.int32, (SUBLANES, tc), 0)
    y = x * cw_ref[GDN_CONV_W - 1:GDN_CONV_W, :]
    for s in range(1, GDN_CONV_W):
        xs = pltpu.roll(x, s, 0)
        hs = pltpu.roll(halo, s, 0)
        top = jnp.where(row8 < s, hs, xs[:SUBLANES])
        shifted = top if tt == SUBLANES else jnp.concatenate([top, xs[SUBLANES:]], axis=0)
        y = y + shifted * cw_ref[GDN_CONV_W - 1 - s:GDN_CONV_W - s, :]
    y = y * _sigmoid(y)
    pieces = []
    for hh in range(tc // GDN_HEAD_DIM):
        yh = y[:, hh * GDN_HEAD_DIM:(hh + 1) * GDN_HEAD_DIM]
        pieces.append(yh * lax.rsqrt(jnp.sum(yh * yh, axis=-1, keepdims=True) + NORM_EPS))
    yn = jnp.concatenate(pieces, axis=1)
    o_ref[...] = jnp.where(j < n_qk_tiles, yn * GDN_HEAD_DIM ** -0.5,
                           jnp.where(j < 2 * n_qk_tiles, yn, y))


def _gdn_prep(proj3, buf8, conv_w, tt, tc):
    b, t, _ = proj3.shape
    tt = min(tt, t)
    assert t % tt == 0 and tt % SUBLANES == 0 and GDN_QK_DIM % tc == 0
    hb = tt // SUBLANES
    return pl.pallas_call(
        functools.partial(_gdn_prep_kernel, n_qk_tiles=GDN_QK_DIM // tc),
        grid=(b, t // tt, GDN_CONV_DIM // tc),
        in_specs=[pl.BlockSpec((None, tt, tc), lambda bb, i, j: (bb, i, j)),
                  pl.BlockSpec((None, SUBLANES, tc), lambda bb, i, j: (bb, jnp.maximum(i * hb - 1, 0), j)),
                  pl.BlockSpec((None, SUBLANES, tc), lambda bb, i, j: (bb, 0, j)),
                  pl.BlockSpec((GDN_CONV_W, tc), lambda bb, i, j: (0, j))],
        out_specs=pl.BlockSpec((None, tt, tc), lambda bb, i, j: (bb, i, j)),
        out_shape=jax.ShapeDtypeStruct((b, t, GDN_CONV_DIM), F32),
        compiler_params=_params("parallel", "parallel", "parallel"), name="gdn_prep")(
            proj3, proj3, buf8, conv_w)


def _gdn_delta_kernel(q_ref, k_ref, v_ref, z_ref, gc_ref, bc_ref, gr_ref, s0_ref, nw_ref,
                      o_ref, so_ref, s_scr, *, cb, hpb, t_valid, n_blocks):
    n = pl.program_id(2)
    r = cb * CHUNK
    dh = GDN_HEAD_DIM

    @pl.when(n == 0)
    def _():
        s_scr[...] = s0_ref[...]

    q = q_ref[...]
    k = k_ref[...]
    v = v_ref[...]
    gcol = gc_ref[...]
    bcol = bc_ref[...]
    grow = gr_ref[...]
    if t_valid is not None:
        okc = (n * r + lax.broadcasted_iota(jnp.int32, (r, 1), 0)) < t_valid
        okr = (n * r + lax.broadcasted_iota(jnp.int32, (1, r), 1)) < t_valid
        k = jnp.where(okc, k, 0.0)
        v = jnp.where(okc, v, 0.0)
        gcol = jnp.where(okc, gcol, 0.0)
        bcol = jnp.where(okc, bcol, 0.0)
        grow = jnp.where(okr, grow, 0.0)

    lm = _chunk_tril(r)
    gcum_col = [sum(_dot(lm, p) for p in _split3(gcol[qh])) for qh in range(hpb)]
    gcum_row = [sum(_dot_nt(p, lm) for p in _split3(grow[qh])) for qh in range(hpb)]

    ri = lax.broadcasted_iota(jnp.int32, (r, r), 0)
    ci = lax.broadcasted_iota(jnp.int32, (r, r), 1)
    eye = jnp.where(ri == ci, 1.0, 0.0)
    tril = ((ri // CHUNK) == (ci // CHUNK)) & (ri >= ci)
    lvl_masks = []
    size = 1
    while size < CHUNK:
        lvl_masks.append(((ri // (2 * size)) == (ci // (2 * size)))
                         & ((ri // size) % 2 == 1) & ((ci // size) % 2 == 0))
        size *= 2
    if t_valid is not None and t_valid <= 1:
        lvl_masks = []

    nw = nw_ref[...]
    heads = [(qh, hh) for qh in range(hpb) for hh in range(2)]
    kq = [k[:, qh * dh:(qh + 1) * dh] for qh in range(hpb)]
    qq = [q[:, qh * dh:(qh + 1) * dh] for qh in range(hpb)]
    gram = [_dot_nt(kq[qh], kq[qh]) for qh in range(hpb)]
    qk = [_dot_nt(qq[qh], kq[qh]) for qh in range(hpb)]
    gc = [gcum_col[qh][:, hh:hh + 1] for qh, hh in heads]
    bc = [bcol[qh][:, hh:hh + 1] for qh, hh in heads]
    decay = [jnp.exp(jnp.where(tril, gc[i] - gcum_row[qh][hh:hh + 1, :], NEG)) for i, (qh, hh) in enumerate(heads)]
    mm_ = [gram[qh] * bc[i] * decay[i] for i, (qh, hh) in enumerate(heads)]
    tinv = [eye for _ in heads]
    for lvl, msk in enumerate(lvl_masks):
        ml = [jnp.where(msk, m, 0.0) for m in mm_]
        if lvl == 0:
            tinv = [t - m for t, m in zip(tinv, ml)]
        else:
            y = [_dot(m, t) for t, m in zip(tinv, ml)]
            tinv = [t - _dot(t, yy) for t, yy in zip(tinv, y)]
    eg = [jnp.exp(g_) for g_ in gc]
    sol = [_dot(tinv[i], jnp.concatenate([v[:, (2 * qh + hh) * dh:(2 * qh + hh + 1) * dh] * bc[i],
                                          kq[qh] * (bc[i] * eg[i])], axis=1))
           for i, (qh, hh) in enumerate(heads)]
    asol = [_dot(qk[qh] * decay[i], sol[i]) for i, (qh, hh) in enumerate(heads)]
    o2 = [a[:, :dh] for a in asol]
    o1 = [qq[qh] * eg[i] - asol[i][:, dh:] for i, (qh, hh) in enumerate(heads)]
    glast = [[g_[(c + 1) * CHUNK - 1:(c + 1) * CHUNK, :] for c in range(cb)] for g_ in gc]
    kd = [kq[qh] * jnp.exp(jnp.concatenate([jnp.broadcast_to(gl, (CHUNK, 1)) for gl in glast[i]], axis=0) - gc[i])
          for i, (qh, hh) in enumerate(heads)]
    qw = [[_dot_tn(kd[i][c * CHUNK:(c + 1) * CHUNK], sol[i][c * CHUNK:(c + 1) * CHUNK]) for c in range(cb)]
          for i in range(len(heads))]
    s = [s_scr[2 * qh + hh] for qh, hh in heads]
    for c in range(cb):
        sl = slice(c * CHUNK, (c + 1) * CHUNK)
        for i, (qh, hh) in enumerate(heads):
            x = _dot(jnp.concatenate([qw[i][c][:, dh:], o1[i][sl]], axis=0), s[i])
            o = x[dh:] + o2[i][sl]
            s[i] = s[i] * jnp.exp(glast[i][c]) - x[:dh] + qw[i][c][:, :dh]
            col = slice((2 * qh + hh) * dh, (2 * qh + hh + 1) * dh)
            zc = z_ref[sl, col]
            o_ref[sl, col] = (_rms(o, nw) * (zc * _sigmoid(zc))).astype(o_ref.dtype)
    for i, (qh, hh) in enumerate(heads):
        s_scr[2 * qh + hh] = s[i]

    @pl.when(n == n_blocks - 1)
    def _():
        so_ref[...] = s_scr[...]


def _gdn_delta(qkv3, proj3, gcol, bcol, grow, s0, norm_w, out_dtype, cb, hpb, t_valid):
    b, t, _ = qkv3.shape
    r = cb * CHUNK
    assert t % r == 0
    nb = t // r
    dh = GDN_HEAD_DIM
    qw_, vw_ = hpb * dh, 2 * hpb * dh
    assert GDN_QK_HEADS % hpb == 0
    koff = GDN_QK_DIM // qw_
    voff = 2 * GDN_QK_DIM // vw_
    zoff = GDN_CONV_DIM // vw_
    return pl.pallas_call(
        functools.partial(_gdn_delta_kernel, cb=cb, hpb=hpb, t_valid=t_valid, n_blocks=nb),
        grid=(b, GDN_QK_HEADS // hpb, nb),
        in_specs=[pl.BlockSpec((None, r, qw_), lambda bb, h, n: (bb, n, h)),
                  pl.BlockSpec((None, r, qw_), lambda bb, h, n: (bb, n, koff + h)),
                  pl.BlockSpec((None, r, vw_), lambda bb, h, n: (bb, n, voff + h)),
                  pl.BlockSpec((None, r, vw_), lambda bb, h, n: (bb, n, zoff + h)),
                  pl.BlockSpec((None, hpb, r, 2), lambda bb, h, n: (bb, h, n, 0)),
                  pl.BlockSpec((None, hpb, r, 2), lambda bb, h, n: (bb, h, n, 0)),
                  pl.BlockSpec((None, hpb, 2, r), lambda bb, h, n: (bb, h, 0, n)),
                  pl.BlockSpec((None, 2 * hpb, dh, dh), lambda bb, h, n: (bb, h, 0, 0)),
                  pl.BlockSpec((1, dh), lambda bb, h, n: (0, 0))],
        out_specs=[pl.BlockSpec((None, r, vw_), lambda bb, h, n: (bb, n, h)),
                   pl.BlockSpec((None, 2 * hpb, dh, dh), lambda bb, h, n: (bb, h, 0, 0))],
        out_shape=[jax.ShapeDtypeStruct((b, t, GDN_V_DIM), out_dtype),
                   jax.ShapeDtypeStruct(s0.shape, F32)],
        scratch_shapes=[pltpu.VMEM((2 * hpb, dh, dh), F32)],
        compiler_params=_params("parallel", "parallel", "arbitrary"), name="gdn_delta")(
            qkv3, qkv3, qkv3, proj3, gcol, bcol, grow, s0, norm_w)


def _gla_kernel(q_ref, k_ref, v_ref, gate_ref, gk_ref, bgk_ref, s0_ref, nw_ref,
                o_ref, so_ref, s_scr, *, cb, t_valid, n_blocks):
    n = pl.program_id(2)
    r = cb * CHUNK
    dk, dv = GLA_DK, GLA_DV

    @pl.when(n == 0)
    def _():
        s_scr[...] = s0_ref[...]

    xg = gk_ref[...] + bgk_ref[...]
    g = (jnp.minimum(xg, 0.0) - jnp.log(1.0 + jnp.exp(-jnp.abs(xg)))) * (1.0 / GLA_GATE_NORMALIZER)
    q = q_ref[...] * GLA_DK ** -0.5
    k = k_ref[...]
    v = v_ref[...]
    if t_valid is not None:
        okc = (n * r + lax.broadcasted_iota(jnp.int32, (r, 1), 0)) < t_valid
        g = jnp.where(okc, g, 0.0)
        k = jnp.where(okc, k, 0.0)
        v = jnp.where(okc, v, 0.0)

    gparts = _split3(g)
    bcum = sum(_dot(_chunk_tril(r), p) for p in gparts)
    ones = jnp.ones((CHUNK, LANES), BF16)
    row = lax.broadcasted_iota(jnp.int32, (CHUNK, 1), 0)
    jrow = lax.broadcasted_iota(jnp.int32, (SUB, 1), 0)
    lane = lax.broadcasted_iota(jnp.int32, (SUB, CHUNK), 1)
    nw = nw_ref[...]
    s = s_scr[...]
    for c in range(cb):
        sl = slice(c * CHUNK, (c + 1) * CHUNK)
        qc, kc, vc, bc = q[sl], k[sl], v[sl], bcum[sl]
        blast = bc[CHUNK - 1:CHUNK, :]
        attn_t = jnp.zeros((CHUNK, CHUNK), F32)
        for bi in range(1, CHUNK // SUB):
            bref = bc[bi * SUB:bi * SUB + 1, :]
            k_i = jnp.where(row < bi * SUB, kc * jnp.exp(jnp.minimum(bref - bc, 0.0)), 0.0)
            q_i = jnp.where((row >= bi * SUB) & (row < (bi + 1) * SUB),
                            qc * jnp.exp(jnp.minimum(bc - bref, 0.0)), 0.0)
            attn_t = attn_t + _dot_nt(k_i, q_i)
        diag = []
        for bi in range(CHUNK // SUB):
            sb = slice(bi * SUB, (bi + 1) * SUB)
            qb, kb, bb = qc[sb], kc[sb], bc[sb]
            d = jnp.zeros((SUB, CHUNK), F32)
            for il in range(SUB):
                e = jnp.exp(jnp.minimum(bb[il:il + 1, :] - bb, 0.0))
                col = jnp.sum(kb * e * qb[il:il + 1, :], axis=-1, keepdims=True)
                col = jnp.where(jrow <= il, col, 0.0)
                d = jnp.where(lane == bi * SUB + il, col, d)
            diag.append(d)
        attn_t = attn_t + jnp.concatenate(diag, axis=0)
        o = _dot(qc * jnp.exp(bc), s) + _dot_tn(attn_t, vc)
        bl_col = sum(_dot_tn(p[sl], ones) for p in gparts)
        decay_col = jnp.concatenate([jnp.exp(bl_col)] * (dv // LANES), axis=1)
        s = s * decay_col + _dot_tn(kc * jnp.exp(blast - bc), vc)
        gt = gate_ref[sl, :]
        o_ref[sl, :] = (_rms(o, nw) * (gt * _sigmoid(gt))).astype(o_ref.dtype)
    s_scr[...] = s

    @pl.when(n == n_blocks - 1)
    def _():
        so_ref[...] = s_scr[...]


def _gla_chunks(qkvg3, gk3, b_gk, s0, norm_w, out_dtype, cb, t_valid):
    b, t, _ = qkvg3.shape
    r = cb * CHUNK
    assert t % r == 0
    nb = t // r
    dk, dv = GLA_DK, GLA_DV
    koff = GLA_HEADS
    voff = 2 * GLA_HEADS * dk // dv
    goff = voff + GLA_HEADS
    return pl.pallas_call(
        functools.partial(_gla_kernel, cb=cb, t_valid=t_valid, n_blocks=nb),
        grid=(b, GLA_HEADS, nb),
        in_specs=[pl.BlockSpec((None, r, dk), lambda bb, h, n: (bb, n, h)),
                  pl.BlockSpec((None, r, dk), lambda bb, h, n: (bb, n, koff + h)),
                  pl.BlockSpec((None, r, dv), lambda bb, h, n: (bb, n, voff + h)),
                  pl.BlockSpec((None, r, dv), lambda bb, h, n: (bb, n, goff + h)),
                  pl.BlockSpec((None, r, dk), lambda bb, h, n: (bb, n, h)),
                  pl.BlockSpec((1, dk), lambda bb, h, n: (0, h)),
                  pl.BlockSpec((None, None, dk, dv), lambda bb, h, n: (bb, h, 0, 0)),
                  pl.BlockSpec((1, dv), lambda bb, h, n: (0, 0))],
        out_specs=[pl.BlockSpec((None, r, dv), lambda bb, h, n: (bb, n, h)),
                   pl.BlockSpec((None, None, dk, dv), lambda bb, h, n: (bb, h, 0, 0))],
        out_shape=[jax.ShapeDtypeStruct((b, t, GLA_HEADS * dv), out_dtype),
                   jax.ShapeDtypeStruct(s0.shape, F32)],
        scratch_shapes=[pltpu.VMEM((dk, dv), F32)],
        compiler_params=_params("parallel", "parallel", "arbitrary"), name="gla_chunks")(
            qkvg3, qkvg3, qkvg3, qkvg3, gk3, b_gk, s0, norm_w)


def _t5_bucket(n):
    n = np.asarray(n)
    max_exact = N_BUCKETS // 2
    nf = np.maximum(n, max_exact).astype(np.float32)
    large = max_exact + (np.log(nf / max_exact) / math.log(MAX_DISTANCE / max_exact)
                         * (N_BUCKETS - max_exact)).astype(np.int32)
    return np.where(n < max_exact, n, np.minimum(large, N_BUCKETS - 1)).astype(np.int32)


def _lambda(lam_ref, lam_init):
    lf = lam_ref[...]
    s1 = jnp.sum(lf[0:1] * lf[1:2], axis=-1, keepdims=True)
    s2 = jnp.sum(lf[2:3] * lf[3:4], axis=-1, keepdims=True)
    return jnp.exp(s1) - jnp.exp(s2) + lam_init


def _flash_kernel(q_ref, k_ref, v_ref, bd_ref, bs_ref, rb_ref, lam_ref, sub_ref, o_ref,
                  m_scr, l_scr, acc_scr, bias_scr, *, lam_init):
    h = pl.program_id(1)
    qi = pl.program_id(2)
    ki = pl.program_id(3)
    tb = q_ref.shape[0]
    dh = DIFF_HEAD_DIM

    @pl.when((qi == 0) & (ki == 0))
    def _():
        ri = lax.broadcasted_iota(jnp.int32, (tb, tb), 0)
        ci = lax.broadcasted_iota(jnp.int32, (tb, tb), 1)
        for mm in range(2):
            bd = jnp.zeros((tb, tb), F32)
            bs = jnp.zeros((tb, tb), F32)
            for bk in range(N_BUCKETS):
                val = rb_ref[bk, 2 * h + mm]
                bd = jnp.where(bd_ref[...] == bk, val, bd)
                bs = jnp.where(bs_ref[...] == bk, val, bs)
            bias_scr[mm, 0] = jnp.where(ci > ri, NEG, bd)
            bias_scr[mm, 1] = bs

    @pl.when(ki == 0)
    def _():
        m_scr[...] = jnp.full_like(m_scr, NEG)
        l_scr[...] = jnp.zeros_like(l_scr)
        acc_scr[...] = jnp.zeros_like(acc_scr)

    @pl.when(ki <= qi)
    def _():
        v = v_ref[...].astype(BF16)
        for mm in range(2):
            s = _dot_nt(q_ref[:, mm * dh:(mm + 1) * dh], k_ref[:, mm * dh:(mm + 1) * dh]) * dh ** -0.5
            far = rb_ref[N_BUCKETS - 1, 2 * h + mm]
            s = s + jnp.where(ki == qi, bias_scr[mm, 0], jnp.where(ki == qi - 1, bias_scr[mm, 1], far))
            m_prev = m_scr[mm]
            m_new = jnp.maximum(m_prev, jnp.max(s, axis=-1, keepdims=True))
            alpha = jnp.exp(m_prev - m_new)
            p = jnp.exp(s - m_new)
            l_scr[mm] = alpha * l_scr[mm] + jnp.sum(p, axis=-1, keepdims=True)
            acc_scr[mm] = alpha * acc_scr[mm] + _dot(p, v)
            m_scr[mm] = m_new

    @pl.when(ki == qi)
    def _():
        lam = _lambda(lam_ref, lam_init)
        out = acc_scr[0] / l_scr[0] - lam * (acc_scr[1] / l_scr[1])
        o_ref[...] = (_rms(out, sub_ref[...]) * (1.0 - lam_init)).astype(o_ref.dtype)


def _flash_diff(qkv3, rel_bias, lambdas, subln, lam_init, out_dtype, tb):
    b, t, _ = qkv3.shape
    tb = min(tb, t)
    assert t % tb == 0 and tb >= MAX_DISTANCE
    nq = t // tb
    dd = 2 * DIFF_HEAD_DIM
    i = np.arange(tb)[:, None]
    j = np.arange(tb)[None, :]
    bkt_diag = jnp.asarray(_t5_bucket(np.maximum(i - j, 0)))
    bkt_sub = jnp.asarray(_t5_bucket(tb + i - j))
    koff = D_MODEL // dd
    return pl.pallas_call(
        functools.partial(_flash_kernel, lam_init=lam_init),
        grid=(b, DIFF_HEADS, nq, nq),
        in_specs=[pl.BlockSpec((None, tb, dd), lambda bb, h, qi, ki: (bb, qi, h)),
                  pl.BlockSpec((None, tb, dd), lambda bb, h, qi, ki: (bb, jnp.minimum(ki, qi), koff + h)),
                  pl.BlockSpec((None, tb, dd), lambda bb, h, qi, ki: (bb, jnp.minimum(ki, qi), 2 * koff + h)),
                  pl.BlockSpec((tb, tb), lambda bb, h, qi, ki: (0, 0)),
                  pl.BlockSpec((tb, tb), lambda bb, h, qi, ki: (0, 0)),
                  pl.BlockSpec(memory_space=pltpu.SMEM),
                  pl.BlockSpec((4, DIFF_HEAD_DIM), lambda bb, h, qi, ki: (0, 0)),
                  pl.BlockSpec((1, dd), lambda bb, h, qi, ki: (0, 0))],
        out_specs=pl.BlockSpec((None, tb, dd), lambda bb, h, qi, ki: (bb, qi, h)),
        out_shape=jax.ShapeDtypeStruct((b, t, D_MODEL), out_dtype),
        scratch_shapes=[pltpu.VMEM((2, tb, 1), F32), pltpu.VMEM((2, tb, 1), F32),
                        pltpu.VMEM((2, tb, dd), F32), pltpu.VMEM((2, 2, tb, tb), F32)],
        compiler_params=_params("parallel", "parallel", "arbitrary", "arbitrary"), name="flash_diff")(
            qkv3, qkv3, qkv3, bkt_diag, bkt_sub, rel_bias, lambdas, subln)


def _decode_kernel(pt_ref, q_ref, kp_ref, vp_ref, kn_ref, vn_ref, bkt_ref, tbl_ref, lam_ref, sub_ref,
                   o_ref, m_scr, l_scr, acc_scr, bias_scr, *, n_pages, lam_init):
    p = pl.program_id(1)
    nh = DIFF_HEADS
    scale = DIFF_HEAD_DIM ** -0.5

    @pl.when(p == 0)
    def _():
        m_scr[...] = jnp.full_like(m_scr, NEG)
        l_scr[...] = jnp.zeros_like(l_scr)
        acc_scr[...] = jnp.zeros_like(acc_scr)
        bkt = bkt_ref[...]
        for par in range(2):
            bias = jnp.zeros((PAGE_SIZE, nh, 1), F32)
            for bk in range(N_BUCKETS):
                bias = jnp.where(bkt == bk, tbl_ref[bk, par][None], bias)
            bias_scr[par] = bias

    vpage = vp_ref[...]
    for par in range(2):
        qv = q_ref[par]
        kk = kp_ref[:, pl.ds(par, nh, stride=2), :]
        s = jnp.sum(kk * qv[None], axis=-1, keepdims=True) * scale
        s = s + jnp.where(p == n_pages - 1, bias_scr[par], tbl_ref[N_BUCKETS - 1, par][None])
        m_prev = m_scr[par]
        m_new = jnp.maximum(m_prev, jnp.max(s, axis=0))
        alpha = jnp.exp(m_prev - m_new)
        pe = jnp.exp(s - m_new[None])
        l_scr[par] = alpha * l_scr[par] + jnp.sum(pe, axis=0)
        acc_scr[par] = alpha * acc_scr[par] + jnp.sum(pe * vpage, axis=0)
        m_scr[par] = m_new

    @pl.when(p == n_pages - 1)
    def _():
        outs = []
        for par in range(2):
            qv = q_ref[par]
            s = jnp.sum(kn_ref[par] * qv, axis=-1, keepdims=True) * scale + tbl_ref[0, par]
            m_prev = m_scr[par]
            m_new = jnp.maximum(m_prev, s)
            alpha = jnp.exp(m_prev - m_new)
            pe = jnp.exp(s - m_new)
            l = alpha * l_scr[par] + pe
            acc = alpha * acc_scr[par] + pe * vn_ref[...]
            outs.append(acc / l)
        out = outs[0] - _lambda(lam_ref, lam_init) * outs[1]
        o_ref[...] = _rms(out, sub_ref[...]) * (1.0 - lam_init)


def _decode_diff(q_eo, kn_eo, v_new, cache_k, cache_v, layer, page_table, rel_bias, lambdas, subln, lam_init):
    b, n_pages = page_table.shape
    nh, dh = DIFF_HEADS, DIFF_HEAD_DIM
    past = n_pages * PAGE_SIZE
    bkt_last = jnp.asarray(_t5_bucket(PAGE_SIZE - np.arange(PAGE_SIZE)).reshape(PAGE_SIZE, 1, 1))
    assert past - (n_pages - 1) * PAGE_SIZE >= PAGE_SIZE and PAGE_SIZE >= MAX_DISTANCE
    tbl = jnp.transpose(rel_bias.reshape(N_BUCKETS, nh, 2), (0, 2, 1))[..., None]
    grid_spec = pltpu.PrefetchScalarGridSpec(
        num_scalar_prefetch=1, grid=(b, n_pages),
        in_specs=[pl.BlockSpec((None, 2, nh, dh), lambda bb, p, pt: (bb, 0, 0, 0)),
                  pl.BlockSpec((None, None, PAGE_SIZE, 2 * nh, dh), lambda bb, p, pt: (layer, pt[bb, p], 0, 0, 0)),
                  pl.BlockSpec((None, None, PAGE_SIZE, nh, 2 * dh), lambda bb, p, pt: (layer, pt[bb, p], 0, 0, 0)),
                  pl.BlockSpec((None, 2, nh, dh), lambda bb, p, pt: (bb, 0, 0, 0)),
                  pl.BlockSpec((None, nh, 2 * dh), lambda bb, p, pt: (bb, 0, 0)),
                  pl.BlockSpec((PAGE_SIZE, 1, 1), lambda bb, p, pt: (0, 0, 0)),
                  pl.BlockSpec((N_BUCKETS, 2, nh, 1), lambda bb, p, pt: (0, 0, 0, 0)),
                  pl.BlockSpec((4, dh), lambda bb, p, pt: (0, 0)),
                  pl.BlockSpec((1, 2 * dh), lambda bb, p, pt: (0, 0))],
        out_specs=pl.BlockSpec((None, nh, 2 * dh), lambda bb, p, pt: (bb, 0, 0)),
        scratch_shapes=[pltpu.VMEM((2, nh, 1), F32), pltpu.VMEM((2, nh, 1), F32),
                        pltpu.VMEM((2, nh, 2 * dh), F32), pltpu.VMEM((2, PAGE_SIZE, nh, 1), F32)])
    return pl.pallas_call(
        functools.partial(_decode_kernel, n_pages=n_pages, lam_init=lam_init),
        grid_spec=grid_spec,
        out_shape=jax.ShapeDtypeStruct((b, nh, 2 * dh), F32),
        compiler_params=_params("parallel", "arbitrary"), name="decode_diff")(
            page_table, q_eo, cache_k, cache_v, kn_eo, v_new, bkt_last, tbl, lambdas, subln)


class _Tiles:
    def __init__(self, prompt):
        self.prompt = prompt
        self.act_dtype = BF16 if prompt else F32
        self.tm = 1024 if prompt else SUBLANES
        self.tn = 512 if prompt else 1024
        self.tn_ff = 512
        self.tm_res = 1024 if prompt else SUBLANES
        self.tk_res = 512
        self.tt_prep = 256 if prompt else SUBLANES
        self.tc_prep = 1024
        self.cb = 4 if prompt else 1
        self.gdn_hpb = 2
        self.tb_attn = 512


def _group_trunk(x3, tiles, gdn_s0, gdn_c0, gla_s0, cache_k, cache_v, page_table, p):
    b, t, d = x3.shape
    m = b * t
    x = x3.reshape(m, d)
    if not tiles.prompt:
        assert t == 1 and m % SUBLANES == 0
    t_pad = t if tiles.prompt else CHUNK
    t_valid = None if tiles.prompt else t
    nw = lambda i, jn: p['norm_w'][i, jn].reshape(1, d)
    adt = tiles.act_dtype

    def pad_time(a3, to):
        return a3 if a3.shape[1] == to else jnp.pad(a3, ((0, 0), (0, to - a3.shape[1]), (0, 0)))

    gdn_s, gdn_c, gla_s, att_k, att_v = [], [], [], [], []
    h = _rmsnorm(x, nw(0, 0), adt, tiles.tm_res)
    for i in range(DEPTH):
        j = i // N_MIXERS
        kind = i % N_MIXERS
        if kind == 0:
            proj = _matmul(h, p['gdn_w_in'], j, F32, tiles.tm, tiles.tn)
            beta, g = _gdn_gates(h, p['gdn_w_ba'], j, p['gdn_a_log'][j].reshape(1, -1),
                                 p['gdn_dt_bias'][j].reshape(1, -1), tiles.tm)
            proj3 = proj.reshape(b, t, -1)
            buf0 = gdn_c0[j]
            nbuf = GDN_CONV_W - 1
            if t >= nbuf:
                new_buf = proj3[:, t - nbuf:, :GDN_CONV_DIM]
            else:
                new_buf = jnp.concatenate([buf0[:, t:], proj3[..., :GDN_CONV_DIM]], axis=1)
            buf8 = jnp.pad(buf0, ((0, 0), (SUBLANES - (GDN_CONV_W - 1), 0), (0, 0)))
            t8 = max(t, SUBLANES)
            proj3p = pad_time(proj3, t8)
            qkv3 = pad_time(_gdn_prep(proj3p, buf8, p['gdn_conv_w'][j], tiles.tt_prep, tiles.tc_prep), t_pad)
            proj3p = pad_time(proj3p, t_pad)

            def heads(a):
                a = pad_time(a.reshape(b, t, GDN_QK_HEADS * 2), t_pad)
                return a.reshape(b, t_pad, GDN_QK_HEADS, 2).transpose(0, 2, 1, 3)
            gcol, bcol = heads(g), heads(beta)
            o3, s_new = _gdn_delta(qkv3, proj3p, gcol, bcol, gcol.transpose(0, 1, 3, 2), gdn_s0[j],
                                   p['gdn_norm_w'][j].reshape(1, -1), adt, tiles.cb, tiles.gdn_hpb, t_valid)
            mix_in, w_out = o3[:, :t].reshape(m, -1), p['gdn_w_out']
            gdn_s.append(s_new)
            gdn_c.append(new_buf)
        elif kind == 1:
            qkvg = _matmul(h, p['gla_w_qkvg'], j, F32, tiles.tm, tiles.tn)
            low = _matmul(h, p['gla_w_gk1'], j, F32, tiles.tm, tiles.tn)
            gk = _matmul(low, p['gla_w_gk2'], j, F32, tiles.tm, tiles.tn)
            o3, s_new = _gla_chunks(pad_time(qkvg.reshape(b, t, -1), t_pad), pad_time(gk.reshape(b, t, -1), t_pad),
                                    p['gla_b_gk'][j].reshape(1, -1), gla_s0[j],
                                    p['gla_norm_w'][j].reshape(1, -1), adt, tiles.cb, t_valid)
            mix_in, w_out = o3[:, :t].reshape(m, -1), p['gla_w_out']
            gla_s.append(s_new)
        else:
            lam_init = 0.8 - 0.6 * math.exp(-0.3 * i)
            qkv = _matmul(h, p['diff_w_qkv'], j, F32, tiles.tm, tiles.tn)
            nh, dh = DIFF_HEADS, DIFF_HEAD_DIM
            k_new = qkv[:, D_MODEL:2 * D_MODEL].reshape(b, t, 2 * nh, dh)
            v_new = qkv[:, 2 * D_MODEL:].reshape(b, t, nh, 2 * dh)
            subln = p['diff_subln'][j].reshape(1, -1)
            if tiles.prompt:
                o3 = _flash_diff(qkv.reshape(b, t, -1), p['rel_bias'], p['diff_lambda'][j], subln,
                                 lam_init, adt, tiles.tb_attn)
                mix_in = o3.reshape(m, -1)
            else:
                eo = lambda a: a.reshape(b, nh, 2, dh).transpose(0, 2, 1, 3)
                o = _decode_diff(eo(qkv[:, :D_MODEL]), eo(qkv[:, D_MODEL:2 * D_MODEL]), v_new.reshape(b, nh, 2 * dh),
                                 cache_k, cache_v, j, page_table, p['rel_bias'], p['diff_lambda'][j],
                                 subln, lam_init)
                mix_in = o.reshape(m, -1)
            w_out = p['diff_w_out']
            att_k.append(k_new)
            att_v.append(v_new)
        x, h = _matmul_residual(mix_in, w_out, j, x, nw(i, 1), nw(i, 2), adt, tiles.tm_res, tiles.tk_res)
        act = _swiglu_up(h, p['ffn_w_up'], i, adt, tiles.tm, tiles.tn_ff)
        nxt = nw(i + 1, 0) if i + 1 < DEPTH else None
        x, h = _matmul_residual(act, p['ffn_w_down'], i, x, nw(i, 3), nxt, adt, tiles.tm_res, tiles.tk_res)
    return (x.reshape(b, t, d), jnp.stack(gdn_s), jnp.stack(gdn_c), jnp.stack(gla_s),
            jnp.stack(att_k), jnp.stack(att_v))


def kernel(x_prompt, x_sample, state_gdn, state_gdn_conv, state_gla, cache_k, cache_v, page_table,
           norm_w, ffn_w_up, ffn_w_down, rel_bias,
           gdn_w_in, gdn_w_ba, gdn_conv_w, gdn_a_log, gdn_dt_bias, gdn_norm_w, gdn_w_out,
           gla_w_qkvg, gla_w_gk1, gla_w_gk2, gla_b_gk, gla_norm_w, gla_w_out,
           diff_w_qkv, diff_lambda, diff_subln, diff_w_out):
    p = dict(norm_w=norm_w, ffn_w_up=ffn_w_up, ffn_w_down=ffn_w_down, rel_bias=rel_bias,
             gdn_w_in=gdn_w_in, gdn_w_ba=gdn_w_ba, gdn_conv_w=gdn_conv_w, gdn_a_log=gdn_a_log,
             gdn_dt_bias=gdn_dt_bias, gdn_norm_w=gdn_norm_w, gdn_w_out=gdn_w_out,
             gla_w_qkvg=gla_w_qkvg, gla_w_gk1=gla_w_gk1, gla_w_gk2=gla_w_gk2, gla_b_gk=gla_b_gk,
             gla_norm_w=gla_norm_w, gla_w_out=gla_w_out,
             diff_w_qkv=diff_w_qkv, diff_lambda=diff_lambda, diff_subln=diff_subln, diff_w_out=diff_w_out)
    bp = x_prompt.shape[0]
    n_gdn, n_gla = state_gdn.shape[0], state_gla.shape[0]
    zeros_gdn = jnp.zeros((n_gdn, bp) + state_gdn.shape[2:], F32)
    zeros_conv = jnp.zeros((n_gdn, bp) + state_gdn_conv.shape[2:], F32)
    zeros_gla = jnp.zeros((n_gla, bp) + state_gla.shape[2:], F32)
    outs_p = _group_trunk(x_prompt, _Tiles(True), zeros_gdn, zeros_conv, zeros_gla, None, None, None, p)
    outs_s = _group_trunk(x_sample, _Tiles(False), state_gdn, state_gdn_conv, state_gla,
                          cache_k, cache_v, page_table, p)
    return (outs_p[0], outs_s[0]) + outs_p[1:] + outs_s[1:]
```

```python
import functools
import math

import numpy as np
import jax
import jax.numpy as jnp
from jax import lax
from jax.experimental import pallas as pl
from jax.experimental.pallas import tpu as pltpu

F32 = jnp.float32
BF16 = jnp.bfloat16

D_MODEL = 2048
DEPTH = 4
PAGE_SIZE = 128
N_MIXERS = 3
GDN_HEAD_DIM = 128
GDN_QK_HEADS = D_MODEL // 128
GDN_V_HEADS = 2 * GDN_QK_HEADS
GDN_QK_DIM = GDN_QK_HEADS * GDN_HEAD_DIM
GDN_V_DIM = GDN_V_HEADS * GDN_HEAD_DIM
GDN_CONV_DIM = 2 * GDN_QK_DIM + GDN_V_DIM
GDN_CONV_W = 4
GLA_HEADS = 4
GLA_DK = D_MODEL // 2 // GLA_HEADS
GLA_DV = D_MODEL // GLA_HEADS
GLA_GATE_NORMALIZER = 16.0
DIFF_HEAD_DIM = 128
DIFF_HEADS = D_MODEL // (2 * DIFF_HEAD_DIM)
N_BUCKETS = 32
MAX_DISTANCE = 128
D_FF = ((8 * D_MODEL + 3 * 256 - 1) // (3 * 256)) * 256
NORM_EPS = 1e-6

LANES = 128
SUBLANES = 8
CHUNK = 64
SUB = 16
VMEM_LIMIT_BYTES = 48 * 1024 * 1024
VMEM_LIMIT_BIG_BYTES = 56 * 1024 * 1024
NEG = -1e30
LOG2E = math.log2(math.e)


def _params(*sem, vmem=VMEM_LIMIT_BYTES):
    return pltpu.CompilerParams(dimension_semantics=sem, vmem_limit_bytes=vmem)


def _dot(a, b):
    return jnp.dot(a.astype(BF16), b.astype(BF16), preferred_element_type=F32)


def _dot_nt(a, b):
    return lax.dot_general(a.astype(BF16), b.astype(BF16), (((1,), (1,)), ((), ())),
                           preferred_element_type=F32)


def _dot_tn(a, b):
    return lax.dot_general(a.astype(BF16), b.astype(BF16), (((0,), (0,)), ((), ())),
                           preferred_element_type=F32)


def _split3(x):
    hi = x.astype(BF16)
    r = x - hi.astype(F32)
    mid = r.astype(BF16)
    lo = (r - mid.astype(F32)).astype(BF16)
    return hi, mid, lo


def _sigmoid(x):
    return 1.0 / (1.0 + jnp.exp(-x))


def _softplus(x):
    return jnp.maximum(x, 0.0) + jnp.log(1.0 + jnp.exp(-jnp.abs(x)))


def _rms(x, w):
    return x * lax.rsqrt(jnp.mean(x * x, axis=-1, keepdims=True) + NORM_EPS) * w


def _chunk_tril(r):
    i = lax.broadcasted_iota(jnp.int32, (r, r), 0)
    j = lax.broadcasted_iota(jnp.int32, (r, r), 1)
    return jnp.where((j <= i) & ((i // CHUNK) == (j // CHUNK)), 1.0, 0.0).astype(BF16)


def _rmsnorm_kernel(x_ref, w_ref, o_ref):
    o_ref[...] = _rms(x_ref[...], w_ref[...]).astype(o_ref.dtype)


def _rmsnorm(x, w, out_dtype, tm):
    m, d = x.shape
    tm = min(tm, m)
    return pl.pallas_call(
        _rmsnorm_kernel, grid=(m // tm,),
        in_specs=[pl.BlockSpec((tm, d), lambda i: (i, 0)), pl.BlockSpec((1, d), lambda i: (0, 0))],
        out_specs=pl.BlockSpec((tm, d), lambda i: (i, 0)),
        out_shape=jax.ShapeDtypeStruct((m, d), out_dtype),
        compiler_params=_params("parallel"), name="rmsnorm")(x, w)


def _mm_kernel(a_ref, w_ref, o_ref):
    o_ref[...] = _dot(a_ref[...], w_ref[...]).astype(o_ref.dtype)


def _matmul(a, w, layer, out_dtype, tm, tn):
    m, k = a.shape
    n = w.shape[2]
    tm, tn = min(tm, m), min(tn, n)
    assert m % tm == 0 and n % tn == 0
    return pl.pallas_call(
        _mm_kernel, grid=(m // tm, n // tn),
        in_specs=[pl.BlockSpec((tm, k), lambda i, j: (i, 0)),
                  pl.BlockSpec((None, k, tn), lambda i, j: (layer, 0, j))],
        out_specs=pl.BlockSpec((tm, tn), lambda i, j: (i, j)),
        out_shape=jax.ShapeDtypeStruct((m, n), out_dtype),
        compiler_params=_params("parallel", "parallel"), name="matmul")(a, w)


def _swiglu_kernel(a_ref, wg_ref, wu_ref, o_ref):
    a = a_ref[...].astype(BF16)
    g = _dot(a, wg_ref[...])
    u = _dot(a, wu_ref[...])
    o_ref[...] = (g * _sigmoid(g) * u).astype(o_ref.dtype)


def _swiglu_up(a, w_up, layer, out_dtype, tm, tn):
    m, k = a.shape
    f = w_up.shape[2] // 2
    tm, tn = min(tm, m), min(tn, f)
    assert m % tm == 0 and f % tn == 0
    nf = f // tn
    return pl.pallas_call(
        _swiglu_kernel, grid=(m // tm, nf),
        in_specs=[pl.BlockSpec((tm, k), lambda i, j: (i, 0)),
                  pl.BlockSpec((None, k, tn), lambda i, j: (layer, 0, j)),
                  pl.BlockSpec((None, k, tn), lambda i, j: (layer, 0, j + nf))],
        out_specs=pl.BlockSpec((tm, tn), lambda i, j: (i, j)),
        out_shape=jax.ShapeDtypeStruct((m, f), out_dtype),
        compiler_params=_params("parallel", "parallel"), name="swiglu_up")(a, w_up, w_up)


def _mm_res_kernel(a_ref, w_ref, x_ref, nw1_ref, nw2_ref, xo_ref, *h_ref, nk, emit_h):
    kk = pl.program_id(1)
    part = _dot(a_ref[...], w_ref[...])

    @pl.when(kk == 0)
    def _():
        xo_ref[...] = part

    @pl.when(kk > 0)
    def _():
        xo_ref[...] += part

    @pl.when(kk == nk - 1)
    def _():
        xn = x_ref[...] + _rms(xo_ref[...], nw1_ref[...])
        xo_ref[...] = xn
        if emit_h:
            h_ref[0][...] = _rms(xn, nw2_ref[...]).astype(h_ref[0].dtype)


def _matmul_residual(a, w, layer, x, nw_post, nw_next, h_dtype, tm, tk):
    m, k = a.shape
    n = w.shape[2]
    tm, tk = min(tm, m), min(tk, k)
    assert m % tm == 0 and k % tk == 0
    nk = k // tk
    emit_h = nw_next is not None
    if not emit_h:
        nw_next = nw_post
    out_shape = [jax.ShapeDtypeStruct((m, n), F32)]
    out_specs = [pl.BlockSpec((tm, n), lambda i, kk: (i, 0))]
    if emit_h:
        out_shape.append(jax.ShapeDtypeStruct((m, n), h_dtype))
        out_specs.append(pl.BlockSpec((tm, n), lambda i, kk: (i, 0)))
    res = pl.pallas_call(
        functools.partial(_mm_res_kernel, nk=nk, emit_h=emit_h), grid=(m // tm, nk),
        in_specs=[pl.BlockSpec((tm, tk), lambda i, kk: (i, kk)),
                  pl.BlockSpec((None, tk, n), lambda i, kk: (layer, kk, 0)),
                  pl.BlockSpec((tm, n), lambda i, kk: (i, 0), pipeline_mode=pl.Buffered(1)),
                  pl.BlockSpec((1, n), lambda i, kk: (0, 0)),
                  pl.BlockSpec((1, n), lambda i, kk: (0, 0))],
        out_specs=out_specs, out_shape=out_shape,
        compiler_params=_params("parallel", "arbitrary", vmem=VMEM_LIMIT_BIG_BYTES), name="matmul_residual")(
            a, w, x, nw_post, nw_next)
    return (res[0], res[1]) if emit_h else (res[0], None)


def _gdn_gates_kernel(a_ref, w_ref, alog_ref, dtb_ref, beta_ref, g_ref):
    ba = _dot(a_ref[...], w_ref[...])
    nh = beta_ref.shape[-1]
    beta_ref[...] = _sigmoid(ba[:, :nh])
    g_ref[...] = -jnp.exp(alog_ref[...]) * _softplus(ba[:, nh:] + dtb_ref[...])


def _gdn_gates(a, w_ba, layer, a_log, dt_bias, tm):
    m, k = a.shape
    nh = GDN_V_HEADS
    tm = min(tm, m)
    return pl.pallas_call(
        _gdn_gates_kernel, grid=(m // tm,),
        in_specs=[pl.BlockSpec((tm, k), lambda i: (i, 0)),
                  pl.BlockSpec((None, k, 2 * nh), lambda i: (layer, 0, 0)),
                  pl.BlockSpec((1, nh), lambda i: (0, 0)),
                  pl.BlockSpec((1, nh), lambda i: (0, 0))],
        out_specs=[pl.BlockSpec((tm, nh), lambda i: (i, 0))] * 2,
        out_shape=[jax.ShapeDtypeStruct((m, nh), F32)] * 2,
        compiler_params=_params("parallel"), name="gdn_gates")(a, w_ba, a_log, dt_bias)


def _gdn_prep_kernel(x_ref, prev_ref, buf_ref, cw_ref, o_ref, *, n_qk_tiles):
    i = pl.program_id(1)
    j = pl.program_id(2)
    x = x_ref[...]
    tt, tc = x.shape
    halo = jnp.where(i == 0, buf_ref[...], prev_ref[...])
    row8 = lax.broadcasted_iota(jnp.int32, (SUBLANES, tc), 0)
    y = x * cw_ref[GDN_CONV_W - 1:GDN_CONV_W, :]
    for s in range(1, GDN_CONV_W):
        xs = pltpu.roll(x, s, 0)
        hs = pltpu.roll(halo, s, 0)
        top = jnp.where(row8 < s, hs, xs[:SUBLANES])
        shifted = top if tt == SUBLANES else jnp.concatenate([top, xs[SUBLANES:]], axis=0)
        y = y + shifted * cw_ref[GDN_CONV_W - 1 - s:GDN_CONV_W - s, :]
    y = y * _sigmoid(y)

    @pl.when(j < 2 * n_qk_tiles)
    def _():
        scale = jnp.where(j < n_qk_tiles, GDN_HEAD_DIM ** -0.5, 1.0)
        for hh in range(tc // GDN_HEAD_DIM):
            cols = slice(hh * GDN_HEAD_DIM, (hh + 1) * GDN_HEAD_DIM)
            yh = y[:, cols]
            o_ref[:, cols] = yh * (lax.rsqrt(jnp.sum(yh * yh, axis=-1, keepdims=True) + NORM_EPS) * scale)

    @pl.when(j >= 2 * n_qk_tiles)
    def _():
        o_ref[...] = y


def _gdn_prep(proj3, buf8, conv_w, tt, tc):
    b, t, _ = proj3.shape
    tt = min(tt, t)
    assert t % tt == 0 and tt % SUBLANES == 0 and GDN_QK_DIM % tc == 0
    hb = tt // SUBLANES
    return pl.pallas_call(
        functools.partial(_gdn_prep_kernel, n_qk_tiles=GDN_QK_DIM // tc),
        grid=(b, t // tt, GDN_CONV_DIM // tc),
        in_specs=[pl.BlockSpec((None, tt, tc), lambda bb, i, j: (bb, i, j)),
                  pl.BlockSpec((None, SUBLANES, tc), lambda bb, i, j: (bb, jnp.maximum(i * hb - 1, 0), j)),
                  pl.BlockSpec((None, SUBLANES, tc), lambda bb, i, j: (bb, 0, j)),
                  pl.BlockSpec((GDN_CONV_W, tc), lambda bb, i, j: (0, j))],
        out_specs=pl.BlockSpec((None, tt, tc), lambda bb, i, j: (bb, i, j)),
        out_shape=jax.ShapeDtypeStruct((b, t, GDN_CONV_DIM), F32),
        compiler_params=_params("parallel", "parallel", "parallel"), name="gdn_prep")(
            proj3, proj3, buf8, conv_w)


def _gdn_delta_kernel(q_ref, k_ref, v_ref, z_ref, gc_ref, bc_ref, gr_ref, s0_ref, nw_ref,
                      o_ref, so_ref, s_scr, *, cb, hpb, t_valid, n_blocks):
    n = pl.program_id(2)
    r = cb * CHUNK
    dh = GDN_HEAD_DIM

    @pl.when(n == 0)
    def _():
        s_scr[...] = s0_ref[...]

    q = q_ref[...]
    k = k_ref[...]
    v = v_ref[...]
    gcol = gc_ref[...]
    bcol = bc_ref[...]
    grow = gr_ref[...]
    if t_valid is not None:
        okc = (n * r + lax.broadcasted_iota(jnp.int32, (r, 1), 0)) < t_valid
        okr = (n * r + lax.broadcasted_iota(jnp.int32, (1, r), 1)) < t_valid
        k = jnp.where(okc, k, 0.0)
        v = jnp.where(okc, v, 0.0)
        gcol = jnp.where(okc, gcol, 0.0)
        bcol = jnp.where(okc, bcol, 0.0)
        grow = jnp.where(okr, grow, 0.0)

    lm = _chunk_tril(r)
    gcum_col = [sum(_dot(lm, p) for p in _split3(gcol[qh])) for qh in range(hpb)]
    gcum_row = [sum(_dot_nt(p, lm) for p in _split3(grow[qh])) for qh in range(hpb)]

    ri = lax.broadcasted_iota(jnp.int32, (r, r), 0)
    ci = lax.broadcasted_iota(jnp.int32, (r, r), 1)
    eye = jnp.where(ri == ci, 1.0, 0.0)
    tril = ((ri // CHUNK) == (ci // CHUNK)) & (ri >= ci)
    lvl_masks = []
    size = 1
    while size < CHUNK:
        lvl_masks.append(((ri // (2 * size)) == (ci // (2 * size)))
                         & ((ri // size) % 2 == 1) & ((ci // size) % 2 == 0))
        size *= 2
    if t_valid is not None and t_valid <= 1:
        lvl_masks = []

    nw = nw_ref[...]
    heads = [(qh, hh) for qh in range(hpb) for hh in range(2)]
    kq = [k[:, qh * dh:(qh + 1) * dh] for qh in range(hpb)]
    qq = [q[:, qh * dh:(qh + 1) * dh] for qh in range(hpb)]
    gram = [_dot_nt(kq[qh], kq[qh]) for qh in range(hpb)]
    qk = [_dot_nt(qq[qh], kq[qh]) for qh in range(hpb)]
    gc = [gcum_col[qh][:, hh:hh + 1] for qh, hh in heads]
    bc = [bcol[qh][:, hh:hh + 1] for qh, hh in heads]
    decay = [jnp.exp(jnp.where(tril, gc[i] - gcum_row[qh][hh:hh + 1, :], NEG)) for i, (qh, hh) in enumerate(heads)]
    mm_ = [gram[qh] * bc[i] * decay[i] for i, (qh, hh) in enumerate(heads)]
    tinv = [eye for _ in heads]
    for lvl, msk in enumerate(lvl_masks):
        ml = [jnp.where(msk, m, 0.0) for m in mm_]
        if lvl == 0:
            tinv = [t - m for t, m in zip(tinv, ml)]
        else:
            y = [_dot(m, t) for t, m in zip(tinv, ml)]
            tinv = [t - _dot(t, yy) for t, yy in zip(tinv, y)]
    eg = [jnp.exp(g_) for g_ in gc]
    sol = [_dot(tinv[i], jnp.concatenate([v[:, (2 * qh + hh) * dh:(2 * qh + hh + 1) * dh] * bc[i],
                                          kq[qh] * (bc[i] * eg[i])], axis=1))
           for i, (qh, hh) in enumerate(heads)]
    asol = [_dot(qk[qh] * decay[i], sol[i]) for i, (qh, hh) in enumerate(heads)]
    o2 = [a[:, :dh] for a in asol]
    o1 = [qq[qh] * eg[i] - asol[i][:, dh:] for i, (qh, hh) in enumerate(heads)]
    glast = [[g_[(c + 1) * CHUNK - 1:(c + 1) * CHUNK, :] for c in range(cb)] for g_ in gc]
    kd = [kq[qh] * jnp.exp(jnp.concatenate([jnp.broadcast_to(gl, (CHUNK, 1)) for gl in glast[i]], axis=0) - gc[i])
          for i, (qh, hh) in enumerate(heads)]
    qw = [[_dot_tn(kd[i][c * CHUNK:(c + 1) * CHUNK], sol[i][c * CHUNK:(c + 1) * CHUNK]) for c in range(cb)]
          for i in range(len(heads))]
    s = [s_scr[2 * qh + hh] for qh, hh in heads]
    for c in range(cb):
        sl = slice(c * CHUNK, (c + 1) * CHUNK)
        for i, (qh, hh) in enumerate(heads):
            x = _dot(jnp.concatenate([qw[i][c][:, dh:], o1[i][sl]], axis=0), s[i])
            o = x[dh:] + o2[i][sl]
            s[i] = s[i] * jnp.exp(glast[i][c]) - x[:dh] + qw[i][c][:, :dh]
            col = slice((2 * qh + hh) * dh, (2 * qh + hh + 1) * dh)
            zc = z_ref[sl, col]
            o_ref[sl, col] = (_rms(o, nw) * (zc * _sigmoid(zc))).astype(o_ref.dtype)
    for i, (qh, hh) in enumerate(heads):
        s_scr[2 * qh + hh] = s[i]

    @pl.when(n == n_blocks - 1)
    def _():
        so_ref[...] = s_scr[...]


def _gdn_delta(qkv3, proj3, gcol, bcol, grow, s0, norm_w, out_dtype, cb, hpb, t_valid):
    b, t, _ = qkv3.shape
    r = cb * CHUNK
    assert t % r == 0
    nb = t // r
    dh = GDN_HEAD_DIM
    qw_, vw_ = hpb * dh, 2 * hpb * dh
    assert GDN_QK_HEADS % hpb == 0
    koff = GDN_QK_DIM // qw_
    voff = 2 * GDN_QK_DIM // vw_
    zoff = GDN_CONV_DIM // vw_
    return pl.pallas_call(
        functools.partial(_gdn_delta_kernel, cb=cb, hpb=hpb, t_valid=t_valid, n_blocks=nb),
        grid=(b, GDN_QK_HEADS // hpb, nb),
        in_specs=[pl.BlockSpec((None, r, qw_), lambda bb, h, n: (bb, n, h)),
                  pl.BlockSpec((None, r, qw_), lambda bb, h, n: (bb, n, koff + h)),
                  pl.BlockSpec((None, r, vw_), lambda bb, h, n: (bb, n, voff + h)),
                  pl.BlockSpec((None, r, vw_), lambda bb, h, n: (bb, n, zoff + h)),
                  pl.BlockSpec((None, hpb, r, 2), lambda bb, h, n: (bb, h, n, 0)),
                  pl.BlockSpec((None, hpb, r, 2), lambda bb, h, n: (bb, h, n, 0)),
                  pl.BlockSpec((None, hpb, 2, r), lambda bb, h, n: (bb, h, 0, n)),
                  pl.BlockSpec((None, 2 * hpb, dh, dh), lambda bb, h, n: (bb, h, 0, 0)),
                  pl.BlockSpec((1, dh), lambda bb, h, n: (0, 0))],
        out_specs=[pl.BlockSpec((None, r, vw_), lambda bb, h, n: (bb, n, h)),
                   pl.BlockSpec((None, 2 * hpb, dh, dh), lambda bb, h, n: (bb, h, 0, 0))],
        out_shape=[jax.ShapeDtypeStruct((b, t, GDN_V_DIM), out_dtype),
                   jax.ShapeDtypeStruct(s0.shape, F32)],
        scratch_shapes=[pltpu.VMEM((2 * hpb, dh, dh), F32)],
        compiler_params=_params("parallel", "parallel", "arbitrary"), name="gdn_delta")(
            qkv3, qkv3, qkv3, proj3, gcol, bcol, grow, s0, norm_w)


def _gla_kernel(q_ref, k_ref, v_ref, gate_ref, gk_ref, bgk_ref, s0_ref, nw_ref,
                o_ref, so_ref, s_scr, *, cb, t_valid, n_blocks):
    n = pl.program_id(2)
    r = cb * CHUNK
    dk, dv = GLA_DK, GLA_DV

    @pl.when(n == 0)
    def _():
        s_scr[...] = s0_ref[...]

    xg = gk_ref[...] + bgk_ref[...]
    g = (jnp.minimum(xg, 0.0) - jnp.log(1.0 + jnp.exp(-jnp.abs(xg)))) * (1.0 / GLA_GATE_NORMALIZER)
    q = q_ref[...] * GLA_DK ** -0.5
    k = k_ref[...]
    v = v_ref[...]
    if t_valid is not None:
        okc = (n * r + lax.broadcasted_iota(jnp.int32, (r, 1), 0)) < t_valid
        g = jnp.where(okc, g, 0.0)
        k = jnp.where(okc, k, 0.0)
        v = jnp.where(okc, v, 0.0)

    gparts = _split3(g)
    bcum = sum(_dot(_chunk_tril(r), p) for p in gparts)
    ones = jnp.ones((CHUNK, LANES), BF16)
    row = lax.broadcasted_iota(jnp.int32, (CHUNK, 1), 0)
    jrow = lax.broadcasted_iota(jnp.int32, (SUB, 1), 0)
    lane = lax.broadcasted_iota(jnp.int32, (SUB, CHUNK), 1)
    nw = nw_ref[...]
    s = s_scr[...]
    for c in range(cb):
        sl = slice(c * CHUNK, (c + 1) * CHUNK)
        qc, kc, vc, bc = q[sl], k[sl], v[sl], bcum[sl]
        blast = bc[CHUNK - 1:CHUNK, :]
        attn_t = jnp.zeros((CHUNK, CHUNK), F32)
        for bi in range(1, CHUNK // SUB):
            bref = bc[bi * SUB:bi * SUB + 1, :]
            k_i = jnp.where(row < bi * SUB, kc * jnp.exp(jnp.minimum(bref - bc, 0.0)), 0.0)
            q_i = jnp.where((row >= bi * SUB) & (row < (bi + 1) * SUB),
                            qc * jnp.exp(jnp.minimum(bc - bref, 0.0)), 0.0)
            attn_t = attn_t + _dot_nt(k_i, q_i)
        diag = []
        for bi in range(CHUNK // SUB):
            sb = slice(bi * SUB, (bi + 1) * SUB)
            qb, kb, bb = qc[sb], kc[sb], bc[sb]
            d = jnp.zeros((SUB, CHUNK), F32)
            for il in range(SUB):
                e = jnp.exp(jnp.minimum(bb[il:il + 1, :] - bb, 0.0))
                col = jnp.sum(kb * e * qb[il:il + 1, :], axis=-1, keepdims=True)
                col = jnp.where(jrow <= il, col, 0.0)
                d = jnp.where(lane == bi * SUB + il, col, d)
            diag.append(d)
        attn_t = attn_t + jnp.concatenate(diag, axis=0)
        o = _dot(qc * jnp.exp(bc), s) + _dot_tn(attn_t, vc)
        bl_col = sum(_dot_tn(p[sl], ones) for p in gparts)
        decay_col = jnp.concatenate([jnp.exp(bl_col)] * (dv // LANES), axis=1)
        s = s * decay_col + _dot_tn(kc * jnp.exp(blast - bc), vc)
        gt = gate_ref[sl, :]
        o_ref[sl, :] = (_rms(o, nw) * (gt * _sigmoid(gt))).astype(o_ref.dtype)
    s_scr[...] = s

    @pl.when(n == n_blocks - 1)
    def _():
        so_ref[...] = s_scr[...]


def _gla_chunks(qkvg3, gk3, b_gk, s0, norm_w, out_dtype, cb, t_valid):
    b, t, _ = qkvg3.shape
    r = cb * CHUNK
    assert t % r == 0
    nb = t // r
    dk, dv = GLA_DK, GLA_DV
    koff = GLA_HEADS
    voff = 2 * GLA_HEADS * dk // dv
    goff = voff + GLA_HEADS
    return pl.pallas_call(
        functools.partial(_gla_kernel, cb=cb, t_valid=t_valid, n_blocks=nb),
        grid=(b, GLA_HEADS, nb),
        in_specs=[pl.BlockSpec((None, r, dk), lambda bb, h, n: (bb, n, h)),
                  pl.BlockSpec((None, r, dk), lambda bb, h, n: (bb, n, koff + h)),
                  pl.BlockSpec((None, r, dv), lambda bb, h, n: (bb, n, voff + h)),
                  pl.BlockSpec((None, r, dv), lambda bb, h, n: (bb, n, goff + h)),
                  pl.BlockSpec((None, r, dk), lambda bb, h, n: (bb, n, h)),
                  pl.BlockSpec((1, dk), lambda bb, h, n: (0, h)),
                  pl.BlockSpec((None, None, dk, dv), lambda bb, h, n: (bb, h, 0, 0)),
                  pl.BlockSpec((1, dv), lambda bb, h, n: (0, 0))],
        out_specs=[pl.BlockSpec((None, r, dv), lambda bb, h, n: (bb, n, h)),
                   pl.BlockSpec((None, None, dk, dv), lambda bb, h, n: (bb, h, 0, 0))],
        out_shape=[jax.ShapeDtypeStruct((b, t, GLA_HEADS * dv), out_dtype),
                   jax.ShapeDtypeStruct(s0.shape, F32)],
        scratch_shapes=[pltpu.VMEM((dk, dv), F32)],
        compiler_params=_params("parallel", "parallel", "arbitrary"), name="gla_chunks")(
            qkvg3, qkvg3, qkvg3, qkvg3, gk3, b_gk, s0, norm_w)


def _t5_bucket(n):
    n = np.asarray(n)
    max_exact = N_BUCKETS // 2
    nf = np.maximum(n, max_exact).astype(np.float32)
    large = max_exact + (np.log(nf / max_exact) / math.log(MAX_DISTANCE / max_exact)
                         * (N_BUCKETS - max_exact)).astype(np.int32)
    return np.where(n < max_exact, n, np.minimum(large, N_BUCKETS - 1)).astype(np.int32)


def _lambda(lam_ref, lam_init):
    lf = lam_ref[...]
    s1 = jnp.sum(lf[0:1] * lf[1:2], axis=-1, keepdims=True)
    s2 = jnp.sum(lf[2:3] * lf[3:4], axis=-1, keepdims=True)
    return jnp.exp(s1) - jnp.exp(s2) + lam_init


def _flash_kernel(qi_ref, ki_ref, q_ref, k_ref, v_ref, bd_ref, bs_ref, rb_ref, lam_ref, sub_ref, o_ref,
                  q_scr, m_scr, l_scr, acc_scr, bias_scr, *, lam_init):
    h = pl.program_id(1)
    step = pl.program_id(2)
    qi = qi_ref[step]
    ki = ki_ref[step]
    tb = q_ref.shape[0]
    dh = DIFF_HEAD_DIM

    @pl.when(step == 0)
    def _():
        ri = lax.broadcasted_iota(jnp.int32, (tb, tb), 0)
        ci = lax.broadcasted_iota(jnp.int32, (tb, tb), 1)
        for mm in range(2):
            bd = jnp.zeros((tb, tb), F32)
            bs = jnp.zeros((tb, tb), F32)
            for bk in range(N_BUCKETS):
                val = rb_ref[bk, 2 * h + mm] * LOG2E
                bd = jnp.where(bd_ref[...] == bk, val, bd)
                bs = jnp.where(bs_ref[...] == bk, val, bs)
            bias_scr[mm, 0] = jnp.where(ci > ri, NEG, bd)
            bias_scr[mm, 1] = bs

    @pl.when(ki == 0)
    def _():
        q_scr[...] = (q_ref[...] * (dh ** -0.5 * LOG2E)).astype(q_scr.dtype)
        m_scr[...] = jnp.full_like(m_scr, NEG)
        l_scr[...] = jnp.zeros_like(l_scr)
        acc_scr[...] = jnp.zeros_like(acc_scr)

    def update(mm, s, shift):
        v = v_ref[...].astype(BF16)
        m_prev = m_scr[mm]
        if shift.ndim == 0:
            m_new = jnp.maximum(m_prev, jnp.max(s, axis=-1, keepdims=True) + shift)
            p = jnp.exp2(s - (m_new - shift))
        else:
            s = s + shift
            m_new = jnp.maximum(m_prev, jnp.max(s, axis=-1, keepdims=True))
            p = jnp.exp2(s - m_new)
        alpha = jnp.exp2(m_prev - m_new)
        l_scr[mm] = alpha * l_scr[mm] + jnp.sum(p, axis=-1, keepdims=True)
        acc_scr[mm] = alpha * acc_scr[mm] + _dot(p, v)
        m_scr[mm] = m_new

    def logits(mm):
        return _dot_nt(q_scr[:, mm * dh:(mm + 1) * dh], k_ref[:, mm * dh:(mm + 1) * dh])

    @pl.when(ki == qi)
    def _():
        for mm in range(2):
            update(mm, logits(mm), bias_scr[mm, 0])

    @pl.when(ki == qi - 1)
    def _():
        for mm in range(2):
            update(mm, logits(mm), bias_scr[mm, 1])

    @pl.when(ki < qi - 1)
    def _():
        for mm in range(2):
            update(mm, logits(mm), rb_ref[N_BUCKETS - 1, 2 * h + mm] * LOG2E)

    @pl.when(ki == qi)
    def _():
        lam = _lambda(lam_ref, lam_init)
        out = acc_scr[0] / l_scr[0] - lam * (acc_scr[1] / l_scr[1])
        o_ref[...] = (_rms(out, sub_ref[...]) * (1.0 - lam_init)).astype(o_ref.dtype)


def _flash_diff(qkv3, rel_bias, lambdas, subln, lam_init, out_dtype, tb):
    b, t, _ = qkv3.shape
    tb = min(tb, t)
    assert t % tb == 0 and tb >= MAX_DISTANCE
    nq = t // tb
    dd = 2 * DIFF_HEAD_DIM
    i = np.arange(tb)[:, None]
    j = np.arange(tb)[None, :]
    bkt_diag = jnp.asarray(_t5_bucket(np.maximum(i - j, 0)))
    bkt_sub = jnp.asarray(_t5_bucket(tb + i - j))
    koff = D_MODEL // dd
    pairs = [(qi, ki) for qi in range(nq) for ki in range(qi + 1)]
    qi_tbl = jnp.asarray([pq for pq, _ in pairs], jnp.int32)
    ki_tbl = jnp.asarray([pk for _, pk in pairs], jnp.int32)
    grid_spec = pltpu.PrefetchScalarGridSpec(
        num_scalar_prefetch=2, grid=(b, DIFF_HEADS, len(pairs)),
        in_specs=[pl.BlockSpec((None, tb, dd), lambda bb, h, s, qt, kt: (bb, qt[s], h)),
                  pl.BlockSpec((None, tb, dd), lambda bb, h, s, qt, kt: (bb, kt[s], koff + h)),
                  pl.BlockSpec((None, tb, dd), lambda bb, h, s, qt, kt: (bb, kt[s], 2 * koff + h)),
                  pl.BlockSpec((tb, tb), lambda bb, h, s, qt, kt: (0, 0)),
                  pl.BlockSpec((tb, tb), lambda bb, h, s, qt, kt: (0, 0)),
                  pl.BlockSpec(memory_space=pltpu.SMEM),
                  pl.BlockSpec((4, DIFF_HEAD_DIM), lambda bb, h, s, qt, kt: (0, 0)),
                  pl.BlockSpec((1, dd), lambda bb, h, s, qt, kt: (0, 0))],
        out_specs=pl.BlockSpec((None, tb, dd), lambda bb, h, s, qt, kt: (bb, qt[s], h)),
        scratch_shapes=[pltpu.VMEM((tb, dd), BF16), pltpu.VMEM((2, tb, 1), F32), pltpu.VMEM((2, tb, 1), F32),
                        pltpu.VMEM((2, tb, dd), F32), pltpu.VMEM((2, 2, tb, tb), F32)])
    return pl.pallas_call(
        functools.partial(_flash_kernel, lam_init=lam_init), grid_spec=grid_spec,
        out_shape=jax.ShapeDtypeStruct((b, t, D_MODEL), out_dtype),
        compiler_params=_params("parallel", "parallel", "arbitrary"), name="flash_diff")(
            qi_tbl, ki_tbl, qkv3, qkv3, qkv3, bkt_diag, bkt_sub, rel_bias, lambdas, subln)


def _decode_kernel(pt_ref, q_ref, *refs, pp, n_steps, lam_init):
    kp_refs, vp_refs = refs[:pp], refs[pp:2 * pp]
    kn_ref, vn_ref, bkt_ref, tbl_ref, lam_ref, sub_ref, o_ref, m_scr, l_scr, acc_scr, bias_scr, p_scr, a_scr = refs[2 * pp:]
    step = pl.program_id(1)
    nh, dh = DIFF_HEADS, DIFF_HEAD_DIM
    ones = jnp.ones((dh, LANES), BF16)
    qs = q_ref[...] * (dh ** -0.5 * LOG2E)

    @pl.when(step == 0)
    def _():
        m_scr[...] = jnp.full_like(m_scr, NEG)
        l_scr[...] = jnp.zeros_like(l_scr)
        acc_scr[...] = jnp.zeros_like(acc_scr)
        bkt = bkt_ref[...]
        bias = jnp.zeros(bkt.shape, F32)
        for bk in range(N_BUCKETS):
            bias = jnp.where(bkt == bk, (tbl_ref[bk] * LOG2E)[None], bias)
        bias_scr[...] = bias

    def lane_sum(x):
        return _dot(x, ones)

    def accumulate(pr, shift_row, values):
        m_prev = m_scr[...]
        if pr.ndim == 3:
            m_new = jnp.maximum(m_prev, jnp.max(pr, axis=0) + shift_row)
            p = jnp.exp2(pr - (m_new - shift_row)[None])
            psum = jnp.sum(p, axis=0)
        else:
            m_new = jnp.maximum(m_prev, pr + shift_row)
            p = jnp.exp2(pr - (m_new - shift_row))
            psum = p
        alpha = jnp.exp2(m_prev - m_new)
        l_scr[...] = alpha * l_scr[...] + psum
        m_scr[...] = m_new
        a_scr[...] = alpha
        if pr.ndim == 3:
            p_scr[...] = p
        else:
            p_scr[0] = p
        for par in range(2):
            ae = a_scr[pl.ds(par, nh, stride=2), :]
            if pr.ndim == 3:
                pe = p_scr[:, pl.ds(par, nh, stride=2), :]
                pv = jnp.sum(jnp.concatenate([pe, pe], axis=-1) * values(), axis=0)
            else:
                pe = p_scr[0, pl.ds(par, nh, stride=2), :]
                pv = jnp.concatenate([pe, pe], axis=-1) * values()
            acc_scr[par] = jnp.concatenate([ae, ae], axis=-1) * acc_scr[par] + pv

    def page_logits(j):
        prod = kp_refs[j][...] * qs[None]
        return lane_sum(prod.reshape(PAGE_SIZE * 2 * nh, dh)).reshape(PAGE_SIZE, 2 * nh, LANES)

    far = tbl_ref[N_BUCKETS - 1] * LOG2E
    zero = jnp.zeros_like(far)
    for j in range(pp - 1):
        accumulate(page_logits(j), far, lambda j=j: vp_refs[j][...])

    @pl.when(step < n_steps - 1)
    def _():
        accumulate(page_logits(pp - 1), far, lambda: vp_refs[pp - 1][...])

    @pl.when(step == n_steps - 1)
    def _():
        accumulate(page_logits(pp - 1) + bias_scr[...], zero, lambda: vp_refs[pp - 1][...])
        accumulate(lane_sum(kn_ref[...] * qs) + tbl_ref[0] * LOG2E, zero, lambda: vn_ref[...])
        l_scr_v = l_scr[...]
        a_scr[...] = l_scr_v
        outs = []
        for par in range(2):
            le = a_scr[pl.ds(par, nh, stride=2), :]
            outs.append(acc_scr[par] / jnp.concatenate([le, le], axis=-1))
        out = outs[0] - _lambda(lam_ref, lam_init) * outs[1]
        o_ref[...] = _rms(out, sub_ref[...]) * (1.0 - lam_init)


def _decode_diff(q, k_new, v_new, cache_k, cache_v, layer, page_table, rel_bias, lambdas, subln, lam_init, pp):
    b, n_pages = page_table.shape
    nh, dh = DIFF_HEADS, DIFF_HEAD_DIM
    pp = min(pp, n_pages)
    assert n_pages % pp == 0 and PAGE_SIZE >= MAX_DISTANCE and dh == LANES
    n_steps = n_pages // pp
    bkt_last = jnp.asarray(np.broadcast_to(
        _t5_bucket(PAGE_SIZE - np.arange(PAGE_SIZE)).reshape(PAGE_SIZE, 1, 1), (PAGE_SIZE, 2 * nh, LANES)))
    tbl = jnp.broadcast_to(rel_bias[:, :, None], (N_BUCKETS, 2 * nh, LANES))
    page_spec = lambda j, width, lanes: pl.BlockSpec(
        (None, None, PAGE_SIZE, width, lanes), lambda bb, s, pt: (layer, pt[bb, s * pp + j], 0, 0, 0))
    full = lambda shape: pl.BlockSpec(shape, lambda bb, s, pt: (0,) * len(shape))
    per_seq = lambda shape: pl.BlockSpec((None,) + shape, lambda bb, s, pt: (bb,) + (0,) * len(shape))
    grid_spec = pltpu.PrefetchScalarGridSpec(
        num_scalar_prefetch=1, grid=(b, n_steps),
        in_specs=([per_seq((2 * nh, dh))]
                  + [page_spec(j, 2 * nh, dh) for j in range(pp)]
                  + [page_spec(j, nh, 2 * dh) for j in range(pp)]
                  + [per_seq((2 * nh, dh)), per_seq((nh, 2 * dh)),
                     full((PAGE_SIZE, 2 * nh, LANES)), full((N_BUCKETS, 2 * nh, LANES)),
                     full((4, dh)), full((1, 2 * dh))]),
        out_specs=per_seq((nh, 2 * dh)),
        scratch_shapes=[pltpu.VMEM((2 * nh, LANES), F32), pltpu.VMEM((2 * nh, LANES), F32),
                        pltpu.VMEM((2, nh, 2 * dh), F32), pltpu.VMEM((PAGE_SIZE, 2 * nh, LANES), F32),
                        pltpu.VMEM((PAGE_SIZE, 2 * nh, LANES), F32), pltpu.VMEM((2 * nh, LANES), F32)])
    return pl.pallas_call(
        functools.partial(_decode_kernel, pp=pp, n_steps=n_steps, lam_init=lam_init),
        grid_spec=grid_spec,
        out_shape=jax.ShapeDtypeStruct((b, nh, 2 * dh), F32),
        compiler_params=_params("parallel", "arbitrary"), name="decode_diff")(
            page_table, q, *([cache_k] * pp), *([cache_v] * pp), k_new, v_new, bkt_last, tbl, lambdas, subln)


class _Tiles:
    def __init__(self, prompt):
        self.prompt = prompt
        self.act_dtype = BF16 if prompt else F32
        self.tm = 1024 if prompt else SUBLANES
        self.tn = 512 if prompt else 1024
        self.tn_ff = 512
        self.tm_res = 1024 if prompt else SUBLANES
        self.tk_res = 512
        self.tt_prep = 256 if prompt else SUBLANES
        self.tc_prep = 1024
        self.cb = 4 if prompt else 1
        self.gdn_hpb = 2
        self.tb_attn = 512
        self.decode_pages = 4


def _group_trunk(x3, tiles, gdn_s0, gdn_c0, gla_s0, cache_k, cache_v, page_table, p):
    b, t, d = x3.shape
    m = b * t
    x = x3.reshape(m, d)
    if not tiles.prompt:
        assert t == 1 and m % SUBLANES == 0
    t_pad = t if tiles.prompt else CHUNK
    t_valid = None if tiles.prompt else t
    nw = lambda i, jn: p['norm_w'][i, jn].reshape(1, d)
    adt = tiles.act_dtype

    def pad_time(a3, to):
        return a3 if a3.shape[1] == to else jnp.pad(a3, ((0, 0), (0, to - a3.shape[1]), (0, 0)))

    gdn_s, gdn_c, gla_s, att_k, att_v = [], [], [], [], []
    h = _rmsnorm(x, nw(0, 0), adt, tiles.tm_res)
    for i in range(DEPTH):
        j = i // N_MIXERS
        kind = i % N_MIXERS
        if kind == 0:
            proj = _matmul(h, p['gdn_w_in'], j, F32, tiles.tm, tiles.tn)
            beta, g = _gdn_gates(h, p['gdn_w_ba'], j, p['gdn_a_log'][j].reshape(1, -1),
                                 p['gdn_dt_bias'][j].reshape(1, -1), tiles.tm)
            proj3 = proj.reshape(b, t, -1)
            buf0 = gdn_c0[j]
            nbuf = GDN_CONV_W - 1
            if t >= nbuf:
                new_buf = proj3[:, t - nbuf:, :GDN_CONV_DIM]
            else:
                new_buf = jnp.concatenate([buf0[:, t:], proj3[..., :GDN_CONV_DIM]], axis=1)
            buf8 = jnp.pad(buf0, ((0, 0), (SUBLANES - (GDN_CONV_W - 1), 0), (0, 0)))
            t8 = max(t, SUBLANES)
            proj3p = pad_time(proj3, t8)
            qkv3 = pad_time(_gdn_prep(proj3p, buf8, p['gdn_conv_w'][j], tiles.tt_prep, tiles.tc_prep), t_pad)
            proj3p = pad_time(proj3p, t_pad)

            def heads(a):
                a = pad_time(a.reshape(b, t, GDN_QK_HEADS * 2), t_pad)
                return a.reshape(b, t_pad, GDN_QK_HEADS, 2).transpose(0, 2, 1, 3)
            gcol, bcol = heads(g), heads(beta)
            o3, s_new = _gdn_delta(qkv3, proj3p, gcol, bcol, gcol.transpose(0, 1, 3, 2), gdn_s0[j],
                                   p['gdn_norm_w'][j].reshape(1, -1), adt, tiles.cb, tiles.gdn_hpb, t_valid)
            mix_in, w_out = o3[:, :t].reshape(m, -1), p['gdn_w_out']
            gdn_s.append(s_new)
            gdn_c.append(new_buf)
        elif kind == 1:
            qkvg = _matmul(h, p['gla_w_qkvg'], j, F32, tiles.tm, tiles.tn)
            low = _matmul(h, p['gla_w_gk1'], j, F32, tiles.tm, tiles.tn)
            gk = _matmul(low, p['gla_w_gk2'], j, F32, tiles.tm, tiles.tn)
            o3, s_new = _gla_chunks(pad_time(qkvg.reshape(b, t, -1), t_pad), pad_time(gk.reshape(b, t, -1), t_pad),
                                    p['gla_b_gk'][j].reshape(1, -1), gla_s0[j],
                                    p['gla_norm_w'][j].reshape(1, -1), adt, tiles.cb, t_valid)
            mix_in, w_out = o3[:, :t].reshape(m, -1), p['gla_w_out']
            gla_s.append(s_new)
        else:
            lam_init = 0.8 - 0.6 * math.exp(-0.3 * i)
            qkv = _matmul(h, p['diff_w_qkv'], j, F32, tiles.tm, tiles.tn)
            nh, dh = DIFF_HEADS, DIFF_HEAD_DIM
            k_new = qkv[:, D_MODEL:2 * D_MODEL].reshape(b, t, 2 * nh, dh)
            v_new = qkv[:, 2 * D_MODEL:].reshape(b, t, nh, 2 * dh)
            subln = p['diff_subln'][j].reshape(1, -1)
            if tiles.prompt:
                o3 = _flash_diff(qkv.reshape(b, t, -1), p['rel_bias'], p['diff_lambda'][j], subln,
                                 lam_init, adt, tiles.tb_attn)
                mix_in = o3.reshape(m, -1)
            else:
                o = _decode_diff(qkv[:, :D_MODEL].reshape(b, 2 * nh, dh), k_new.reshape(b, 2 * nh, dh),
                                 v_new.reshape(b, nh, 2 * dh), cache_k, cache_v, j, page_table,
                                 p['rel_bias'], p['diff_lambda'][j], subln, lam_init, tiles.decode_pages)
                mix_in = o.reshape(m, -1)
            w_out = p['diff_w_out']
            att_k.append(k_new)
            att_v.append(v_new)
        x, h = _matmul_residual(mix_in, w_out, j, x, nw(i, 1), nw(i, 2), adt, tiles.tm_res, tiles.tk_res)
        act = _swiglu_up(h, p['ffn_w_up'], i, adt, tiles.tm, tiles.tn_ff)
        nxt = nw(i + 1, 0) if i + 1 < DEPTH else None
        x, h = _matmul_residual(act, p['ffn_w_down'], i, x, nw(i, 3), nxt, adt, tiles.tm_res, tiles.tk_res)
    return (x.reshape(b, t, d), jnp.stack(gdn_s), jnp.stack(gdn_c), jnp.stack(gla_s),
            jnp.stack(att_k), jnp.stack(att_v))


def kernel(x_prompt, x_sample, state_gdn, state_gdn_conv, state_gla, cache_k, cache_v, page_table,
           norm_w, ffn_w_up, ffn_w_down, rel_bias,
           gdn_w_in, gdn_w_ba, gdn_conv_w, gdn_a_log, gdn_dt_bias, gdn_norm_w, gdn_w_out,
           gla_w_qkvg, gla_w_gk1, gla_w_gk2, gla_b_gk, gla_norm_w, gla_w_out,
           diff_w_qkv, diff_lambda, diff_subln, diff_w_out):
    p = dict(norm_w=norm_w, ffn_w_up=ffn_w_up, ffn_w_down=ffn_w_down, rel_bias=rel_bias,
             gdn_w_in=gdn_w_in, gdn_w_ba=gdn_w_ba, gdn_conv_w=gdn_conv_w, gdn_a_log=gdn_a_log,
             gdn_dt_bias=gdn_dt_bias, gdn_norm_w=gdn_norm_w, gdn_w_out=gdn_w_out,
             gla_w_qkvg=gla_w_qkvg, gla_w_gk1=gla_w_gk1, gla_w_gk2=gla_w_gk2, gla_b_gk=gla_b_gk,
             gla_norm_w=gla_norm_w, gla_w_out=gla_w_out,
             diff_w_qkv=diff_w_qkv, diff_lambda=diff_lambda, diff_subln=diff_subln, diff_w_out=diff_w_out)
    bp = x_prompt.shape[0]
    n_gdn, n_gla = state_gdn.shape[0], state_gla.shape[0]
    zeros_gdn = jnp.zeros((n_gdn, bp) + state_gdn.shape[2:], F32)
    zeros_conv = jnp.zeros((n_gdn, bp) + state_gdn_conv.shape[2:], F32)
    zeros_gla = jnp.zeros((n_gla, bp) + state_gla.shape[2:], F32)
    outs_p = _group_trunk(x_prompt, _Tiles(True), zeros_gdn, zeros_conv, zeros_gla, None, None, None, p)
    outs_s = _group_trunk(x_sample, _Tiles(False), state_gdn, state_gdn_conv, state_gla,
                          cache_k, cache_v, page_table, p)
    return (outs_p[0], outs_s[0]) + outs_p[1:] + outs_s[1:]
```

```python
import functools
import math

import numpy as np
import jax
import jax.numpy as jnp
from jax import lax
from jax.experimental import pallas as pl
from jax.experimental.pallas import tpu as pltpu

F32 = jnp.float32
BF16 = jnp.bfloat16

D_MODEL = 2048
DEPTH = 4
PAGE_SIZE = 128
N_MIXERS = 3
GDN_HEAD_DIM = 128
GDN_QK_HEADS = D_MODEL // 128
GDN_V_HEADS = 2 * GDN_QK_HEADS
GDN_QK_DIM = GDN_QK_HEADS * GDN_HEAD_DIM
GDN_V_DIM = GDN_V_HEADS * GDN_HEAD_DIM
GDN_CONV_DIM = 2 * GDN_QK_DIM + GDN_V_DIM
GDN_CONV_W = 4
GLA_HEADS = 4
GLA_DK = D_MODEL // 2 // GLA_HEADS
GLA_DV = D_MODEL // GLA_HEADS
GLA_GATE_NORMALIZER = 16.0
DIFF_HEAD_DIM = 128
DIFF_HEADS = D_MODEL // (2 * DIFF_HEAD_DIM)
N_BUCKETS = 32
MAX_DISTANCE = 128
D_FF = ((8 * D_MODEL + 3 * 256 - 1) // (3 * 256)) * 256
NORM_EPS = 1e-6

LANES = 128
SUBLANES = 8
CHUNK = 64
SUB = 16
VMEM_LIMIT_BYTES = 48 * 1024 * 1024
VMEM_LIMIT_BIG_BYTES = 56 * 1024 * 1024
NEG = -1e30
LOG2E = math.log2(math.e)


def _params(*sem, vmem=VMEM_LIMIT_BYTES):
    return pltpu.CompilerParams(dimension_semantics=sem, vmem_limit_bytes=vmem)


def _dot(a, b):
    return jnp.dot(a.astype(BF16), b.astype(BF16), preferred_element_type=F32)


def _dot_nt(a, b):
    return lax.dot_general(a.astype(BF16), b.astype(BF16), (((1,), (1,)), ((), ())),
                           preferred_element_type=F32)


def _dot_tn(a, b):
    return lax.dot_general(a.astype(BF16), b.astype(BF16), (((0,), (0,)), ((), ())),
                           preferred_element_type=F32)


def _split3(x):
    hi = x.astype(BF16)
    r = x - hi.astype(F32)
    mid = r.astype(BF16)
    lo = (r - mid.astype(F32)).astype(BF16)
    return hi, mid, lo


def _sigmoid(x):
    return 1.0 / (1.0 + jnp.exp(-x))


def _softplus(x):
    return jnp.maximum(x, 0.0) + jnp.log(1.0 + jnp.exp(-jnp.abs(x)))


def _rms(x, w):
    return x * lax.rsqrt(jnp.mean(x * x, axis=-1, keepdims=True) + NORM_EPS) * w


def _chunk_tril(r):
    i = lax.broadcasted_iota(jnp.int32, (r, r), 0)
    j = lax.broadcasted_iota(jnp.int32, (r, r), 1)
    return jnp.where((j <= i) & ((i // CHUNK) == (j // CHUNK)), 1.0, 0.0).astype(BF16)


def _rmsnorm_kernel(x_ref, w_ref, o_ref):
    o_ref[...] = _rms(x_ref[...], w_ref[...]).astype(o_ref.dtype)


def _rmsnorm(x, w, out_dtype, tm):
    m, d = x.shape
    tm = min(tm, m)
    return pl.pallas_call(
        _rmsnorm_kernel, grid=(m // tm,),
        in_specs=[pl.BlockSpec((tm, d), lambda i: (i, 0)), pl.BlockSpec((1, d), lambda i: (0, 0))],
        out_specs=pl.BlockSpec((tm, d), lambda i: (i, 0)),
        out_shape=jax.ShapeDtypeStruct((m, d), out_dtype),
        compiler_params=_params("parallel"), name="rmsnorm")(x, w)


def _weight_spec(w, layer, block, index):
    if w.ndim == 2:
        return pl.BlockSpec(block, index)
    return pl.BlockSpec((None,) + block, lambda *g: (layer,) + index(*g))


def _mm_kernel(a_ref, w_ref, o_ref, *wb_ref):
    w = w_ref[...].astype(BF16)
    if wb_ref:
        wb_ref[0][...] = w
    o_ref[...] = _dot(a_ref[...], w).astype(o_ref.dtype)


def _matmul(a, w, layer, out_dtype, tm, tn, emit_wb=False):
    m, k = a.shape
    n = w.shape[-1]
    tm, tn = min(tm, m), min(tn, n)
    assert m % tm == 0 and n % tn == 0 and (not emit_wb or m == tm)
    out_shape = [jax.ShapeDtypeStruct((m, n), out_dtype)]
    out_specs = [pl.BlockSpec((tm, tn), lambda i, j: (i, j))]
    if emit_wb:
        out_shape.append(jax.ShapeDtypeStruct((k, n), BF16))
        out_specs.append(pl.BlockSpec((k, tn), lambda i, j: (0, j)))
    res = pl.pallas_call(
        _mm_kernel, grid=(m // tm, n // tn),
        in_specs=[pl.BlockSpec((tm, k), lambda i, j: (i, 0)),
                  _weight_spec(w, layer, (k, tn), lambda i, j: (0, j))],
        out_specs=out_specs, out_shape=out_shape,
        compiler_params=_params("parallel", "parallel"), name="matmul")(a, w)
    return (res[0], res[1]) if emit_wb else res[0]


def _swiglu_kernel(a_ref, wg_ref, wu_ref, o_ref, *wb_refs):
    a = a_ref[...].astype(BF16)
    wg = wg_ref[...].astype(BF16)
    wu = wu_ref[...].astype(BF16)
    if wb_refs:
        wb_refs[0][...] = wg
        wb_refs[1][...] = wu
    g = _dot(a, wg)
    u = _dot(a, wu)
    o_ref[...] = (g * _sigmoid(g) * u).astype(o_ref.dtype)


def _swiglu_up(a, w_gate, w_up, layer, out_dtype, tm, tn, emit_wb=False):
    m, k = a.shape
    stacked = w_up.ndim == 3
    f = w_up.shape[-1] // 2 if stacked else w_up.shape[-1]
    tm, tn = min(tm, m), min(tn, f)
    assert m % tm == 0 and f % tn == 0 and (not emit_wb or m == tm)
    nf = f // tn
    up_off = nf if stacked else 0
    out_shape = [jax.ShapeDtypeStruct((m, f), out_dtype)]
    out_specs = [pl.BlockSpec((tm, tn), lambda i, j: (i, j))]
    if emit_wb:
        out_shape += [jax.ShapeDtypeStruct((k, f), BF16)] * 2
        out_specs += [pl.BlockSpec((k, tn), lambda i, j: (0, j))] * 2
    res = pl.pallas_call(
        _swiglu_kernel, grid=(m // tm, nf),
        in_specs=[pl.BlockSpec((tm, k), lambda i, j: (i, 0)),
                  _weight_spec(w_gate, layer, (k, tn), lambda i, j: (0, j)),
                  _weight_spec(w_up, layer, (k, tn), lambda i, j: (0, j + up_off))],
        out_specs=out_specs, out_shape=out_shape,
        compiler_params=_params("parallel", "parallel"), name="swiglu_up")(a, w_gate, w_up)
    return tuple(res) if emit_wb else res[0]


def _residual_epilogue(mix, x_ref, nw1_ref, nw2_ref, xo_ref, h_ref):
    xn = x_ref[...] + _rms(mix, nw1_ref[...])
    xo_ref[...] = xn
    if h_ref:
        h_ref[0][...] = _rms(xn, nw2_ref[...]).astype(h_ref[0].dtype)


def _mm_res_kernel(a_ref, w_ref, x_ref, nw1_ref, nw2_ref, xo_ref, *rest, nk, emit_h, emit_wb):
    h_ref = rest[:1] if emit_h else ()
    kk = pl.program_id(1)
    w = w_ref[...].astype(BF16)
    if emit_wb:
        rest[-1][...] = w
    part = _dot(a_ref[...], w)

    @pl.when(kk == 0)
    def _():
        xo_ref[...] = part

    @pl.when(kk > 0)
    def _():
        xo_ref[...] += part

    @pl.when(kk == nk - 1)
    def _():
        _residual_epilogue(xo_ref[...], x_ref, nw1_ref, nw2_ref, xo_ref, h_ref)


def _mm_res_resident_kernel(a_ref, w_ref, x_ref, nw1_ref, nw2_ref, xo_ref, *h_ref):
    _residual_epilogue(_dot(a_ref[...], w_ref[...]), x_ref, nw1_ref, nw2_ref, xo_ref, h_ref)


def _matmul_residual(a, w, layer, x, nw_post, nw_next, h_dtype, tm, tk, emit_wb=False):
    m, k = a.shape
    n = w.shape[-1]
    tm = min(tm, m)
    assert m % tm == 0 and (not emit_wb or m == tm)
    emit_h = nw_next is not None
    if not emit_h:
        nw_next = nw_post
    resident = w.ndim == 2
    if resident:
        grid = (m // tm,)
        row = lambda i: (i, 0)
        fixed = lambda i: (0, 0)
        a_spec = pl.BlockSpec((tm, k), row)
        w_spec = pl.BlockSpec((k, n), fixed, pipeline_mode=pl.Buffered(1))
        x_spec = pl.BlockSpec((tm, n), row)
        body = _mm_res_resident_kernel
        sem = ("parallel",)
    else:
        tk = min(tk, k)
        assert k % tk == 0
        nk = k // tk
        grid = (m // tm, nk)
        row = lambda i, kk: (i, 0)
        fixed = lambda i, kk: (0, 0)
        a_spec = pl.BlockSpec((tm, tk), lambda i, kk: (i, kk))
        w_spec = pl.BlockSpec((None, tk, n), lambda i, kk: (layer, kk, 0))
        x_spec = pl.BlockSpec((tm, n), row, pipeline_mode=pl.Buffered(1))
        body = functools.partial(_mm_res_kernel, nk=nk, emit_h=emit_h, emit_wb=emit_wb)
        sem = ("parallel", "arbitrary")
    out_shape = [jax.ShapeDtypeStruct((m, n), F32)]
    out_specs = [pl.BlockSpec((tm, n), row)]
    if emit_h:
        out_shape.append(jax.ShapeDtypeStruct((m, n), h_dtype))
        out_specs.append(pl.BlockSpec((tm, n), row))
    if emit_wb:
        out_shape.append(jax.ShapeDtypeStruct((k, n), BF16))
        out_specs.append(pl.BlockSpec((tk, n), lambda i, kk: (kk, 0)))
    res = pl.pallas_call(
        body, grid=grid,
        in_specs=[a_spec, w_spec, x_spec, pl.BlockSpec((1, n), fixed), pl.BlockSpec((1, n), fixed)],
        out_specs=out_specs, out_shape=out_shape,
        compiler_params=_params(*sem, vmem=VMEM_LIMIT_BIG_BYTES), name="matmul_residual")(
            a, w, x, nw_post, nw_next)
    xo = res[0]
    h = res[1] if emit_h else None
    return (xo, h, res[-1]) if emit_wb else (xo, h)


def _gdn_gates_kernel(a_ref, w_ref, alog_ref, dtb_ref, beta_ref, g_ref):
    ba = _dot(a_ref[...], w_ref[...])
    nh = beta_ref.shape[-1]
    beta_ref[...] = _sigmoid(ba[:, :nh])
    g_ref[...] = -jnp.exp(alog_ref[...]) * _softplus(ba[:, nh:] + dtb_ref[...])


def _gdn_gates(a, w_ba, layer, a_log, dt_bias, tm):
    m, k = a.shape
    nh = GDN_V_HEADS
    tm = min(tm, m)
    return pl.pallas_call(
        _gdn_gates_kernel, grid=(m // tm,),
        in_specs=[pl.BlockSpec((tm, k), lambda i: (i, 0)),
                  pl.BlockSpec((None, k, 2 * nh), lambda i: (layer, 0, 0)),
                  pl.BlockSpec((1, nh), lambda i: (0, 0)),
                  pl.BlockSpec((1, nh), lambda i: (0, 0))],
        out_specs=[pl.BlockSpec((tm, nh), lambda i: (i, 0))] * 2,
        out_shape=[jax.ShapeDtypeStruct((m, nh), F32)] * 2,
        compiler_params=_params("parallel"), name="gdn_gates")(a, w_ba, a_log, dt_bias)


def _gdn_prep_kernel(x_ref, prev_ref, buf_ref, cw_ref, o_ref, *, n_qk_tiles):
    i = pl.program_id(1)
    j = pl.program_id(2)
    x = x_ref[...]
    tt, tc = x.shape
    halo = jnp.where(i == 0, buf_ref[...], prev_ref[...])
    row8 = lax.broadcasted_iota(jnp.int32, (SUBLANES, tc), 0)
    y = x * cw_ref[GDN_CONV_W - 1:GDN_CONV_W, :]
    for s in range(1, GDN_CONV_W):
        xs = pltpu.roll(x, s, 0)
        hs = pltpu.roll(halo, s, 0)
        top = jnp.where(row8 < s, hs, xs[:SUBLANES])
        shifted = top if tt == SUBLANES else jnp.concatenate([top, xs[SUBLANES:]], axis=0)
        y = y + shifted * cw_ref[GDN_CONV_W - 1 - s:GDN_CONV_W - s, :]
    y = y * _sigmoid(y)
    pieces = []
    for hh in range(tc // GDN_HEAD_DIM):
        yh = y[:, hh * GDN_HEAD_DIM:(hh + 1) * GDN_HEAD_DIM]
        pieces.append(yh * lax.rsqrt(jnp.sum(yh * yh, axis=-1, keepdims=True) + NORM_EPS))
    yn = jnp.concatenate(pieces, axis=1)
    o_ref[...] = jnp.where(j < n_qk_tiles, yn * GDN_HEAD_DIM ** -0.5,
                           jnp.where(j < 2 * n_qk_tiles, yn, y))


def _gdn_prep(proj3, buf8, conv_w, tt, tc):
    b, t, _ = proj3.shape
    tt = min(tt, t)
    assert t % tt == 0 and tt % SUBLANES == 0 and GDN_QK_DIM % tc == 0
    hb = tt // SUBLANES
    return pl.pallas_call(
        functools.partial(_gdn_prep_kernel, n_qk_tiles=GDN_QK_DIM // tc),
        grid=(b, t // tt, GDN_CONV_DIM // tc),
        in_specs=[pl.BlockSpec((None, tt, tc), lambda bb, i, j: (bb, i, j)),
                  pl.BlockSpec((None, SUBLANES, tc), lambda bb, i, j: (bb, jnp.maximum(i * hb - 1, 0), j)),
                  pl.BlockSpec((None, SUBLANES, tc), lambda bb, i, j: (bb, 0, j)),
                  pl.BlockSpec((GDN_CONV_W, tc), lambda bb, i, j: (0, j))],
        out_specs=pl.BlockSpec((None, tt, tc), lambda bb, i, j: (bb, i, j)),
        out_shape=jax.ShapeDtypeStruct((b, t, GDN_CONV_DIM), F32),
        compiler_params=_params("parallel", "parallel", "parallel"), name="gdn_prep")(
            proj3, proj3, buf8, conv_w)


def _gdn_delta_kernel(q_ref, k_ref, v_ref, z_ref, gc_ref, bc_ref, gr_ref, s0_ref, nw_ref,
                      o_ref, so_ref, s_scr, *, cb, hpb, t_valid, n_blocks):
    n = pl.program_id(2)
    r = cb * CHUNK
    dh = GDN_HEAD_DIM

    @pl.when(n == 0)
    def _():
        s_scr[...] = s0_ref[...]

    q = q_ref[...]
    k = k_ref[...]
    v = v_ref[...]
    gcol = gc_ref[...]
    bcol = bc_ref[...]
    grow = gr_ref[...]
    if t_valid is not None:
        okc = (n * r + lax.broadcasted_iota(jnp.int32, (r, 1), 0)) < t_valid
        okr = (n * r + lax.broadcasted_iota(jnp.int32, (1, r), 1)) < t_valid
        k = jnp.where(okc, k, 0.0)
        v = jnp.where(okc, v, 0.0)
        gcol = jnp.where(okc, gcol, 0.0)
        bcol = jnp.where(okc, bcol, 0.0)
        grow = jnp.where(okr, grow, 0.0)

    lm = _chunk_tril(r)
    gcum_col = [sum(_dot(lm, p) for p in _split3(gcol[qh])) for qh in range(hpb)]
    gcum_row = [sum(_dot_nt(p, lm) for p in _split3(grow[qh])) for qh in range(hpb)]

    ri = lax.broadcasted_iota(jnp.int32, (r, r), 0)
    ci = lax.broadcasted_iota(jnp.int32, (r, r), 1)
    eye = jnp.where(ri == ci, 1.0, 0.0)
    tril = ((ri // CHUNK) == (ci // CHUNK)) & (ri >= ci)
    lvl_masks = []
    size = 1
    while size < CHUNK:
        lvl_masks.append(((ri // (2 * size)) == (ci // (2 * size)))
                         & ((ri // size) % 2 == 1) & ((ci // size) % 2 == 0))
        size *= 2
    if t_valid is not None and t_valid <= 1:
        lvl_masks = []

    nw = nw_ref[...]
    heads = [(qh, hh) for qh in range(hpb) for hh in range(2)]
    kq = [k[:, qh * dh:(qh + 1) * dh] for qh in range(hpb)]
    qq = [q[:, qh * dh:(qh + 1) * dh] for qh in range(hpb)]
    gram = [_dot_nt(kq[qh], kq[qh]) for qh in range(hpb)]
    qk = [_dot_nt(qq[qh], kq[qh]) for qh in range(hpb)]
    gc = [gcum_col[qh][:, hh:hh + 1] for qh, hh in heads]
    bc = [bcol[qh][:, hh:hh + 1] for qh, hh in heads]
    decay = [jnp.exp(jnp.where(tril, gc[i] - gcum_row[qh][hh:hh + 1, :], NEG)) for i, (qh, hh) in enumerate(heads)]
    mm_ = [gram[qh] * bc[i] * decay[i] for i, (qh, hh) in enumerate(heads)]
    tinv = [eye for _ in heads]
    for lvl, msk in enumerate(lvl_masks):
        ml = [jnp.where(msk, m, 0.0) for m in mm_]
        if lvl == 0:
            tinv = [t - m for t, m in zip(tinv, ml)]
        else:
            y = [_dot(m, t) for t, m in zip(tinv, ml)]
            tinv = [t - _dot(t, yy) for t, yy in zip(tinv, y)]
    eg = [jnp.exp(g_) for g_ in gc]
    sol = [_dot(tinv[i], jnp.concatenate([v[:, (2 * qh + hh) * dh:(2 * qh + hh + 1) * dh] * bc[i],
                                          kq[qh] * (bc[i] * eg[i])], axis=1))
           for i, (qh, hh) in enumerate(heads)]
    asol = [_dot(qk[qh] * decay[i], sol[i]) for i, (qh, hh) in enumerate(heads)]
    o2 = [a[:, :dh] for a in asol]
    o1 = [qq[qh] * eg[i] - asol[i][:, dh:] for i, (qh, hh) in enumerate(heads)]
    glast = [[g_[(c + 1) * CHUNK - 1:(c + 1) * CHUNK, :] for c in range(cb)] for g_ in gc]
    kd = [kq[qh] * jnp.exp(jnp.concatenate([jnp.broadcast_to(gl, (CHUNK, 1)) for gl in glast[i]], axis=0) - gc[i])
          for i, (qh, hh) in enumerate(heads)]
    qw = [[_dot_tn(kd[i][c * CHUNK:(c + 1) * CHUNK], sol[i][c * CHUNK:(c + 1) * CHUNK]) for c in range(cb)]
          for i in range(len(heads))]
    s = [s_scr[2 * qh + hh] for qh, hh in heads]
    for c in range(cb):
        sl = slice(c * CHUNK, (c + 1) * CHUNK)
        for i, (qh, hh) in enumerate(heads):
            x = _dot(jnp.concatenate([qw[i][c][:, dh:], o1[i][sl]], axis=0), s[i])
            o = x[dh:] + o2[i][sl]
            s[i] = s[i] * jnp.exp(glast[i][c]) - x[:dh] + qw[i][c][:, :dh]
            col = slice((2 * qh + hh) * dh, (2 * qh + hh + 1) * dh)
            zc = z_ref[sl, col]
            o_ref[sl, col] = (_rms(o, nw) * (zc * _sigmoid(zc))).astype(o_ref.dtype)
    for i, (qh, hh) in enumerate(heads):
        s_scr[2 * qh + hh] = s[i]

    @pl.when(n == n_blocks - 1)
    def _():
        so_ref[...] = s_scr[...]


def _gdn_delta(qkv3, proj3, gcol, bcol, grow, s0, norm_w, out_dtype, cb, hpb, t_valid):
    b, t, _ = qkv3.shape
    r = cb * CHUNK
    assert t % r == 0
    nb = t // r
    dh = GDN_HEAD_DIM
    qw_, vw_ = hpb * dh, 2 * hpb * dh
    assert GDN_QK_HEADS % hpb == 0
    koff = GDN_QK_DIM // qw_
    voff = 2 * GDN_QK_DIM // vw_
    zoff = GDN_CONV_DIM // vw_
    return pl.pallas_call(
        functools.partial(_gdn_delta_kernel, cb=cb, hpb=hpb, t_valid=t_valid, n_blocks=nb),
        grid=(b, GDN_QK_HEADS // hpb, nb),
        in_specs=[pl.BlockSpec((None, r, qw_), lambda bb, h, n: (bb, n, h)),
                  pl.BlockSpec((None, r, qw_), lambda bb, h, n: (bb, n, koff + h)),
                  pl.BlockSpec((None, r, vw_), lambda bb, h, n: (bb, n, voff + h)),
                  pl.BlockSpec((None, r, vw_), lambda bb, h, n: (bb, n, zoff + h)),
                  pl.BlockSpec((None, hpb, r, 2), lambda bb, h, n: (bb, h, n, 0)),
                  pl.BlockSpec((None, hpb, r, 2), lambda bb, h, n: (bb, h, n, 0)),
                  pl.BlockSpec((None, hpb, 2, r), lambda bb, h, n: (bb, h, 0, n)),
                  pl.BlockSpec((None, 2 * hpb, dh, dh), lambda bb, h, n: (bb, h, 0, 0)),
                  pl.BlockSpec((1, dh), lambda bb, h, n: (0, 0))],
        out_specs=[pl.BlockSpec((None, r, vw_), lambda bb, h, n: (bb, n, h)),
                   pl.BlockSpec((None, 2 * hpb, dh, dh), lambda bb, h, n: (bb, h, 0, 0))],
        out_shape=[jax.ShapeDtypeStruct((b, t, GDN_V_DIM), out_dtype),
                   jax.ShapeDtypeStruct(s0.shape, F32)],
        scratch_shapes=[pltpu.VMEM((2 * hpb, dh, dh), F32)],
        compiler_params=_params("parallel", "parallel", "arbitrary"), name="gdn_delta")(
            qkv3, qkv3, qkv3, proj3, gcol, bcol, grow, s0, norm_w)


def _gla_kernel(q_ref, k_ref, v_ref, gate_ref, gk_ref, bgk_ref, s0_ref, nw_ref,
                o_ref, so_ref, s_scr, *, cb, t_valid, n_blocks):
    n = pl.program_id(2)
    r = cb * CHUNK
    dk, dv = GLA_DK, GLA_DV

    @pl.when(n == 0)
    def _():
        s_scr[...] = s0_ref[...]

    xg = gk_ref[...] + bgk_ref[...]
    g = (jnp.minimum(xg, 0.0) - jnp.log(1.0 + jnp.exp(-jnp.abs(xg)))) * (1.0 / GLA_GATE_NORMALIZER)
    q = q_ref[...] * GLA_DK ** -0.5
    k = k_ref[...]
    v = v_ref[...]
    if t_valid is not None:
        okc = (n * r + lax.broadcasted_iota(jnp.int32, (r, 1), 0)) < t_valid
        g = jnp.where(okc, g, 0.0)
        k = jnp.where(okc, k, 0.0)
        v = jnp.where(okc, v, 0.0)

    gparts = _split3(g)
    bcum = sum(_dot(_chunk_tril(r), p) for p in gparts)
    ones = jnp.ones((CHUNK, LANES), BF16)
    row = lax.broadcasted_iota(jnp.int32, (CHUNK, 1), 0)
    jrow = lax.broadcasted_iota(jnp.int32, (SUB, 1), 0)
    lane = lax.broadcasted_iota(jnp.int32, (SUB, CHUNK), 1)
    nw = nw_ref[...]
    s = s_scr[...]
    for c in range(cb):
        sl = slice(c * CHUNK, (c + 1) * CHUNK)
        qc, kc, vc, bc = q[sl], k[sl], v[sl], bcum[sl]
        blast = bc[CHUNK - 1:CHUNK, :]
        attn_t = jnp.zeros((CHUNK, CHUNK), F32)
        for bi in range(1, CHUNK // SUB):
            bref = bc[bi * SUB:bi * SUB + 1, :]
            k_i = jnp.where(row < bi * SUB, kc * jnp.exp(jnp.minimum(bref - bc, 0.0)), 0.0)
            q_i = jnp.where((row >= bi * SUB) & (row < (bi + 1) * SUB),
                            qc * jnp.exp(jnp.minimum(bc - bref, 0.0)), 0.0)
            attn_t = attn_t + _dot_nt(k_i, q_i)
        diag = []
        for bi in range(CHUNK // SUB):
            sb = slice(bi * SUB, (bi + 1) * SUB)
            qb, kb, bb = qc[sb], kc[sb], bc[sb]
            d = jnp.zeros((SUB, CHUNK), F32)
            for il in range(SUB):
                e = jnp.exp(jnp.minimum(bb[il:il + 1, :] - bb, 0.0))
                col = jnp.sum(kb * e * qb[il:il + 1, :], axis=-1, keepdims=True)
                col = jnp.where(jrow <= il, col, 0.0)
                d = jnp.where(lane == bi * SUB + il, col, d)
            diag.append(d)
        attn_t = attn_t + jnp.concatenate(diag, axis=0)
        o = _dot(qc * jnp.exp(bc), s) + _dot_tn(attn_t, vc)
        bl_col = sum(_dot_tn(p[sl], ones) for p in gparts)
        decay_col = jnp.concatenate([jnp.exp(bl_col)] * (dv // LANES), axis=1)
        s = s * decay_col + _dot_tn(kc * jnp.exp(blast - bc), vc)
        gt = gate_ref[sl, :]
        o_ref[sl, :] = (_rms(o, nw) * (gt * _sigmoid(gt))).astype(o_ref.dtype)
    s_scr[...] = s

    @pl.when(n == n_blocks - 1)
    def _():
        so_ref[...] = s_scr[...]


def _gla_chunks(qkvg3, gk3, b_gk, s0, norm_w, out_dtype, cb, t_valid):
    b, t, _ = qkvg3.shape
    r = cb * CHUNK
    assert t % r == 0
    nb = t // r
    dk, dv = GLA_DK, GLA_DV
    koff = GLA_HEADS
    voff = 2 * GLA_HEADS * dk // dv
    goff = voff + GLA_HEADS
    return pl.pallas_call(
        functools.partial(_gla_kernel, cb=cb, t_valid=t_valid, n_blocks=nb),
        grid=(b, GLA_HEADS, nb),
        in_specs=[pl.BlockSpec((None, r, dk), lambda bb, h, n: (bb, n, h)),
                  pl.BlockSpec((None, r, dk), lambda bb, h, n: (bb, n, koff + h)),
                  pl.BlockSpec((None, r, dv), lambda bb, h, n: (bb, n, voff + h)),
                  pl.BlockSpec((None, r, dv), lambda bb, h, n: (bb, n, goff + h)),
                  pl.BlockSpec((None, r, dk), lambda bb, h, n: (bb, n, h)),
                  pl.BlockSpec((1, dk), lambda bb, h, n: (0, h)),
                  pl.BlockSpec((None, None, dk, dv), lambda bb, h, n: (bb, h, 0, 0)),
                  pl.BlockSpec((1, dv), lambda bb, h, n: (0, 0))],
        out_specs=[pl.BlockSpec((None, r, dv), lambda bb, h, n: (bb, n, h)),
                   pl.BlockSpec((None, None, dk, dv), lambda bb, h, n: (bb, h, 0, 0))],
        out_shape=[jax.ShapeDtypeStruct((b, t, GLA_HEADS * dv), out_dtype),
                   jax.ShapeDtypeStruct(s0.shape, F32)],
        scratch_shapes=[pltpu.VMEM((dk, dv), F32)],
        compiler_params=_params("parallel", "parallel", "arbitrary"), name="gla_chunks")(
            qkvg3, qkvg3, qkvg3, qkvg3, gk3, b_gk, s0, norm_w)


def _t5_bucket(n):
    n = np.asarray(n)
    max_exact = N_BUCKETS // 2
    nf = np.maximum(n, max_exact).astype(np.float32)
    large = max_exact + (np.log(nf / max_exact) / math.log(MAX_DISTANCE / max_exact)
                         * (N_BUCKETS - max_exact)).astype(np.int32)
    return np.where(n < max_exact, n, np.minimum(large, N_BUCKETS - 1)).astype(np.int32)


def _lambda(lam_ref, lam_init):
    lf = lam_ref[...]
    s1 = jnp.sum(lf[0:1] * lf[1:2], axis=-1, keepdims=True)
    s2 = jnp.sum(lf[2:3] * lf[3:4], axis=-1, keepdims=True)
    return jnp.exp(s1) - jnp.exp(s2) + lam_init


def _flash_kernel(qi_ref, ki_ref, q_ref, k_ref, v_ref, bd_ref, bs_ref, rb_ref, lam_ref, sub_ref, o_ref,
                  q_scr, m_scr, l_scr, acc_scr, bias_scr, *, lam_init):
    h = pl.program_id(1)
    step = pl.program_id(2)
    qi = qi_ref[step]
    ki = ki_ref[step]
    tb = q_ref.shape[0]
    dh = DIFF_HEAD_DIM

    @pl.when(step == 0)
    def _():
        ri = lax.broadcasted_iota(jnp.int32, (tb, tb), 0)
        ci = lax.broadcasted_iota(jnp.int32, (tb, tb), 1)
        for mm in range(2):
            bd = jnp.zeros((tb, tb), F32)
            bs = jnp.zeros((tb, tb), F32)
            for bk in range(N_BUCKETS):
                val = rb_ref[bk, 2 * h + mm] * LOG2E
                bd = jnp.where(bd_ref[...] == bk, val, bd)
                bs = jnp.where(bs_ref[...] == bk, val, bs)
            bias_scr[mm, 0] = jnp.where(ci > ri, NEG, bd)
            bias_scr[mm, 1] = bs

    @pl.when(ki == 0)
    def _():
        q_scr[...] = (q_ref[...] * (dh ** -0.5 * LOG2E)).astype(q_scr.dtype)
        m_scr[...] = jnp.full_like(m_scr, NEG)
        l_scr[...] = jnp.zeros_like(l_scr)
        acc_scr[...] = jnp.zeros_like(acc_scr)

    def update(mm, s, shift):
        v = v_ref[...].astype(BF16)
        m_prev = m_scr[mm]
        if shift.ndim == 0:
            m_new = jnp.maximum(m_prev, jnp.max(s, axis=-1, keepdims=True) + shift)
            p = jnp.exp2(s - (m_new - shift))
        else:
            s = s + shift
            m_new = jnp.maximum(m_prev, jnp.max(s, axis=-1, keepdims=True))
            p = jnp.exp2(s - m_new)
        alpha = jnp.exp2(m_prev - m_new)
        l_scr[mm] = alpha * l_scr[mm] + jnp.sum(p, axis=-1, keepdims=True)
        acc_scr[mm] = alpha * acc_scr[mm] + _dot(p, v)
        m_scr[mm] = m_new

    def logits(mm):
        return _dot_nt(q_scr[:, mm * dh:(mm + 1) * dh], k_ref[:, mm * dh:(mm + 1) * dh])

    @pl.when(ki == qi)
    def _():
        for mm in range(2):
            update(mm, logits(mm), bias_scr[mm, 0])

    @pl.when(ki == qi - 1)
    def _():
        for mm in range(2):
            update(mm, logits(mm), bias_scr[mm, 1])

    @pl.when(ki < qi - 1)
    def _():
        for mm in range(2):
            update(mm, logits(mm), rb_ref[N_BUCKETS - 1, 2 * h + mm] * LOG2E)

    @pl.when(ki == qi)
    def _():
        lam = _lambda(lam_ref, lam_init)
        out = acc_scr[0] / l_scr[0] - lam * (acc_scr[1] / l_scr[1])
        o_ref[...] = (_rms(out, sub_ref[...]) * (1.0 - lam_init)).astype(o_ref.dtype)


def _flash_diff(qkv3, rel_bias, lambdas, subln, lam_init, out_dtype, tb):
    b, t, _ = qkv3.shape
    tb = min(tb, t)
    assert t % tb == 0 and tb >= MAX_DISTANCE
    nq = t // tb
    dd = 2 * DIFF_HEAD_DIM
    i = np.arange(tb)[:, None]
    j = np.arange(tb)[None, :]
    bkt_diag = jnp.asarray(_t5_bucket(np.maximum(i - j, 0)))
    bkt_sub = jnp.asarray(_t5_bucket(tb + i - j))
    koff = D_MODEL // dd
    pairs = [(qi, ki) for qi in range(nq) for ki in range(qi + 1)]
    qi_tbl = jnp.asarray([pq for pq, _ in pairs], jnp.int32)
    ki_tbl = jnp.asarray([pk for _, pk in pairs], jnp.int32)
    grid_spec = pltpu.PrefetchScalarGridSpec(
        num_scalar_prefetch=2, grid=(b, DIFF_HEADS, len(pairs)),
        in_specs=[pl.BlockSpec((None, tb, dd), lambda bb, h, s, qt, kt: (bb, qt[s], h)),
                  pl.BlockSpec((None, tb, dd), lambda bb, h, s, qt, kt: (bb, kt[s], koff + h)),
                  pl.BlockSpec((None, tb, dd), lambda bb, h, s, qt, kt: (bb, kt[s], 2 * koff + h)),
                  pl.BlockSpec((tb, tb), lambda bb, h, s, qt, kt: (0, 0)),
                  pl.BlockSpec((tb, tb), lambda bb, h, s, qt, kt: (0, 0)),
                  pl.BlockSpec(memory_space=pltpu.SMEM),
                  pl.BlockSpec((4, DIFF_HEAD_DIM), lambda bb, h, s, qt, kt: (0, 0)),
                  pl.BlockSpec((1, dd), lambda bb, h, s, qt, kt: (0, 0))],
        out_specs=pl.BlockSpec((None, tb, dd), lambda bb, h, s, qt, kt: (bb, qt[s], h)),
        scratch_shapes=[pltpu.VMEM((tb, dd), BF16), pltpu.VMEM((2, tb, 1), F32), pltpu.VMEM((2, tb, 1), F32),
                        pltpu.VMEM((2, tb, dd), F32), pltpu.VMEM((2, 2, tb, tb), F32)])
    return pl.pallas_call(
        functools.partial(_flash_kernel, lam_init=lam_init), grid_spec=grid_spec,
        out_shape=jax.ShapeDtypeStruct((b, t, D_MODEL), out_dtype),
        compiler_params=_params("parallel", "parallel", "arbitrary"), name="flash_diff")(
            qi_tbl, ki_tbl, qkv3, qkv3, qkv3, bkt_diag, bkt_sub, rel_bias, lambdas, subln)


def _decode_kernel(pt_ref, q_ref, *refs, pp, n_steps, lam_init):
    kp_refs, vp_refs = refs[:pp], refs[pp:2 * pp]
    kn_ref, vn_ref, bkt_ref, tbl_ref, lam_ref, sub_ref, o_ref, m_scr, l_scr, acc_scr, bias_scr, p_scr, a_scr = refs[2 * pp:]
    step = pl.program_id(1)
    nh, dh = DIFF_HEADS, DIFF_HEAD_DIM
    ones = jnp.ones((dh, LANES), BF16)
    qs = q_ref[...] * (dh ** -0.5 * LOG2E)

    @pl.when(step == 0)
    def _():
        m_scr[...] = jnp.full_like(m_scr, NEG)
        l_scr[...] = jnp.zeros_like(l_scr)
        acc_scr[...] = jnp.zeros_like(acc_scr)
        bkt = bkt_ref[...]
        bias = jnp.zeros(bkt.shape, F32)
        for bk in range(N_BUCKETS):
            bias = jnp.where(bkt == bk, (tbl_ref[bk] * LOG2E)[None], bias)
        bias_scr[...] = bias

    def lane_sum(x):
        return _dot(x, ones)

    def accumulate(pr, shift_row, values):
        m_prev = m_scr[...]
        if pr.ndim == 3:
            m_new = jnp.maximum(m_prev, jnp.max(pr, axis=0) + shift_row)
            p = jnp.exp2(pr - (m_new - shift_row)[None])
            psum = jnp.sum(p, axis=0)
        else:
            m_new = jnp.maximum(m_prev, pr + shift_row)
            p = jnp.exp2(pr - (m_new - shift_row))
            psum = p
        alpha = jnp.exp2(m_prev - m_new)
        l_scr[...] = alpha * l_scr[...] + psum
        m_scr[...] = m_new
        a_scr[...] = alpha
        if pr.ndim == 3:
            p_scr[...] = p
        else:
            p_scr[0] = p
        for par in range(2):
            ae = a_scr[pl.ds(par, nh, stride=2), :]
            if pr.ndim == 3:
                pe = p_scr[:, pl.ds(par, nh, stride=2), :]
                pv = jnp.sum(jnp.concatenate([pe, pe], axis=-1) * values(), axis=0)
            else:
                pe = p_scr[0, pl.ds(par, nh, stride=2), :]
                pv = jnp.concatenate([pe, pe], axis=-1) * values()
            acc_scr[par] = jnp.concatenate([ae, ae], axis=-1) * acc_scr[par] + pv

    def page_logits(j):
        prod = kp_refs[j][...] * qs[None]
        return lane_sum(prod.reshape(PAGE_SIZE * 2 * nh, dh)).reshape(PAGE_SIZE, 2 * nh, LANES)

    far = tbl_ref[N_BUCKETS - 1] * LOG2E
    zero = jnp.zeros_like(far)
    for j in range(pp - 1):
        accumulate(page_logits(j), far, lambda j=j: vp_refs[j][...])

    @pl.when(step < n_steps - 1)
    def _():
        accumulate(page_logits(pp - 1), far, lambda: vp_refs[pp - 1][...])

    @pl.when(step == n_steps - 1)
    def _():
        accumulate(page_logits(pp - 1) + bias_scr[...], zero, lambda: vp_refs[pp - 1][...])
        accumulate(lane_sum(kn_ref[...] * qs) + tbl_ref[0] * LOG2E, zero, lambda: vn_ref[...])
        l_scr_v = l_scr[...]
        a_scr[...] = l_scr_v
        outs = []
        for par in range(2):
            le = a_scr[pl.ds(par, nh, stride=2), :]
            outs.append(acc_scr[par] / jnp.concatenate([le, le], axis=-1))
        out = outs[0] - _lambda(lam_ref, lam_init) * outs[1]
        o_ref[...] = _rms(out, sub_ref[...]) * (1.0 - lam_init)


def _decode_diff(q, k_new, v_new, cache_k, cache_v, layer, page_table, rel_bias, lambdas, subln, lam_init, pp):
    b, n_pages = page_table.shape
    nh, dh = DIFF_HEADS, DIFF_HEAD_DIM
    pp = min(pp, n_pages)
    assert n_pages % pp == 0 and PAGE_SIZE >= MAX_DISTANCE and dh == LANES
    n_steps = n_pages // pp
    bkt_last = jnp.asarray(np.broadcast_to(
        _t5_bucket(PAGE_SIZE - np.arange(PAGE_SIZE)).reshape(PAGE_SIZE, 1, 1), (PAGE_SIZE, 2 * nh, LANES)))
    tbl = jnp.broadcast_to(rel_bias[:, :, None], (N_BUCKETS, 2 * nh, LANES))
    page_spec = lambda j, width, lanes: pl.BlockSpec(
        (None, None, PAGE_SIZE, width, lanes), lambda bb, s, pt: (layer, pt[bb, s * pp + j], 0, 0, 0))
    full = lambda shape: pl.BlockSpec(shape, lambda bb, s, pt: (0,) * len(shape))
    per_seq = lambda shape: pl.BlockSpec((None,) + shape, lambda bb, s, pt: (bb,) + (0,) * len(shape))
    grid_spec = pltpu.PrefetchScalarGridSpec(
        num_scalar_prefetch=1, grid=(b, n_steps),
        in_specs=([per_seq((2 * nh, dh))]
                  + [page_spec(j, 2 * nh, dh) for j in range(pp)]
                  + [page_spec(j, nh, 2 * dh) for j in range(pp)]
                  + [per_seq((2 * nh, dh)), per_seq((nh, 2 * dh)),
                     full((PAGE_SIZE, 2 * nh, LANES)), full((N_BUCKETS, 2 * nh, LANES)),
                     full((4, dh)), full((1, 2 * dh))]),
        out_specs=per_seq((nh, 2 * dh)),
        scratch_shapes=[pltpu.VMEM((2 * nh, LANES), F32), pltpu.VMEM((2 * nh, LANES), F32),
                        pltpu.VMEM((2, nh, 2 * dh), F32), pltpu.VMEM((PAGE_SIZE, 2 * nh, LANES), F32),
                        pltpu.VMEM((PAGE_SIZE, 2 * nh, LANES), F32), pltpu.VMEM((2 * nh, LANES), F32)])
    return pl.pallas_call(
        functools.partial(_decode_kernel, pp=pp, n_steps=n_steps, lam_init=lam_init),
        grid_spec=grid_spec,
        out_shape=jax.ShapeDtypeStruct((b, nh, 2 * dh), F32),
        compiler_params=_params("parallel", "arbitrary"), name="decode_diff")(
            page_table, q, *([cache_k] * pp), *([cache_v] * pp), k_new, v_new, bkt_last, tbl, lambdas, subln)


class _Tiles:
    def __init__(self, prompt):
        self.prompt = prompt
        self.act_dtype = BF16 if prompt else F32
        self.tm = 1024 if prompt else SUBLANES
        self.tn = 1024
        self.tn_ff = 512
        self.tm_res = 256 if prompt else SUBLANES
        self.tk_res = 512
        self.tt_prep = 256 if prompt else SUBLANES
        self.tc_prep = 1024
        self.cb = 4 if prompt else 1
        self.gdn_hpb = 2
        self.tb_attn = 512
        self.decode_pages = 4


def _group_trunk(x3, tiles, gdn_s0, gdn_c0, gla_s0, cache_k, cache_v, page_table, p, wb):
    b, t, d = x3.shape
    m = b * t
    x = x3.reshape(m, d)
    if not tiles.prompt:
        assert t == 1 and m % SUBLANES == 0
    t_pad = t if tiles.prompt else CHUNK
    t_valid = None if tiles.prompt else t
    nw = lambda i, jn: p['norm_w'][i, jn].reshape(1, d)
    adt = tiles.act_dtype

    def pad_time(a3, to):
        return a3 if a3.shape[1] == to else jnp.pad(a3, ((0, 0), (0, to - a3.shape[1]), (0, 0)))

    def big_matmul(a, name, layer):
        if tiles.prompt:
            return _matmul(a, wb[name, layer], layer, F32, tiles.tm, tiles.tn)
        out, wb[name, layer] = _matmul(a, p[name], layer, F32, tiles.tm, tiles.tn, emit_wb=True)
        return out

    def residual_matmul(a, name, layer, x_in, nw_post, nw_next):
        if tiles.prompt:
            return _matmul_residual(a, wb[name, layer], layer, x_in, nw_post, nw_next, adt,
                                    tiles.tm_res, tiles.tk_res)
        xo, h_next, wb[name, layer] = _matmul_residual(a, p[name], layer, x_in, nw_post, nw_next, adt,
                                                       tiles.tm_res, tiles.tk_res, emit_wb=True)
        return xo, h_next

    def ffn_up(a, layer):
        if tiles.prompt:
            return _swiglu_up(a, wb['ffn_gate', layer], wb['ffn_up', layer], layer, adt, tiles.tm, tiles.tn_ff)
        act_, wb['ffn_gate', layer], wb['ffn_up', layer] = _swiglu_up(
            a, p['ffn_w_up'], p['ffn_w_up'], layer, adt, tiles.tm, tiles.tn_ff, emit_wb=True)
        return act_

    gdn_s, gdn_c, gla_s, att_k, att_v = [], [], [], [], []
    h = _rmsnorm(x, nw(0, 0), adt, tiles.tm)
    for i in range(DEPTH):
        j = i // N_MIXERS
        kind = i % N_MIXERS
        if kind == 0:
            proj = big_matmul(h, 'gdn_w_in', j)
            beta, g = _gdn_gates(h, p['gdn_w_ba'], j, p['gdn_a_log'][j].reshape(1, -1),
                                 p['gdn_dt_bias'][j].reshape(1, -1), tiles.tm)
            proj3 = proj.reshape(b, t, -1)
            buf0 = gdn_c0[j]
            nbuf = GDN_CONV_W - 1
            if t >= nbuf:
                new_buf = proj3[:, t - nbuf:, :GDN_CONV_DIM]
            else:
                new_buf = jnp.concatenate([buf0[:, t:], proj3[..., :GDN_CONV_DIM]], axis=1)
            buf8 = jnp.pad(buf0, ((0, 0), (SUBLANES - (GDN_CONV_W - 1), 0), (0, 0)))
            t8 = max(t, SUBLANES)
            proj3p = pad_time(proj3, t8)
            qkv3 = pad_time(_gdn_prep(proj3p, buf8, p['gdn_conv_w'][j], tiles.tt_prep, tiles.tc_prep), t_pad)
            proj3p = pad_time(proj3p, t_pad)

            def heads(a):
                a = pad_time(a.reshape(b, t, GDN_QK_HEADS * 2), t_pad)
                return a.reshape(b, t_pad, GDN_QK_HEADS, 2).transpose(0, 2, 1, 3)
            gcol, bcol = heads(g), heads(beta)
            o3, s_new = _gdn_delta(qkv3, proj3p, gcol, bcol, gcol.transpose(0, 1, 3, 2), gdn_s0[j],
                                   p['gdn_norm_w'][j].reshape(1, -1), adt, tiles.cb, tiles.gdn_hpb, t_valid)
            mix_in, w_out = o3[:, :t].reshape(m, -1), 'gdn_w_out'
            gdn_s.append(s_new)
            gdn_c.append(new_buf)
        elif kind == 1:
            qkvg = big_matmul(h, 'gla_w_qkvg', j)
            low = _matmul(h, p['gla_w_gk1'], j, F32, tiles.tm, tiles.tn)
            gk = _matmul(low, p['gla_w_gk2'], j, F32, tiles.tm, tiles.tn)
            o3, s_new = _gla_chunks(pad_time(qkvg.reshape(b, t, -1), t_pad), pad_time(gk.reshape(b, t, -1), t_pad),
                                    p['gla_b_gk'][j].reshape(1, -1), gla_s0[j],
                                    p['gla_norm_w'][j].reshape(1, -1), adt, tiles.cb, t_valid)
            mix_in, w_out = o3[:, :t].reshape(m, -1), 'gla_w_out'
            gla_s.append(s_new)
        else:
            lam_init = 0.8 - 0.6 * math.exp(-0.3 * i)
            qkv = big_matmul(h, 'diff_w_qkv', j)
            nh, dh = DIFF_HEADS, DIFF_HEAD_DIM
            k_new = qkv[:, D_MODEL:2 * D_MODEL].reshape(b, t, 2 * nh, dh)
            v_new = qkv[:, 2 * D_MODEL:].reshape(b, t, nh, 2 * dh)
            subln = p['diff_subln'][j].reshape(1, -1)
            if tiles.prompt:
                o3 = _flash_diff(qkv.reshape(b, t, -1), p['rel_bias'], p['diff_lambda'][j], subln,
                                 lam_init, adt, tiles.tb_attn)
                mix_in = o3.reshape(m, -1)
            else:
                o = _decode_diff(qkv[:, :D_MODEL].reshape(b, 2 * nh, dh), k_new.reshape(b, 2 * nh, dh),
                                 v_new.reshape(b, nh, 2 * dh), cache_k, cache_v, j, page_table,
                                 p['rel_bias'], p['diff_lambda'][j], subln, lam_init, tiles.decode_pages)
                mix_in = o.reshape(m, -1)
            w_out = 'diff_w_out'
            att_k.append(k_new)
            att_v.append(v_new)
        x, h = residual_matmul(mix_in, w_out, j, x, nw(i, 1), nw(i, 2))
        act = ffn_up(h, i)
        nxt = nw(i + 1, 0) if i + 1 < DEPTH else None
        x, h = residual_matmul(act, 'ffn_w_down', i, x, nw(i, 3), nxt)
    return (x.reshape(b, t, d), jnp.stack(gdn_s), jnp.stack(gdn_c), jnp.stack(gla_s),
            jnp.stack(att_k), jnp.stack(att_v))


def kernel(x_prompt, x_sample, state_gdn, state_gdn_conv, state_gla, cache_k, cache_v, page_table,
           norm_w, ffn_w_up, ffn_w_down, rel_bias,
           gdn_w_in, gdn_w_ba, gdn_conv_w, gdn_a_log, gdn_dt_bias, gdn_norm_w, gdn_w_out,
           gla_w_qkvg, gla_w_gk1, gla_w_gk2, gla_b_gk, gla_norm_w, gla_w_out,
           diff_w_qkv, diff_lambda, diff_subln, diff_w_out):
    p = dict(norm_w=norm_w, ffn_w_up=ffn_w_up, ffn_w_down=ffn_w_down, rel_bias=rel_bias,
             gdn_w_in=gdn_w_in, gdn_w_ba=gdn_w_ba, gdn_conv_w=gdn_conv_w, gdn_a_log=gdn_a_log,
             gdn_dt_bias=gdn_dt_bias, gdn_norm_w=gdn_norm_w, gdn_w_out=gdn_w_out,
             gla_w_qkvg=gla_w_qkvg, gla_w_gk1=gla_w_gk1, gla_w_gk2=gla_w_gk2, gla_b_gk=gla_b_gk,
             gla_norm_w=gla_norm_w, gla_w_out=gla_w_out,
             diff_w_qkv=diff_w_qkv, diff_lambda=diff_lambda, diff_subln=diff_subln, diff_w_out=diff_w_out)
    bp = x_prompt.shape[0]
    n_gdn, n_gla = state_gdn.shape[0], state_gla.shape[0]
    zeros_gdn = jnp.zeros((n_gdn, bp) + state_gdn.shape[2:], F32)
    zeros_conv = jnp.zeros((n_gdn, bp) + state_gdn_conv.shape[2:], F32)
    zeros_gla = jnp.zeros((n_gla, bp) + state_gla.shape[2:], F32)
    wb = {}
    outs_s = _group_trunk(x_sample, _Tiles(False), state_gdn, state_gdn_conv, state_gla,
                          cache_k, cache_v, page_table, p, wb)
    outs_p = _group_trunk(x_prompt, _Tiles(True), zeros_gdn, zeros_conv, zeros_gla, None, None, None, p, wb)
    return (outs_p[0], outs_s[0]) + outs_p[1:] + outs_s[1:]
```

```python
import functools
import math

import numpy as np
import jax
import jax.numpy as jnp
from jax import lax
from jax.experimental import pallas as pl
from jax.experimental.pallas import tpu as pltpu

F32 = jnp.float32
BF16 = jnp.bfloat16

D_MODEL = 2048
DEPTH = 4
PAGE_SIZE = 128
N_MIXERS = 3
GDN_HEAD_DIM = 128
GDN_QK_HEADS = D_MODEL // 128
GDN_V_HEADS = 2 * GDN_QK_HEADS
GDN_QK_DIM = GDN_QK_HEADS * GDN_HEAD_DIM
GDN_V_DIM = GDN_V_HEADS * GDN_HEAD_DIM
GDN_CONV_DIM = 2 * GDN_QK_DIM + GDN_V_DIM
GDN_CONV_W = 4
GLA_HEADS = 4
GLA_DK = D_MODEL // 2 // GLA_HEADS
GLA_DV = D_MODEL // GLA_HEADS
GLA_GATE_NORMALIZER = 16.0
DIFF_HEAD_DIM = 128
DIFF_HEADS = D_MODEL // (2 * DIFF_HEAD_DIM)
N_BUCKETS = 32
MAX_DISTANCE = 128
D_FF = ((8 * D_MODEL + 3 * 256 - 1) // (3 * 256)) * 256
NORM_EPS = 1e-6

LANES = 128
SUBLANES = 8
CHUNK = 64
SUB = 16
VMEM_LIMIT_BYTES = 48 * 1024 * 1024
VMEM_LIMIT_BIG_BYTES = 56 * 1024 * 1024
NEG = -1e30
LOG2E = math.log2(math.e)


def _params(*sem, vmem=VMEM_LIMIT_BYTES):
    return pltpu.CompilerParams(dimension_semantics=sem, vmem_limit_bytes=vmem)


def _dot(a, b):
    return jnp.dot(a.astype(BF16), b.astype(BF16), preferred_element_type=F32)


def _dot_nt(a, b):
    return lax.dot_general(a.astype(BF16), b.astype(BF16), (((1,), (1,)), ((), ())),
                           preferred_element_type=F32)


def _dot_tn(a, b):
    return lax.dot_general(a.astype(BF16), b.astype(BF16), (((0,), (0,)), ((), ())),
                           preferred_element_type=F32)


def _split3(x):
    hi = x.astype(BF16)
    r = x - hi.astype(F32)
    mid = r.astype(BF16)
    lo = (r - mid.astype(F32)).astype(BF16)
    return hi, mid, lo


def _sigmoid(x):
    return 0.5 * jnp.tanh(0.5 * x) + 0.5


def _silu(x):
    h = 0.5 * x
    return h + h * jnp.tanh(h)


def _softplus(x):
    return jnp.maximum(x, 0.0) + jnp.log(1.0 + jnp.exp(-jnp.abs(x)))


def _rms(x, w):
    return x * lax.rsqrt(jnp.mean(x * x, axis=-1, keepdims=True) + NORM_EPS) * w


def _chunk_tril(r):
    i = lax.broadcasted_iota(jnp.int32, (r, r), 0)
    j = lax.broadcasted_iota(jnp.int32, (r, r), 1)
    return jnp.where((j <= i) & ((i // CHUNK) == (j // CHUNK)), 1.0, 0.0).astype(BF16)


def _rmsnorm_kernel(x_ref, w_ref, o_ref):
    o_ref[...] = _rms(x_ref[...], w_ref[...]).astype(o_ref.dtype)


def _rmsnorm(x, w, out_dtype, tm):
    m, d = x.shape
    tm = min(tm, m)
    return pl.pallas_call(
        _rmsnorm_kernel, grid=(m // tm,),
        in_specs=[pl.BlockSpec((tm, d), lambda i: (i, 0)), pl.BlockSpec((1, d), lambda i: (0, 0))],
        out_specs=pl.BlockSpec((tm, d), lambda i: (i, 0)),
        out_shape=jax.ShapeDtypeStruct((m, d), out_dtype),
        compiler_params=_params("parallel"), name="rmsnorm")(x, w)


def _weight_spec(w, layer, block, index):
    if w.ndim == 2:
        return pl.BlockSpec(block, index)
    return pl.BlockSpec((None,) + block, lambda *g: (layer,) + index(*g))


def _mm_kernel(a_ref, w_ref, o_ref, *wb_ref):
    w = w_ref[...].astype(BF16)
    if wb_ref:
        wb_ref[0][...] = w
    o_ref[...] = _dot(a_ref[...], w).astype(o_ref.dtype)


def _matmul(a, w, layer, out_dtype, tm, tn, emit_wb=False):
    m, k = a.shape
    n = w.shape[-1]
    tm, tn = min(tm, m), min(tn, n)
    assert m % tm == 0 and n % tn == 0 and (not emit_wb or m == tm)
    out_shape = [jax.ShapeDtypeStruct((m, n), out_dtype)]
    out_specs = [pl.BlockSpec((tm, tn), lambda i, j: (i, j))]
    if emit_wb:
        out_shape.append(jax.ShapeDtypeStruct((k, n), BF16))
        out_specs.append(pl.BlockSpec((k, tn), lambda i, j: (0, j)))
    res = pl.pallas_call(
        _mm_kernel, grid=(m // tm, n // tn),
        in_specs=[pl.BlockSpec((tm, k), lambda i, j: (i, 0)),
                  _weight_spec(w, layer, (k, tn), lambda i, j: (0, j))],
        out_specs=out_specs, out_shape=out_shape,
        compiler_params=_params("parallel", "parallel"), name="matmul")(a, w)
    return (res[0], res[1]) if emit_wb else res[0]


def _swiglu_kernel(a_ref, wg_ref, wu_ref, o_ref, *wb_refs):
    a = a_ref[...].astype(BF16)
    wg = wg_ref[...].astype(BF16)
    wu = wu_ref[...].astype(BF16)
    if wb_refs:
        wb_refs[0][...] = wg
        wb_refs[1][...] = wu
    g = _dot(a, wg)
    u = _dot(a, wu)
    o_ref[...] = (_silu(g) * u).astype(o_ref.dtype)


def _swiglu_up(a, w_gate, w_up, layer, out_dtype, tm, tn, emit_wb=False):
    m, k = a.shape
    stacked = w_up.ndim == 3
    f = w_up.shape[-1] // 2 if stacked else w_up.shape[-1]
    tm, tn = min(tm, m), min(tn, f)
    assert m % tm == 0 and f % tn == 0 and (not emit_wb or m == tm)
    nf = f // tn
    up_off = nf if stacked else 0
    out_shape = [jax.ShapeDtypeStruct((m, f), out_dtype)]
    out_specs = [pl.BlockSpec((tm, tn), lambda i, j: (i, j))]
    if emit_wb:
        out_shape += [jax.ShapeDtypeStruct((k, f), BF16)] * 2
        out_specs += [pl.BlockSpec((k, tn), lambda i, j: (0, j))] * 2
    res = pl.pallas_call(
        _swiglu_kernel, grid=(m // tm, nf),
        in_specs=[pl.BlockSpec((tm, k), lambda i, j: (i, 0)),
                  _weight_spec(w_gate, layer, (k, tn), lambda i, j: (0, j)),
                  _weight_spec(w_up, layer, (k, tn), lambda i, j: (0, j + up_off))],
        out_specs=out_specs, out_shape=out_shape,
        compiler_params=_params("parallel", "parallel"), name="swiglu_up")(a, w_gate, w_up)
    return tuple(res) if emit_wb else res[0]


def _residual_epilogue(mix, x_ref, nw1_ref, nw2_ref, xo_ref, h_ref):
    xn = x_ref[...] + _rms(mix, nw1_ref[...])
    xo_ref[...] = xn
    if h_ref:
        h_ref[0][...] = _rms(xn, nw2_ref[...]).astype(h_ref[0].dtype)


def _mm_res_kernel(a_ref, w_ref, x_ref, nw1_ref, nw2_ref, xo_ref, *rest, nk, emit_h, emit_wb):
    h_ref = rest[:1] if emit_h else ()
    kk = pl.program_id(1)
    w = w_ref[...].astype(BF16)
    if emit_wb:
        rest[-1][...] = w
    part = _dot(a_ref[...], w)

    @pl.when(kk == 0)
    def _():
        xo_ref[...] = part

    @pl.when(kk > 0)
    def _():
        xo_ref[...] += part

    @pl.when(kk == nk - 1)
    def _():
        _residual_epilogue(xo_ref[...], x_ref, nw1_ref, nw2_ref, xo_ref, h_ref)


def _mm_res_resident_kernel(a_ref, w_ref, x_ref, nw1_ref, nw2_ref, xo_ref, *h_ref):
    _residual_epilogue(_dot(a_ref[...], w_ref[...]), x_ref, nw1_ref, nw2_ref, xo_ref, h_ref)


def _matmul_residual(a, w, layer, x, nw_post, nw_next, h_dtype, tm, tk, emit_wb=False):
    m, k = a.shape
    n = w.shape[-1]
    tm = min(tm, m)
    assert m % tm == 0 and (not emit_wb or m == tm)
    emit_h = nw_next is not None
    if not emit_h:
        nw_next = nw_post
    resident = w.ndim == 2
    if resident:
        grid = (m // tm,)
        row = lambda i: (i, 0)
        fixed = lambda i: (0, 0)
        a_spec = pl.BlockSpec((tm, k), row)
        w_spec = pl.BlockSpec((k, n), fixed, pipeline_mode=pl.Buffered(1))
        x_spec = pl.BlockSpec((tm, n), row)
        body = _mm_res_resident_kernel
        sem = ("parallel",)
    else:
        tk = min(tk, k)
        assert k % tk == 0
        nk = k // tk
        grid = (m // tm, nk)
        row = lambda i, kk: (i, 0)
        fixed = lambda i, kk: (0, 0)
        a_spec = pl.BlockSpec((tm, tk), lambda i, kk: (i, kk))
        w_spec = pl.BlockSpec((None, tk, n), lambda i, kk: (layer, kk, 0))
        x_spec = pl.BlockSpec((tm, n), row, pipeline_mode=pl.Buffered(1))
        body = functools.partial(_mm_res_kernel, nk=nk, emit_h=emit_h, emit_wb=emit_wb)
        sem = ("parallel", "arbitrary")
    out_shape = [jax.ShapeDtypeStruct((m, n), F32)]
    out_specs = [pl.BlockSpec((tm, n), row)]
    if emit_h:
        out_shape.append(jax.ShapeDtypeStruct((m, n), h_dtype))
        out_specs.append(pl.BlockSpec((tm, n), row))
    if emit_wb:
        out_shape.append(jax.ShapeDtypeStruct((k, n), BF16))
        out_specs.append(pl.BlockSpec((tk, n), lambda i, kk: (kk, 0)))
    res = pl.pallas_call(
        body, grid=grid,
        in_specs=[a_spec, w_spec, x_spec, pl.BlockSpec((1, n), fixed), pl.BlockSpec((1, n), fixed)],
        out_specs=out_specs, out_shape=out_shape,
        compiler_params=_params(*sem, vmem=VMEM_LIMIT_BIG_BYTES), name="matmul_residual")(
            a, w, x, nw_post, nw_next)
    xo = res[0]
    h = res[1] if emit_h else None
    return (xo, h, res[-1]) if emit_wb else (xo, h)


def _gdn_gates_kernel(a_ref, w_ref, alog_ref, dtb_ref, beta_ref, g_ref):
    ba = _dot(a_ref[...], w_ref[...])
    nh = beta_ref.shape[-1]
    beta_ref[...] = _sigmoid(ba[:, :nh])
    g_ref[...] = -jnp.exp(alog_ref[...]) * _softplus(ba[:, nh:] + dtb_ref[...])


def _gdn_gates(a, w_ba, layer, a_log, dt_bias, tm):
    m, k = a.shape
    nh = GDN_V_HEADS
    tm = min(tm, m)
    return pl.pallas_call(
        _gdn_gates_kernel, grid=(m // tm,),
        in_specs=[pl.BlockSpec((tm, k), lambda i: (i, 0)),
                  pl.BlockSpec((None, k, 2 * nh), lambda i: (layer, 0, 0)),
                  pl.BlockSpec((1, nh), lambda i: (0, 0)),
                  pl.BlockSpec((1, nh), lambda i: (0, 0))],
        out_specs=[pl.BlockSpec((tm, nh), lambda i: (i, 0))] * 2,
        out_shape=[jax.ShapeDtypeStruct((m, nh), F32)] * 2,
        compiler_params=_params("parallel"), name="gdn_gates")(a, w_ba, a_log, dt_bias)


def _gdn_prep_kernel(x_ref, prev_ref, buf_ref, cw_ref, o_ref, ext_scr, *, n_qk_tiles):
    i = pl.program_id(1)
    j = pl.program_id(2)
    tt, tc = x_ref.shape
    ext_scr[:SUBLANES, :] = jnp.where(i == 0, buf_ref[...], prev_ref[...])
    ext_scr[SUBLANES:, :] = x_ref[...]
    y = x_ref[...] * cw_ref[GDN_CONV_W - 1:GDN_CONV_W, :]
    for s in range(1, GDN_CONV_W):
        y = y + ext_scr[SUBLANES - s:SUBLANES - s + tt, :] * cw_ref[GDN_CONV_W - 1 - s:GDN_CONV_W - s, :]
    y = _silu(y)
    qscale = jnp.where(j < n_qk_tiles, GDN_HEAD_DIM ** -0.5, 1.0)
    pieces = []
    for hh in range(tc // GDN_HEAD_DIM):
        yh = y[:, hh * GDN_HEAD_DIM:(hh + 1) * GDN_HEAD_DIM]
        r = lax.rsqrt(jnp.sum(yh * yh, axis=-1, keepdims=True) + NORM_EPS) * qscale
        pieces.append(yh * jnp.where(j < 2 * n_qk_tiles, r, 1.0))
    o_ref[...] = jnp.concatenate(pieces, axis=1)


def _gdn_prep(proj3, buf8, conv_w, tt, tc):
    b, t, _ = proj3.shape
    tt = min(tt, t)
    assert t % tt == 0 and tt % SUBLANES == 0 and GDN_QK_DIM % tc == 0
    hb = tt // SUBLANES
    return pl.pallas_call(
        functools.partial(_gdn_prep_kernel, n_qk_tiles=GDN_QK_DIM // tc),
        grid=(b, t // tt, GDN_CONV_DIM // tc),
        in_specs=[pl.BlockSpec((None, tt, tc), lambda bb, i, j: (bb, i, j)),
                  pl.BlockSpec((None, SUBLANES, tc), lambda bb, i, j: (bb, jnp.maximum(i * hb - 1, 0), j)),
                  pl.BlockSpec((None, SUBLANES, tc), lambda bb, i, j: (bb, 0, j)),
                  pl.BlockSpec((GDN_CONV_W, tc), lambda bb, i, j: (0, j))],
        out_specs=pl.BlockSpec((None, tt, tc), lambda bb, i, j: (bb, i, j)),
        out_shape=jax.ShapeDtypeStruct((b, t, GDN_CONV_DIM), F32),
        scratch_shapes=[pltpu.VMEM((SUBLANES + tt, tc), F32)],
        compiler_params=_params("parallel", "parallel", "parallel"), name="gdn_prep")(
            proj3, proj3, buf8, conv_w)


def _gdn_delta_kernel(q_ref, k_ref, v_ref, z_ref, gc_ref, bc_ref, gr_ref, s0_ref, nw_ref,
                      o_ref, so_ref, s_scr, *, cb, hpb, t_valid, n_blocks):
    n = pl.program_id(2)
    r = cb * CHUNK
    dh = GDN_HEAD_DIM

    @pl.when(n == 0)
    def _():
        s_scr[...] = s0_ref[...]

    q = q_ref[...]
    k = k_ref[...]
    v = v_ref[...]
    gcol = gc_ref[...]
    bcol = bc_ref[...]
    grow = gr_ref[...]
    if t_valid is not None:
        okc = (n * r + lax.broadcasted_iota(jnp.int32, (r, 1), 0)) < t_valid
        okr = (n * r + lax.broadcasted_iota(jnp.int32, (1, r), 1)) < t_valid
        k = jnp.where(okc, k, 0.0)
        v = jnp.where(okc, v, 0.0)
        gcol = jnp.where(okc, gcol, 0.0)
        bcol = jnp.where(okc, bcol, 0.0)
        grow = jnp.where(okr, grow, 0.0)

    lm = _chunk_tril(r)
    gcum_col = [sum(_dot(lm, p) for p in _split3(gcol[qh])) for qh in range(hpb)]
    gcum_row = [sum(_dot_nt(p, lm) for p in _split3(grow[qh])) for qh in range(hpb)]

    ri = lax.broadcasted_iota(jnp.int32, (r, r), 0)
    ci = lax.broadcasted_iota(jnp.int32, (r, r), 1)
    eye = jnp.where(ri == ci, 1.0, 0.0)
    tril = ((ri // CHUNK) == (ci // CHUNK)) & (ri >= ci)
    lvl_masks = []
    size = 1
    while size < CHUNK:
        lvl_masks.append(((ri // (2 * size)) == (ci // (2 * size)))
                         & ((ri // size) % 2 == 1) & ((ci // size) % 2 == 0))
        size *= 2
    if t_valid is not None and t_valid <= 1:
        lvl_masks = []

    nw = nw_ref[...]
    heads = [(qh, hh) for qh in range(hpb) for hh in range(2)]
    kq = [k[:, qh * dh:(qh + 1) * dh] for qh in range(hpb)]
    qq = [q[:, qh * dh:(qh + 1) * dh] for qh in range(hpb)]
    gram = [_dot_nt(kq[qh], kq[qh]) for qh in range(hpb)]
    qk = [_dot_nt(qq[qh], kq[qh]) for qh in range(hpb)]
    gc = [gcum_col[qh][:, hh:hh + 1] for qh, hh in heads]
    bc = [bcol[qh][:, hh:hh + 1] for qh, hh in heads]
    decay = [jnp.exp(jnp.where(tril, gc[i] - gcum_row[qh][hh:hh + 1, :], NEG)) for i, (qh, hh) in enumerate(heads)]
    mm_ = [gram[qh] * bc[i] * decay[i] for i, (qh, hh) in enumerate(heads)]
    tinv = [eye for _ in heads]
    for lvl, msk in enumerate(lvl_masks):
        ml = [jnp.where(msk, m, 0.0) for m in mm_]
        if lvl == 0:
            tinv = [t - m for t, m in zip(tinv, ml)]
        else:
            y = [_dot(m, t) for t, m in zip(tinv, ml)]
            tinv = [t - _dot(t, yy) for t, yy in zip(tinv, y)]
    eg = [jnp.exp(g_) for g_ in gc]
    sol = [_dot(tinv[i], jnp.concatenate([v[:, (2 * qh + hh) * dh:(2 * qh + hh + 1) * dh] * bc[i],
                                          kq[qh] * (bc[i] * eg[i])], axis=1))
           for i, (qh, hh) in enumerate(heads)]
    asol = [_dot(qk[qh] * decay[i], sol[i]) for i, (qh, hh) in enumerate(heads)]
    o2 = [a[:, :dh] for a in asol]
    o1 = [qq[qh] * eg[i] - asol[i][:, dh:] for i, (qh, hh) in enumerate(heads)]
    glast = [[g_[(c + 1) * CHUNK - 1:(c + 1) * CHUNK, :] for c in range(cb)] for g_ in gc]
    kd = [kq[qh] * jnp.exp(jnp.concatenate([jnp.broadcast_to(gl, (CHUNK, 1)) for gl in glast[i]], axis=0) - gc[i])
          for i, (qh, hh) in enumerate(heads)]
    qw = [[_dot_tn(kd[i][c * CHUNK:(c + 1) * CHUNK], sol[i][c * CHUNK:(c + 1) * CHUNK]) for c in range(cb)]
          for i in range(len(heads))]
    s = [s_scr[2 * qh + hh] for qh, hh in heads]
    for c in range(cb):
        sl = slice(c * CHUNK, (c + 1) * CHUNK)
        for i, (qh, hh) in enumerate(heads):
            x = _dot(jnp.concatenate([qw[i][c][:, dh:], o1[i][sl]], axis=0), s[i])
            o = x[dh:] + o2[i][sl]
            s[i] = s[i] * jnp.exp(glast[i][c]) - x[:dh] + qw[i][c][:, :dh]
            col = slice((2 * qh + hh) * dh, (2 * qh + hh + 1) * dh)
            zc = z_ref[sl, col]
            o_ref[sl, col] = (_rms(o, nw) * _silu(zc)).astype(o_ref.dtype)
    for i, (qh, hh) in enumerate(heads):
        s_scr[2 * qh + hh] = s[i]

    @pl.when(n == n_blocks - 1)
    def _():
        so_ref[...] = s_scr[...]


def _gdn_delta(qkv3, proj3, gcol, bcol, grow, s0, norm_w, out_dtype, cb, hpb, t_valid):
    b, t, _ = qkv3.shape
    r = cb * CHUNK
    assert t % r == 0
    nb = t // r
    dh = GDN_HEAD_DIM
    qw_, vw_ = hpb * dh, 2 * hpb * dh
    assert GDN_QK_HEADS % hpb == 0
    koff = GDN_QK_DIM // qw_
    voff = 2 * GDN_QK_DIM // vw_
    zoff = GDN_CONV_DIM // vw_
    return pl.pallas_call(
        functools.partial(_gdn_delta_kernel, cb=cb, hpb=hpb, t_valid=t_valid, n_blocks=nb),
        grid=(b, GDN_QK_HEADS // hpb, nb),
        in_specs=[pl.BlockSpec((None, r, qw_), lambda bb, h, n: (bb, n, h)),
                  pl.BlockSpec((None, r, qw_), lambda bb, h, n: (bb, n, koff + h)),
                  pl.BlockSpec((None, r, vw_), lambda bb, h, n: (bb, n, voff + h)),
                  pl.BlockSpec((None, r, vw_), lambda bb, h, n: (bb, n, zoff + h)),
                  pl.BlockSpec((None, hpb, r, 2), lambda bb, h, n: (bb, h, n, 0)),
                  pl.BlockSpec((None, hpb, r, 2), lambda bb, h, n: (bb, h, n, 0)),
                  pl.BlockSpec((None, hpb, 2, r), lambda bb, h, n: (bb, h, 0, n)),
                  pl.BlockSpec((None, 2 * hpb, dh, dh), lambda bb, h, n: (bb, h, 0, 0)),
                  pl.BlockSpec((1, dh), lambda bb, h, n: (0, 0))],
        out_specs=[pl.BlockSpec((None, r, vw_), lambda bb, h, n: (bb, n, h)),
                   pl.BlockSpec((None, 2 * hpb, dh, dh), lambda bb, h, n: (bb, h, 0, 0))],
        out_shape=[jax.ShapeDtypeStruct((b, t, GDN_V_DIM), out_dtype),
                   jax.ShapeDtypeStruct(s0.shape, F32)],
        scratch_shapes=[pltpu.VMEM((2 * hpb, dh, dh), F32)],
        compiler_params=_params("parallel", "parallel", "arbitrary"), name="gdn_delta")(
            qkv3, qkv3, qkv3, proj3, gcol, bcol, grow, s0, norm_w)


def _gla_kernel(q_ref, k_ref, v_ref, gate_ref, gk_ref, bgk_ref, s0_ref, nw_ref,
                o_ref, so_ref, s_scr, *, cb, t_valid, n_blocks):
    n = pl.program_id(2)
    r = cb * CHUNK
    dk, dv = GLA_DK, GLA_DV

    @pl.when(n == 0)
    def _():
        s_scr[...] = s0_ref[...]

    xg = gk_ref[...] + bgk_ref[...]
    g = (jnp.minimum(xg, 0.0) - jnp.log(1.0 + jnp.exp(-jnp.abs(xg)))) * (1.0 / GLA_GATE_NORMALIZER)
    q = q_ref[...] * GLA_DK ** -0.5
    k = k_ref[...]
    v = v_ref[...]
    if t_valid is not None:
        okc = (n * r + lax.broadcasted_iota(jnp.int32, (r, 1), 0)) < t_valid
        g = jnp.where(okc, g, 0.0)
        k = jnp.where(okc, k, 0.0)
        v = jnp.where(okc, v, 0.0)

    gparts = _split3(g)
    bcum = sum(_dot(_chunk_tril(r), p) for p in gparts)
    ones = jnp.ones((CHUNK, LANES), BF16)
    row = lax.broadcasted_iota(jnp.int32, (CHUNK, 1), 0)
    jrow = lax.broadcasted_iota(jnp.int32, (SUB, 1), 0)
    lane = lax.broadcasted_iota(jnp.int32, (SUB, CHUNK), 1)
    nw = nw_ref[...]
    s = s_scr[...]
    for c in range(cb):
        sl = slice(c * CHUNK, (c + 1) * CHUNK)
        qc, kc, vc, bc = q[sl], k[sl], v[sl], bcum[sl]
        blast = bc[CHUNK - 1:CHUNK, :]
        attn_t = jnp.zeros((CHUNK, CHUNK), F32)
        for bi in range(1, CHUNK // SUB):
            bref = bc[bi * SUB:bi * SUB + 1, :]
            k_i = jnp.where(row < bi * SUB, kc * jnp.exp(jnp.minimum(bref - bc, 0.0)), 0.0)
            q_i = jnp.where((row >= bi * SUB) & (row < (bi + 1) * SUB),
                            qc * jnp.exp(jnp.minimum(bc - bref, 0.0)), 0.0)
            attn_t = attn_t + _dot_nt(k_i, q_i)
        diag = []
        for bi in range(CHUNK // SUB):
            sb = slice(bi * SUB, (bi + 1) * SUB)
            qb, kb, bb = qc[sb], kc[sb], bc[sb]
            d = jnp.zeros((SUB, CHUNK), F32)
            for il in range(SUB):
                e = jnp.exp(jnp.minimum(bb[il:il + 1, :] - bb, 0.0))
                col = jnp.sum(kb * e * qb[il:il + 1, :], axis=-1, keepdims=True)
                col = jnp.where(jrow <= il, col, 0.0)
                d = jnp.where(lane == bi * SUB + il, col, d)
            diag.append(d)
        attn_t = attn_t + jnp.concatenate(diag, axis=0)
        o = _dot(qc * jnp.exp(bc), s) + _dot_tn(attn_t, vc)
        bl_col = sum(_dot_tn(p[sl], ones) for p in gparts)
        decay_col = jnp.concatenate([jnp.exp(bl_col)] * (dv // LANES), axis=1)
        s = s * decay_col + _dot_tn(kc * jnp.exp(blast - bc), vc)
        gt = gate_ref[sl, :]
        o_ref[sl, :] = (_rms(o, nw) * _silu(gt)).astype(o_ref.dtype)
    s_scr[...] = s

    @pl.when(n == n_blocks - 1)
    def _():
        so_ref[...] = s_scr[...]


def _gla_chunks(qkvg3, gk3, b_gk, s0, norm_w, out_dtype, cb, t_valid):
    b, t, _ = qkvg3.shape
    r = cb * CHUNK
    assert t % r == 0
    nb = t // r
    dk, dv = GLA_DK, GLA_DV
    koff = GLA_HEADS
    voff = 2 * GLA_HEADS * dk // dv
    goff = voff + GLA_HEADS
    return pl.pallas_call(
        functools.partial(_gla_kernel, cb=cb, t_valid=t_valid, n_blocks=nb),
        grid=(b, GLA_HEADS, nb),
        in_specs=[pl.BlockSpec((None, r, dk), lambda bb, h, n: (bb, n, h)),
                  pl.BlockSpec((None, r, dk), lambda bb, h, n: (bb, n, koff + h)),
                  pl.BlockSpec((None, r, dv), lambda bb, h, n: (bb, n, voff + h)),
                  pl.BlockSpec((None, r, dv), lambda bb, h, n: (bb, n, goff + h)),
                  pl.BlockSpec((None, r, dk), lambda bb, h, n: (bb, n, h)),
                  pl.BlockSpec((1, dk), lambda bb, h, n: (0, h)),
                  pl.BlockSpec((None, None, dk, dv), lambda bb, h, n: (bb, h, 0, 0)),
                  pl.BlockSpec((1, dv), lambda bb, h, n: (0, 0))],
        out_specs=[pl.BlockSpec((None, r, dv), lambda bb, h, n: (bb, n, h)),
                   pl.BlockSpec((None, None, dk, dv), lambda bb, h, n: (bb, h, 0, 0))],
        out_shape=[jax.ShapeDtypeStruct((b, t, GLA_HEADS * dv), out_dtype),
                   jax.ShapeDtypeStruct(s0.shape, F32)],
        scratch_shapes=[pltpu.VMEM((dk, dv), F32)],
        compiler_params=_params("parallel", "parallel", "arbitrary"), name="gla_chunks")(
            qkvg3, qkvg3, qkvg3, qkvg3, gk3, b_gk, s0, norm_w)


def _t5_bucket(n):
    n = np.asarray(n)
    max_exact = N_BUCKETS // 2
    nf = np.maximum(n, max_exact).astype(np.float32)
    large = max_exact + (np.log(nf / max_exact) / math.log(MAX_DISTANCE / max_exact)
                         * (N_BUCKETS - max_exact)).astype(np.int32)
    return np.where(n < max_exact, n, np.minimum(large, N_BUCKETS - 1)).astype(np.int32)


def _lambda(lam_ref, lam_init):
    lf = lam_ref[...]
    s1 = jnp.sum(lf[0:1] * lf[1:2], axis=-1, keepdims=True)
    s2 = jnp.sum(lf[2:3] * lf[3:4], axis=-1, keepdims=True)
    return jnp.exp(s1) - jnp.exp(s2) + lam_init


def _flash_kernel(qi_ref, ki_ref, q_ref, k_ref, v_ref, bd_ref, bs_ref, rb_ref, lam_ref, sub_ref, o_ref,
                  q_scr, m_scr, l_scr, acc_scr, bias_scr, *, lam_init):
    h = pl.program_id(1)
    step = pl.program_id(2)
    qi = qi_ref[step]
    ki = ki_ref[step]
    tb = q_ref.shape[0]
    dh = DIFF_HEAD_DIM

    @pl.when(step == 0)
    def _():
        ri = lax.broadcasted_iota(jnp.int32, (tb, tb), 0)
        ci = lax.broadcasted_iota(jnp.int32, (tb, tb), 1)
        for mm in range(2):
            bd = jnp.zeros((tb, tb), F32)
            bs = jnp.zeros((tb, tb), F32)
            for bk in range(N_BUCKETS):
                val = rb_ref[bk, 2 * h + mm] * LOG2E
                bd = jnp.where(bd_ref[...] == bk, val, bd)
                bs = jnp.where(bs_ref[...] == bk, val, bs)
            bias_scr[mm, 0] = jnp.where(ri > ci, NEG, bd)
            bias_scr[mm, 1] = bs

    @pl.when(ki == 0)
    def _():
        q_scr[...] = (q_ref[...] * (dh ** -0.5 * LOG2E)).astype(q_scr.dtype)
        m_scr[...] = jnp.full_like(m_scr, NEG)
        l_scr[...] = jnp.zeros_like(l_scr)
        acc_scr[...] = jnp.zeros_like(acc_scr)

    def update(mm, s, shift):
        m_prev = m_scr[mm]
        if shift.ndim == 0:
            m_new = jnp.maximum(m_prev, jnp.max(s, axis=0, keepdims=True) + shift)
            p = jnp.exp2(s - (m_new - shift))
        else:
            s = s + shift
            m_new = jnp.maximum(m_prev, jnp.max(s, axis=0, keepdims=True))
            p = jnp.exp2(s - m_new)
        alpha = jnp.exp2(m_prev - m_new)
        l_scr[mm] = alpha * l_scr[mm] + jnp.sum(p, axis=0, keepdims=True)
        acc_scr[mm] = alpha * acc_scr[mm] + _dot(v_ref[...], p)
        m_scr[mm] = m_new

    def logits(mm):
        return _dot_nt(k_ref[:, mm * dh:(mm + 1) * dh], q_scr[:, mm * dh:(mm + 1) * dh])

    @pl.when(ki == qi)
    def _():
        for mm in range(2):
            update(mm, logits(mm), bias_scr[mm, 0])

    @pl.when(ki == qi - 1)
    def _():
        for mm in range(2):
            update(mm, logits(mm), bias_scr[mm, 1])

    @pl.when(ki < qi - 1)
    def _():
        for mm in range(2):
            update(mm, logits(mm), rb_ref[N_BUCKETS - 1, 2 * h + mm] * LOG2E)

    @pl.when(ki == qi)
    def _():
        lam = _lambda(lam_ref, lam_init)
        out = acc_scr[0] / l_scr[0] - lam * (acc_scr[1] / l_scr[1])
        out = out * lax.rsqrt(jnp.mean(out * out, axis=0, keepdims=True) + NORM_EPS)
        o_ref[...] = (out * (sub_ref[...] * (1.0 - lam_init))).astype(o_ref.dtype)


def _flash_diff(qkv3, v_t, rel_bias, lambdas, subln_col, lam_init, out_dtype, tb):
    b, t, _ = qkv3.shape
    tb = min(tb, t)
    assert t % tb == 0 and tb >= MAX_DISTANCE
    nq = t // tb
    dd = 2 * DIFF_HEAD_DIM
    j = np.arange(tb)[:, None]
    i = np.arange(tb)[None, :]
    bkt_diag = jnp.asarray(_t5_bucket(np.maximum(i - j, 0)))
    bkt_sub = jnp.asarray(_t5_bucket(tb + i - j))
    koff = D_MODEL // dd
    pairs = [(qi, ki) for qi in range(nq) for ki in range(qi + 1)]
    qi_tbl = jnp.asarray([pq for pq, _ in pairs], jnp.int32)
    ki_tbl = jnp.asarray([pk for _, pk in pairs], jnp.int32)
    grid_spec = pltpu.PrefetchScalarGridSpec(
        num_scalar_prefetch=2, grid=(b, DIFF_HEADS, len(pairs)),
        in_specs=[pl.BlockSpec((None, tb, dd), lambda bb, h, s, qt, kt: (bb, qt[s], h)),
                  pl.BlockSpec((None, tb, dd), lambda bb, h, s, qt, kt: (bb, kt[s], koff + h)),
                  pl.BlockSpec((None, dd, tb), lambda bb, h, s, qt, kt: (bb, h, kt[s])),
                  pl.BlockSpec((tb, tb), lambda bb, h, s, qt, kt: (0, 0)),
                  pl.BlockSpec((tb, tb), lambda bb, h, s, qt, kt: (0, 0)),
                  pl.BlockSpec(memory_space=pltpu.SMEM),
                  pl.BlockSpec((4, DIFF_HEAD_DIM), lambda bb, h, s, qt, kt: (0, 0)),
                  pl.BlockSpec((dd, 1), lambda bb, h, s, qt, kt: (0, 0))],
        out_specs=pl.BlockSpec((None, dd, tb), lambda bb, h, s, qt, kt: (bb, h, qt[s])),
        scratch_shapes=[pltpu.VMEM((tb, dd), BF16), pltpu.VMEM((2, 1, tb), F32), pltpu.VMEM((2, 1, tb), F32),
                        pltpu.VMEM((2, dd, tb), F32), pltpu.VMEM((2, 2, tb, tb), F32)])
    return pl.pallas_call(
        functools.partial(_flash_kernel, lam_init=lam_init), grid_spec=grid_spec,
        out_shape=jax.ShapeDtypeStruct((b, D_MODEL, t), out_dtype),
        compiler_params=_params("parallel", "parallel", "arbitrary"), name="flash_diff")(
            qi_tbl, ki_tbl, qkv3, qkv3, v_t, bkt_diag, bkt_sub, rel_bias, lambdas, subln_col)


def _decode_kernel(pt_ref, q_ref, *refs, pp, n_steps, lam_init):
    kp_refs, vp_refs = refs[:pp], refs[pp:2 * pp]
    kn_ref, vn_ref, bkt_ref, tbl_ref, lam_ref, sub_ref, o_ref, m_scr, l_scr, acc_scr, bias_scr, p_scr, a_scr = refs[2 * pp:]
    step = pl.program_id(1)
    nh, dh = DIFF_HEADS, DIFF_HEAD_DIM
    ones = jnp.ones((dh, LANES), BF16)
    qs = q_ref[...] * (dh ** -0.5 * LOG2E)

    @pl.when(step == 0)
    def _():
        m_scr[...] = jnp.full_like(m_scr, NEG)
        l_scr[...] = jnp.zeros_like(l_scr)
        acc_scr[...] = jnp.zeros_like(acc_scr)
        bkt = bkt_ref[...]
        bias = jnp.zeros(bkt.shape, F32)
        for bk in range(N_BUCKETS):
            bias = jnp.where(bkt == bk, (tbl_ref[bk] * LOG2E)[None], bias)
        bias_scr[...] = bias

    def lane_sum(x):
        return _dot(x, ones)

    def accumulate(pr, shift_row, values):
        m_prev = m_scr[...]
        if pr.ndim == 3:
            m_new = jnp.maximum(m_prev, jnp.max(pr, axis=0) + shift_row)
            p = jnp.exp2(pr - (m_new - shift_row)[None])
            psum = jnp.sum(p, axis=0)
        else:
            m_new = jnp.maximum(m_prev, pr + shift_row)
            p = jnp.exp2(pr - (m_new - shift_row))
            psum = p
        alpha = jnp.exp2(m_prev - m_new)
        l_scr[...] = alpha * l_scr[...] + psum
        m_scr[...] = m_new
        a_scr[...] = alpha
        if pr.ndim == 3:
            p_scr[...] = p
        else:
            p_scr[0] = p
        for par in range(2):
            ae = a_scr[pl.ds(par, nh, stride=2), :]
            if pr.ndim == 3:
                pe = p_scr[:, pl.ds(par, nh, stride=2), :]
                pv = jnp.sum(jnp.concatenate([pe, pe], axis=-1) * values(), axis=0)
            else:
                pe = p_scr[0, pl.ds(par, nh, stride=2), :]
                pv = jnp.concatenate([pe, pe], axis=-1) * values()
            acc_scr[par] = jnp.concatenate([ae, ae], axis=-1) * acc_scr[par] + pv

    def page_logits(j):
        prod = kp_refs[j][...] * qs[None]
        return lane_sum(prod.reshape(PAGE_SIZE * 2 * nh, dh)).reshape(PAGE_SIZE, 2 * nh, LANES)

    far = tbl_ref[N_BUCKETS - 1] * LOG2E
    zero = jnp.zeros_like(far)
    for j in range(pp - 1):
        accumulate(page_logits(j), far, lambda j=j: vp_refs[j][...])

    @pl.when(step < n_steps - 1)
    def _():
        accumulate(page_logits(pp - 1), far, lambda: vp_refs[pp - 1][...])

    @pl.when(step == n_steps - 1)
    def _():
        accumulate(page_logits(pp - 1) + bias_scr[...], zero, lambda: vp_refs[pp - 1][...])
        accumulate(lane_sum(kn_ref[...] * qs) + tbl_ref[0] * LOG2E, zero, lambda: vn_ref[...])
        l_scr_v = l_scr[...]
        a_scr[...] = l_scr_v
        outs = []
        for par in range(2):
            le = a_scr[pl.ds(par, nh, stride=2), :]
            outs.append(acc_scr[par] / jnp.concatenate([le, le], axis=-1))
        out = outs[0] - _lambda(lam_ref, lam_init) * outs[1]
        o_ref[...] = _rms(out, sub_ref[...]) * (1.0 - lam_init)


def _decode_diff(q, k_new, v_new, cache_k, cache_v, layer, page_table, rel_bias, lambdas, subln, lam_init, pp):
    b, n_pages = page_table.shape
    nh, dh = DIFF_HEADS, DIFF_HEAD_DIM
    pp = min(pp, n_pages)
    assert n_pages % pp == 0 and PAGE_SIZE >= MAX_DISTANCE and dh == LANES
    n_steps = n_pages // pp
    bkt_last = jnp.asarray(np.broadcast_to(
        _t5_bucket(PAGE_SIZE - np.arange(PAGE_SIZE)).reshape(PAGE_SIZE, 1, 1), (PAGE_SIZE, 2 * nh, LANES)))
    tbl = jnp.broadcast_to(rel_bias[:, :, None], (N_BUCKETS, 2 * nh, LANES))
    page_spec = lambda j, width, lanes: pl.BlockSpec(
        (None, None, PAGE_SIZE, width, lanes), lambda bb, s, pt: (layer, pt[bb, s * pp + j], 0, 0, 0))
    full = lambda shape: pl.BlockSpec(shape, lambda bb, s, pt: (0,) * len(shape))
    per_seq = lambda shape: pl.BlockSpec((None,) + shape, lambda bb, s, pt: (bb,) + (0,) * len(shape))
    grid_spec = pltpu.PrefetchScalarGridSpec(
        num_scalar_prefetch=1, grid=(b, n_steps),
        in_specs=([per_seq((2 * nh, dh))]
                  + [page_spec(j, 2 * nh, dh) for j in range(pp)]
                  + [page_spec(j, nh, 2 * dh) for j in range(pp)]
                  + [per_seq((2 * nh, dh)), per_seq((nh, 2 * dh)),
                     full((PAGE_SIZE, 2 * nh, LANES)), full((N_BUCKETS, 2 * nh, LANES)),
                     full((4, dh)), full((1, 2 * dh))]),
        out_specs=per_seq((nh, 2 * dh)),
        scratch_shapes=[pltpu.VMEM((2 * nh, LANES), F32), pltpu.VMEM((2 * nh, LANES), F32),
                        pltpu.VMEM((2, nh, 2 * dh), F32), pltpu.VMEM((PAGE_SIZE, 2 * nh, LANES), F32),
                        pltpu.VMEM((PAGE_SIZE, 2 * nh, LANES), F32), pltpu.VMEM((2 * nh, LANES), F32)])
    return pl.pallas_call(
        functools.partial(_decode_kernel, pp=pp, n_steps=n_steps, lam_init=lam_init),
        grid_spec=grid_spec,
        out_shape=jax.ShapeDtypeStruct((b, nh, 2 * dh), F32),
        compiler_params=_params("parallel", "arbitrary"), name="decode_diff")(
            page_table, q, *([cache_k] * pp), *([cache_v] * pp), k_new, v_new, bkt_last, tbl, lambdas, subln)


class _Tiles:
    def __init__(self, prompt):
        self.prompt = prompt
        self.act_dtype = BF16 if prompt else F32
        self.tm = 1024 if prompt else SUBLANES
        self.tn = 1024
        self.tn_ff = 512
        self.tm_res = 256 if prompt else SUBLANES
        self.tk_res = 512
        self.tt_prep = 256 if prompt else SUBLANES
        self.tc_prep = 1024
        self.cb = 4 if prompt else 1
        self.gdn_hpb = 4
        self.tb_attn = 512
        self.decode_pages = 4


def _group_trunk(x3, tiles, gdn_s0, gdn_c0, gla_s0, cache_k, cache_v, page_table, p, wb):
    b, t, d = x3.shape
    m = b * t
    x = x3.reshape(m, d)
    if not tiles.prompt:
        assert t == 1 and m % SUBLANES == 0
    t_pad = t if tiles.prompt else CHUNK
    t_valid = None if tiles.prompt else t
    nw = lambda i, jn: p['norm_w'][i, jn].reshape(1, d)
    adt = tiles.act_dtype

    def pad_time(a3, to):
        return a3 if a3.shape[1] == to else jnp.pad(a3, ((0, 0), (0, to - a3.shape[1]), (0, 0)))

    def big_matmul(a, name, layer):
        if tiles.prompt:
            return _matmul(a, wb[name, layer], layer, F32, tiles.tm, tiles.tn)
        out, wb[name, layer] = _matmul(a, p[name], layer, F32, tiles.tm, tiles.tn, emit_wb=True)
        return out

    def residual_matmul(a, name, layer, x_in, nw_post, nw_next):
        if tiles.prompt:
            return _matmul_residual(a, wb[name, layer], layer, x_in, nw_post, nw_next, adt,
                                    tiles.tm_res, tiles.tk_res)
        xo, h_next, wb[name, layer] = _matmul_residual(a, p[name], layer, x_in, nw_post, nw_next, adt,
                                                       tiles.tm_res, tiles.tk_res, emit_wb=True)
        return xo, h_next

    def ffn_up(a, layer):
        if tiles.prompt:
            return _swiglu_up(a, wb['ffn_gate', layer], wb['ffn_up', layer], layer, adt, tiles.tm, tiles.tn_ff)
        act_, wb['ffn_gate', layer], wb['ffn_up', layer] = _swiglu_up(
            a, p['ffn_w_up'], p['ffn_w_up'], layer, adt, tiles.tm, tiles.tn_ff, emit_wb=True)
        return act_

    gdn_s, gdn_c, gla_s, att_k, att_v = [], [], [], [], []
    h = _rmsnorm(x, nw(0, 0), adt, tiles.tm)
    for i in range(DEPTH):
        j = i // N_MIXERS
        kind = i % N_MIXERS
        if kind == 0:
            proj = big_matmul(h, 'gdn_w_in', j)
            beta, g = _gdn_gates(h, p['gdn_w_ba'], j, p['gdn_a_log'][j].reshape(1, -1),
                                 p['gdn_dt_bias'][j].reshape(1, -1), tiles.tm)
            proj3 = proj.reshape(b, t, -1)
            buf0 = gdn_c0[j]
            nbuf = GDN_CONV_W - 1
            if t >= nbuf:
                new_buf = proj3[:, t - nbuf:, :GDN_CONV_DIM]
            else:
                new_buf = jnp.concatenate([buf0[:, t:], proj3[..., :GDN_CONV_DIM]], axis=1)
            buf8 = jnp.pad(buf0, ((0, 0), (SUBLANES - (GDN_CONV_W - 1), 0), (0, 0)))
            t8 = max(t, SUBLANES)
            proj3p = pad_time(proj3, t8)
            qkv3 = pad_time(_gdn_prep(proj3p, buf8, p['gdn_conv_w'][j], tiles.tt_prep, tiles.tc_prep), t_pad)
            proj3p = pad_time(proj3p, t_pad)

            def heads(a):
                a = pad_time(a.reshape(b, t, GDN_QK_HEADS * 2), t_pad)
                return a.reshape(b, t_pad, GDN_QK_HEADS, 2).transpose(0, 2, 1, 3)
            gcol, bcol = heads(g), heads(beta)
            o3, s_new = _gdn_delta(qkv3, proj3p, gcol, bcol, gcol.transpose(0, 1, 3, 2), gdn_s0[j],
                                   p['gdn_norm_w'][j].reshape(1, -1), adt, tiles.cb, tiles.gdn_hpb, t_valid)
            mix_in, w_out = o3[:, :t].reshape(m, -1), 'gdn_w_out'
            gdn_s.append(s_new)
            gdn_c.append(new_buf)
        elif kind == 1:
            qkvg = big_matmul(h, 'gla_w_qkvg', j)
            low = _matmul(h, p['gla_w_gk1'], j, F32, tiles.tm, tiles.tn)
            gk = _matmul(low, p['gla_w_gk2'], j, F32, tiles.tm, tiles.tn)
            o3, s_new = _gla_chunks(pad_time(qkvg.reshape(b, t, -1), t_pad), pad_time(gk.reshape(b, t, -1), t_pad),
                                    p['gla_b_gk'][j].reshape(1, -1), gla_s0[j],
                                    p['gla_norm_w'][j].reshape(1, -1), adt, tiles.cb, t_valid)
            mix_in, w_out = o3[:, :t].reshape(m, -1), 'gla_w_out'
            gla_s.append(s_new)
        else:
            lam_init = 0.8 - 0.6 * math.exp(-0.3 * i)
            qkv = big_matmul(h, 'diff_w_qkv', j)
            nh, dh = DIFF_HEADS, DIFF_HEAD_DIM
            k_new = qkv[:, D_MODEL:2 * D_MODEL].reshape(b, t, 2 * nh, dh)
            v_new = qkv[:, 2 * D_MODEL:].reshape(b, t, nh, 2 * dh)
            subln = p['diff_subln'][j].reshape(1, -1)
            if tiles.prompt:
                v_t = jnp.swapaxes(qkv[:, 2 * D_MODEL:].reshape(b, t, D_MODEL), 1, 2).astype(BF16)
                o_t = _flash_diff(qkv.reshape(b, t, -1), v_t, p['rel_bias'], p['diff_lambda'][j],
                                  subln.reshape(-1, 1), lam_init, adt, tiles.tb_attn)
                mix_in = jnp.swapaxes(o_t, 1, 2).reshape(m, -1)
            else:
                o = _decode_diff(qkv[:, :D_MODEL].reshape(b, 2 * nh, dh), k_new.reshape(b, 2 * nh, dh),
                                 v_new.reshape(b, nh, 2 * dh), cache_k, cache_v, j, page_table,
                                 p['rel_bias'], p['diff_lambda'][j], subln, lam_init, tiles.decode_pages)
                mix_in = o.reshape(m, -1)
            w_out = 'diff_w_out'
            att_k.append(k_new)
            att_v.append(v_new)
        x, h = residual_matmul(mix_in, w_out, j, x, nw(i, 1), nw(i, 2))
        act = ffn_up(h, i)
        nxt = nw(i + 1, 0) if i + 1 < DEPTH else None
        x, h = residual_matmul(act, 'ffn_w_down', i, x, nw(i, 3), nxt)
    return (x.reshape(b, t, d), jnp.stack(gdn_s), jnp.stack(gdn_c), jnp.stack(gla_s),
            jnp.stack(att_k), jnp.stack(att_v))


def kernel(x_prompt, x_sample, state_gdn, state_gdn_conv, state_gla, cache_k, cache_v, page_table,
           norm_w, ffn_w_up, ffn_w_down, rel_bias,
           gdn_w_in, gdn_w_ba, gdn_conv_w, gdn_a_log, gdn_dt_bias, gdn_norm_w, gdn_w_out,
           gla_w_qkvg, gla_w_gk1, gla_w_gk2, gla_b_gk, gla_norm_w, gla_w_out,
           diff_w_qkv, diff_lambda, diff_subln, diff_w_out):
    p = dict(norm_w=norm_w, ffn_w_up=ffn_w_up, ffn_w_down=ffn_w_down, rel_bias=rel_bias,
             gdn_w_in=gdn_w_in, gdn_w_ba=gdn_w_ba, gdn_conv_w=gdn_conv_w, gdn_a_log=gdn_a_log,
             gdn_dt_bias=gdn_dt_bias, gdn_norm_w=gdn_norm_w, gdn_w_out=gdn_w_out,
             gla_w_qkvg=gla_w_qkvg, gla_w_gk1=gla_w_gk1, gla_w_gk2=gla_w_gk2, gla_b_gk=gla_b_gk,
             gla_norm_w=gla_norm_w, gla_w_out=gla_w_out,
             diff_w_qkv=diff_w_qkv, diff_lambda=diff_lambda, diff_subln=diff_subln, diff_w_out=diff_w_out)
    bp = x_prompt.shape[0]
    n_gdn, n_gla = state_gdn.shape[0], state_gla.shape[0]
    zeros_gdn = jnp.zeros((n_gdn, bp) + state_gdn.shape[2:], F32)
    zeros_conv = jnp.zeros((n_gdn, bp) + state_gdn_conv.shape[2:], F32)
    zeros_gla = jnp.zeros((n_gla, bp) + state_gla.shape[2:], F32)
    wb = {}
    outs_s = _group_trunk(x_sample, _Tiles(False), state_gdn, state_gdn_conv, state_gla,
                          cache_k, cache_v, page_table, p, wb)
    outs_p = _group_trunk(x_prompt, _Tiles(True), zeros_gdn, zeros_conv, zeros_gla, None, None, None, p, wb)
    return (outs_p[0], outs_s[0]) + outs_p[1:] + outs_s[1:]
```

```python
import functools
import math

import numpy as np
import jax
import jax.numpy as jnp
from jax import lax
from jax.experimental import pallas as pl
from jax.experimental.pallas import tpu as pltpu

F32 = jnp.float32
BF16 = jnp.bfloat16

D_MODEL = 2048
DEPTH = 4
PAGE_SIZE = 128
N_MIXERS = 3
GDN_HEAD_DIM = 128
GDN_QK_HEADS = D_MODEL // 128
GDN_V_HEADS = 2 * GDN_QK_HEADS
GDN_QK_DIM = GDN_QK_HEADS * GDN_HEAD_DIM
GDN_V_DIM = GDN_V_HEADS * GDN_HEAD_DIM
GDN_CONV_DIM = 2 * GDN_QK_DIM + GDN_V_DIM
GDN_CONV_W = 4
GLA_HEADS = 4
GLA_DK = D_MODEL // 2 // GLA_HEADS
GLA_DV = D_MODEL // GLA_HEADS
GLA_GATE_NORMALIZER = 16.0
DIFF_HEAD_DIM = 128
DIFF_HEADS = D_MODEL // (2 * DIFF_HEAD_DIM)
N_BUCKETS = 32
MAX_DISTANCE = 128
D_FF = ((8 * D_MODEL + 3 * 256 - 1) // (3 * 256)) * 256
NORM_EPS = 1e-6

LANES = 128
SUBLANES = 8
CHUNK = 64
SUB = 16
GDN_PROJ_SUBBLOCKS = 4
VMEM_LIMIT_BYTES = 48 * 1024 * 1024
VMEM_LIMIT_BIG_BYTES = 56 * 1024 * 1024
NEG = -1e30
LOG2E = math.log2(math.e)


def _params(*sem, vmem=VMEM_LIMIT_BYTES):
    return pltpu.CompilerParams(dimension_semantics=sem, vmem_limit_bytes=vmem)


def _dot(a, b):
    return jnp.dot(a.astype(BF16), b.astype(BF16), preferred_element_type=F32)


def _dot_nt(a, b):
    return lax.dot_general(a.astype(BF16), b.astype(BF16), (((1,), (1,)), ((), ())),
                           preferred_element_type=F32)


def _dot_tn(a, b):
    return lax.dot_general(a.astype(BF16), b.astype(BF16), (((0,), (0,)), ((), ())),
                           preferred_element_type=F32)


def _split3(x):
    hi = x.astype(BF16)
    r = x - hi.astype(F32)
    mid = r.astype(BF16)
    lo = (r - mid.astype(F32)).astype(BF16)
    return hi, mid, lo


def _sigmoid(x):
    return 0.5 * jnp.tanh(0.5 * x) + 0.5


def _silu(x):
    h = 0.5 * x
    return h + h * jnp.tanh(h)


def _softplus(x):
    return jnp.maximum(x, 0.0) + jnp.log(1.0 + jnp.exp(-jnp.abs(x)))


def _rms(x, w):
    return x * lax.rsqrt(jnp.mean(x * x, axis=-1, keepdims=True) + NORM_EPS) * w


def _chunk_tril(r):
    i = lax.broadcasted_iota(jnp.int32, (r, r), 0)
    j = lax.broadcasted_iota(jnp.int32, (r, r), 1)
    return jnp.where((j <= i) & ((i // CHUNK) == (j // CHUNK)), 1.0, 0.0).astype(BF16)


def _rmsnorm_kernel(x_ref, w_ref, o_ref):
    o_ref[...] = _rms(x_ref[...], w_ref[...]).astype(o_ref.dtype)


def _rmsnorm(x, w, out_dtype, tm):
    m, d = x.shape
    tm = min(tm, m)
    return pl.pallas_call(
        _rmsnorm_kernel, grid=(m // tm,),
        in_specs=[pl.BlockSpec((tm, d), lambda i: (i, 0)), pl.BlockSpec((1, d), lambda i: (0, 0))],
        out_specs=pl.BlockSpec((tm, d), lambda i: (i, 0)),
        out_shape=jax.ShapeDtypeStruct((m, d), out_dtype),
        compiler_params=_params("parallel"), name="rmsnorm")(x, w)


def _weight_spec(w, layer, block, index):
    if w.ndim == 2:
        return pl.BlockSpec(block, index)
    return pl.BlockSpec((None,) + block, lambda *g: (layer,) + index(*g))


def _mm_kernel(a_ref, w_ref, o_ref, *wb_ref):
    w = w_ref[...].astype(BF16)
    if wb_ref:
        wb_ref[0][...] = w
    o_ref[...] = _dot(a_ref[...], w).astype(o_ref.dtype)


def _matmul(a, w, layer, out_dtype, tm, tn, emit_wb=False):
    m, k = a.shape
    n = w.shape[-1]
    tm, tn = min(tm, m), min(tn, n)
    assert m % tm == 0 and n % tn == 0 and (not emit_wb or m == tm)
    out_shape = [jax.ShapeDtypeStruct((m, n), out_dtype)]
    out_specs = [pl.BlockSpec((tm, tn), lambda i, j: (i, j))]
    if emit_wb:
        out_shape.append(jax.ShapeDtypeStruct((k, n), BF16))
        out_specs.append(pl.BlockSpec((k, tn), lambda i, j: (0, j)))
    res = pl.pallas_call(
        _mm_kernel, grid=(m // tm, n // tn),
        in_specs=[pl.BlockSpec((tm, k), lambda i, j: (i, 0)),
                  _weight_spec(w, layer, (k, tn), lambda i, j: (0, j))],
        out_specs=out_specs, out_shape=out_shape,
        compiler_params=_params("parallel", "parallel"), name="matmul")(a, w)
    return (res[0], res[1]) if emit_wb else res[0]


def _swiglu_kernel(a_ref, wg_ref, wu_ref, o_ref, *wb_refs):
    a = a_ref[...].astype(BF16)
    wg = wg_ref[...].astype(BF16)
    wu = wu_ref[...].astype(BF16)
    if wb_refs:
        wb_refs[0][...] = wg
        wb_refs[1][...] = wu
    g = _dot(a, wg)
    u = _dot(a, wu)
    o_ref[...] = (_silu(g) * u).astype(o_ref.dtype)


def _swiglu_up(a, w_gate, w_up, layer, out_dtype, tm, tn, emit_wb=False):
    m, k = a.shape
    stacked = w_up.ndim == 3
    f = w_up.shape[-1] // 2 if stacked else w_up.shape[-1]
    tm, tn = min(tm, m), min(tn, f)
    assert m % tm == 0 and f % tn == 0 and (not emit_wb or m == tm)
    nf = f // tn
    up_off = nf if stacked else 0
    out_shape = [jax.ShapeDtypeStruct((m, f), out_dtype)]
    out_specs = [pl.BlockSpec((tm, tn), lambda i, j: (i, j))]
    if emit_wb:
        out_shape += [jax.ShapeDtypeStruct((k, f), BF16)] * 2
        out_specs += [pl.BlockSpec((k, tn), lambda i, j: (0, j))] * 2
    res = pl.pallas_call(
        _swiglu_kernel, grid=(m // tm, nf),
        in_specs=[pl.BlockSpec((tm, k), lambda i, j: (i, 0)),
                  _weight_spec(w_gate, layer, (k, tn), lambda i, j: (0, j)),
                  _weight_spec(w_up, layer, (k, tn), lambda i, j: (0, j + up_off))],
        out_specs=out_specs, out_shape=out_shape,
        compiler_params=_params("parallel", "parallel"), name="swiglu_up")(a, w_gate, w_up)
    return tuple(res) if emit_wb else res[0]


def _residual_epilogue(mix, x_ref, nw1_ref, nw2_ref, xo_ref, h_ref):
    xn = x_ref[...] + _rms(mix, nw1_ref[...])
    xo_ref[...] = xn
    if h_ref:
        h_ref[0][...] = _rms(xn, nw2_ref[...]).astype(h_ref[0].dtype)


def _mm_res_kernel(a_ref, w_ref, x_ref, nw1_ref, nw2_ref, xo_ref, *rest, nk, emit_h, emit_wb):
    h_ref = rest[:1] if emit_h else ()
    kk = pl.program_id(1)
    w = w_ref[...].astype(BF16)
    if emit_wb:
        rest[-1][...] = w
    part = _dot(a_ref[...], w)

    @pl.when(kk == 0)
    def _():
        xo_ref[...] = part

    @pl.when(kk > 0)
    def _():
        xo_ref[...] += part

    @pl.when(kk == nk - 1)
    def _():
        _residual_epilogue(xo_ref[...], x_ref, nw1_ref, nw2_ref, xo_ref, h_ref)


def _mm_res_resident_kernel(a_ref, w_ref, x_ref, nw1_ref, nw2_ref, xo_ref, *h_ref):
    _residual_epilogue(_dot(a_ref[...], w_ref[...]), x_ref, nw1_ref, nw2_ref, xo_ref, h_ref)


def _matmul_residual(a, w, layer, x, nw_post, nw_next, h_dtype, tm, tk, emit_wb=False):
    m, k = a.shape
    n = w.shape[-1]
    tm = min(tm, m)
    assert m % tm == 0 and (not emit_wb or m == tm)
    emit_h = nw_next is not None
    if not emit_h:
        nw_next = nw_post
    resident = w.ndim == 2
    if resident:
        grid = (m // tm,)
        row = lambda i: (i, 0)
        fixed = lambda i: (0, 0)
        a_spec = pl.BlockSpec((tm, k), row)
        w_spec = pl.BlockSpec((k, n), fixed, pipeline_mode=pl.Buffered(1))
        x_spec = pl.BlockSpec((tm, n), row)
        body = _mm_res_resident_kernel
        sem = ("parallel",)
    else:
        tk = min(tk, k)
        assert k % tk == 0
        nk = k // tk
        grid = (m // tm, nk)
        row = lambda i, kk: (i, 0)
        fixed = lambda i, kk: (0, 0)
        a_spec = pl.BlockSpec((tm, tk), lambda i, kk: (i, kk))
        w_spec = pl.BlockSpec((None, tk, n), lambda i, kk: (layer, kk, 0))
        x_spec = pl.BlockSpec((tm, n), row, pipeline_mode=pl.Buffered(1))
        body = functools.partial(_mm_res_kernel, nk=nk, emit_h=emit_h, emit_wb=emit_wb)
        sem = ("parallel", "arbitrary")
    out_shape = [jax.ShapeDtypeStruct((m, n), F32)]
    out_specs = [pl.BlockSpec((tm, n), row)]
    if emit_h:
        out_shape.append(jax.ShapeDtypeStruct((m, n), h_dtype))
        out_specs.append(pl.BlockSpec((tm, n), row))
    if emit_wb:
        out_shape.append(jax.ShapeDtypeStruct((k, n), BF16))
        out_specs.append(pl.BlockSpec((tk, n), lambda i, kk: (kk, 0)))
    res = pl.pallas_call(
        body, grid=grid,
        in_specs=[a_spec, w_spec, x_spec, pl.BlockSpec((1, n), fixed), pl.BlockSpec((1, n), fixed)],
        out_specs=out_specs, out_shape=out_shape,
        compiler_params=_params(*sem, vmem=VMEM_LIMIT_BIG_BYTES), name="matmul_residual")(
            a, w, x, nw_post, nw_next)
    xo = res[0]
    h = res[1] if emit_h else None
    return (xo, h, res[-1]) if emit_wb else (xo, h)


def _gdn_gates_kernel(a_ref, w_ref, alog_ref, dtb_ref, beta_ref, g_ref):
    ba = _dot(a_ref[...], w_ref[...])
    nh = beta_ref.shape[-1]
    beta_ref[...] = _sigmoid(ba[:, :nh])
    g_ref[...] = -jnp.exp(alog_ref[...]) * _softplus(ba[:, nh:] + dtb_ref[...])


def _gdn_gates(a, w_ba, layer, a_log, dt_bias, tm):
    m, k = a.shape
    nh = GDN_V_HEADS
    tm = min(tm, m)
    return pl.pallas_call(
        _gdn_gates_kernel, grid=(m // tm,),
        in_specs=[pl.BlockSpec((tm, k), lambda i: (i, 0)),
                  pl.BlockSpec((None, k, 2 * nh), lambda i: (layer, 0, 0)),
                  pl.BlockSpec((1, nh), lambda i: (0, 0)),
                  pl.BlockSpec((1, nh), lambda i: (0, 0))],
        out_specs=[pl.BlockSpec((tm, nh), lambda i: (i, 0))] * 2,
        out_shape=[jax.ShapeDtypeStruct((m, nh), F32)] * 2,
        compiler_params=_params("parallel"), name="gdn_gates")(a, w_ba, a_log, dt_bias)


def _conv_silu_norm(x, halo, ext_scr, cw_ref, j, n_qk_tiles):
    tt, tc = x.shape
    ext_scr[:SUBLANES, :] = halo
    ext_scr[SUBLANES:, :] = x
    y = x * cw_ref[GDN_CONV_W - 1:GDN_CONV_W, :]
    for s in range(1, GDN_CONV_W):
        y = y + ext_scr[SUBLANES - s:SUBLANES - s + tt, :] * cw_ref[GDN_CONV_W - 1 - s:GDN_CONV_W - s, :]
    y = _silu(y)
    qscale = jnp.where(j < n_qk_tiles, GDN_HEAD_DIM ** -0.5, 1.0)
    pieces = []
    for hh in range(tc // GDN_HEAD_DIM):
        yh = y[:, hh * GDN_HEAD_DIM:(hh + 1) * GDN_HEAD_DIM]
        r = lax.rsqrt(jnp.sum(yh * yh, axis=-1, keepdims=True) + NORM_EPS) * qscale
        pieces.append(yh * jnp.where(j < 2 * n_qk_tiles, r, 1.0))
    return jnp.concatenate(pieces, axis=1)


def _gdn_prep_kernel(x_ref, prev_ref, buf_ref, cw_ref, o_ref, ext_scr, *, n_qk_tiles):
    i = pl.program_id(1)
    j = pl.program_id(2)
    halo = jnp.where(i == 0, buf_ref[...], prev_ref[...])
    o_ref[...] = _conv_silu_norm(x_ref[...], halo, ext_scr, cw_ref, j, n_qk_tiles)


def _gdn_in_proj_kernel(a_ref, w_ref, buf_ref, cw_ref, o_ref, tail_ref, ext_scr, halo_scr, *,
                        tiles_per_seq, n_conv_tiles, n_qk_tiles):
    i = pl.program_id(0)
    j = pl.program_id(1)

    @pl.when((i == 0) & (j == 0))
    def _():
        halo_scr[...] = jnp.zeros_like(halo_scr)

    halo = jnp.where(i % tiles_per_seq == 0, buf_ref[...], halo_scr[j])
    w = w_ref[...]
    n_sub, rows = ext_scr.shape[0], ext_scr.shape[1] - SUBLANES
    for s in range(n_sub):
        sl = slice(s * rows, (s + 1) * rows)
        x = _dot(a_ref[sl, :], w)
        y = _conv_silu_norm(x, halo, ext_scr.at[s], cw_ref, j, n_qk_tiles)
        o_ref[sl, :] = jnp.where(j < n_conv_tiles, y, x)
        halo = x[rows - SUBLANES:, :]
    halo_scr[j] = halo
    tail_ref[...] = halo


def _gdn_in_proj(a, w, buf8, conv_w, seq_len, tm, tn):
    m, k = a.shape
    n = w.shape[-1]
    b = m // seq_len
    assert w.ndim == 2 and seq_len % tm == 0 and n % tn == 0 and GDN_QK_DIM % tn == 0 and tm > SUBLANES
    tps = seq_len // tm
    nct = GDN_CONV_DIM // tn
    proj, tails = pl.pallas_call(
        functools.partial(_gdn_in_proj_kernel, tiles_per_seq=tps, n_conv_tiles=nct, n_qk_tiles=GDN_QK_DIM // tn),
        grid=(m // tm, n // tn),
        in_specs=[pl.BlockSpec((tm, k), lambda i, j: (i, 0)),
                  pl.BlockSpec((k, tn), lambda i, j: (0, j)),
                  pl.BlockSpec((None, SUBLANES, tn), lambda i, j: (i // tps, 0, jnp.minimum(j, nct - 1))),
                  pl.BlockSpec((GDN_CONV_W, tn), lambda i, j: (0, jnp.minimum(j, nct - 1)))],
        out_specs=[pl.BlockSpec((tm, tn), lambda i, j: (i, j)),
                   pl.BlockSpec((None, SUBLANES, tn), lambda i, j: (i, 0, j))],
        out_shape=[jax.ShapeDtypeStruct((m, n), F32), jax.ShapeDtypeStruct((m // tm, SUBLANES, n), F32)],
        scratch_shapes=[pltpu.VMEM((GDN_PROJ_SUBBLOCKS, SUBLANES + tm // GDN_PROJ_SUBBLOCKS, tn), F32),
                        pltpu.VMEM((n // tn, SUBLANES, tn), F32)],
        compiler_params=_params("arbitrary", "arbitrary", vmem=VMEM_LIMIT_BIG_BYTES), name="gdn_in_proj")(
            a, w, buf8, conv_w)
    return proj, tails[tps - 1::tps]


def _gdn_prep(proj3, buf8, conv_w, tt, tc):
    b, t, _ = proj3.shape
    tt = min(tt, t)
    assert t % tt == 0 and tt % SUBLANES == 0 and GDN_QK_DIM % tc == 0
    hb = tt // SUBLANES
    return pl.pallas_call(
        functools.partial(_gdn_prep_kernel, n_qk_tiles=GDN_QK_DIM // tc),
        grid=(b, t // tt, GDN_CONV_DIM // tc),
        in_specs=[pl.BlockSpec((None, tt, tc), lambda bb, i, j: (bb, i, j)),
                  pl.BlockSpec((None, SUBLANES, tc), lambda bb, i, j: (bb, jnp.maximum(i * hb - 1, 0), j)),
                  pl.BlockSpec((None, SUBLANES, tc), lambda bb, i, j: (bb, 0, j)),
                  pl.BlockSpec((GDN_CONV_W, tc), lambda bb, i, j: (0, j))],
        out_specs=pl.BlockSpec((None, tt, tc), lambda bb, i, j: (bb, i, j)),
        out_shape=jax.ShapeDtypeStruct((b, t, GDN_CONV_DIM), F32),
        scratch_shapes=[pltpu.VMEM((SUBLANES + tt, tc), F32)],
        compiler_params=_params("parallel", "parallel", "parallel"), name="gdn_prep")(
            proj3, proj3, buf8, conv_w)


def _gdn_delta_kernel(q_ref, k_ref, v_ref, z_ref, gc_ref, bc_ref, gr_ref, s0_ref, nw_ref,
                      o_ref, so_ref, s_scr, *, cb, hpb, t_valid, n_blocks):
    n = pl.program_id(2)
    r = cb * CHUNK
    dh = GDN_HEAD_DIM

    @pl.when(n == 0)
    def _():
        s_scr[...] = s0_ref[...]

    q = q_ref[...]
    k = k_ref[...]
    v = v_ref[...]
    gcol = gc_ref[...]
    bcol = bc_ref[...]
    grow = gr_ref[...]
    if t_valid is not None:
        okc = (n * r + lax.broadcasted_iota(jnp.int32, (r, 1), 0)) < t_valid
        okr = (n * r + lax.broadcasted_iota(jnp.int32, (1, r), 1)) < t_valid
        k = jnp.where(okc, k, 0.0)
        v = jnp.where(okc, v, 0.0)
        gcol = jnp.where(okc, gcol, 0.0)
        bcol = jnp.where(okc, bcol, 0.0)
        grow = jnp.where(okr, grow, 0.0)

    lm = _chunk_tril(r)
    gcum_col = [sum(_dot(lm, p) for p in _split3(gcol[qh])) for qh in range(hpb)]
    gcum_row = [sum(_dot_nt(p, lm) for p in _split3(grow[qh])) for qh in range(hpb)]

    ri = lax.broadcasted_iota(jnp.int32, (r, r), 0)
    ci = lax.broadcasted_iota(jnp.int32, (r, r), 1)
    eye = jnp.where(ri == ci, 1.0, 0.0)
    tril = ((ri // CHUNK) == (ci // CHUNK)) & (ri >= ci)
    lvl_masks = []
    size = 1
    while size < CHUNK:
        lvl_masks.append(((ri // (2 * size)) == (ci // (2 * size)))
                         & ((ri // size) % 2 == 1) & ((ci // size) % 2 == 0))
        size *= 2
    if t_valid is not None and t_valid <= 1:
        lvl_masks = []

    nw = nw_ref[...]
    heads = [(qh, hh) for qh in range(hpb) for hh in range(2)]
    kq = [k[:, qh * dh:(qh + 1) * dh] for qh in range(hpb)]
    qq = [q[:, qh * dh:(qh + 1) * dh] for qh in range(hpb)]
    gram = [_dot_nt(kq[qh], kq[qh]) for qh in range(hpb)]
    qk = [_dot_nt(qq[qh], kq[qh]) for qh in range(hpb)]
    gc = [gcum_col[qh][:, hh:hh + 1] for qh, hh in heads]
    bc = [bcol[qh][:, hh:hh + 1] for qh, hh in heads]
    decay = [jnp.exp(jnp.where(tril, gc[i] - gcum_row[qh][hh:hh + 1, :], NEG)) for i, (qh, hh) in enumerate(heads)]
    mm_ = [gram[qh] * bc[i] * decay[i] for i, (qh, hh) in enumerate(heads)]
    tinv = [eye for _ in heads]
    for lvl, msk in enumerate(lvl_masks):
        ml = [jnp.where(msk, m, 0.0) for m in mm_]
        if lvl == 0:
            tinv = [t - m for t, m in zip(tinv, ml)]
        else:
            y = [_dot(m, t) for t, m in zip(tinv, ml)]
            tinv = [t - _dot(t, yy) for t, yy in zip(tinv, y)]
    eg = [jnp.exp(g_) for g_ in gc]
    sol = [_dot(tinv[i], jnp.concatenate([v[:, (2 * qh + hh) * dh:(2 * qh + hh + 1) * dh] * bc[i],
                                          kq[qh] * (bc[i] * eg[i])], axis=1))
           for i, (qh, hh) in enumerate(heads)]
    asol = [_dot(qk[qh] * decay[i], sol[i]) for i, (qh, hh) in enumerate(heads)]
    o2 = [a[:, :dh] for a in asol]
    o1 = [qq[qh] * eg[i] - asol[i][:, dh:] for i, (qh, hh) in enumerate(heads)]
    glast = [[g_[(c + 1) * CHUNK - 1:(c + 1) * CHUNK, :] for c in range(cb)] for g_ in gc]
    kd = [kq[qh] * jnp.exp(jnp.concatenate([jnp.broadcast_to(gl, (CHUNK, 1)) for gl in glast[i]], axis=0) - gc[i])
          for i, (qh, hh) in enumerate(heads)]
    qw = [[_dot_tn(kd[i][c * CHUNK:(c + 1) * CHUNK], sol[i][c * CHUNK:(c + 1) * CHUNK]) for c in range(cb)]
          for i in range(len(heads))]
    s = [s_scr[2 * qh + hh] for qh, hh in heads]
    for c in range(cb):
        sl = slice(c * CHUNK, (c + 1) * CHUNK)
        for i, (qh, hh) in enumerate(heads):
            x = _dot(jnp.concatenate([qw[i][c][:, dh:], o1[i][sl]], axis=0), s[i])
            o = x[dh:] + o2[i][sl]
            s[i] = s[i] * jnp.exp(glast[i][c]) - x[:dh] + qw[i][c][:, :dh]
            col = slice((2 * qh + hh) * dh, (2 * qh + hh + 1) * dh)
            zc = z_ref[sl, col]
            o_ref[sl, col] = (_rms(o, nw) * _silu(zc)).astype(o_ref.dtype)
    for i, (qh, hh) in enumerate(heads):
        s_scr[2 * qh + hh] = s[i]

    @pl.when(n == n_blocks - 1)
    def _():
        so_ref[...] = s_scr[...]


def _gdn_delta(qkv3, proj3, gcol, bcol, grow, s0, norm_w, out_dtype, cb, hpb, t_valid):
    b, t, _ = qkv3.shape
    r = cb * CHUNK
    assert t % r == 0
    nb = t // r
    dh = GDN_HEAD_DIM
    qw_, vw_ = hpb * dh, 2 * hpb * dh
    assert GDN_QK_HEADS % hpb == 0
    koff = GDN_QK_DIM // qw_
    voff = 2 * GDN_QK_DIM // vw_
    zoff = GDN_CONV_DIM // vw_
    return pl.pallas_call(
        functools.partial(_gdn_delta_kernel, cb=cb, hpb=hpb, t_valid=t_valid, n_blocks=nb),
        grid=(b, GDN_QK_HEADS // hpb, nb),
        in_specs=[pl.BlockSpec((None, r, qw_), lambda bb, h, n: (bb, n, h)),
                  pl.BlockSpec((None, r, qw_), lambda bb, h, n: (bb, n, koff + h)),
                  pl.BlockSpec((None, r, vw_), lambda bb, h, n: (bb, n, voff + h)),
                  pl.BlockSpec((None, r, vw_), lambda bb, h, n: (bb, n, zoff + h)),
                  pl.BlockSpec((None, hpb, r, 2), lambda bb, h, n: (bb, h, n, 0)),
                  pl.BlockSpec((None, hpb, r, 2), lambda bb, h, n: (bb, h, n, 0)),
                  pl.BlockSpec((None, hpb, 2, r), lambda bb, h, n: (bb, h, 0, n)),
                  pl.BlockSpec((None, 2 * hpb, dh, dh), lambda bb, h, n: (bb, h, 0, 0)),
                  pl.BlockSpec((1, dh), lambda bb, h, n: (0, 0))],
        out_specs=[pl.BlockSpec((None, r, vw_), lambda bb, h, n: (bb, n, h)),
                   pl.BlockSpec((None, 2 * hpb, dh, dh), lambda bb, h, n: (bb, h, 0, 0))],
        out_shape=[jax.ShapeDtypeStruct((b, t, GDN_V_DIM), out_dtype),
                   jax.ShapeDtypeStruct(s0.shape, F32)],
        scratch_shapes=[pltpu.VMEM((2 * hpb, dh, dh), F32)],
        compiler_params=_params("parallel", "parallel", "arbitrary"), name="gdn_delta")(
            qkv3, qkv3, qkv3, proj3, gcol, bcol, grow, s0, norm_w)


def _gla_kernel(q_ref, k_ref, v_ref, gate_ref, gk_ref, bgk_ref, s0_ref, nw_ref,
                o_ref, so_ref, s_scr, *, cb, t_valid, n_blocks):
    n = pl.program_id(2)
    r = cb * CHUNK
    dk, dv = GLA_DK, GLA_DV

    @pl.when(n == 0)
    def _():
        s_scr[...] = s0_ref[...]

    xg = gk_ref[...] + bgk_ref[...]
    g = (jnp.minimum(xg, 0.0) - jnp.log(1.0 + jnp.exp(-jnp.abs(xg)))) * (1.0 / GLA_GATE_NORMALIZER)
    q = q_ref[...] * GLA_DK ** -0.5
    k = k_ref[...]
    v = v_ref[...]
    if t_valid is not None:
        okc = (n * r + lax.broadcasted_iota(jnp.int32, (r, 1), 0)) < t_valid
        g = jnp.where(okc, g, 0.0)
        k = jnp.where(okc, k, 0.0)
        v = jnp.where(okc, v, 0.0)

    gparts = _split3(g)
    bcum = sum(_dot(_chunk_tril(r), p) for p in gparts)
    ones = jnp.ones((CHUNK, LANES), BF16)
    row = lax.broadcasted_iota(jnp.int32, (CHUNK, 1), 0)
    jrow = lax.broadcasted_iota(jnp.int32, (SUB, 1), 0)
    lane = lax.broadcasted_iota(jnp.int32, (SUB, CHUNK), 1)
    nw = nw_ref[...]
    s = s_scr[...]
    for c in range(cb):
        sl = slice(c * CHUNK, (c + 1) * CHUNK)
        qc, kc, vc, bc = q[sl], k[sl], v[sl], bcum[sl]
        blast = bc[CHUNK - 1:CHUNK, :]
        attn_t = jnp.zeros((CHUNK, CHUNK), F32)
        for bi in range(1, CHUNK // SUB):
            bref = bc[bi * SUB:bi * SUB + 1, :]
            k_i = jnp.where(row < bi * SUB, kc * jnp.exp(jnp.minimum(bref - bc, 0.0)), 0.0)
            q_i = jnp.where((row >= bi * SUB) & (row < (bi + 1) * SUB),
                            qc * jnp.exp(jnp.minimum(bc - bref, 0.0)), 0.0)
            attn_t = attn_t + _dot_nt(k_i, q_i)
        diag = []
        for bi in range(CHUNK // SUB):
            sb = slice(bi * SUB, (bi + 1) * SUB)
            qb, kb, bb = qc[sb], kc[sb], bc[sb]
            d = jnp.zeros((SUB, CHUNK), F32)
            for il in range(SUB):
                e = jnp.exp(jnp.minimum(bb[il:il + 1, :] - bb, 0.0))
                col = jnp.sum(kb * e * qb[il:il + 1, :], axis=-1, keepdims=True)
                col = jnp.where(jrow <= il, col, 0.0)
                d = jnp.where(lane == bi * SUB + il, col, d)
            diag.append(d)
        attn_t = attn_t + jnp.concatenate(diag, axis=0)
        o = _dot(qc * jnp.exp(bc), s) + _dot_tn(attn_t, vc)
        bl_col = sum(_dot_tn(p[sl], ones) for p in gparts)
        decay_col = jnp.concatenate([jnp.exp(bl_col)] * (dv // LANES), axis=1)
        s = s * decay_col + _dot_tn(kc * jnp.exp(blast - bc), vc)
        gt = gate_ref[sl, :]
        o_ref[sl, :] = (_rms(o, nw) * _silu(gt)).astype(o_ref.dtype)
    s_scr[...] = s

    @pl.when(n == n_blocks - 1)
    def _():
        so_ref[...] = s_scr[...]


def _gla_chunks(qkvg3, gk3, b_gk, s0, norm_w, out_dtype, cb, t_valid):
    b, t, _ = qkvg3.shape
    r = cb * CHUNK
    assert t % r == 0
    nb = t // r
    dk, dv = GLA_DK, GLA_DV
    koff = GLA_HEADS
    voff = 2 * GLA_HEADS * dk // dv
    goff = voff + GLA_HEADS
    return pl.pallas_call(
        functools.partial(_gla_kernel, cb=cb, t_valid=t_valid, n_blocks=nb),
        grid=(b, GLA_HEADS, nb),
        in_specs=[pl.BlockSpec((None, r, dk), lambda bb, h, n: (bb, n, h)),
                  pl.BlockSpec((None, r, dk), lambda bb, h, n: (bb, n, koff + h)),
                  pl.BlockSpec((None, r, dv), lambda bb, h, n: (bb, n, voff + h)),
                  pl.BlockSpec((None, r, dv), lambda bb, h, n: (bb, n, goff + h)),
                  pl.BlockSpec((None, r, dk), lambda bb, h, n: (bb, n, h)),
                  pl.BlockSpec((1, dk), lambda bb, h, n: (0, h)),
                  pl.BlockSpec((None, None, dk, dv), lambda bb, h, n: (bb, h, 0, 0)),
                  pl.BlockSpec((1, dv), lambda bb, h, n: (0, 0))],
        out_specs=[pl.BlockSpec((None, r, dv), lambda bb, h, n: (bb, n, h)),
                   pl.BlockSpec((None, None, dk, dv), lambda bb, h, n: (bb, h, 0, 0))],
        out_shape=[jax.ShapeDtypeStruct((b, t, GLA_HEADS * dv), out_dtype),
                   jax.ShapeDtypeStruct(s0.shape, F32)],
        scratch_shapes=[pltpu.VMEM((dk, dv), F32)],
        compiler_params=_params("parallel", "parallel", "arbitrary"), name="gla_chunks")(
            qkvg3, qkvg3, qkvg3, qkvg3, gk3, b_gk, s0, norm_w)


def _t5_bucket(n):
    n = np.asarray(n)
    max_exact = N_BUCKETS // 2
    nf = np.maximum(n, max_exact).astype(np.float32)
    large = max_exact + (np.log(nf / max_exact) / math.log(MAX_DISTANCE / max_exact)
                         * (N_BUCKETS - max_exact)).astype(np.int32)
    return np.where(n < max_exact, n, np.minimum(large, N_BUCKETS - 1)).astype(np.int32)


def _lambda(lam_ref, lam_init):
    lf = lam_ref[...]
    s1 = jnp.sum(lf[0:1] * lf[1:2], axis=-1, keepdims=True)
    s2 = jnp.sum(lf[2:3] * lf[3:4], axis=-1, keepdims=True)
    return jnp.exp(s1) - jnp.exp(s2) + lam_init


def _flash_kernel(qi_ref, ki_ref, q_ref, k_ref, v_ref, bd_ref, bs_ref, rb_ref, lam_ref, sub_ref, o_ref,
                  q_scr, m_scr, l_scr, acc_scr, bias_scr, *, lam_init):
    h = pl.program_id(0)
    step = pl.program_id(2)
    qi = qi_ref[step]
    ki = ki_ref[step]
    tb = q_ref.shape[0]
    dh = DIFF_HEAD_DIM

    @pl.when((pl.program_id(1) == 0) & (step == 0))
    def _():
        ri = lax.broadcasted_iota(jnp.int32, (tb, tb), 0)
        ci = lax.broadcasted_iota(jnp.int32, (tb, tb), 1)
        for mm in range(2):
            bd = jnp.zeros((tb, tb), F32)
            bs = jnp.zeros((tb, tb), F32)
            for bk in range(N_BUCKETS):
                val = rb_ref[bk, 2 * h + mm] * LOG2E
                bd = jnp.where(bd_ref[...] == bk, val, bd)
                bs = jnp.where(bs_ref[...] == bk, val, bs)
            bias_scr[mm, 0] = jnp.where(ri > ci, NEG, bd)
            bias_scr[mm, 1] = bs

    @pl.when(ki == 0)
    def _():
        q_scr[...] = (q_ref[...] * (dh ** -0.5 * LOG2E)).astype(q_scr.dtype)
        m_scr[...] = jnp.full_like(m_scr, NEG)
        l_scr[...] = jnp.zeros_like(l_scr)
        acc_scr[...] = jnp.zeros_like(acc_scr)

    def update(mm, s, shift):
        m_prev = m_scr[mm]
        if shift.ndim == 0:
            m_new = jnp.maximum(m_prev, jnp.max(s, axis=0, keepdims=True) + shift)
            p = jnp.exp2(s - (m_new - shift))
        else:
            s = s + shift
            m_new = jnp.maximum(m_prev, jnp.max(s, axis=0, keepdims=True))
            p = jnp.exp2(s - m_new)
        alpha = jnp.exp2(m_prev - m_new)
        l_scr[mm] = alpha * l_scr[mm] + jnp.sum(p, axis=0, keepdims=True)
        acc_scr[mm] = alpha * acc_scr[mm] + _dot(v_ref[...], p)
        m_scr[mm] = m_new

    def logits(mm):
        return _dot_nt(k_ref[:, mm * dh:(mm + 1) * dh], q_scr[:, mm * dh:(mm + 1) * dh])

    @pl.when(ki == qi)
    def _():
        for mm in range(2):
            update(mm, logits(mm), bias_scr[mm, 0])

    @pl.when(ki == qi - 1)
    def _():
        for mm in range(2):
            update(mm, logits(mm), bias_scr[mm, 1])

    @pl.when(ki < qi - 1)
    def _():
        for mm in range(2):
            update(mm, logits(mm), rb_ref[N_BUCKETS - 1, 2 * h + mm] * LOG2E)

    @pl.when(ki == qi)
    def _():
        lam = _lambda(lam_ref, lam_init)
        out = acc_scr[0] / l_scr[0] - lam * (acc_scr[1] / l_scr[1])
        out = out * lax.rsqrt(jnp.mean(out * out, axis=0, keepdims=True) + NORM_EPS)
        o_ref[...] = (out * (sub_ref[...] * (1.0 - lam_init))).astype(o_ref.dtype)


def _flash_diff(qkv3, v_t, rel_bias, lambdas, subln_col, lam_init, out_dtype, tb):
    b, t, _ = qkv3.shape
    tb = min(tb, t)
    assert t % tb == 0 and tb >= MAX_DISTANCE
    nq = t // tb
    dd = 2 * DIFF_HEAD_DIM
    j = np.arange(tb)[:, None]
    i = np.arange(tb)[None, :]
    bkt_diag = jnp.asarray(_t5_bucket(np.maximum(i - j, 0)))
    bkt_sub = jnp.asarray(_t5_bucket(tb + i - j))
    koff = D_MODEL // dd
    pairs = [(qi, ki) for qi in range(nq) for ki in range(qi + 1)]
    qi_tbl = jnp.asarray([pq for pq, _ in pairs], jnp.int32)
    ki_tbl = jnp.asarray([pk for _, pk in pairs], jnp.int32)
    grid_spec = pltpu.PrefetchScalarGridSpec(
        num_scalar_prefetch=2, grid=(DIFF_HEADS, b, len(pairs)),
        in_specs=[pl.BlockSpec((None, tb, dd), lambda h, bb, s, qt, kt: (bb, qt[s], h)),
                  pl.BlockSpec((None, tb, dd), lambda h, bb, s, qt, kt: (bb, kt[s], koff + h)),
                  pl.BlockSpec((None, dd, tb), lambda h, bb, s, qt, kt: (bb, h, kt[s])),
                  pl.BlockSpec((tb, tb), lambda h, bb, s, qt, kt: (0, 0)),
                  pl.BlockSpec((tb, tb), lambda h, bb, s, qt, kt: (0, 0)),
                  pl.BlockSpec(memory_space=pltpu.SMEM),
                  pl.BlockSpec((4, DIFF_HEAD_DIM), lambda h, bb, s, qt, kt: (0, 0)),
                  pl.BlockSpec((dd, 1), lambda h, bb, s, qt, kt: (0, 0))],
        out_specs=pl.BlockSpec((None, dd, tb), lambda h, bb, s, qt, kt: (bb, h, qt[s])),
        scratch_shapes=[pltpu.VMEM((tb, dd), BF16), pltpu.VMEM((2, 1, tb), F32), pltpu.VMEM((2, 1, tb), F32),
                        pltpu.VMEM((2, dd, tb), F32), pltpu.VMEM((2, 2, tb, tb), F32)])
    return pl.pallas_call(
        functools.partial(_flash_kernel, lam_init=lam_init), grid_spec=grid_spec,
        out_shape=jax.ShapeDtypeStruct((b, D_MODEL, t), out_dtype),
        compiler_params=_params("parallel", "arbitrary", "arbitrary"), name="flash_diff")(
            qi_tbl, ki_tbl, qkv3, qkv3, v_t, bkt_diag, bkt_sub, rel_bias, lambdas, subln_col)


def _decode_kernel(pt_ref, q_ref, *refs, pp, n_steps, lam_init):
    kp_refs, vp_refs = refs[:pp], refs[pp:2 * pp]
    kn_ref, vn_ref, bkt_ref, tbl_ref, lam_ref, sub_ref, o_ref, m_scr, l_scr, acc_scr, bias_scr, p_scr, a_scr = refs[2 * pp:]
    step = pl.program_id(1)
    nh, dh = DIFF_HEADS, DIFF_HEAD_DIM
    ones = jnp.ones((dh, LANES), BF16)
    qs = q_ref[...] * (dh ** -0.5 * LOG2E)

    @pl.when(step == 0)
    def _():
        m_scr[...] = jnp.full_like(m_scr, NEG)
        l_scr[...] = jnp.zeros_like(l_scr)
        acc_scr[...] = jnp.zeros_like(acc_scr)
        bkt = bkt_ref[...]
        bias = jnp.zeros(bkt.shape, F32)
        for bk in range(N_BUCKETS):
            bias = jnp.where(bkt == bk, (tbl_ref[bk] * LOG2E)[None], bias)
        bias_scr[...] = bias

    def lane_sum(x):
        return _dot(x, ones)

    def accumulate(pr, shift_row, values):
        m_prev = m_scr[...]
        if pr.ndim == 3:
            m_new = jnp.maximum(m_prev, jnp.max(pr, axis=0) + shift_row)
            p = jnp.exp2(pr - (m_new - shift_row)[None])
            psum = jnp.sum(p, axis=0)
        else:
            m_new = jnp.maximum(m_prev, pr + shift_row)
            p = jnp.exp2(pr - (m_new - shift_row))
            psum = p
        alpha = jnp.exp2(m_prev - m_new)
        l_scr[...] = alpha * l_scr[...] + psum
        m_scr[...] = m_new
        a_scr[...] = alpha
        if pr.ndim == 3:
            p_scr[...] = p
        else:
            p_scr[0] = p
        for par in range(2):
            ae = a_scr[pl.ds(par, nh, stride=2), :]
            if pr.ndim == 3:
                pe = p_scr[:, pl.ds(par, nh, stride=2), :]
                pv = jnp.sum(jnp.concatenate([pe, pe], axis=-1) * values(), axis=0)
            else:
                pe = p_scr[0, pl.ds(par, nh, stride=2), :]
                pv = jnp.concatenate([pe, pe], axis=-1) * values()
            acc_scr[par] = jnp.concatenate([ae, ae], axis=-1) * acc_scr[par] + pv

    def page_logits(j):
        prod = kp_refs[j][...] * qs[None]
        return lane_sum(prod.reshape(PAGE_SIZE * 2 * nh, dh)).reshape(PAGE_SIZE, 2 * nh, LANES)

    far = tbl_ref[N_BUCKETS - 1] * LOG2E
    zero = jnp.zeros_like(far)
    for j in range(pp - 1):
        accumulate(page_logits(j), far, lambda j=j: vp_refs[j][...])

    @pl.when(step < n_steps - 1)
    def _():
        accumulate(page_logits(pp - 1), far, lambda: vp_refs[pp - 1][...])

    @pl.when(step == n_steps - 1)
    def _():
        accumulate(page_logits(pp - 1) + bias_scr[...], zero, lambda: vp_refs[pp - 1][...])
        accumulate(lane_sum(kn_ref[...] * qs) + tbl_ref[0] * LOG2E, zero, lambda: vn_ref[...])
        l_scr_v = l_scr[...]
        a_scr[...] = l_scr_v
        outs = []
        for par in range(2):
            le = a_scr[pl.ds(par, nh, stride=2), :]
            outs.append(acc_scr[par] / jnp.concatenate([le, le], axis=-1))
        out = outs[0] - _lambda(lam_ref, lam_init) * outs[1]
        o_ref[...] = _rms(out, sub_ref[...]) * (1.0 - lam_init)


def _decode_diff(q, k_new, v_new, cache_k, cache_v, layer, page_table, rel_bias, lambdas, subln, lam_init, pp):
    b, n_pages = page_table.shape
    nh, dh = DIFF_HEADS, DIFF_HEAD_DIM
    pp = min(pp, n_pages)
    assert n_pages % pp == 0 and PAGE_SIZE >= MAX_DISTANCE and dh == LANES
    n_steps = n_pages // pp
    bkt_last = jnp.asarray(np.broadcast_to(
        _t5_bucket(PAGE_SIZE - np.arange(PAGE_SIZE)).reshape(PAGE_SIZE, 1, 1), (PAGE_SIZE, 2 * nh, LANES)))
    tbl = jnp.broadcast_to(rel_bias[:, :, None], (N_BUCKETS, 2 * nh, LANES))
    page_spec = lambda j, width, lanes: pl.BlockSpec(
        (None, None, PAGE_SIZE, width, lanes), lambda bb, s, pt: (layer, pt[bb, s * pp + j], 0, 0, 0))
    full = lambda shape: pl.BlockSpec(shape, lambda bb, s, pt: (0,) * len(shape))
    per_seq = lambda shape: pl.BlockSpec((None,) + shape, lambda bb, s, pt: (bb,) + (0,) * len(shape))
    grid_spec = pltpu.PrefetchScalarGridSpec(
        num_scalar_prefetch=1, grid=(b, n_steps),
        in_specs=([per_seq((2 * nh, dh))]
                  + [page_spec(j, 2 * nh, dh) for j in range(pp)]
                  + [page_spec(j, nh, 2 * dh) for j in range(pp)]
                  + [per_seq((2 * nh, dh)), per_seq((nh, 2 * dh)),
                     full((PAGE_SIZE, 2 * nh, LANES)), full((N_BUCKETS, 2 * nh, LANES)),
                     full((4, dh)), full((1, 2 * dh))]),
        out_specs=per_seq((nh, 2 * dh)),
        scratch_shapes=[pltpu.VMEM((2 * nh, LANES), F32), pltpu.VMEM((2 * nh, LANES), F32),
                        pltpu.VMEM((2, nh, 2 * dh), F32), pltpu.VMEM((PAGE_SIZE, 2 * nh, LANES), F32),
                        pltpu.VMEM((PAGE_SIZE, 2 * nh, LANES), F32), pltpu.VMEM((2 * nh, LANES), F32)])
    return pl.pallas_call(
        functools.partial(_decode_kernel, pp=pp, n_steps=n_steps, lam_init=lam_init),
        grid_spec=grid_spec,
        out_shape=jax.ShapeDtypeStruct((b, nh, 2 * dh), F32),
        compiler_params=_params("parallel", "arbitrary"), name="decode_diff")(
            page_table, q, *([cache_k] * pp), *([cache_v] * pp), k_new, v_new, bkt_last, tbl, lambdas, subln)


class _Tiles:
    def __init__(self, prompt):
        self.prompt = prompt
        self.act_dtype = BF16 if prompt else F32
        self.tm = 1024 if prompt else SUBLANES
        self.tn = 1024
        self.tn_ff = 512
        self.tm_res = 256 if prompt else SUBLANES
        self.tk_res = 512
        self.tt_prep = 256 if prompt else SUBLANES
        self.tc_prep = 1024
        self.cb = 4 if prompt else 1
        self.gdn_hpb = 4
        self.tb_attn = 512
        self.decode_pages = 8


def _group_trunk(x3, tiles, gdn_s0, gdn_c0, gla_s0, cache_k, cache_v, page_table, p, wb):
    b, t, d = x3.shape
    m = b * t
    x = x3.reshape(m, d)
    if not tiles.prompt:
        assert t == 1 and m % SUBLANES == 0
    t_pad = t if tiles.prompt else CHUNK
    t_valid = None if tiles.prompt else t
    nw = lambda i, jn: p['norm_w'][i, jn].reshape(1, d)
    adt = tiles.act_dtype

    def pad_time(a3, to):
        return a3 if a3.shape[1] == to else jnp.pad(a3, ((0, 0), (0, to - a3.shape[1]), (0, 0)))

    def big_matmul(a, name, layer):
        if tiles.prompt:
            return _matmul(a, wb[name, layer], layer, F32, tiles.tm, tiles.tn)
        out, wb[name, layer] = _matmul(a, p[name], layer, F32, tiles.tm, tiles.tn, emit_wb=True)
        return out

    def residual_matmul(a, name, layer, x_in, nw_post, nw_next):
        if tiles.prompt:
            return _matmul_residual(a, wb[name, layer], layer, x_in, nw_post, nw_next, adt,
                                    tiles.tm_res, tiles.tk_res)
        xo, h_next, wb[name, layer] = _matmul_residual(a, p[name], layer, x_in, nw_post, nw_next, adt,
                                                       tiles.tm_res, tiles.tk_res, emit_wb=True)
        return xo, h_next

    def ffn_up(a, layer):
        if tiles.prompt:
            return _swiglu_up(a, wb['ffn_gate', layer], wb['ffn_up', layer], layer, adt, tiles.tm, tiles.tn_ff)
        act_, wb['ffn_gate', layer], wb['ffn_up', layer] = _swiglu_up(
            a, p['ffn_w_up'], p['ffn_w_up'], layer, adt, tiles.tm, tiles.tn_ff, emit_wb=True)
        return act_

    gdn_s, gdn_c, gla_s, att_k, att_v = [], [], [], [], []
    h = _rmsnorm(x, nw(0, 0), adt, tiles.tm)
    for i in range(DEPTH):
        j = i // N_MIXERS
        kind = i % N_MIXERS
        if kind == 0:
            beta, g = _gdn_gates(h, p['gdn_w_ba'], j, p['gdn_a_log'][j].reshape(1, -1),
                                 p['gdn_dt_bias'][j].reshape(1, -1), tiles.tm)
            buf0 = gdn_c0[j]
            nbuf = GDN_CONV_W - 1
            buf8 = jnp.pad(buf0, ((0, 0), (SUBLANES - nbuf, 0), (0, 0)))
            if tiles.prompt:
                proj, tail = _gdn_in_proj(h, wb['gdn_w_in', j], buf8, p['gdn_conv_w'][j], t, tiles.tm, tiles.tn)
                qkv3 = proj3p = proj.reshape(b, t, -1)
                new_buf = tail[:, SUBLANES - nbuf:, :GDN_CONV_DIM]
            else:
                proj3 = big_matmul(h, 'gdn_w_in', j).reshape(b, t, -1)
                new_buf = jnp.concatenate([buf0[:, t:], proj3[..., :GDN_CONV_DIM]], axis=1)
                proj3p = pad_time(proj3, SUBLANES)
                qkv3 = pad_time(_gdn_prep(proj3p, buf8, p['gdn_conv_w'][j], tiles.tt_prep, tiles.tc_prep), t_pad)
                proj3p = pad_time(proj3p, t_pad)

            def heads(a):
                a = pad_time(a.reshape(b, t, GDN_QK_HEADS * 2), t_pad)
                return a.reshape(b, t_pad, GDN_QK_HEADS, 2).transpose(0, 2, 1, 3)
            gcol, bcol = heads(g), heads(beta)
            o3, s_new = _gdn_delta(qkv3, proj3p, gcol, bcol, gcol.transpose(0, 1, 3, 2), gdn_s0[j],
                                   p['gdn_norm_w'][j].reshape(1, -1), adt, tiles.cb, tiles.gdn_hpb, t_valid)
            mix_in, w_out = o3[:, :t].reshape(m, -1), 'gdn_w_out'
            gdn_s.append(s_new)
            gdn_c.append(new_buf)
        elif kind == 1:
            qkvg = big_matmul(h, 'gla_w_qkvg', j)
            low = _matmul(h, p['gla_w_gk1'], j, F32, tiles.tm, tiles.tn)
            gk = _matmul(low, p['gla_w_gk2'], j, F32, tiles.tm, tiles.tn)
            o3, s_new = _gla_chunks(pad_time(qkvg.reshape(b, t, -1), t_pad), pad_time(gk.reshape(b, t, -1), t_pad),
                                    p['gla_b_gk'][j].reshape(1, -1), gla_s0[j],
                                    p['gla_norm_w'][j].reshape(1, -1), adt, tiles.cb, t_valid)
            mix_in, w_out = o3[:, :t].reshape(m, -1), 'gla_w_out'
            gla_s.append(s_new)
        else:
            lam_init = 0.8 - 0.6 * math.exp(-0.3 * i)
            qkv = big_matmul(h, 'diff_w_qkv', j)
            nh, dh = DIFF_HEADS, DIFF_HEAD_DIM
            k_new = qkv[:, D_MODEL:2 * D_MODEL].reshape(b, t, 2 * nh, dh)
            v_new = qkv[:, 2 * D_MODEL:].reshape(b, t, nh, 2 * dh)
            subln = p['diff_subln'][j].reshape(1, -1)
            if tiles.prompt:
                v_t = jnp.swapaxes(qkv[:, 2 * D_MODEL:].reshape(b, t, D_MODEL), 1, 2).astype(BF16)
                o_t = _flash_diff(qkv.reshape(b, t, -1), v_t, p['rel_bias'], p['diff_lambda'][j],
                                  subln.reshape(-1, 1), lam_init, adt, tiles.tb_attn)
                mix_in = jnp.swapaxes(o_t, 1, 2).reshape(m, -1)
            else:
                o = _decode_diff(qkv[:, :D_MODEL].reshape(b, 2 * nh, dh), k_new.reshape(b, 2 * nh, dh),
                                 v_new.reshape(b, nh, 2 * dh), cache_k, cache_v, j, page_table,
                                 p['rel_bias'], p['diff_lambda'][j], subln, lam_init, tiles.decode_pages)
                mix_in = o.reshape(m, -1)
            w_out = 'diff_w_out'
            att_k.append(k_new)
            att_v.append(v_new)
        x, h = residual_matmul(mix_in, w_out, j, x, nw(i, 1), nw(i, 2))
        act = ffn_up(h, i)
        nxt = nw(i + 1, 0) if i + 1 < DEPTH else None
        x, h = residual_matmul(act, 'ffn_w_down', i, x, nw(i, 3), nxt)
    return (x.reshape(b, t, d), jnp.stack(gdn_s), jnp.stack(gdn_c), jnp.stack(gla_s),
            jnp.stack(att_k), jnp.stack(att_v))


def kernel(x_prompt, x_sample, state_gdn, state_gdn_conv, state_gla, cache_k, cache_v, page_table,
           norm_w, ffn_w_up, ffn_w_down, rel_bias,
           gdn_w_in, gdn_w_ba, gdn_conv_w, gdn_a_log, gdn_dt_bias, gdn_norm_w, gdn_w_out,
           gla_w_qkvg, gla_w_gk1, gla_w_gk2, gla_b_gk, gla_norm_w, gla_w_out,
           diff_w_qkv, diff_lambda, diff_subln, diff_w_out):
    p = dict(norm_w=norm_w, ffn_w_up=ffn_w_up, ffn_w_down=ffn_w_down, rel_bias=rel_bias,
             gdn_w_in=gdn_w_in, gdn_w_ba=gdn_w_ba, gdn_conv_w=gdn_conv_w, gdn_a_log=gdn_a_log,
             gdn_dt_bias=gdn_dt_bias, gdn_norm_w=gdn_norm_w, gdn_w_out=gdn_w_out,
             gla_w_qkvg=gla_w_qkvg, gla_w_gk1=gla_w_gk1, gla_w_gk2=gla_w_gk2, gla_b_gk=gla_b_gk,
             gla_norm_w=gla_norm_w, gla_w_out=gla_w_out,
             diff_w_qkv=diff_w_qkv, diff_lambda=diff_lambda, diff_subln=diff_subln, diff_w_out=diff_w_out)
    bp = x_prompt.shape[0]
    n_gdn, n_gla = state_gdn.shape[0], state_gla.shape[0]
    zeros_gdn = jnp.zeros((n_gdn, bp) + state_gdn.shape[2:], F32)
    zeros_conv = jnp.zeros((n_gdn, bp) + state_gdn_conv.shape[2:], F32)
    zeros_gla = jnp.zeros((n_gla, bp) + state_gla.shape[2:], F32)
    wb = {}
    outs_s = _group_trunk(x_sample, _Tiles(False), state_gdn, state_gdn_conv, state_gla,
                          cache_k, cache_v, page_table, p, wb)
    outs_p = _group_trunk(x_prompt, _Tiles(True), zeros_gdn, zeros_conv, zeros_gla, None, None, None, p, wb)
    return (outs_p[0], outs_s[0]) + outs_p[1:] + outs_s[1:]
```

```python
import functools
import math

import numpy as np
import jax
import jax.numpy as jnp
from jax import lax
from jax.experimental import pallas as pl
from jax.experimental.pallas import tpu as pltpu

F32 = jnp.float32
BF16 = jnp.bfloat16

D_MODEL = 2048
DEPTH = 4
PAGE_SIZE = 128
N_MIXERS = 3
GDN_HEAD_DIM = 128
GDN_QK_HEADS = D_MODEL // 128
GDN_V_HEADS = 2 * GDN_QK_HEADS
GDN_QK_DIM = GDN_QK_HEADS * GDN_HEAD_DIM
GDN_V_DIM = GDN_V_HEADS * GDN_HEAD_DIM
GDN_CONV_DIM = 2 * GDN_QK_DIM + GDN_V_DIM
GDN_CONV_W = 4
GLA_HEADS = 4
GLA_DK = D_MODEL // 2 // GLA_HEADS
GLA_DV = D_MODEL // GLA_HEADS
GLA_GATE_NORMALIZER = 16.0
DIFF_HEAD_DIM = 128
DIFF_HEADS = D_MODEL // (2 * DIFF_HEAD_DIM)
N_BUCKETS = 32
MAX_DISTANCE = 128
D_FF = ((8 * D_MODEL + 3 * 256 - 1) // (3 * 256)) * 256
NORM_EPS = 1e-6

LANES = 128
SUBLANES = 8
CHUNK = 64
SUB = 16
GDN_PROJ_SUBBLOCKS = 4
VMEM_LIMIT_BYTES = 48 * 1024 * 1024
VMEM_LIMIT_BIG_BYTES = 56 * 1024 * 1024
NEG = -1e30
LOG2E = math.log2(math.e)


def _params(*sem, vmem=VMEM_LIMIT_BYTES):
    return pltpu.CompilerParams(dimension_semantics=sem, vmem_limit_bytes=vmem)


def _dot(a, b):
    return jnp.dot(a.astype(BF16), b.astype(BF16), preferred_element_type=F32)


def _dot_nt(a, b):
    return lax.dot_general(a.astype(BF16), b.astype(BF16), (((1,), (1,)), ((), ())),
                           preferred_element_type=F32)


def _dot_tn(a, b):
    return lax.dot_general(a.astype(BF16), b.astype(BF16), (((0,), (0,)), ((), ())),
                           preferred_element_type=F32)


def _split3(x):
    hi = x.astype(BF16)
    r = x - hi.astype(F32)
    mid = r.astype(BF16)
    lo = (r - mid.astype(F32)).astype(BF16)
    return hi, mid, lo


def _sigmoid(x):
    return 0.5 * jnp.tanh(0.5 * x) + 0.5


def _silu(x):
    h = 0.5 * x
    return h + h * jnp.tanh(h)


def _softplus(x):
    return jnp.maximum(x, 0.0) + jnp.log(1.0 + jnp.exp(-jnp.abs(x)))


def _rms(x, w):
    return x * lax.rsqrt(jnp.mean(x * x, axis=-1, keepdims=True) + NORM_EPS) * w


def _chunk_tril(r):
    i = lax.broadcasted_iota(jnp.int32, (r, r), 0)
    j = lax.broadcasted_iota(jnp.int32, (r, r), 1)
    return jnp.where((j <= i) & ((i // CHUNK) == (j // CHUNK)), 1.0, 0.0).astype(BF16)


def _rmsnorm_kernel(x_ref, w_ref, o_ref):
    o_ref[...] = _rms(x_ref[...], w_ref[...]).astype(o_ref.dtype)


def _rmsnorm(x, w, out_dtype, tm):
    m, d = x.shape
    tm = min(tm, m)
    return pl.pallas_call(
        _rmsnorm_kernel, grid=(m // tm,),
        in_specs=[pl.BlockSpec((tm, d), lambda i: (i, 0)), pl.BlockSpec((1, d), lambda i: (0, 0))],
        out_specs=pl.BlockSpec((tm, d), lambda i: (i, 0)),
        out_shape=jax.ShapeDtypeStruct((m, d), out_dtype),
        compiler_params=_params("parallel"), name="rmsnorm")(x, w)


def _weight_spec(w, layer, block, index):
    if w.ndim == 2:
        return pl.BlockSpec(block, index)
    return pl.BlockSpec((None,) + block, lambda *g: (layer,) + index(*g))


def _mm_kernel(a_ref, w_ref, o_ref, *wb_ref):
    w = w_ref[...].astype(BF16)
    if wb_ref:
        wb_ref[0][...] = w
    o_ref[...] = _dot(a_ref[...], w).astype(o_ref.dtype)


def _matmul(a, w, layer, out_dtype, tm, tn, emit_wb=False):
    m, k = a.shape
    n = w.shape[-1]
    tm, tn = min(tm, m), min(tn, n)
    assert m % tm == 0 and n % tn == 0 and (not emit_wb or m == tm)
    out_shape = [jax.ShapeDtypeStruct((m, n), out_dtype)]
    out_specs = [pl.BlockSpec((tm, tn), lambda i, j: (i, j))]
    if emit_wb:
        out_shape.append(jax.ShapeDtypeStruct((k, n), BF16))
        out_specs.append(pl.BlockSpec((k, tn), lambda i, j: (0, j)))
    res = pl.pallas_call(
        _mm_kernel, grid=(m // tm, n // tn),
        in_specs=[pl.BlockSpec((tm, k), lambda i, j: (i, 0)),
                  _weight_spec(w, layer, (k, tn), lambda i, j: (0, j))],
        out_specs=out_specs, out_shape=out_shape,
        compiler_params=_params("parallel", "parallel"), name="matmul")(a, w)
    return (res[0], res[1]) if emit_wb else res[0]


def _swiglu_kernel(a_ref, wg_ref, wu_ref, o_ref, *wb_refs):
    a = a_ref[...].astype(BF16)
    wg = wg_ref[...].astype(BF16)
    wu = wu_ref[...].astype(BF16)
    if wb_refs:
        wb_refs[0][...] = wg
        wb_refs[1][...] = wu
    g = _dot(a, wg)
    u = _dot(a, wu)
    o_ref[...] = (_silu(g) * u).astype(o_ref.dtype)


def _swiglu_up(a, w_gate, w_up, layer, out_dtype, tm, tn, emit_wb=False):
    m, k = a.shape
    stacked = w_up.ndim == 3
    f = w_up.shape[-1] // 2 if stacked else w_up.shape[-1]
    tm, tn = min(tm, m), min(tn, f)
    assert m % tm == 0 and f % tn == 0 and (not emit_wb or m == tm)
    nf = f // tn
    up_off = nf if stacked else 0
    out_shape = [jax.ShapeDtypeStruct((m, f), out_dtype)]
    out_specs = [pl.BlockSpec((tm, tn), lambda i, j: (i, j))]
    if emit_wb:
        out_shape += [jax.ShapeDtypeStruct((k, f), BF16)] * 2
        out_specs += [pl.BlockSpec((k, tn), lambda i, j: (0, j))] * 2
    res = pl.pallas_call(
        _swiglu_kernel, grid=(m // tm, nf),
        in_specs=[pl.BlockSpec((tm, k), lambda i, j: (i, 0)),
                  _weight_spec(w_gate, layer, (k, tn), lambda i, j: (0, j)),
                  _weight_spec(w_up, layer, (k, tn), lambda i, j: (0, j + up_off))],
        out_specs=out_specs, out_shape=out_shape,
        compiler_params=_params("parallel", "parallel"), name="swiglu_up")(a, w_gate, w_up)
    return tuple(res) if emit_wb else res[0]


def _residual_epilogue(mix, x_ref, nw1_ref, nw2_ref, xo_ref, h_ref):
    xn = x_ref[...] + _rms(mix, nw1_ref[...])
    xo_ref[...] = xn
    if h_ref:
        h_ref[0][...] = _rms(xn, nw2_ref[...]).astype(h_ref[0].dtype)


def _mm_res_kernel(a_ref, w_ref, x_ref, nw1_ref, nw2_ref, xo_ref, *rest, nk, emit_h, emit_wb):
    h_ref = rest[:1] if emit_h else ()
    kk = pl.program_id(1)
    w = w_ref[...].astype(BF16)
    if emit_wb:
        rest[-1][...] = w
    part = _dot(a_ref[...], w)

    @pl.when(kk == 0)
    def _():
        xo_ref[...] = part

    @pl.when(kk > 0)
    def _():
        xo_ref[...] += part

    @pl.when(kk == nk - 1)
    def _():
        _residual_epilogue(xo_ref[...], x_ref, nw1_ref, nw2_ref, xo_ref, h_ref)


def _mm_res_resident_kernel(a_ref, w_ref, x_ref, nw1_ref, nw2_ref, xo_ref, *h_ref):
    _residual_epilogue(_dot(a_ref[...], w_ref[...]), x_ref, nw1_ref, nw2_ref, xo_ref, h_ref)


def _matmul_residual(a, w, layer, x, nw_post, nw_next, h_dtype, tm, tk, emit_wb=False):
    m, k = a.shape
    n = w.shape[-1]
    tm = min(tm, m)
    assert m % tm == 0 and (not emit_wb or m == tm)
    emit_h = nw_next is not None
    if not emit_h:
        nw_next = nw_post
    resident = w.ndim == 2
    if resident:
        grid = (m // tm,)
        row = lambda i: (i, 0)
        fixed = lambda i: (0, 0)
        a_spec = pl.BlockSpec((tm, k), row)
        w_spec = pl.BlockSpec((k, n), fixed, pipeline_mode=pl.Buffered(1))
        x_spec = pl.BlockSpec((tm, n), row)
        body = _mm_res_resident_kernel
        sem = ("parallel",)
    else:
        tk = min(tk, k)
        assert k % tk == 0
        nk = k // tk
        grid = (m // tm, nk)
        row = lambda i, kk: (i, 0)
        fixed = lambda i, kk: (0, 0)
        a_spec = pl.BlockSpec((tm, tk), lambda i, kk: (i, kk))
        w_spec = pl.BlockSpec((None, tk, n), lambda i, kk: (layer, kk, 0))
        x_spec = pl.BlockSpec((tm, n), row, pipeline_mode=pl.Buffered(1))
        body = functools.partial(_mm_res_kernel, nk=nk, emit_h=emit_h, emit_wb=emit_wb)
        sem = ("parallel", "arbitrary")
    out_shape = [jax.ShapeDtypeStruct((m, n), F32)]
    out_specs = [pl.BlockSpec((tm, n), row)]
    if emit_h:
        out_shape.append(jax.ShapeDtypeStruct((m, n), h_dtype))
        out_specs.append(pl.BlockSpec((tm, n), row))
    if emit_wb:
        out_shape.append(jax.ShapeDtypeStruct((k, n), BF16))
        out_specs.append(pl.BlockSpec((tk, n), lambda i, kk: (kk, 0)))
    res = pl.pallas_call(
        body, grid=grid,
        in_specs=[a_spec, w_spec, x_spec, pl.BlockSpec((1, n), fixed), pl.BlockSpec((1, n), fixed)],
        out_specs=out_specs, out_shape=out_shape,
        compiler_params=_params(*sem, vmem=VMEM_LIMIT_BIG_BYTES), name="matmul_residual")(
            a, w, x, nw_post, nw_next)
    xo = res[0]
    h = res[1] if emit_h else None
    return (xo, h, res[-1]) if emit_wb else (xo, h)


def _gdn_gates_kernel(a_ref, w_ref, alog_ref, dtb_ref, beta_ref, g_ref):
    ba = _dot(a_ref[...], w_ref[...])
    nh = beta_ref.shape[-1]
    beta_ref[...] = _sigmoid(ba[:, :nh])
    g_ref[...] = -jnp.exp(alog_ref[...]) * _softplus(ba[:, nh:] + dtb_ref[...])


def _gdn_gates(a, w_ba, layer, a_log, dt_bias, tm):
    m, k = a.shape
    nh = GDN_V_HEADS
    tm = min(tm, m)
    return pl.pallas_call(
        _gdn_gates_kernel, grid=(m // tm,),
        in_specs=[pl.BlockSpec((tm, k), lambda i: (i, 0)),
                  pl.BlockSpec((None, k, 2 * nh), lambda i: (layer, 0, 0)),
                  pl.BlockSpec((1, nh), lambda i: (0, 0)),
                  pl.BlockSpec((1, nh), lambda i: (0, 0))],
        out_specs=[pl.BlockSpec((tm, nh), lambda i: (i, 0))] * 2,
        out_shape=[jax.ShapeDtypeStruct((m, nh), F32)] * 2,
        compiler_params=_params("parallel"), name="gdn_gates")(a, w_ba, a_log, dt_bias)


def _conv_silu_norm(x, halo, ext_scr, cw_ref, j, n_qk_tiles):
    tt, tc = x.shape
    ext_scr[:SUBLANES, :] = halo
    ext_scr[SUBLANES:, :] = x
    y = x * cw_ref[GDN_CONV_W - 1:GDN_CONV_W, :]
    for s in range(1, GDN_CONV_W):
        y = y + ext_scr[SUBLANES - s:SUBLANES - s + tt, :] * cw_ref[GDN_CONV_W - 1 - s:GDN_CONV_W - s, :]
    y = _silu(y)
    qscale = jnp.where(j < n_qk_tiles, GDN_HEAD_DIM ** -0.5, 1.0)
    pieces = []
    for hh in range(tc // GDN_HEAD_DIM):
        yh = y[:, hh * GDN_HEAD_DIM:(hh + 1) * GDN_HEAD_DIM]
        r = lax.rsqrt(jnp.sum(yh * yh, axis=-1, keepdims=True) + NORM_EPS) * qscale
        pieces.append(yh * jnp.where(j < 2 * n_qk_tiles, r, 1.0))
    return jnp.concatenate(pieces, axis=1)


def _gdn_prep_kernel(x_ref, prev_ref, buf_ref, cw_ref, o_ref, ext_scr, *, n_qk_tiles):
    i = pl.program_id(1)
    j = pl.program_id(2)
    halo = jnp.where(i == 0, buf_ref[...], prev_ref[...])
    o_ref[...] = _conv_silu_norm(x_ref[...], halo, ext_scr, cw_ref, j, n_qk_tiles)


def _gdn_in_proj_kernel(a_ref, w_ref, buf_ref, cw_ref, o_ref, tail_ref, ext_scr, halo_scr, *,
                        tiles_per_seq, n_conv_tiles, n_qk_tiles):
    i = pl.program_id(0)
    j = pl.program_id(1)

    @pl.when((i == 0) & (j == 0))
    def _():
        halo_scr[...] = jnp.zeros_like(halo_scr)

    halo = jnp.where(i % tiles_per_seq == 0, buf_ref[...], halo_scr[j])
    w = w_ref[...]
    n_sub, rows = ext_scr.shape[0], ext_scr.shape[1] - SUBLANES
    for s in range(n_sub):
        sl = slice(s * rows, (s + 1) * rows)
        x = _dot(a_ref[sl, :], w)
        y = _conv_silu_norm(x, halo, ext_scr.at[s], cw_ref, j, n_qk_tiles)
        o_ref[sl, :] = jnp.where(j < n_conv_tiles, y, x)
        halo = x[rows - SUBLANES:, :]
    halo_scr[j] = halo
    tail_ref[...] = halo


def _gdn_in_proj(a, w, buf8, conv_w, seq_len, tm, tn):
    m, k = a.shape
    n = w.shape[-1]
    b = m // seq_len
    tm = min(tm, seq_len)
    assert w.ndim == 2 and seq_len % tm == 0 and n % tn == 0 and GDN_QK_DIM % tn == 0
    assert tm % (GDN_PROJ_SUBBLOCKS * SUBLANES) == 0
    tps = seq_len // tm
    nct = GDN_CONV_DIM // tn
    proj, tails = pl.pallas_call(
        functools.partial(_gdn_in_proj_kernel, tiles_per_seq=tps, n_conv_tiles=nct, n_qk_tiles=GDN_QK_DIM // tn),
        grid=(m // tm, n // tn),
        in_specs=[pl.BlockSpec((tm, k), lambda i, j: (i, 0)),
                  pl.BlockSpec((k, tn), lambda i, j: (0, j)),
                  pl.BlockSpec((None, SUBLANES, tn), lambda i, j: (i // tps, 0, jnp.minimum(j, nct - 1))),
                  pl.BlockSpec((GDN_CONV_W, tn), lambda i, j: (0, jnp.minimum(j, nct - 1)))],
        out_specs=[pl.BlockSpec((tm, tn), lambda i, j: (i, j)),
                   pl.BlockSpec((None, SUBLANES, tn), lambda i, j: (i, 0, j))],
        out_shape=[jax.ShapeDtypeStruct((m, n), F32), jax.ShapeDtypeStruct((m // tm, SUBLANES, n), F32)],
        scratch_shapes=[pltpu.VMEM((GDN_PROJ_SUBBLOCKS, SUBLANES + tm // GDN_PROJ_SUBBLOCKS, tn), F32),
                        pltpu.VMEM((n // tn, SUBLANES, tn), F32)],
        compiler_params=_params("arbitrary", "arbitrary", vmem=VMEM_LIMIT_BIG_BYTES), name="gdn_in_proj")(
            a, w, buf8, conv_w)
    return proj, tails[tps - 1::tps]


def _gdn_prep(proj3, buf8, conv_w, tt, tc):
    b, t, _ = proj3.shape
    tt = min(tt, t)
    assert t % tt == 0 and tt % SUBLANES == 0 and GDN_QK_DIM % tc == 0
    hb = tt // SUBLANES
    return pl.pallas_call(
        functools.partial(_gdn_prep_kernel, n_qk_tiles=GDN_QK_DIM // tc),
        grid=(b, t // tt, GDN_CONV_DIM // tc),
        in_specs=[pl.BlockSpec((None, tt, tc), lambda bb, i, j: (bb, i, j)),
                  pl.BlockSpec((None, SUBLANES, tc), lambda bb, i, j: (bb, jnp.maximum(i * hb - 1, 0), j)),
                  pl.BlockSpec((None, SUBLANES, tc), lambda bb, i, j: (bb, 0, j)),
                  pl.BlockSpec((GDN_CONV_W, tc), lambda bb, i, j: (0, j))],
        out_specs=pl.BlockSpec((None, tt, tc), lambda bb, i, j: (bb, i, j)),
        out_shape=jax.ShapeDtypeStruct((b, t, GDN_CONV_DIM), F32),
        scratch_shapes=[pltpu.VMEM((SUBLANES + tt, tc), F32)],
        compiler_params=_params("parallel", "parallel", "parallel"), name="gdn_prep")(
            proj3, proj3, buf8, conv_w)


def _gdn_delta_kernel(q_ref, k_ref, v_ref, z_ref, gc_ref, bc_ref, gr_ref, s0_ref, nw_ref,
                      o_ref, so_ref, s_scr, *, cb, hpb, t_valid, n_blocks):
    n = pl.program_id(2)
    r = cb * CHUNK
    dh = GDN_HEAD_DIM

    @pl.when(n == 0)
    def _():
        s_scr[...] = s0_ref[...]

    q = q_ref[...]
    k = k_ref[...]
    v = v_ref[...]
    gcol = gc_ref[...]
    bcol = bc_ref[...]
    grow = gr_ref[...]
    if t_valid is not None:
        okc = (n * r + lax.broadcasted_iota(jnp.int32, (r, 1), 0)) < t_valid
        okr = (n * r + lax.broadcasted_iota(jnp.int32, (1, r), 1)) < t_valid
        k = jnp.where(okc, k, 0.0)
        v = jnp.where(okc, v, 0.0)
        gcol = jnp.where(okc, gcol, 0.0)
        bcol = jnp.where(okc, bcol, 0.0)
        grow = jnp.where(okr, grow, 0.0)

    lm = _chunk_tril(r)
    gcum_col = [sum(_dot(lm, p) for p in _split3(gcol[qh])) for qh in range(hpb)]
    gcum_row = [sum(_dot_nt(p, lm) for p in _split3(grow[qh])) for qh in range(hpb)]

    ri = lax.broadcasted_iota(jnp.int32, (r, r), 0)
    ci = lax.broadcasted_iota(jnp.int32, (r, r), 1)
    eye = jnp.where(ri == ci, 1.0, 0.0)
    tril = ((ri // CHUNK) == (ci // CHUNK)) & (ri >= ci)
    lvl_masks = []
    size = 1
    while size < CHUNK:
        lvl_masks.append(((ri // (2 * size)) == (ci // (2 * size)))
                         & ((ri // size) % 2 == 1) & ((ci // size) % 2 == 0))
        size *= 2
    if t_valid is not None and t_valid <= 1:
        lvl_masks = []

    nw = nw_ref[...]
    heads = [(qh, hh) for qh in range(hpb) for hh in range(2)]
    kq = [k[:, qh * dh:(qh + 1) * dh] for qh in range(hpb)]
    qq = [q[:, qh * dh:(qh + 1) * dh] for qh in range(hpb)]
    gram = [_dot_nt(kq[qh], kq[qh]) for qh in range(hpb)]
    qk = [_dot_nt(qq[qh], kq[qh]) for qh in range(hpb)]
    gc = [gcum_col[qh][:, hh:hh + 1] for qh, hh in heads]
    bc = [bcol[qh][:, hh:hh + 1] for qh, hh in heads]
    decay = [jnp.exp(jnp.where(tril, gc[i] - gcum_row[qh][hh:hh + 1, :], NEG)) for i, (qh, hh) in enumerate(heads)]
    mm_ = [(gram[qh] * bc[i] * decay[i]).astype(BF16) for i, (qh, hh) in enumerate(heads)]
    tinv = [eye.astype(BF16) for _ in heads]
    for lvl, msk in enumerate(lvl_masks):
        mskb = jnp.where(msk, 1.0, 0.0).astype(BF16)
        ml = [m * mskb for m in mm_]
        if lvl == 0:
            tinv = [t - m for t, m in zip(tinv, ml)]
        else:
            y = [_dot(m, t) for t, m in zip(tinv, ml)]
            tinv = [t - _dot(t, yy).astype(BF16) for t, yy in zip(tinv, y)]
    eg = [jnp.exp(g_) for g_ in gc]
    sol = [_dot(tinv[i], jnp.concatenate([v[:, (2 * qh + hh) * dh:(2 * qh + hh + 1) * dh] * bc[i],
                                          kq[qh] * (bc[i] * eg[i])], axis=1))
           for i, (qh, hh) in enumerate(heads)]
    asol = [_dot(qk[qh] * decay[i], sol[i]) for i, (qh, hh) in enumerate(heads)]
    o2 = [a[:, :dh] for a in asol]
    o1 = [qq[qh] * eg[i] - asol[i][:, dh:] for i, (qh, hh) in enumerate(heads)]
    glast = [[g_[(c + 1) * CHUNK - 1:(c + 1) * CHUNK, :] for c in range(cb)] for g_ in gc]
    kd = [kq[qh] * jnp.exp(jnp.concatenate([jnp.broadcast_to(gl, (CHUNK, 1)) for gl in glast[i]], axis=0) - gc[i])
          for i, (qh, hh) in enumerate(heads)]
    qw = [[_dot_tn(kd[i][c * CHUNK:(c + 1) * CHUNK], sol[i][c * CHUNK:(c + 1) * CHUNK]) for c in range(cb)]
          for i in range(len(heads))]
    s = [s_scr[2 * qh + hh] for qh, hh in heads]
    for c in range(cb):
        sl = slice(c * CHUNK, (c + 1) * CHUNK)
        for i, (qh, hh) in enumerate(heads):
            x = _dot(jnp.concatenate([qw[i][c][:, dh:], o1[i][sl]], axis=0), s[i])
            o = x[dh:] + o2[i][sl]
            s[i] = s[i] * jnp.exp(glast[i][c]) - x[:dh] + qw[i][c][:, :dh]
            col = slice((2 * qh + hh) * dh, (2 * qh + hh + 1) * dh)
            zc = z_ref[sl, col]
            o_ref[sl, col] = (_rms(o, nw) * _silu(zc)).astype(o_ref.dtype)
    for i, (qh, hh) in enumerate(heads):
        s_scr[2 * qh + hh] = s[i]

    @pl.when(n == n_blocks - 1)
    def _():
        so_ref[...] = s_scr[...]


def _gdn_delta(qkv3, proj3, gcol, bcol, grow, s0, norm_w, out_dtype, cb, hpb, t_valid):
    b, t, _ = qkv3.shape
    r = cb * CHUNK
    assert t % r == 0
    nb = t // r
    dh = GDN_HEAD_DIM
    qw_, vw_ = hpb * dh, 2 * hpb * dh
    assert GDN_QK_HEADS % hpb == 0
    koff = GDN_QK_DIM // qw_
    voff = 2 * GDN_QK_DIM // vw_
    zoff = GDN_CONV_DIM // vw_
    return pl.pallas_call(
        functools.partial(_gdn_delta_kernel, cb=cb, hpb=hpb, t_valid=t_valid, n_blocks=nb),
        grid=(b, GDN_QK_HEADS // hpb, nb),
        in_specs=[pl.BlockSpec((None, r, qw_), lambda bb, h, n: (bb, n, h)),
                  pl.BlockSpec((None, r, qw_), lambda bb, h, n: (bb, n, koff + h)),
                  pl.BlockSpec((None, r, vw_), lambda bb, h, n: (bb, n, voff + h)),
                  pl.BlockSpec((None, r, vw_), lambda bb, h, n: (bb, n, zoff + h)),
                  pl.BlockSpec((None, hpb, r, 2), lambda bb, h, n: (bb, h, n, 0)),
                  pl.BlockSpec((None, hpb, r, 2), lambda bb, h, n: (bb, h, n, 0)),
                  pl.BlockSpec((None, hpb, 2, r), lambda bb, h, n: (bb, h, 0, n)),
                  pl.BlockSpec((None, 2 * hpb, dh, dh), lambda bb, h, n: (bb, h, 0, 0)),
                  pl.BlockSpec((1, dh), lambda bb, h, n: (0, 0))],
        out_specs=[pl.BlockSpec((None, r, vw_), lambda bb, h, n: (bb, n, h)),
                   pl.BlockSpec((None, 2 * hpb, dh, dh), lambda bb, h, n: (bb, h, 0, 0))],
        out_shape=[jax.ShapeDtypeStruct((b, t, GDN_V_DIM), out_dtype),
                   jax.ShapeDtypeStruct(s0.shape, F32)],
        scratch_shapes=[pltpu.VMEM((2 * hpb, dh, dh), F32)],
        compiler_params=_params("parallel", "parallel", "arbitrary"), name="gdn_delta")(
            qkv3, qkv3, qkv3, proj3, gcol, bcol, grow, s0, norm_w)


def _gla_kernel(q_ref, k_ref, v_ref, gate_ref, gk_ref, bgk_ref, s0_ref, nw_ref,
                o_ref, so_ref, s_scr, *, cb, t_valid, n_blocks):
    n = pl.program_id(2)
    r = cb * CHUNK
    dk, dv = GLA_DK, GLA_DV

    @pl.when(n == 0)
    def _():
        s_scr[...] = s0_ref[...]

    xg = gk_ref[...] + bgk_ref[...]
    g = (jnp.minimum(xg, 0.0) - jnp.log(1.0 + jnp.exp(-jnp.abs(xg)))) * (1.0 / GLA_GATE_NORMALIZER)
    q = q_ref[...] * GLA_DK ** -0.5
    k = k_ref[...]
    v = v_ref[...]
    if t_valid is not None:
        okc = (n * r + lax.broadcasted_iota(jnp.int32, (r, 1), 0)) < t_valid
        g = jnp.where(okc, g, 0.0)
        k = jnp.where(okc, k, 0.0)
        v = jnp.where(okc, v, 0.0)

    gparts = _split3(g)
    bcum = sum(_dot(_chunk_tril(r), p) for p in gparts)
    ones = jnp.ones((CHUNK, LANES), BF16)
    row = lax.broadcasted_iota(jnp.int32, (CHUNK, 1), 0)
    jrow = lax.broadcasted_iota(jnp.int32, (SUB, 1), 0)
    lane = lax.broadcasted_iota(jnp.int32, (SUB, CHUNK), 1)
    nw = nw_ref[...]
    s = s_scr[...]
    for c in range(cb):
        sl = slice(c * CHUNK, (c + 1) * CHUNK)
        qc, kc, vc, bc = q[sl], k[sl], v[sl], bcum[sl]
        blast = bc[CHUNK - 1:CHUNK, :]
        attn_t = jnp.zeros((CHUNK, CHUNK), F32)
        for bi in range(1, CHUNK // SUB):
            bref = bc[bi * SUB:bi * SUB + 1, :]
            k_i = jnp.where(row < bi * SUB, kc * jnp.exp(jnp.minimum(bref - bc, 0.0)), 0.0)
            q_i = jnp.where((row >= bi * SUB) & (row < (bi + 1) * SUB),
                            qc * jnp.exp(jnp.minimum(bc - bref, 0.0)), 0.0)
            attn_t = attn_t + _dot_nt(k_i, q_i)
        diag = []
        for bi in range(CHUNK // SUB):
            sb = slice(bi * SUB, (bi + 1) * SUB)
            qb, kb, bb = qc[sb], kc[sb], bc[sb]
            d = jnp.zeros((SUB, CHUNK), F32)
            for il in range(SUB):
                e = jnp.exp(jnp.minimum(bb[il:il + 1, :] - bb, 0.0))
                col = jnp.sum(kb * e * qb[il:il + 1, :], axis=-1, keepdims=True)
                col = jnp.where(jrow <= il, col, 0.0)
                d = jnp.where(lane == bi * SUB + il, col, d)
            diag.append(d)
        attn_t = attn_t + jnp.concatenate(diag, axis=0)
        o = _dot(qc * jnp.exp(bc), s) + _dot_tn(attn_t, vc)
        bl_col = sum(_dot_tn(p[sl], ones) for p in gparts)
        decay_col = jnp.concatenate([jnp.exp(bl_col)] * (dv // LANES), axis=1)
        s = s * decay_col + _dot_tn(kc * jnp.exp(blast - bc), vc)
        gt = gate_ref[sl, :]
        o_ref[sl, :] = (_rms(o, nw) * _silu(gt)).astype(o_ref.dtype)
    s_scr[...] = s

    @pl.when(n == n_blocks - 1)
    def _():
        so_ref[...] = s_scr[...]


def _gla_chunks(qkvg3, gk3, b_gk, s0, norm_w, out_dtype, cb, t_valid):
    b, t, _ = qkvg3.shape
    r = cb * CHUNK
    assert t % r == 0
    nb = t // r
    dk, dv = GLA_DK, GLA_DV
    koff = GLA_HEADS
    voff = 2 * GLA_HEADS * dk // dv
    goff = voff + GLA_HEADS
    return pl.pallas_call(
        functools.partial(_gla_kernel, cb=cb, t_valid=t_valid, n_blocks=nb),
        grid=(b, GLA_HEADS, nb),
        in_specs=[pl.BlockSpec((None, r, dk), lambda bb, h, n: (bb, n, h)),
                  pl.BlockSpec((None, r, dk), lambda bb, h, n: (bb, n, koff + h)),
                  pl.BlockSpec((None, r, dv), lambda bb, h, n: (bb, n, voff + h)),
                  pl.BlockSpec((None, r, dv), lambda bb, h, n: (bb, n, goff + h)),
                  pl.BlockSpec((None, r, dk), lambda bb, h, n: (bb, n, h)),
                  pl.BlockSpec((1, dk), lambda bb, h, n: (0, h)),
                  pl.BlockSpec((None, None, dk, dv), lambda bb, h, n: (bb, h, 0, 0)),
                  pl.BlockSpec((1, dv), lambda bb, h, n: (0, 0))],
        out_specs=[pl.BlockSpec((None, r, dv), lambda bb, h, n: (bb, n, h)),
                   pl.BlockSpec((None, None, dk, dv), lambda bb, h, n: (bb, h, 0, 0))],
        out_shape=[jax.ShapeDtypeStruct((b, t, GLA_HEADS * dv), out_dtype),
                   jax.ShapeDtypeStruct(s0.shape, F32)],
        scratch_shapes=[pltpu.VMEM((dk, dv), F32)],
        compiler_params=_params("parallel", "parallel", "arbitrary"), name="gla_chunks")(
            qkvg3, qkvg3, qkvg3, qkvg3, gk3, b_gk, s0, norm_w)


def _t5_bucket(n):
    n = np.asarray(n)
    max_exact = N_BUCKETS // 2
    nf = np.maximum(n, max_exact).astype(np.float32)
    large = max_exact + (np.log(nf / max_exact) / math.log(MAX_DISTANCE / max_exact)
                         * (N_BUCKETS - max_exact)).astype(np.int32)
    return np.where(n < max_exact, n, np.minimum(large, N_BUCKETS - 1)).astype(np.int32)


def _lambda(lam_ref, lam_init):
    lf = lam_ref[...]
    s1 = jnp.sum(lf[0:1] * lf[1:2], axis=-1, keepdims=True)
    s2 = jnp.sum(lf[2:3] * lf[3:4], axis=-1, keepdims=True)
    return jnp.exp(s1) - jnp.exp(s2) + lam_init


def _flash_kernel(qi_ref, ki_ref, q_ref, k_ref, v_ref, bd_ref, bs_ref, rb_ref, lam_ref, sub_ref, o_ref,
                  q_scr, m_scr, l_scr, acc_scr, bias_scr, *, lam_init):
    h = pl.program_id(0)
    step = pl.program_id(2)
    qi = qi_ref[step]
    ki = ki_ref[step]
    tb = q_ref.shape[0]
    dh = DIFF_HEAD_DIM

    @pl.when((pl.program_id(1) == 0) & (step == 0))
    def _():
        ri = lax.broadcasted_iota(jnp.int32, (tb, tb), 0)
        ci = lax.broadcasted_iota(jnp.int32, (tb, tb), 1)
        for mm in range(2):
            bd = jnp.zeros((tb, tb), F32)
            bs = jnp.zeros((tb, tb), F32)
            for bk in range(N_BUCKETS):
                val = rb_ref[bk, 2 * h + mm] * LOG2E
                bd = jnp.where(bd_ref[...] == bk, val, bd)
                bs = jnp.where(bs_ref[...] == bk, val, bs)
            bias_scr[mm, 0] = jnp.where(ri > ci, NEG, bd)
            bias_scr[mm, 1] = bs

    @pl.when(ki == 0)
    def _():
        q_scr[...] = (q_ref[...] * (dh ** -0.5 * LOG2E)).astype(q_scr.dtype)
        m_scr[...] = jnp.full_like(m_scr, NEG)
        l_scr[...] = jnp.zeros_like(l_scr)
        acc_scr[...] = jnp.zeros_like(acc_scr)

    def update(mm, s, shift, vt):
        m_prev = m_scr[mm]
        if shift.ndim == 0:
            m_new = jnp.maximum(m_prev, jnp.max(s, axis=0, keepdims=True) + shift)
            p = jnp.exp2(s - (m_new - shift))
        else:
            s = s + shift
            m_new = jnp.maximum(m_prev, jnp.max(s, axis=0, keepdims=True))
            p = jnp.exp2(s - m_new)
        alpha = jnp.exp2(m_prev - m_new)
        l_scr[mm] = alpha * l_scr[mm] + jnp.sum(p, axis=0, keepdims=True)
        acc_scr[mm] = alpha * acc_scr[mm] + _dot(vt, p)
        m_scr[mm] = m_new

    def both_maps(shift_of):
        vt = v_ref[...].T.astype(BF16)
        for mm in range(2):
            s = _dot_nt(k_ref[:, mm * dh:(mm + 1) * dh], q_scr[:, mm * dh:(mm + 1) * dh])
            update(mm, s, shift_of(mm), vt)

    @pl.when(ki == qi)
    def _():
        both_maps(lambda mm: bias_scr[mm, 0])

    @pl.when(ki == qi - 1)
    def _():
        both_maps(lambda mm: bias_scr[mm, 1])

    @pl.when(ki < qi - 1)
    def _():
        both_maps(lambda mm: rb_ref[N_BUCKETS - 1, 2 * h + mm] * LOG2E)

    @pl.when(ki == qi)
    def _():
        lam = _lambda(lam_ref, lam_init)
        out = acc_scr[0] / l_scr[0] - lam * (acc_scr[1] / l_scr[1])
        out = out * lax.rsqrt(jnp.mean(out * out, axis=0, keepdims=True) + NORM_EPS)
        o_ref[...] = (out * (sub_ref[...] * (1.0 - lam_init))).T.astype(o_ref.dtype)


def _flash_diff(qkv3, rel_bias, lambdas, subln_col, lam_init, out_dtype, tb):
    b, t, _ = qkv3.shape
    tb = min(tb, t)
    assert t % tb == 0 and tb >= MAX_DISTANCE
    nq = t // tb
    dd = 2 * DIFF_HEAD_DIM
    j = np.arange(tb)[:, None]
    i = np.arange(tb)[None, :]
    bkt_diag = jnp.asarray(_t5_bucket(np.maximum(i - j, 0)))
    bkt_sub = jnp.asarray(_t5_bucket(tb + i - j))
    koff = D_MODEL // dd
    pairs = [(qi, ki) for qi in range(nq) for ki in range(qi + 1)]
    qi_tbl = jnp.asarray([pq for pq, _ in pairs], jnp.int32)
    ki_tbl = jnp.asarray([pk for _, pk in pairs], jnp.int32)
    grid_spec = pltpu.PrefetchScalarGridSpec(
        num_scalar_prefetch=2, grid=(DIFF_HEADS, b, len(pairs)),
        in_specs=[pl.BlockSpec((None, tb, dd), lambda h, bb, s, qt, kt: (bb, qt[s], h)),
                  pl.BlockSpec((None, tb, dd), lambda h, bb, s, qt, kt: (bb, kt[s], koff + h)),
                  pl.BlockSpec((None, tb, dd), lambda h, bb, s, qt, kt: (bb, kt[s], 2 * koff + h)),
                  pl.BlockSpec((tb, tb), lambda h, bb, s, qt, kt: (0, 0)),
                  pl.BlockSpec((tb, tb), lambda h, bb, s, qt, kt: (0, 0)),
                  pl.BlockSpec(memory_space=pltpu.SMEM),
                  pl.BlockSpec((4, DIFF_HEAD_DIM), lambda h, bb, s, qt, kt: (0, 0)),
                  pl.BlockSpec((dd, 1), lambda h, bb, s, qt, kt: (0, 0))],
        out_specs=pl.BlockSpec((None, tb, dd), lambda h, bb, s, qt, kt: (bb, qt[s], h)),
        scratch_shapes=[pltpu.VMEM((tb, dd), BF16),
                        pltpu.VMEM((2, 1, tb), F32), pltpu.VMEM((2, 1, tb), F32),
                        pltpu.VMEM((2, dd, tb), F32), pltpu.VMEM((2, 2, tb, tb), F32)])
    return pl.pallas_call(
        functools.partial(_flash_kernel, lam_init=lam_init), grid_spec=grid_spec,
        out_shape=jax.ShapeDtypeStruct((b, t, D_MODEL), out_dtype),
        compiler_params=_params("parallel", "arbitrary", "arbitrary"), name="flash_diff")(
            qi_tbl, ki_tbl, qkv3, qkv3, qkv3, bkt_diag, bkt_sub, rel_bias, lambdas, subln_col)


def _decode_kernel(pt_ref, q_ref, *refs, pp, n_steps, lam_init):
    kp_refs, vp_refs = refs[:pp], refs[pp:2 * pp]
    kn_ref, vn_ref, bkt_ref, tbl_ref, lam_ref, sub_ref, o_ref, m_scr, l_scr, acc_scr, bias_scr, p_scr, a_scr = refs[2 * pp:]
    step = pl.program_id(1)
    nh, dh = DIFF_HEADS, DIFF_HEAD_DIM
    ones = jnp.ones((dh, LANES), BF16)
    qs = q_ref[...] * (dh ** -0.5 * LOG2E)

    @pl.when(step == 0)
    def _():
        m_scr[...] = jnp.full_like(m_scr, NEG)
        l_scr[...] = jnp.zeros_like(l_scr)
        acc_scr[...] = jnp.zeros_like(acc_scr)
        bkt = bkt_ref[...]
        bias = jnp.zeros(bkt.shape, F32)
        for bk in range(N_BUCKETS):
            bias = jnp.where(bkt == bk, (tbl_ref[bk] * LOG2E)[None], bias)
        bias_scr[...] = bias

    def lane_sum(x):
        return _dot(x, ones)

    def accumulate(pr, shift_row, values):
        m_prev = m_scr[...]
        if pr.ndim == 3:
            m_new = jnp.maximum(m_prev, jnp.max(pr, axis=0) + shift_row)
            p = jnp.exp2(pr - (m_new - shift_row)[None])
            psum = jnp.sum(p, axis=0)
        else:
            m_new = jnp.maximum(m_prev, pr + shift_row)
            p = jnp.exp2(pr - (m_new - shift_row))
            psum = p
        alpha = jnp.exp2(m_prev - m_new)
        l_scr[...] = alpha * l_scr[...] + psum
        m_scr[...] = m_new
        a_scr[...] = alpha
        if pr.ndim == 3:
            p_scr[...] = p
        else:
            p_scr[0] = p
        for par in range(2):
            ae = a_scr[pl.ds(par, nh, stride=2), :]
            if pr.ndim == 3:
                pe = p_scr[:, pl.ds(par, nh, stride=2), :]
                pv = jnp.sum(jnp.concatenate([pe, pe], axis=-1) * values(), axis=0)
            else:
                pe = p_scr[0, pl.ds(par, nh, stride=2), :]
                pv = jnp.concatenate([pe, pe], axis=-1) * values()
            acc_scr[par] = jnp.concatenate([ae, ae], axis=-1) * acc_scr[par] + pv

    def page_logits(j):
        prod = kp_refs[j][...] * qs[None]
        return lane_sum(prod.reshape(PAGE_SIZE * 2 * nh, dh)).reshape(PAGE_SIZE, 2 * nh, LANES)

    far = tbl_ref[N_BUCKETS - 1] * LOG2E
    zero = jnp.zeros_like(far)
    for j in range(pp - 1):
        accumulate(page_logits(j), far, lambda j=j: vp_refs[j][...])

    @pl.when(step < n_steps - 1)
    def _():
        accumulate(page_logits(pp - 1), far, lambda: vp_refs[pp - 1][...])

    @pl.when(step == n_steps - 1)
    def _():
        accumulate(page_logits(pp - 1) + bias_scr[...], zero, lambda: vp_refs[pp - 1][...])
        accumulate(lane_sum(kn_ref[...] * qs) + tbl_ref[0] * LOG2E, zero, lambda: vn_ref[...])
        l_scr_v = l_scr[...]
        a_scr[...] = l_scr_v
        outs = []
        for par in range(2):
            le = a_scr[pl.ds(par, nh, stride=2), :]
            outs.append(acc_scr[par] / jnp.concatenate([le, le], axis=-1))
        out = outs[0] - _lambda(lam_ref, lam_init) * outs[1]
        o_ref[...] = _rms(out, sub_ref[...]) * (1.0 - lam_init)


def _decode_diff(q, k_new, v_new, cache_k, cache_v, layer, page_table, rel_bias, lambdas, subln, lam_init, pp):
    b, n_pages = page_table.shape
    nh, dh = DIFF_HEADS, DIFF_HEAD_DIM
    pp = min(pp, n_pages)
    assert n_pages % pp == 0 and PAGE_SIZE >= MAX_DISTANCE and dh == LANES
    n_steps = n_pages // pp
    bkt_last = jnp.asarray(np.broadcast_to(
        _t5_bucket(PAGE_SIZE - np.arange(PAGE_SIZE)).reshape(PAGE_SIZE, 1, 1), (PAGE_SIZE, 2 * nh, LANES)))
    tbl = jnp.broadcast_to(rel_bias[:, :, None], (N_BUCKETS, 2 * nh, LANES))
    page_spec = lambda j, width, lanes: pl.BlockSpec(
        (None, None, PAGE_SIZE, width, lanes), lambda bb, s, pt: (layer, pt[bb, s * pp + j], 0, 0, 0))
    full = lambda shape: pl.BlockSpec(shape, lambda bb, s, pt: (0,) * len(shape))
    per_seq = lambda shape: pl.BlockSpec((None,) + shape, lambda bb, s, pt: (bb,) + (0,) * len(shape))
    grid_spec = pltpu.PrefetchScalarGridSpec(
        num_scalar_prefetch=1, grid=(b, n_steps),
        in_specs=([per_seq((2 * nh, dh))]
                  + [page_spec(j, 2 * nh, dh) for j in range(pp)]
                  + [page_spec(j, nh, 2 * dh) for j in range(pp)]
                  + [per_seq((2 * nh, dh)), per_seq((nh, 2 * dh)),
                     full((PAGE_SIZE, 2 * nh, LANES)), full((N_BUCKETS, 2 * nh, LANES)),
                     full((4, dh)), full((1, 2 * dh))]),
        out_specs=per_seq((nh, 2 * dh)),
        scratch_shapes=[pltpu.VMEM((2 * nh, LANES), F32), pltpu.VMEM((2 * nh, LANES), F32),
                        pltpu.VMEM((2, nh, 2 * dh), F32), pltpu.VMEM((PAGE_SIZE, 2 * nh, LANES), F32),
                        pltpu.VMEM((PAGE_SIZE, 2 * nh, LANES), F32), pltpu.VMEM((2 * nh, LANES), F32)])
    return pl.pallas_call(
        functools.partial(_decode_kernel, pp=pp, n_steps=n_steps, lam_init=lam_init),
        grid_spec=grid_spec,
        out_shape=jax.ShapeDtypeStruct((b, nh, 2 * dh), F32),
        compiler_params=_params("parallel", "arbitrary"), name="decode_diff")(
            page_table, q, *([cache_k] * pp), *([cache_v] * pp), k_new, v_new, bkt_last, tbl, lambdas, subln)


class _Tiles:
    def __init__(self, prompt):
        self.prompt = prompt
        self.act_dtype = BF16 if prompt else F32
        self.tm = 1024 if prompt else SUBLANES
        self.tn = 1024
        self.tn_ff = 512
        self.tm_ff = 2048
        self.tm_res = 256 if prompt else SUBLANES
        self.tk_res = 512
        self.tt_prep = 256 if prompt else SUBLANES
        self.tc_prep = 1024
        self.cb = 4 if prompt else 1
        self.gdn_hpb = 4
        self.tb_attn = 512
        self.decode_pages = 8


def _group_trunk(x3, tiles, gdn_s0, gdn_c0, gla_s0, cache_k, cache_v, page_table, p, wb):
    b, t, d = x3.shape
    m = b * t
    x = x3.reshape(m, d)
    if not tiles.prompt:
        assert t == 1 and m % SUBLANES == 0
    t_pad = t if tiles.prompt else CHUNK
    t_valid = None if tiles.prompt else t
    nw = lambda i, jn: p['norm_w'][i, jn].reshape(1, d)
    adt = tiles.act_dtype

    def pad_time(a3, to):
        return a3 if a3.shape[1] == to else jnp.pad(a3, ((0, 0), (0, to - a3.shape[1]), (0, 0)))

    def big_matmul(a, name, layer):
        if tiles.prompt:
            return _matmul(a, wb[name, layer], layer, F32, tiles.tm, tiles.tn)
        out, wb[name, layer] = _matmul(a, p[name], layer, F32, tiles.tm, tiles.tn, emit_wb=True)
        return out

    def residual_matmul(a, name, layer, x_in, nw_post, nw_next):
        if tiles.prompt:
            return _matmul_residual(a, wb[name, layer], layer, x_in, nw_post, nw_next, adt,
                                    tiles.tm_res, tiles.tk_res)
        xo, h_next, wb[name, layer] = _matmul_residual(a, p[name], layer, x_in, nw_post, nw_next, adt,
                                                       tiles.tm_res, tiles.tk_res, emit_wb=True)
        return xo, h_next

    def ffn_up(a, layer):
        if tiles.prompt:
            return _swiglu_up(a, wb['ffn_gate', layer], wb['ffn_up', layer], layer, adt, tiles.tm_ff, tiles.tn_ff)
        act_, wb['ffn_gate', layer], wb['ffn_up', layer] = _swiglu_up(
            a, p['ffn_w_up'], p['ffn_w_up'], layer, adt, tiles.tm, tiles.tn_ff, emit_wb=True)
        return act_

    gdn_s, gdn_c, gla_s, att_k, att_v = [], [], [], [], []
    h = _rmsnorm(x, nw(0, 0), adt, tiles.tm)
    for i in range(DEPTH):
        j = i // N_MIXERS
        kind = i % N_MIXERS
        if kind == 0:
            beta, g = _gdn_gates(h, p['gdn_w_ba'], j, p['gdn_a_log'][j].reshape(1, -1),
                                 p['gdn_dt_bias'][j].reshape(1, -1), tiles.tm)
            buf0 = gdn_c0[j]
            nbuf = GDN_CONV_W - 1
            buf8 = jnp.pad(buf0, ((0, 0), (SUBLANES - nbuf, 0), (0, 0)))
            if tiles.prompt:
                proj, tail = _gdn_in_proj(h, wb['gdn_w_in', j], buf8, p['gdn_conv_w'][j], t, tiles.tm, tiles.tn)
                qkv3 = proj3p = proj.reshape(b, t, -1)
                new_buf = tail[:, SUBLANES - nbuf:, :GDN_CONV_DIM]
            else:
                proj3 = big_matmul(h, 'gdn_w_in', j).reshape(b, t, -1)
                new_buf = jnp.concatenate([buf0[:, t:], proj3[..., :GDN_CONV_DIM]], axis=1)
                proj3p = pad_time(proj3, SUBLANES)
                qkv3 = pad_time(_gdn_prep(proj3p, buf8, p['gdn_conv_w'][j], tiles.tt_prep, tiles.tc_prep), t_pad)
                proj3p = pad_time(proj3p, t_pad)

            def heads(a):
                a = pad_time(a.reshape(b, t, GDN_QK_HEADS * 2), t_pad)
                return a.reshape(b, t_pad, GDN_QK_HEADS, 2).transpose(0, 2, 1, 3)
            gcol, bcol = heads(g), heads(beta)
            o3, s_new = _gdn_delta(qkv3, proj3p, gcol, bcol, gcol.transpose(0, 1, 3, 2), gdn_s0[j],
                                   p['gdn_norm_w'][j].reshape(1, -1), adt, tiles.cb, tiles.gdn_hpb, t_valid)
            mix_in, w_out = o3[:, :t].reshape(m, -1), 'gdn_w_out'
            gdn_s.append(s_new)
            gdn_c.append(new_buf)
        elif kind == 1:
            qkvg = big_matmul(h, 'gla_w_qkvg', j)
            low = _matmul(h, p['gla_w_gk1'], j, F32, tiles.tm, tiles.tn)
            gk = _matmul(low, p['gla_w_gk2'], j, F32, tiles.tm, tiles.tn)
            o3, s_new = _gla_chunks(pad_time(qkvg.reshape(b, t, -1), t_pad), pad_time(gk.reshape(b, t, -1), t_pad),
                                    p['gla_b_gk'][j].reshape(1, -1), gla_s0[j],
                                    p['gla_norm_w'][j].reshape(1, -1), adt, tiles.cb, t_valid)
            mix_in, w_out = o3[:, :t].reshape(m, -1), 'gla_w_out'
            gla_s.append(s_new)
        else:
            lam_init = 0.8 - 0.6 * math.exp(-0.3 * i)
            qkv = big_matmul(h, 'diff_w_qkv', j)
            nh, dh = DIFF_HEADS, DIFF_HEAD_DIM
            k_new = qkv[:, D_MODEL:2 * D_MODEL].reshape(b, t, 2 * nh, dh)
            v_new = qkv[:, 2 * D_MODEL:].reshape(b, t, nh, 2 * dh)
            subln = p['diff_subln'][j].reshape(1, -1)
            if tiles.prompt:
                o3 = _flash_diff(qkv.reshape(b, t, -1), p['rel_bias'], p['diff_lambda'][j],
                                 subln.reshape(-1, 1), lam_init, adt, tiles.tb_attn)
                mix_in = o3.reshape(m, -1)
            else:
                o = _decode_diff(qkv[:, :D_MODEL].reshape(b, 2 * nh, dh), k_new.reshape(b, 2 * nh, dh),
                                 v_new.reshape(b, nh, 2 * dh), cache_k, cache_v, j, page_table,
                                 p['rel_bias'], p['diff_lambda'][j], subln, lam_init, tiles.decode_pages)
                mix_in = o.reshape(m, -1)
            w_out = 'diff_w_out'
            att_k.append(k_new)
            att_v.append(v_new)
        x, h = residual_matmul(mix_in, w_out, j, x, nw(i, 1), nw(i, 2))
        act = ffn_up(h, i)
        nxt = nw(i + 1, 0) if i + 1 < DEPTH else None
        x, h = residual_matmul(act, 'ffn_w_down', i, x, nw(i, 3), nxt)
    return (x.reshape(b, t, d), jnp.stack(gdn_s), jnp.stack(gdn_c), jnp.stack(gla_s),
            jnp.stack(att_k), jnp.stack(att_v))


def kernel(x_prompt, x_sample, state_gdn, state_gdn_conv, state_gla, cache_k, cache_v, page_table,
           norm_w, ffn_w_up, ffn_w_down, rel_bias,
           gdn_w_in, gdn_w_ba, gdn_conv_w, gdn_a_log, gdn_dt_bias, gdn_norm_w, gdn_w_out,
           gla_w_qkvg, gla_w_gk1, gla_w_gk2, gla_b_gk, gla_norm_w, gla_w_out,
           diff_w_qkv, diff_lambda, diff_subln, diff_w_out):
    p = dict(norm_w=norm_w, ffn_w_up=ffn_w_up, ffn_w_down=ffn_w_down, rel_bias=rel_bias,
             gdn_w_in=gdn_w_in, gdn_w_ba=gdn_w_ba, gdn_conv_w=gdn_conv_w, gdn_a_log=gdn_a_log,
             gdn_dt_bias=gdn_dt_bias, gdn_norm_w=gdn_norm_w, gdn_w_out=gdn_w_out,
             gla_w_qkvg=gla_w_qkvg, gla_w_gk1=gla_w_gk1, gla_w_gk2=gla_w_gk2, gla_b_gk=gla_b_gk,
             gla_norm_w=gla_norm_w, gla_w_out=gla_w_out,
             diff_w_qkv=diff_w_qkv, diff_lambda=diff_lambda, diff_subln=diff_subln, diff_w_out=diff_w_out)
    bp = x_prompt.shape[0]
    n_gdn, n_gla = state_gdn.shape[0], state_gla.shape[0]
    zeros_gdn = jnp.zeros((n_gdn, bp) + state_gdn.shape[2:], F32)
    zeros_conv = jnp.zeros((n_gdn, bp) + state_gdn_conv.shape[2:], F32)
    zeros_gla = jnp.zeros((n_gla, bp) + state_gla.shape[2:], F32)
    wb = {}
    outs_s = _group_trunk(x_sample, _Tiles(False), state_gdn, state_gdn_conv, state_gla,
                          cache_k, cache_v, page_table, p, wb)
    outs_p = _group_trunk(x_prompt, _Tiles(True), zeros_gdn, zeros_conv, zeros_gla, None, None, None, p, wb)
    return (outs_p[0], outs_s[0]) + outs_p[1:] + outs_s[1:]
```

```python
import functools
import math

import numpy as np
import jax
import jax.numpy as jnp
from jax import lax
from jax.experimental import pallas as pl
from jax.experimental.pallas import tpu as pltpu

F32 = jnp.float32
BF16 = jnp.bfloat16

D_MODEL = 2048
DEPTH = 4
PAGE_SIZE = 128
N_MIXERS = 3
GDN_HEAD_DIM = 128
GDN_QK_HEADS = D_MODEL // 128
GDN_V_HEADS = 2 * GDN_QK_HEADS
GDN_QK_DIM = GDN_QK_HEADS * GDN_HEAD_DIM
GDN_V_DIM = GDN_V_HEADS * GDN_HEAD_DIM
GDN_CONV_DIM = 2 * GDN_QK_DIM + GDN_V_DIM
GDN_CONV_W = 4
GLA_HEADS = 4
GLA_DK = D_MODEL // 2 // GLA_HEADS
GLA_DV = D_MODEL // GLA_HEADS
GLA_GATE_NORMALIZER = 16.0
DIFF_HEAD_DIM = 128
DIFF_HEADS = D_MODEL // (2 * DIFF_HEAD_DIM)
N_BUCKETS = 32
MAX_DISTANCE = 128
D_FF = ((8 * D_MODEL + 3 * 256 - 1) // (3 * 256)) * 256
NORM_EPS = 1e-6

LANES = 128
SUBLANES = 8
CHUNK = 64
SUB = 16
GDN_PROJ_SUBBLOCKS = 4
VMEM_LIMIT_BYTES = 48 * 1024 * 1024
VMEM_LIMIT_BIG_BYTES = 56 * 1024 * 1024
NEG = -1e30
LOG2E = math.log2(math.e)


def _params(*sem, vmem=VMEM_LIMIT_BYTES):
    return pltpu.CompilerParams(dimension_semantics=sem, vmem_limit_bytes=vmem)


def _dot(a, b):
    return jnp.dot(a.astype(BF16), b.astype(BF16), preferred_element_type=F32)


def _dot_nt(a, b):
    return lax.dot_general(a.astype(BF16), b.astype(BF16), (((1,), (1,)), ((), ())),
                           preferred_element_type=F32)


def _dot_tn(a, b):
    return lax.dot_general(a.astype(BF16), b.astype(BF16), (((0,), (0,)), ((), ())),
                           preferred_element_type=F32)


def _split3(x):
    hi = x.astype(BF16)
    r = x - hi.astype(F32)
    mid = r.astype(BF16)
    lo = (r - mid.astype(F32)).astype(BF16)
    return hi, mid, lo


def _sigmoid(x):
    return 0.5 * jnp.tanh(0.5 * x) + 0.5


def _silu(x):
    h = 0.5 * x
    return h + h * jnp.tanh(h)


def _softplus(x):
    return jnp.maximum(x, 0.0) + jnp.log(1.0 + jnp.exp(-jnp.abs(x)))


def _rms(x, w):
    return x * lax.rsqrt(jnp.mean(x * x, axis=-1, keepdims=True) + NORM_EPS) * w


def _chunk_tril(r):
    i = lax.broadcasted_iota(jnp.int32, (r, r), 0)
    j = lax.broadcasted_iota(jnp.int32, (r, r), 1)
    return jnp.where((j <= i) & ((i // CHUNK) == (j // CHUNK)), 1.0, 0.0).astype(BF16)


def _rmsnorm_kernel(x_ref, w_ref, o_ref):
    o_ref[...] = _rms(x_ref[...], w_ref[...]).astype(o_ref.dtype)


def _rmsnorm(x, w, out_dtype, tm):
    m, d = x.shape
    tm = min(tm, m)
    return pl.pallas_call(
        _rmsnorm_kernel, grid=(m // tm,),
        in_specs=[pl.BlockSpec((tm, d), lambda i: (i, 0)), pl.BlockSpec((1, d), lambda i: (0, 0))],
        out_specs=pl.BlockSpec((tm, d), lambda i: (i, 0)),
        out_shape=jax.ShapeDtypeStruct((m, d), out_dtype),
        compiler_params=_params("parallel"), name="rmsnorm")(x, w)


def _weight_spec(w, layer, block, index):
    if w.ndim == 2:
        return pl.BlockSpec(block, index)
    return pl.BlockSpec((None,) + block, lambda *g: (layer,) + index(*g))


def _mm_kernel(a_ref, w_ref, o_ref, *wb_ref):
    w = w_ref[...].astype(BF16)
    if wb_ref:
        wb_ref[0][...] = w
    o_ref[...] = _dot(a_ref[...], w).astype(o_ref.dtype)


def _matmul(a, w, layer, out_dtype, tm, tn, emit_wb=False, col_start=0, n_out=None):
    m, k = a.shape
    n = w.shape[-1] if n_out is None else n_out
    tm, tn = min(tm, m), min(tn, n)
    assert m % tm == 0 and n % tn == 0 and col_start % tn == 0 and (not emit_wb or (m == tm and n_out is None))
    c0 = col_start // tn
    out_shape = [jax.ShapeDtypeStruct((m, n), out_dtype)]
    out_specs = [pl.BlockSpec((tm, tn), lambda i, j: (i, j))]
    if emit_wb:
        out_shape.append(jax.ShapeDtypeStruct((k, n), BF16))
        out_specs.append(pl.BlockSpec((k, tn), lambda i, j: (0, j)))
    res = pl.pallas_call(
        _mm_kernel, grid=(m // tm, n // tn),
        in_specs=[pl.BlockSpec((tm, k), lambda i, j: (i, 0)),
                  _weight_spec(w, layer, (k, tn), lambda i, j: (0, j + c0))],
        out_specs=out_specs, out_shape=out_shape,
        compiler_params=_params("parallel", "parallel"), name="matmul")(a, w)
    return (res[0], res[1]) if emit_wb else res[0]


def _swiglu_kernel(a_ref, wg_ref, wu_ref, o_ref, *wb_refs):
    a = a_ref[...].astype(BF16)
    wg = wg_ref[...].astype(BF16)
    wu = wu_ref[...].astype(BF16)
    if wb_refs:
        wb_refs[0][...] = wg
        wb_refs[1][...] = wu
    g = _dot(a, wg)
    u = _dot(a, wu)
    o_ref[...] = (_silu(g) * u).astype(o_ref.dtype)


def _swiglu_up(a, w_gate, w_up, layer, out_dtype, tm, tn, emit_wb=False):
    m, k = a.shape
    stacked = w_up.ndim == 3
    f = w_up.shape[-1] // 2 if stacked else w_up.shape[-1]
    tm, tn = min(tm, m), min(tn, f)
    assert m % tm == 0 and f % tn == 0 and (not emit_wb or m == tm)
    nf = f // tn
    up_off = nf if stacked else 0
    out_shape = [jax.ShapeDtypeStruct((m, f), out_dtype)]
    out_specs = [pl.BlockSpec((tm, tn), lambda i, j: (i, j))]
    if emit_wb:
        out_shape += [jax.ShapeDtypeStruct((k, f), BF16)] * 2
        out_specs += [pl.BlockSpec((k, tn), lambda i, j: (0, j))] * 2
    res = pl.pallas_call(
        _swiglu_kernel, grid=(m // tm, nf),
        in_specs=[pl.BlockSpec((tm, k), lambda i, j: (i, 0)),
                  _weight_spec(w_gate, layer, (k, tn), lambda i, j: (0, j)),
                  _weight_spec(w_up, layer, (k, tn), lambda i, j: (0, j + up_off))],
        out_specs=out_specs, out_shape=out_shape,
        compiler_params=_params("parallel", "parallel"), name="swiglu_up")(a, w_gate, w_up)
    return tuple(res) if emit_wb else res[0]


def _residual_epilogue(mix, x_ref, nw1_ref, nw2_ref, xo_ref, h_ref):
    xn = x_ref[...] + _rms(mix, nw1_ref[...])
    xo_ref[...] = xn
    if h_ref:
        h_ref[0][...] = _rms(xn, nw2_ref[...]).astype(h_ref[0].dtype)


def _mm_res_kernel(a_ref, w_ref, x_ref, nw1_ref, nw2_ref, xo_ref, *rest, nk, emit_h, emit_wb):
    h_ref = rest[:1] if emit_h else ()
    kk = pl.program_id(1)
    w = w_ref[...].astype(BF16)
    if emit_wb:
        rest[-1][...] = w
    part = _dot(a_ref[...], w)

    @pl.when(kk == 0)
    def _():
        xo_ref[...] = part

    @pl.when(kk > 0)
    def _():
        xo_ref[...] += part

    @pl.when(kk == nk - 1)
    def _():
        _residual_epilogue(xo_ref[...], x_ref, nw1_ref, nw2_ref, xo_ref, h_ref)


def _mm_res_resident_kernel(a_ref, w_ref, x_ref, nw1_ref, nw2_ref, xo_ref, *h_ref):
    _residual_epilogue(_dot(a_ref[...], w_ref[...]), x_ref, nw1_ref, nw2_ref, xo_ref, h_ref)


def _matmul_residual(a, w, layer, x, nw_post, nw_next, h_dtype, tm, tk, emit_wb=False):
    m, k = a.shape
    n = w.shape[-1]
    tm = min(tm, m)
    assert m % tm == 0 and (not emit_wb or m == tm)
    emit_h = nw_next is not None
    if not emit_h:
        nw_next = nw_post
    resident = w.ndim == 2
    if resident:
        grid = (m // tm,)
        row = lambda i: (i, 0)
        fixed = lambda i: (0, 0)
        a_spec = pl.BlockSpec((tm, k), row)
        w_spec = pl.BlockSpec((k, n), fixed, pipeline_mode=pl.Buffered(1))
        x_spec = pl.BlockSpec((tm, n), row)
        body = _mm_res_resident_kernel
        sem = ("parallel",)
    else:
        tk = min(tk, k)
        assert k % tk == 0
        nk = k // tk
        grid = (m // tm, nk)
        row = lambda i, kk: (i, 0)
        fixed = lambda i, kk: (0, 0)
        a_spec = pl.BlockSpec((tm, tk), lambda i, kk: (i, kk))
        w_spec = pl.BlockSpec((None, tk, n), lambda i, kk: (layer, kk, 0))
        x_spec = pl.BlockSpec((tm, n), row, pipeline_mode=pl.Buffered(1))
        body = functools.partial(_mm_res_kernel, nk=nk, emit_h=emit_h, emit_wb=emit_wb)
        sem = ("parallel", "arbitrary")
    out_shape = [jax.ShapeDtypeStruct((m, n), F32)]
    out_specs = [pl.BlockSpec((tm, n), row)]
    if emit_h:
        out_shape.append(jax.ShapeDtypeStruct((m, n), h_dtype))
        out_specs.append(pl.BlockSpec((tm, n), row))
    if emit_wb:
        out_shape.append(jax.ShapeDtypeStruct((k, n), BF16))
        out_specs.append(pl.BlockSpec((tk, n), lambda i, kk: (kk, 0)))
    res = pl.pallas_call(
        body, grid=grid,
        in_specs=[a_spec, w_spec, x_spec, pl.BlockSpec((1, n), fixed), pl.BlockSpec((1, n), fixed)],
        out_specs=out_specs, out_shape=out_shape,
        compiler_params=_params(*sem, vmem=VMEM_LIMIT_BIG_BYTES), name="matmul_residual")(
            a, w, x, nw_post, nw_next)
    xo = res[0]
    h = res[1] if emit_h else None
    return (xo, h, res[-1]) if emit_wb else (xo, h)


def _gdn_gates_kernel(a_ref, w_ref, alog_ref, dtb_ref, beta_ref, g_ref):
    ba = _dot(a_ref[...], w_ref[...])
    nh = beta_ref.shape[-1]
    beta_ref[...] = _sigmoid(ba[:, :nh])
    g_ref[...] = -jnp.exp(alog_ref[...]) * _softplus(ba[:, nh:] + dtb_ref[...])


def _gdn_gates(a, w_ba, layer, a_log, dt_bias, tm):
    m, k = a.shape
    nh = GDN_V_HEADS
    tm = min(tm, m)
    return pl.pallas_call(
        _gdn_gates_kernel, grid=(m // tm,),
        in_specs=[pl.BlockSpec((tm, k), lambda i: (i, 0)),
                  pl.BlockSpec((None, k, 2 * nh), lambda i: (layer, 0, 0)),
                  pl.BlockSpec((1, nh), lambda i: (0, 0)),
                  pl.BlockSpec((1, nh), lambda i: (0, 0))],
        out_specs=[pl.BlockSpec((tm, nh), lambda i: (i, 0))] * 2,
        out_shape=[jax.ShapeDtypeStruct((m, nh), F32)] * 2,
        compiler_params=_params("parallel"), name="gdn_gates")(a, w_ba, a_log, dt_bias)


def _conv_silu_norm(x, halo, ext_scr, cw_ref, j, n_qk_tiles):
    tt, tc = x.shape
    ext_scr[:SUBLANES, :] = halo
    ext_scr[SUBLANES:, :] = x
    y = x * cw_ref[GDN_CONV_W - 1:GDN_CONV_W, :]
    for s in range(1, GDN_CONV_W):
        y = y + ext_scr[SUBLANES - s:SUBLANES - s + tt, :] * cw_ref[GDN_CONV_W - 1 - s:GDN_CONV_W - s, :]
    y = _silu(y)
    qscale = jnp.where(j < n_qk_tiles, GDN_HEAD_DIM ** -0.5, 1.0)
    pieces = []
    for hh in range(tc // GDN_HEAD_DIM):
        yh = y[:, hh * GDN_HEAD_DIM:(hh + 1) * GDN_HEAD_DIM]
        r = lax.rsqrt(jnp.sum(yh * yh, axis=-1, keepdims=True) + NORM_EPS) * qscale
        pieces.append(yh * jnp.where(j < 2 * n_qk_tiles, r, 1.0))
    return jnp.concatenate(pieces, axis=1)


def _gdn_prep_kernel(x_ref, prev_ref, buf_ref, cw_ref, o_ref, ext_scr, *, n_qk_tiles):
    i = pl.program_id(1)
    j = pl.program_id(2)
    halo = jnp.where(i == 0, buf_ref[...], prev_ref[...])
    o_ref[...] = _conv_silu_norm(x_ref[...], halo, ext_scr, cw_ref, j, n_qk_tiles)


def _gdn_in_proj_kernel(a_ref, w_ref, buf_ref, cw_ref, o_ref, tail_ref, ext_scr, halo_scr, *,
                        tiles_per_seq, n_qk_tiles):
    i = pl.program_id(0)
    j = pl.program_id(1)

    @pl.when((i == 0) & (j == 0))
    def _():
        halo_scr[...] = jnp.zeros_like(halo_scr)

    halo = jnp.where(i % tiles_per_seq == 0, buf_ref[...], halo_scr[j])
    w = w_ref[...]
    n_sub, rows = ext_scr.shape[0], ext_scr.shape[1] - SUBLANES
    for s in range(n_sub):
        sl = slice(s * rows, (s + 1) * rows)
        x = _dot(a_ref[sl, :], w)
        o_ref[sl, :] = _conv_silu_norm(x, halo, ext_scr.at[s], cw_ref, j, n_qk_tiles)
        halo = x[rows - SUBLANES:, :]
    halo_scr[j] = halo
    tail_ref[...] = halo


def _gdn_in_proj(a, w, buf8, conv_w, seq_len, tm, tn):
    m, k = a.shape
    n = GDN_CONV_DIM
    tm = min(tm, seq_len)
    assert w.ndim == 2 and seq_len % tm == 0 and GDN_QK_DIM % tn == 0
    assert tm % (GDN_PROJ_SUBBLOCKS * SUBLANES) == 0
    tps = seq_len // tm
    proj, tails = pl.pallas_call(
        functools.partial(_gdn_in_proj_kernel, tiles_per_seq=tps, n_qk_tiles=GDN_QK_DIM // tn),
        grid=(m // tm, n // tn),
        in_specs=[pl.BlockSpec((tm, k), lambda i, j: (i, 0)),
                  pl.BlockSpec((k, tn), lambda i, j: (0, j)),
                  pl.BlockSpec((None, SUBLANES, tn), lambda i, j: (i // tps, 0, j)),
                  pl.BlockSpec((GDN_CONV_W, tn), lambda i, j: (0, j))],
        out_specs=[pl.BlockSpec((tm, tn), lambda i, j: (i, j)),
                   pl.BlockSpec((None, SUBLANES, tn), lambda i, j: (i, 0, j))],
        out_shape=[jax.ShapeDtypeStruct((m, n), F32), jax.ShapeDtypeStruct((m // tm, SUBLANES, n), F32)],
        scratch_shapes=[pltpu.VMEM((GDN_PROJ_SUBBLOCKS, SUBLANES + tm // GDN_PROJ_SUBBLOCKS, tn), F32),
                        pltpu.VMEM((n // tn, SUBLANES, tn), F32)],
        compiler_params=_params("arbitrary", "arbitrary", vmem=VMEM_LIMIT_BIG_BYTES), name="gdn_in_proj")(
            a, w, buf8, conv_w)
    return proj, tails[tps - 1::tps]


def _gdn_prep(proj3, buf8, conv_w, tt, tc):
    b, t, _ = proj3.shape
    tt = min(tt, t)
    assert t % tt == 0 and tt % SUBLANES == 0 and GDN_QK_DIM % tc == 0
    hb = tt // SUBLANES
    return pl.pallas_call(
        functools.partial(_gdn_prep_kernel, n_qk_tiles=GDN_QK_DIM // tc),
        grid=(b, t // tt, GDN_CONV_DIM // tc),
        in_specs=[pl.BlockSpec((None, tt, tc), lambda bb, i, j: (bb, i, j)),
                  pl.BlockSpec((None, SUBLANES, tc), lambda bb, i, j: (bb, jnp.maximum(i * hb - 1, 0), j)),
                  pl.BlockSpec((None, SUBLANES, tc), lambda bb, i, j: (bb, 0, j)),
                  pl.BlockSpec((GDN_CONV_W, tc), lambda bb, i, j: (0, j))],
        out_specs=pl.BlockSpec((None, tt, tc), lambda bb, i, j: (bb, i, j)),
        out_shape=jax.ShapeDtypeStruct((b, t, GDN_CONV_DIM), F32),
        scratch_shapes=[pltpu.VMEM((SUBLANES + tt, tc), F32)],
        compiler_params=_params("parallel", "parallel", "parallel"), name="gdn_prep")(
            proj3, proj3, buf8, conv_w)


def _gdn_delta_kernel(q_ref, k_ref, v_ref, z_ref, gc_ref, bc_ref, gr_ref, s0_ref, nw_ref,
                      o_ref, so_ref, s_scr, *, cb, hpb, t_valid, n_blocks):
    n = pl.program_id(2)
    r = cb * CHUNK
    dh = GDN_HEAD_DIM

    @pl.when(n == 0)
    def _():
        s_scr[...] = s0_ref[...]

    q = q_ref[...]
    k = k_ref[...]
    v = v_ref[...]
    gcol = gc_ref[...]
    bcol = bc_ref[...]
    grow = gr_ref[...]
    if t_valid is not None:
        okc = (n * r + lax.broadcasted_iota(jnp.int32, (r, 1), 0)) < t_valid
        okr = (n * r + lax.broadcasted_iota(jnp.int32, (1, r), 1)) < t_valid
        k = jnp.where(okc, k, 0.0)
        v = jnp.where(okc, v, 0.0)
        gcol = jnp.where(okc, gcol, 0.0)
        bcol = jnp.where(okc, bcol, 0.0)
        grow = jnp.where(okr, grow, 0.0)

    lm = _chunk_tril(r)
    gcum_col = [sum(_dot(lm, p) for p in _split3(gcol[qh])) for qh in range(hpb)]
    gcum_row = [sum(_dot_nt(p, lm) for p in _split3(grow[qh])) for qh in range(hpb)]

    ri = lax.broadcasted_iota(jnp.int32, (r, r), 0)
    ci = lax.broadcasted_iota(jnp.int32, (r, r), 1)
    eye = jnp.where(ri == ci, 1.0, 0.0)
    tril = ((ri // CHUNK) == (ci // CHUNK)) & (ri >= ci)
    lvl_masks = []
    size = 1
    while size < CHUNK:
        lvl_masks.append(((ri // (2 * size)) == (ci // (2 * size)))
                         & ((ri // size) % 2 == 1) & ((ci // size) % 2 == 0))
        size *= 2
    if t_valid is not None and t_valid <= 1:
        lvl_masks = []

    nw = nw_ref[...]
    heads = [(qh, hh) for qh in range(hpb) for hh in range(2)]
    kq = [k[:, qh * dh:(qh + 1) * dh] for qh in range(hpb)]
    qq = [q[:, qh * dh:(qh + 1) * dh] for qh in range(hpb)]
    gram = [_dot_nt(kq[qh], kq[qh]) for qh in range(hpb)]
    qk = [_dot_nt(qq[qh], kq[qh]) for qh in range(hpb)]
    gc = [gcum_col[qh][:, hh:hh + 1] for qh, hh in heads]
    bc = [bcol[qh][:, hh:hh + 1] for qh, hh in heads]
    decay = [jnp.exp(jnp.where(tril, gc[i] - gcum_row[qh][hh:hh + 1, :], NEG)) for i, (qh, hh) in enumerate(heads)]
    mm_ = [(gram[qh] * bc[i] * decay[i]).astype(BF16) for i, (qh, hh) in enumerate(heads)]
    tinv = [eye.astype(BF16) for _ in heads]
    for lvl, msk in enumerate(lvl_masks):
        mskb = jnp.where(msk, 1.0, 0.0).astype(BF16)
        ml = [m * mskb for m in mm_]
        if lvl == 0:
            tinv = [t - m for t, m in zip(tinv, ml)]
        else:
            y = [_dot(m, t) for t, m in zip(tinv, ml)]
            tinv = [t - _dot(t, yy).astype(BF16) for t, yy in zip(tinv, y)]
    eg = [jnp.exp(g_) for g_ in gc]
    sol = [_dot(tinv[i], jnp.concatenate([v[:, (2 * qh + hh) * dh:(2 * qh + hh + 1) * dh] * bc[i],
                                          kq[qh] * (bc[i] * eg[i])], axis=1))
           for i, (qh, hh) in enumerate(heads)]
    asol = [_dot(qk[qh] * decay[i], sol[i]) for i, (qh, hh) in enumerate(heads)]
    o2 = [a[:, :dh] for a in asol]
    o1 = [qq[qh] * eg[i] - asol[i][:, dh:] for i, (qh, hh) in enumerate(heads)]
    glast = [[g_[(c + 1) * CHUNK - 1:(c + 1) * CHUNK, :] for c in range(cb)] for g_ in gc]
    kd = [kq[qh] * jnp.exp(jnp.concatenate([jnp.broadcast_to(gl, (CHUNK, 1)) for gl in glast[i]], axis=0) - gc[i])
          for i, (qh, hh) in enumerate(heads)]
    qw = [[_dot_tn(kd[i][c * CHUNK:(c + 1) * CHUNK], sol[i][c * CHUNK:(c + 1) * CHUNK]) for c in range(cb)]
          for i in range(len(heads))]
    s = [s_scr[2 * qh + hh] for qh, hh in heads]
    for c in range(cb):
        sl = slice(c * CHUNK, (c + 1) * CHUNK)
        for i, (qh, hh) in enumerate(heads):
            x = _dot(jnp.concatenate([qw[i][c][:, dh:], o1[i][sl]], axis=0), s[i])
            o = x[dh:] + o2[i][sl]
            s[i] = s[i] * jnp.exp(glast[i][c]) - x[:dh] + qw[i][c][:, :dh]
            col = slice((2 * qh + hh) * dh, (2 * qh + hh + 1) * dh)
            zc = z_ref[sl, col]
            o_ref[sl, col] = (_rms(o, nw) * _silu(zc)).astype(o_ref.dtype)
    for i, (qh, hh) in enumerate(heads):
        s_scr[2 * qh + hh] = s[i]

    @pl.when(n == n_blocks - 1)
    def _():
        so_ref[...] = s_scr[...]


def _gdn_delta(qkv3, proj3, gcol, bcol, grow, s0, norm_w, out_dtype, cb, hpb, t_valid):
    b, t, _ = qkv3.shape
    r = cb * CHUNK
    assert t % r == 0
    nb = t // r
    dh = GDN_HEAD_DIM
    qw_, vw_ = hpb * dh, 2 * hpb * dh
    assert GDN_QK_HEADS % hpb == 0
    koff = GDN_QK_DIM // qw_
    voff = 2 * GDN_QK_DIM // vw_
    zoff = (proj3.shape[-1] - GDN_V_DIM) // vw_
    return pl.pallas_call(
        functools.partial(_gdn_delta_kernel, cb=cb, hpb=hpb, t_valid=t_valid, n_blocks=nb),
        grid=(b, GDN_QK_HEADS // hpb, nb),
        in_specs=[pl.BlockSpec((None, r, qw_), lambda bb, h, n: (bb, n, h)),
                  pl.BlockSpec((None, r, qw_), lambda bb, h, n: (bb, n, koff + h)),
                  pl.BlockSpec((None, r, vw_), lambda bb, h, n: (bb, n, voff + h)),
                  pl.BlockSpec((None, r, vw_), lambda bb, h, n: (bb, n, zoff + h)),
                  pl.BlockSpec((None, hpb, r, 2), lambda bb, h, n: (bb, h, n, 0)),
                  pl.BlockSpec((None, hpb, r, 2), lambda bb, h, n: (bb, h, n, 0)),
                  pl.BlockSpec((None, hpb, 2, r), lambda bb, h, n: (bb, h, 0, n)),
                  pl.BlockSpec((None, 2 * hpb, dh, dh), lambda bb, h, n: (bb, h, 0, 0)),
                  pl.BlockSpec((1, dh), lambda bb, h, n: (0, 0))],
        out_specs=[pl.BlockSpec((None, r, vw_), lambda bb, h, n: (bb, n, h)),
                   pl.BlockSpec((None, 2 * hpb, dh, dh), lambda bb, h, n: (bb, h, 0, 0))],
        out_shape=[jax.ShapeDtypeStruct((b, t, GDN_V_DIM), out_dtype),
                   jax.ShapeDtypeStruct(s0.shape, F32)],
        scratch_shapes=[pltpu.VMEM((2 * hpb, dh, dh), F32)],
        compiler_params=_params("parallel", "parallel", "arbitrary"), name="gdn_delta")(
            qkv3, qkv3, qkv3, proj3, gcol, bcol, grow, s0, norm_w)


def _gla_kernel(q_ref, k_ref, v_ref, gate_ref, gk_ref, bgk_ref, s0_ref, nw_ref,
                o_ref, so_ref, s_scr, *, cb, t_valid, n_blocks):
    n = pl.program_id(2)
    r = cb * CHUNK
    dk, dv = GLA_DK, GLA_DV

    @pl.when(n == 0)
    def _():
        s_scr[...] = s0_ref[...]

    xg = gk_ref[...] + bgk_ref[...]
    g = (jnp.minimum(xg, 0.0) - jnp.log(1.0 + jnp.exp(-jnp.abs(xg)))) * (1.0 / GLA_GATE_NORMALIZER)
    q = q_ref[...] * GLA_DK ** -0.5
    k = k_ref[...]
    v = v_ref[...]
    if t_valid is not None:
        okc = (n * r + lax.broadcasted_iota(jnp.int32, (r, 1), 0)) < t_valid
        g = jnp.where(okc, g, 0.0)
        k = jnp.where(okc, k, 0.0)
        v = jnp.where(okc, v, 0.0)

    gparts = _split3(g)
    bcum = sum(_dot(_chunk_tril(r), p) for p in gparts)
    ones = jnp.ones((CHUNK, LANES), BF16)
    row = lax.broadcasted_iota(jnp.int32, (CHUNK, 1), 0)
    jrow = lax.broadcasted_iota(jnp.int32, (SUB, 1), 0)
    lane = lax.broadcasted_iota(jnp.int32, (SUB, CHUNK), 1)
    nw = nw_ref[...]
    s = s_scr[...]
    for c in range(cb):
        sl = slice(c * CHUNK, (c + 1) * CHUNK)
        qc, kc, vc, bc = q[sl], k[sl], v[sl], bcum[sl]
        blast = bc[CHUNK - 1:CHUNK, :]
        attn_t = jnp.zeros((CHUNK, CHUNK), F32)
        for bi in range(1, CHUNK // SUB):
            bref = bc[bi * SUB:bi * SUB + 1, :]
            k_i = jnp.where(row < bi * SUB, kc * jnp.exp(jnp.minimum(bref - bc, 0.0)), 0.0)
            q_i = jnp.where((row >= bi * SUB) & (row < (bi + 1) * SUB),
                            qc * jnp.exp(jnp.minimum(bc - bref, 0.0)), 0.0)
            attn_t = attn_t + _dot_nt(k_i, q_i)
        diag = []
        for bi in range(CHUNK // SUB):
            sb = slice(bi * SUB, (bi + 1) * SUB)
            qb, kb, bb = qc[sb], kc[sb], bc[sb]
            d = jnp.zeros((SUB, CHUNK), F32)
            for il in range(SUB):
                e = jnp.exp(jnp.minimum(bb[il:il + 1, :] - bb, 0.0))
                col = jnp.sum(kb * e * qb[il:il + 1, :], axis=-1, keepdims=True)
                col = jnp.where(jrow <= il, col, 0.0)
                d = jnp.where(lane == bi * SUB + il, col, d)
            diag.append(d)
        attn_t = attn_t + jnp.concatenate(diag, axis=0)
        o = _dot(qc * jnp.exp(bc), s) + _dot_tn(attn_t, vc)
        bl_col = sum(_dot_tn(p[sl], ones) for p in gparts)
        decay_col = jnp.concatenate([jnp.exp(bl_col)] * (dv // LANES), axis=1)
        s = s * decay_col + _dot_tn(kc * jnp.exp(blast - bc), vc)
        gt = gate_ref[sl, :]
        o_ref[sl, :] = (_rms(o, nw) * _silu(gt)).astype(o_ref.dtype)
    s_scr[...] = s

    @pl.when(n == n_blocks - 1)
    def _():
        so_ref[...] = s_scr[...]


def _gla_chunks(qkvg3, gk3, b_gk, s0, norm_w, out_dtype, cb, t_valid):
    b, t, _ = qkvg3.shape
    r = cb * CHUNK
    assert t % r == 0
    nb = t // r
    dk, dv = GLA_DK, GLA_DV
    koff = GLA_HEADS
    voff = 2 * GLA_HEADS * dk // dv
    goff = voff + GLA_HEADS
    return pl.pallas_call(
        functools.partial(_gla_kernel, cb=cb, t_valid=t_valid, n_blocks=nb),
        grid=(b, GLA_HEADS, nb),
        in_specs=[pl.BlockSpec((None, r, dk), lambda bb, h, n: (bb, n, h)),
                  pl.BlockSpec((None, r, dk), lambda bb, h, n: (bb, n, koff + h)),
                  pl.BlockSpec((None, r, dv), lambda bb, h, n: (bb, n, voff + h)),
                  pl.BlockSpec((None, r, dv), lambda bb, h, n: (bb, n, goff + h)),
                  pl.BlockSpec((None, r, dk), lambda bb, h, n: (bb, n, h)),
                  pl.BlockSpec((1, dk), lambda bb, h, n: (0, h)),
                  pl.BlockSpec((None, None, dk, dv), lambda bb, h, n: (bb, h, 0, 0)),
                  pl.BlockSpec((1, dv), lambda bb, h, n: (0, 0))],
        out_specs=[pl.BlockSpec((None, r, dv), lambda bb, h, n: (bb, n, h)),
                   pl.BlockSpec((None, None, dk, dv), lambda bb, h, n: (bb, h, 0, 0))],
        out_shape=[jax.ShapeDtypeStruct((b, t, GLA_HEADS * dv), out_dtype),
                   jax.ShapeDtypeStruct(s0.shape, F32)],
        scratch_shapes=[pltpu.VMEM((dk, dv), F32)],
        compiler_params=_params("parallel", "parallel", "arbitrary"), name="gla_chunks")(
            qkvg3, qkvg3, qkvg3, qkvg3, gk3, b_gk, s0, norm_w)


def _t5_bucket(n):
    n = np.asarray(n)
    max_exact = N_BUCKETS // 2
    nf = np.maximum(n, max_exact).astype(np.float32)
    large = max_exact + (np.log(nf / max_exact) / math.log(MAX_DISTANCE / max_exact)
                         * (N_BUCKETS - max_exact)).astype(np.int32)
    return np.where(n < max_exact, n, np.minimum(large, N_BUCKETS - 1)).astype(np.int32)


def _lambda(lam_ref, lam_init):
    lf = lam_ref[...]
    s1 = jnp.sum(lf[0:1] * lf[1:2], axis=-1, keepdims=True)
    s2 = jnp.sum(lf[2:3] * lf[3:4], axis=-1, keepdims=True)
    return jnp.exp(s1) - jnp.exp(s2) + lam_init


def _flash_kernel(qi_ref, ki_ref, q_ref, k_ref, v_ref, bd_ref, bs_ref, rb_ref, lam_ref, sub_ref, o_ref,
                  q_scr, m_scr, l_scr, acc_scr, bias_scr, *, lam_init):
    h = pl.program_id(0)
    step = pl.program_id(2)
    qi = qi_ref[step]
    ki = ki_ref[step]
    tb = q_ref.shape[0]
    dh = DIFF_HEAD_DIM

    @pl.when((pl.program_id(1) == 0) & (step == 0))
    def _():
        ri = lax.broadcasted_iota(jnp.int32, (tb, tb), 0)
        ci = lax.broadcasted_iota(jnp.int32, (tb, tb), 1)
        for mm in range(2):
            bd = jnp.zeros((tb, tb), F32)
            bs = jnp.zeros((tb, tb), F32)
            for bk in range(N_BUCKETS):
                val = rb_ref[bk, 2 * h + mm] * LOG2E
                bd = jnp.where(bd_ref[...] == bk, val, bd)
                bs = jnp.where(bs_ref[...] == bk, val, bs)
            bias_scr[mm, 0] = jnp.where(ri > ci, NEG, bd)
            bias_scr[mm, 1] = bs

    @pl.when(ki == 0)
    def _():
        q_scr[...] = (q_ref[...] * (dh ** -0.5 * LOG2E)).astype(q_scr.dtype)
        m_scr[...] = jnp.full_like(m_scr, NEG)
        l_scr[...] = jnp.zeros_like(l_scr)
        acc_scr[...] = jnp.zeros_like(acc_scr)

    def update(mm, s, shift, vt):
        m_prev = m_scr[mm]
        if shift.ndim == 0:
            m_new = jnp.maximum(m_prev, jnp.max(s, axis=0, keepdims=True) + shift)
            p = jnp.exp2(s - (m_new - shift))
        else:
            s = s + shift
            m_new = jnp.maximum(m_prev, jnp.max(s, axis=0, keepdims=True))
            p = jnp.exp2(s - m_new)
        alpha = jnp.exp2(m_prev - m_new)
        l_scr[mm] = alpha * l_scr[mm] + jnp.sum(p, axis=0, keepdims=True)
        acc_scr[mm] = alpha * acc_scr[mm] + _dot(vt, p)
        m_scr[mm] = m_new

    def both_maps(shift_of):
        rows = pl.ds(pl.multiple_of(ki * tb, tb), tb)
        vt = v_ref[rows, :].T.astype(BF16)
        for mm in range(2):
            s = _dot_nt(k_ref[rows, mm * dh:(mm + 1) * dh], q_scr[:, mm * dh:(mm + 1) * dh])
            update(mm, s, shift_of(mm), vt)

    @pl.when(ki == qi)
    def _():
        both_maps(lambda mm: bias_scr[mm, 0])

    @pl.when(ki == qi - 1)
    def _():
        both_maps(lambda mm: bias_scr[mm, 1])

    @pl.when(ki < qi - 1)
    def _():
        both_maps(lambda mm: rb_ref[N_BUCKETS - 1, 2 * h + mm] * LOG2E)

    @pl.when(ki == qi)
    def _():
        lam = _lambda(lam_ref, lam_init)
        out = acc_scr[0] / l_scr[0] - lam * (acc_scr[1] / l_scr[1])
        out = out * lax.rsqrt(jnp.mean(out * out, axis=0, keepdims=True) + NORM_EPS)
        o_ref[...] = (out * (sub_ref[...] * (1.0 - lam_init))).T.astype(o_ref.dtype)


def _flash_diff(qkv3, rel_bias, lambdas, subln_col, lam_init, out_dtype, tb):
    b, t, _ = qkv3.shape
    tb = min(tb, t)
    assert t % tb == 0 and tb >= MAX_DISTANCE
    nq = t // tb
    dd = 2 * DIFF_HEAD_DIM
    j = np.arange(tb)[:, None]
    i = np.arange(tb)[None, :]
    bkt_diag = jnp.asarray(_t5_bucket(np.maximum(i - j, 0)))
    bkt_sub = jnp.asarray(_t5_bucket(tb + i - j))
    koff = D_MODEL // dd
    pairs = [(qi, ki) for qi in range(nq) for ki in range(qi + 1)]
    qi_tbl = jnp.asarray([pq for pq, _ in pairs], jnp.int32)
    ki_tbl = jnp.asarray([pk for _, pk in pairs], jnp.int32)
    grid_spec = pltpu.PrefetchScalarGridSpec(
        num_scalar_prefetch=2, grid=(DIFF_HEADS, b, len(pairs)),
        in_specs=[pl.BlockSpec((None, tb, dd), lambda h, bb, s, qt, kt: (bb, qt[s], h)),
                  pl.BlockSpec((None, t, dd), lambda h, bb, s, qt, kt: (bb, 0, koff + h)),
                  pl.BlockSpec((None, t, dd), lambda h, bb, s, qt, kt: (bb, 0, 2 * koff + h)),
                  pl.BlockSpec((tb, tb), lambda h, bb, s, qt, kt: (0, 0)),
                  pl.BlockSpec((tb, tb), lambda h, bb, s, qt, kt: (0, 0)),
                  pl.BlockSpec(memory_space=pltpu.SMEM),
                  pl.BlockSpec((4, DIFF_HEAD_DIM), lambda h, bb, s, qt, kt: (0, 0)),
                  pl.BlockSpec((dd, 1), lambda h, bb, s, qt, kt: (0, 0))],
        out_specs=pl.BlockSpec((None, tb, dd), lambda h, bb, s, qt, kt: (bb, qt[s], h)),
        scratch_shapes=[pltpu.VMEM((tb, dd), BF16),
                        pltpu.VMEM((2, 1, tb), F32), pltpu.VMEM((2, 1, tb), F32),
                        pltpu.VMEM((2, dd, tb), F32), pltpu.VMEM((2, 2, tb, tb), F32)])
    return pl.pallas_call(
        functools.partial(_flash_kernel, lam_init=lam_init), grid_spec=grid_spec,
        out_shape=jax.ShapeDtypeStruct((b, t, D_MODEL), out_dtype),
        compiler_params=_params("parallel", "arbitrary", "arbitrary"), name="flash_diff")(
            qi_tbl, ki_tbl, qkv3, qkv3, qkv3, bkt_diag, bkt_sub, rel_bias, lambdas, subln_col)


def _decode_kernel(pt_ref, q_ref, *refs, pp, n_steps, lam_init):
    kp_refs, vp_refs = refs[:pp], refs[pp:2 * pp]
    kn_ref, vn_ref, bkt_ref, tbl_ref, lam_ref, sub_ref, o_ref, m_scr, l_scr, acc_scr, bias_scr, p_scr, a_scr = refs[2 * pp:]
    step = pl.program_id(1)
    nh, dh = DIFF_HEADS, DIFF_HEAD_DIM
    ones = jnp.ones((dh, LANES), BF16)
    qs = q_ref[...] * (dh ** -0.5 * LOG2E)

    @pl.when(step == 0)
    def _():
        m_scr[...] = jnp.full_like(m_scr, NEG)
        l_scr[...] = jnp.zeros_like(l_scr)
        acc_scr[...] = jnp.zeros_like(acc_scr)
        bkt = bkt_ref[...]
        bias = jnp.zeros(bkt.shape, F32)
        for bk in range(N_BUCKETS):
            bias = jnp.where(bkt == bk, (tbl_ref[bk] * LOG2E)[None], bias)
        bias_scr[...] = bias

    def lane_sum(x):
        return _dot(x, ones)

    def accumulate(pr, shift_row, values):
        m_prev = m_scr[...]
        if pr.ndim == 3:
            m_new = jnp.maximum(m_prev, jnp.max(pr, axis=0) + shift_row)
            p = jnp.exp2(pr - (m_new - shift_row)[None])
            psum = jnp.sum(p, axis=0)
        else:
            m_new = jnp.maximum(m_prev, pr + shift_row)
            p = jnp.exp2(pr - (m_new - shift_row))
            psum = p
        alpha = jnp.exp2(m_prev - m_new)
        l_scr[...] = alpha * l_scr[...] + psum
        m_scr[...] = m_new
        a_scr[...] = alpha
        if pr.ndim == 3:
            p_scr[...] = p
        else:
            p_scr[0] = p
        for par in range(2):
            ae = a_scr[pl.ds(par, nh, stride=2), :]
            if pr.ndim == 3:
                pe = p_scr[:, pl.ds(par, nh, stride=2), :]
                pv = jnp.sum(jnp.concatenate([pe, pe], axis=-1) * values(), axis=0)
            else:
                pe = p_scr[0, pl.ds(par, nh, stride=2), :]
                pv = jnp.concatenate([pe, pe], axis=-1) * values()
            acc_scr[par] = jnp.concatenate([ae, ae], axis=-1) * acc_scr[par] + pv

    def page_logits(j):
        prod = kp_refs[j][...] * qs[None]
        return lane_sum(prod.reshape(PAGE_SIZE * 2 * nh, dh)).reshape(PAGE_SIZE, 2 * nh, LANES)

    far = tbl_ref[N_BUCKETS - 1] * LOG2E
    zero = jnp.zeros_like(far)
    for j in range(pp - 1):
        accumulate(page_logits(j), far, lambda j=j: vp_refs[j][...])

    @pl.when(step < n_steps - 1)
    def _():
        accumulate(page_logits(pp - 1), far, lambda: vp_refs[pp - 1][...])

    @pl.when(step == n_steps - 1)
    def _():
        accumulate(page_logits(pp - 1) + bias_scr[...], zero, lambda: vp_refs[pp - 1][...])
        accumulate(lane_sum(kn_ref[...] * qs) + tbl_ref[0] * LOG2E, zero, lambda: vn_ref[...])
        l_scr_v = l_scr[...]
        a_scr[...] = l_scr_v
        outs = []
        for par in range(2):
            le = a_scr[pl.ds(par, nh, stride=2), :]
            outs.append(acc_scr[par] / jnp.concatenate([le, le], axis=-1))
        out = outs[0] - _lambda(lam_ref, lam_init) * outs[1]
        o_ref[...] = _rms(out, sub_ref[...]) * (1.0 - lam_init)


def _decode_diff(q, k_new, v_new, cache_k, cache_v, layer, page_table, rel_bias, lambdas, subln, lam_init, pp):
    b, n_pages = page_table.shape
    nh, dh = DIFF_HEADS, DIFF_HEAD_DIM
    pp = min(pp, n_pages)
    assert n_pages % pp == 0 and PAGE_SIZE >= MAX_DISTANCE and dh == LANES
    n_steps = n_pages // pp
    bkt_last = jnp.asarray(np.broadcast_to(
        _t5_bucket(PAGE_SIZE - np.arange(PAGE_SIZE)).reshape(PAGE_SIZE, 1, 1), (PAGE_SIZE, 2 * nh, LANES)))
    tbl = jnp.broadcast_to(rel_bias[:, :, None], (N_BUCKETS, 2 * nh, LANES))
    page_spec = lambda j, width, lanes: pl.BlockSpec(
        (None, None, PAGE_SIZE, width, lanes), lambda bb, s, pt: (layer, pt[bb, s * pp + j], 0, 0, 0))
    full = lambda shape: pl.BlockSpec(shape, lambda bb, s, pt: (0,) * len(shape))
    per_seq = lambda shape: pl.BlockSpec((None,) + shape, lambda bb, s, pt: (bb,) + (0,) * len(shape))
    grid_spec = pltpu.PrefetchScalarGridSpec(
        num_scalar_prefetch=1, grid=(b, n_steps),
        in_specs=([per_seq((2 * nh, dh))]
                  + [page_spec(j, 2 * nh, dh) for j in range(pp)]
                  + [page_spec(j, nh, 2 * dh) for j in range(pp)]
                  + [per_seq((2 * nh, dh)), per_seq((nh, 2 * dh)),
                     full((PAGE_SIZE, 2 * nh, LANES)), full((N_BUCKETS, 2 * nh, LANES)),
                     full((4, dh)), full((1, 2 * dh))]),
        out_specs=per_seq((nh, 2 * dh)),
        scratch_shapes=[pltpu.VMEM((2 * nh, LANES), F32), pltpu.VMEM((2 * nh, LANES), F32),
                        pltpu.VMEM((2, nh, 2 * dh), F32), pltpu.VMEM((PAGE_SIZE, 2 * nh, LANES), F32),
                        pltpu.VMEM((PAGE_SIZE, 2 * nh, LANES), F32), pltpu.VMEM((2 * nh, LANES), F32)])
    return pl.pallas_call(
        functools.partial(_decode_kernel, pp=pp, n_steps=n_steps, lam_init=lam_init),
        grid_spec=grid_spec,
        out_shape=jax.ShapeDtypeStruct((b, nh, 2 * dh), F32),
        compiler_params=_params("parallel", "arbitrary"), name="decode_diff")(
            page_table, q, *([cache_k] * pp), *([cache_v] * pp), k_new, v_new, bkt_last, tbl, lambdas, subln)


class _Tiles:
    def __init__(self, prompt):
        self.prompt = prompt
        self.act_dtype = BF16 if prompt else F32
        self.tm = 1024 if prompt else SUBLANES
        self.tn = 1024
        self.tn_ff = 512
        self.tm_ff = 1024
        self.tm_res = 256 if prompt else SUBLANES
        self.tk_res = 512
        self.tt_prep = 256 if prompt else SUBLANES
        self.tc_prep = 1024
        self.cb = 4 if prompt else 1
        self.gdn_hpb = 4
        self.tb_attn = 512
        self.decode_pages = 8


def _group_trunk(x3, tiles, gdn_s0, gdn_c0, gla_s0, cache_k, cache_v, page_table, p, wb):
    b, t, d = x3.shape
    m = b * t
    x = x3.reshape(m, d)
    if not tiles.prompt:
        assert t == 1 and m % SUBLANES == 0
    t_pad = t if tiles.prompt else CHUNK
    t_valid = None if tiles.prompt else t
    nw = lambda i, jn: p['norm_w'][i, jn].reshape(1, d)
    adt = tiles.act_dtype

    def pad_time(a3, to):
        return a3 if a3.shape[1] == to else jnp.pad(a3, ((0, 0), (0, to - a3.shape[1]), (0, 0)))

    def big_matmul(a, name, layer):
        if tiles.prompt:
            return _matmul(a, wb[name, layer], layer, F32, tiles.tm, tiles.tn)
        out, wb[name, layer] = _matmul(a, p[name], layer, F32, tiles.tm, tiles.tn, emit_wb=True)
        return out

    def residual_matmul(a, name, layer, x_in, nw_post, nw_next):
        if tiles.prompt:
            return _matmul_residual(a, wb[name, layer], layer, x_in, nw_post, nw_next, adt,
                                    tiles.tm_res, tiles.tk_res)
        xo, h_next, wb[name, layer] = _matmul_residual(a, p[name], layer, x_in, nw_post, nw_next, adt,
                                                       tiles.tm_res, tiles.tk_res, emit_wb=True)
        return xo, h_next

    def ffn_up(a, layer):
        if tiles.prompt:
            return _swiglu_up(a, wb['ffn_gate', layer], wb['ffn_up', layer], layer, adt, tiles.tm_ff, tiles.tn_ff)
        act_, wb['ffn_gate', layer], wb['ffn_up', layer] = _swiglu_up(
            a, p['ffn_w_up'], p['ffn_w_up'], layer, adt, tiles.tm, tiles.tn_ff, emit_wb=True)
        return act_

    gdn_s, gdn_c, gla_s, att_k, att_v = [], [], [], [], []
    h = _rmsnorm(x, nw(0, 0), adt, tiles.tm)
    for i in range(DEPTH):
        j = i // N_MIXERS
        kind = i % N_MIXERS
        if kind == 0:
            beta, g = _gdn_gates(h, p['gdn_w_ba'], j, p['gdn_a_log'][j].reshape(1, -1),
                                 p['gdn_dt_bias'][j].reshape(1, -1), tiles.tm)
            buf0 = gdn_c0[j]
            nbuf = GDN_CONV_W - 1
            buf8 = jnp.pad(buf0, ((0, 0), (SUBLANES - nbuf, 0), (0, 0)))
            if tiles.prompt:
                proj, tail = _gdn_in_proj(h, wb['gdn_w_in', j], buf8, p['gdn_conv_w'][j], t, tiles.tm, tiles.tn)
                qkv3 = proj.reshape(b, t, -1)
                proj3p = _matmul(h, wb['gdn_w_in', j], j, F32, tiles.tm, tiles.tn,
                                 col_start=GDN_CONV_DIM, n_out=GDN_V_DIM).reshape(b, t, -1)
                new_buf = tail[:, SUBLANES - nbuf:, :]
            else:
                proj3 = big_matmul(h, 'gdn_w_in', j).reshape(b, t, -1)
                new_buf = jnp.concatenate([buf0[:, t:], proj3[..., :GDN_CONV_DIM]], axis=1)
                proj3p = pad_time(proj3, SUBLANES)
                qkv3 = pad_time(_gdn_prep(proj3p, buf8, p['gdn_conv_w'][j], tiles.tt_prep, tiles.tc_prep), t_pad)
                proj3p = pad_time(proj3p, t_pad)

            def heads(a):
                a = pad_time(a.reshape(b, t, GDN_QK_HEADS * 2), t_pad)
                return a.reshape(b, t_pad, GDN_QK_HEADS, 2).transpose(0, 2, 1, 3)
            gcol, bcol = heads(g), heads(beta)
            o3, s_new = _gdn_delta(qkv3, proj3p, gcol, bcol, gcol.transpose(0, 1, 3, 2), gdn_s0[j],
                                   p['gdn_norm_w'][j].reshape(1, -1), adt, tiles.cb, tiles.gdn_hpb, t_valid)
            mix_in, w_out = o3[:, :t].reshape(m, -1), 'gdn_w_out'
            gdn_s.append(s_new)
            gdn_c.append(new_buf)
        elif kind == 1:
            qkvg = big_matmul(h, 'gla_w_qkvg', j)
            low = _matmul(h, p['gla_w_gk1'], j, F32, tiles.tm, tiles.tn)
            gk = _matmul(low, p['gla_w_gk2'], j, F32, tiles.tm, tiles.tn)
            o3, s_new = _gla_chunks(pad_time(qkvg.reshape(b, t, -1), t_pad), pad_time(gk.reshape(b, t, -1), t_pad),
                                    p['gla_b_gk'][j].reshape(1, -1), gla_s0[j],
                                    p['gla_norm_w'][j].reshape(1, -1), adt, tiles.cb, t_valid)
            mix_in, w_out = o3[:, :t].reshape(m, -1), 'gla_w_out'
            gla_s.append(s_new)
        else:
            lam_init = 0.8 - 0.6 * math.exp(-0.3 * i)
            qkv = big_matmul(h, 'diff_w_qkv', j)
            nh, dh = DIFF_HEADS, DIFF_HEAD_DIM
            k_new = qkv[:, D_MODEL:2 * D_MODEL].reshape(b, t, 2 * nh, dh)
            v_new = qkv[:, 2 * D_MODEL:].reshape(b, t, nh, 2 * dh)
            subln = p['diff_subln'][j].reshape(1, -1)
            if tiles.prompt:
                o3 = _flash_diff(qkv.reshape(b, t, -1), p['rel_bias'], p['diff_lambda'][j],
                                 subln.reshape(-1, 1), lam_init, adt, tiles.tb_attn)
                mix_in = o3.reshape(m, -1)
            else:
                o = _decode_diff(qkv[:, :D_MODEL].reshape(b, 2 * nh, dh), k_new.reshape(b, 2 * nh, dh),
                                 v_new.reshape(b, nh, 2 * dh), cache_k, cache_v, j, page_table,
                                 p['rel_bias'], p['diff_lambda'][j], subln, lam_init, tiles.decode_pages)
                mix_in = o.reshape(m, -1)
            w_out = 'diff_w_out'
            att_k.append(k_new)
            att_v.append(v_new)
        x, h = residual_matmul(mix_in, w_out, j, x, nw(i, 1), nw(i, 2))
        act = ffn_up(h, i)
        nxt = nw(i + 1, 0) if i + 1 < DEPTH else None
        x, h = residual_matmul(act, 'ffn_w_down', i, x, nw(i, 3), nxt)
    return (x.reshape(b, t, d), jnp.stack(gdn_s), jnp.stack(gdn_c), jnp.stack(gla_s),
            jnp.stack(att_k), jnp.stack(att_v))


def kernel(x_prompt, x_sample, state_gdn, state_gdn_conv, state_gla, cache_k, cache_v, page_table,
           norm_w, ffn_w_up, ffn_w_down, rel_bias,
           gdn_w_in, gdn_w_ba, gdn_conv_w, gdn_a_log, gdn_dt_bias, gdn_norm_w, gdn_w_out,
           gla_w_qkvg, gla_w_gk1, gla_w_gk2, gla_b_gk, gla_norm_w, gla_w_out,
           diff_w_qkv, diff_lambda, diff_subln, diff_w_out):
    p = dict(norm_w=norm_w, ffn_w_up=ffn_w_up, ffn_w_down=ffn_w_down, rel_bias=rel_bias,
             gdn_w_in=gdn_w_in, gdn_w_ba=gdn_w_ba, gdn_conv_w=gdn_conv_w, gdn_a_log=gdn_a_log,
             gdn_dt_bias=gdn_dt_bias, gdn_norm_w=gdn_norm_w, gdn_w_out=gdn_w_out,
             gla_w_qkvg=gla_w_qkvg, gla_w_gk1=gla_w_gk1, gla_w_gk2=gla_w_gk2, gla_b_gk=gla_b_gk,
             gla_norm_w=gla_norm_w, gla_w_out=gla_w_out,
             diff_w_qkv=diff_w_qkv, diff_lambda=diff_lambda, diff_subln=diff_subln, diff_w_out=diff_w_out)
    bp = x_prompt.shape[0]
    n_gdn, n_gla = state_gdn.shape[0], state_gla.shape[0]
    zeros_gdn = jnp.zeros((n_gdn, bp) + state_gdn.shape[2:], F32)
    zeros_conv = jnp.zeros((n_gdn, bp) + state_gdn_conv.shape[2:], F32)
    zeros_gla = jnp.zeros((n_gla, bp) + state_gla.shape[2:], F32)
    wb = {}
    outs_s = _group_trunk(x_sample, _Tiles(False), state_gdn, state_gdn_conv, state_gla,
                          cache_k, cache_v, page_table, p, wb)
    outs_p = _group_trunk(x_prompt, _Tiles(True), zeros_gdn, zeros_conv, zeros_gla, None, None, None, p, wb)
    return (outs_p[0], outs_s[0]) + outs_p[1:] + outs_s[1:]
```

```python
import functools
import math

import numpy as np
import jax
import jax.numpy as jnp
from jax import lax
from jax.experimental import pallas as pl
from jax.experimental.pallas import tpu as pltpu

F32 = jnp.float32
BF16 = jnp.bfloat16

D_MODEL = 2048
DEPTH = 4
PAGE_SIZE = 128
N_MIXERS = 3
GDN_HEAD_DIM = 128
GDN_QK_HEADS = D_MODEL // 128
GDN_V_HEADS = 2 * GDN_QK_HEADS
GDN_QK_DIM = GDN_QK_HEADS * GDN_HEAD_DIM
GDN_V_DIM = GDN_V_HEADS * GDN_HEAD_DIM
GDN_CONV_DIM = 2 * GDN_QK_DIM + GDN_V_DIM
GDN_CONV_W = 4
GLA_HEADS = 4
GLA_DK = D_MODEL // 2 // GLA_HEADS
GLA_DV = D_MODEL // GLA_HEADS
GLA_GATE_NORMALIZER = 16.0
DIFF_HEAD_DIM = 128
DIFF_HEADS = D_MODEL // (2 * DIFF_HEAD_DIM)
N_BUCKETS = 32
MAX_DISTANCE = 128
D_FF = ((8 * D_MODEL + 3 * 256 - 1) // (3 * 256)) * 256
NORM_EPS = 1e-6

LANES = 128
SUBLANES = 8
CHUNK = 64
SUB = 16
GDN_PROJ_SUBBLOCKS = 4
VMEM_LIMIT_BYTES = 48 * 1024 * 1024
VMEM_LIMIT_BIG_BYTES = 56 * 1024 * 1024
NEG = -1e30
LOG2E = math.log2(math.e)


def _params(*sem, vmem=VMEM_LIMIT_BYTES):
    return pltpu.CompilerParams(dimension_semantics=sem, vmem_limit_bytes=vmem)


def _dot(a, b):
    return jnp.dot(a.astype(BF16), b.astype(BF16), preferred_element_type=F32)


def _dot_nt(a, b):
    return lax.dot_general(a.astype(BF16), b.astype(BF16), (((1,), (1,)), ((), ())),
                           preferred_element_type=F32)


def _dot_tn(a, b):
    return lax.dot_general(a.astype(BF16), b.astype(BF16), (((0,), (0,)), ((), ())),
                           preferred_element_type=F32)


def _split3(x):
    hi = x.astype(BF16)
    r = x - hi.astype(F32)
    mid = r.astype(BF16)
    lo = (r - mid.astype(F32)).astype(BF16)
    return hi, mid, lo


def _sigmoid(x):
    return 0.5 * jnp.tanh(0.5 * x) + 0.5


def _silu(x):
    h = 0.5 * x
    return h + h * jnp.tanh(h)


def _softplus(x):
    return jnp.maximum(x, 0.0) + jnp.log(1.0 + jnp.exp(-jnp.abs(x)))


def _rms(x, w):
    return x * lax.rsqrt(jnp.mean(x * x, axis=-1, keepdims=True) + NORM_EPS) * w


def _chunk_tril(r):
    i = lax.broadcasted_iota(jnp.int32, (r, r), 0)
    j = lax.broadcasted_iota(jnp.int32, (r, r), 1)
    return jnp.where((j <= i) & ((i // CHUNK) == (j // CHUNK)), 1.0, 0.0).astype(BF16)


def _rmsnorm_kernel(x_ref, w_ref, o_ref):
    o_ref[...] = _rms(x_ref[...], w_ref[...]).astype(o_ref.dtype)


def _rmsnorm(x, w, out_dtype, tm):
    m, d = x.shape
    tm = min(tm, m)
    return pl.pallas_call(
        _rmsnorm_kernel, grid=(m // tm,),
        in_specs=[pl.BlockSpec((tm, d), lambda i: (i, 0)), pl.BlockSpec((1, d), lambda i: (0, 0))],
        out_specs=pl.BlockSpec((tm, d), lambda i: (i, 0)),
        out_shape=jax.ShapeDtypeStruct((m, d), out_dtype),
        compiler_params=_params("parallel"), name="rmsnorm")(x, w)


def _weight_spec(w, layer, block, index):
    if w.ndim == 2:
        return pl.BlockSpec(block, index)
    return pl.BlockSpec((None,) + block, lambda *g: (layer,) + index(*g))


def _mm_kernel(a_ref, w_ref, o_ref, *wb_ref):
    w = w_ref[...].astype(BF16)
    if wb_ref:
        wb_ref[0][...] = w
    o_ref[...] = _dot(a_ref[...], w).astype(o_ref.dtype)


def _matmul(a, w, layer, out_dtype, tm, tn, emit_wb=False, col_start=0, n_out=None):
    m, k = a.shape
    n = w.shape[-1] if n_out is None else n_out
    tm, tn = min(tm, m), min(tn, n)
    assert m % tm == 0 and n % tn == 0 and col_start % tn == 0 and (not emit_wb or (m == tm and n_out is None))
    c0 = col_start // tn
    out_shape = [jax.ShapeDtypeStruct((m, n), out_dtype)]
    out_specs = [pl.BlockSpec((tm, tn), lambda i, j: (i, j))]
    if emit_wb:
        out_shape.append(jax.ShapeDtypeStruct((k, n), BF16))
        out_specs.append(pl.BlockSpec((k, tn), lambda i, j: (0, j)))
    res = pl.pallas_call(
        _mm_kernel, grid=(m // tm, n // tn),
        in_specs=[pl.BlockSpec((tm, k), lambda i, j: (i, 0)),
                  _weight_spec(w, layer, (k, tn), lambda i, j: (0, j + c0))],
        out_specs=out_specs, out_shape=out_shape,
        compiler_params=_params("parallel", "parallel"), name="matmul")(a, w)
    return (res[0], res[1]) if emit_wb else res[0]


def _swiglu_kernel(a_ref, wg_ref, wu_ref, o_ref, *wb_refs):
    a = a_ref[...].astype(BF16)
    wg = wg_ref[...].astype(BF16)
    wu = wu_ref[...].astype(BF16)
    if wb_refs:
        wb_refs[0][...] = wg
        wb_refs[1][...] = wu
    g = _dot(a, wg)
    u = _dot(a, wu)
    o_ref[...] = (_silu(g) * u).astype(o_ref.dtype)


def _swiglu_up(a, w_gate, w_up, layer, out_dtype, tm, tn, emit_wb=False):
    m, k = a.shape
    stacked = w_up.ndim == 3
    f = w_up.shape[-1] // 2 if stacked else w_up.shape[-1]
    tm, tn = min(tm, m), min(tn, f)
    assert m % tm == 0 and f % tn == 0 and (not emit_wb or m == tm)
    nf = f // tn
    up_off = nf if stacked else 0
    out_shape = [jax.ShapeDtypeStruct((m, f), out_dtype)]
    out_specs = [pl.BlockSpec((tm, tn), lambda i, j: (i, j))]
    if emit_wb:
        out_shape += [jax.ShapeDtypeStruct((k, f), BF16)] * 2
        out_specs += [pl.BlockSpec((k, tn), lambda i, j: (0, j))] * 2
    res = pl.pallas_call(
        _swiglu_kernel, grid=(m // tm, nf),
        in_specs=[pl.BlockSpec((tm, k), lambda i, j: (i, 0)),
                  _weight_spec(w_gate, layer, (k, tn), lambda i, j: (0, j)),
                  _weight_spec(w_up, layer, (k, tn), lambda i, j: (0, j + up_off))],
        out_specs=out_specs, out_shape=out_shape,
        compiler_params=_params("parallel", "parallel"), name="swiglu_up")(a, w_gate, w_up)
    return tuple(res) if emit_wb else res[0]


def _residual_epilogue(mix, x_ref, nw1_ref, nw2_ref, xo_ref, h_ref):
    xn = x_ref[...] + _rms(mix, nw1_ref[...])
    xo_ref[...] = xn
    if h_ref:
        h_ref[0][...] = _rms(xn, nw2_ref[...]).astype(h_ref[0].dtype)


def _mm_res_kernel(a_ref, w_ref, x_ref, nw1_ref, nw2_ref, xo_ref, *rest, nk, emit_h, emit_wb):
    h_ref = rest[:1] if emit_h else ()
    kk = pl.program_id(1)
    w = w_ref[...].astype(BF16)
    if emit_wb:
        rest[-1][...] = w
    part = _dot(a_ref[...], w)

    @pl.when(kk == 0)
    def _():
        xo_ref[...] = part

    @pl.when(kk > 0)
    def _():
        xo_ref[...] += part

    @pl.when(kk == nk - 1)
    def _():
        _residual_epilogue(xo_ref[...], x_ref, nw1_ref, nw2_ref, xo_ref, h_ref)


def _mm_res_resident_kernel(a_ref, w_ref, x_ref, nw1_ref, nw2_ref, xo_ref, *h_ref):
    _residual_epilogue(_dot(a_ref[...], w_ref[...]), x_ref, nw1_ref, nw2_ref, xo_ref, h_ref)


def _matmul_residual(a, w, layer, x, nw_post, nw_next, h_dtype, tm, tk, emit_wb=False):
    m, k = a.shape
    n = w.shape[-1]
    tm = min(tm, m)
    assert m % tm == 0 and (not emit_wb or m == tm)
    emit_h = nw_next is not None
    if not emit_h:
        nw_next = nw_post
    resident = w.ndim == 2
    if resident:
        grid = (m // tm,)
        row = lambda i: (i, 0)
        fixed = lambda i: (0, 0)
        a_spec = pl.BlockSpec((tm, k), row)
        w_spec = pl.BlockSpec((k, n), fixed, pipeline_mode=pl.Buffered(1))
        x_spec = pl.BlockSpec((tm, n), row)
        body = _mm_res_resident_kernel
        sem = ("parallel",)
    else:
        tk = min(tk, k)
        assert k % tk == 0
        nk = k // tk
        grid = (m // tm, nk)
        row = lambda i, kk: (i, 0)
        fixed = lambda i, kk: (0, 0)
        a_spec = pl.BlockSpec((tm, tk), lambda i, kk: (i, kk))
        w_spec = pl.BlockSpec((None, tk, n), lambda i, kk: (layer, kk, 0))
        x_spec = pl.BlockSpec((tm, n), row, pipeline_mode=pl.Buffered(1))
        body = functools.partial(_mm_res_kernel, nk=nk, emit_h=emit_h, emit_wb=emit_wb)
        sem = ("parallel", "arbitrary")
    out_shape = [jax.ShapeDtypeStruct((m, n), F32)]
    out_specs = [pl.BlockSpec((tm, n), row)]
    if emit_h:
        out_shape.append(jax.ShapeDtypeStruct((m, n), h_dtype))
        out_specs.append(pl.BlockSpec((tm, n), row))
    if emit_wb:
        out_shape.append(jax.ShapeDtypeStruct((k, n), BF16))
        out_specs.append(pl.BlockSpec((tk, n), lambda i, kk: (kk, 0)))
    res = pl.pallas_call(
        body, grid=grid,
        in_specs=[a_spec, w_spec, x_spec, pl.BlockSpec((1, n), fixed), pl.BlockSpec((1, n), fixed)],
        out_specs=out_specs, out_shape=out_shape,
        compiler_params=_params(*sem, vmem=VMEM_LIMIT_BIG_BYTES), name="matmul_residual")(
            a, w, x, nw_post, nw_next)
    xo = res[0]
    h = res[1] if emit_h else None
    return (xo, h, res[-1]) if emit_wb else (xo, h)


def _gdn_gates_kernel(a_ref, w_ref, alog_ref, dtb_ref, beta_ref, g_ref):
    ba_t = lax.dot_general(w_ref[...].astype(BF16), a_ref[...].astype(BF16), (((0,), (1,)), ((), ())),
                           preferred_element_type=F32)
    nh = beta_ref.shape[0]
    beta_ref[...] = _sigmoid(ba_t[:nh])
    g_ref[...] = -jnp.exp(alog_ref[...]) * _softplus(ba_t[nh:] + dtb_ref[...])


def _gdn_gates(a, w_ba, layer, a_log_col, dt_bias_col, tm):
    m, k = a.shape
    nh = GDN_V_HEADS
    tm = min(tm, m)
    return pl.pallas_call(
        _gdn_gates_kernel, grid=(m // tm,),
        in_specs=[pl.BlockSpec((tm, k), lambda i: (i, 0)),
                  pl.BlockSpec((None, k, 2 * nh), lambda i: (layer, 0, 0)),
                  pl.BlockSpec((nh, 1), lambda i: (0, 0)),
                  pl.BlockSpec((nh, 1), lambda i: (0, 0))],
        out_specs=[pl.BlockSpec((nh, tm), lambda i: (0, i))] * 2,
        out_shape=[jax.ShapeDtypeStruct((nh, m), F32)] * 2,
        compiler_params=_params("parallel"), name="gdn_gates")(a, w_ba, a_log_col, dt_bias_col)


def _conv_silu_norm(x, halo, ext_scr, cw_ref, j, n_qk_tiles):
    tt, tc = x.shape
    ext_scr[:SUBLANES, :] = halo
    ext_scr[SUBLANES:, :] = x
    y = x * cw_ref[GDN_CONV_W - 1:GDN_CONV_W, :]
    for s in range(1, GDN_CONV_W):
        y = y + ext_scr[SUBLANES - s:SUBLANES - s + tt, :] * cw_ref[GDN_CONV_W - 1 - s:GDN_CONV_W - s, :]
    y = _silu(y)
    qscale = jnp.where(j < n_qk_tiles, GDN_HEAD_DIM ** -0.5, 1.0)
    pieces = []
    for hh in range(tc // GDN_HEAD_DIM):
        yh = y[:, hh * GDN_HEAD_DIM:(hh + 1) * GDN_HEAD_DIM]
        r = lax.rsqrt(jnp.sum(yh * yh, axis=-1, keepdims=True) + NORM_EPS) * qscale
        pieces.append(yh * jnp.where(j < 2 * n_qk_tiles, r, 1.0))
    return jnp.concatenate(pieces, axis=1)


def _gdn_prep_kernel(x_ref, prev_ref, buf_ref, cw_ref, o_ref, ext_scr, *, n_qk_tiles):
    i = pl.program_id(1)
    j = pl.program_id(2)
    halo = jnp.where(i == 0, buf_ref[...], prev_ref[...])
    o_ref[...] = _conv_silu_norm(x_ref[...], halo, ext_scr, cw_ref, j, n_qk_tiles)


def _gdn_in_proj_kernel(a_ref, w_ref, buf_ref, cw_ref, o_ref, tail_ref, ext_scr, halo_scr, *,
                        tiles_per_seq, n_qk_tiles):
    i = pl.program_id(0)
    j = pl.program_id(1)

    @pl.when((i == 0) & (j == 0))
    def _():
        halo_scr[...] = jnp.zeros_like(halo_scr)

    halo = jnp.where(i % tiles_per_seq == 0, buf_ref[...], halo_scr[j])
    w = w_ref[...]
    n_sub, rows = ext_scr.shape[0], ext_scr.shape[1] - SUBLANES
    for s in range(n_sub):
        sl = slice(s * rows, (s + 1) * rows)
        x = _dot(a_ref[sl, :], w)
        o_ref[sl, :] = _conv_silu_norm(x, halo, ext_scr.at[s], cw_ref, j, n_qk_tiles)
        halo = x[rows - SUBLANES:, :]
    halo_scr[j] = halo
    tail_ref[...] = halo


def _gdn_in_proj(a, w, buf8, conv_w, seq_len, tm, tn):
    m, k = a.shape
    n = GDN_CONV_DIM
    tm = min(tm, seq_len)
    assert w.ndim == 2 and seq_len % tm == 0 and GDN_QK_DIM % tn == 0
    assert tm % (GDN_PROJ_SUBBLOCKS * SUBLANES) == 0
    tps = seq_len // tm
    proj, tails = pl.pallas_call(
        functools.partial(_gdn_in_proj_kernel, tiles_per_seq=tps, n_qk_tiles=GDN_QK_DIM // tn),
        grid=(m // tm, n // tn),
        in_specs=[pl.BlockSpec((tm, k), lambda i, j: (i, 0)),
                  pl.BlockSpec((k, tn), lambda i, j: (0, j)),
                  pl.BlockSpec((None, SUBLANES, tn), lambda i, j: (i // tps, 0, j)),
                  pl.BlockSpec((GDN_CONV_W, tn), lambda i, j: (0, j))],
        out_specs=[pl.BlockSpec((tm, tn), lambda i, j: (i, j)),
                   pl.BlockSpec((None, SUBLANES, tn), lambda i, j: (i, 0, j))],
        out_shape=[jax.ShapeDtypeStruct((m, n), F32), jax.ShapeDtypeStruct((m // tm, SUBLANES, n), F32)],
        scratch_shapes=[pltpu.VMEM((GDN_PROJ_SUBBLOCKS, SUBLANES + tm // GDN_PROJ_SUBBLOCKS, tn), F32),
                        pltpu.VMEM((n // tn, SUBLANES, tn), F32)],
        compiler_params=_params("arbitrary", "arbitrary", vmem=VMEM_LIMIT_BIG_BYTES), name="gdn_in_proj")(
            a, w, buf8, conv_w)
    return proj, tails[tps - 1::tps]


def _gdn_prep(proj3, buf8, conv_w, tt, tc):
    b, t, _ = proj3.shape
    tt = min(tt, t)
    assert t % tt == 0 and tt % SUBLANES == 0 and GDN_QK_DIM % tc == 0
    hb = tt // SUBLANES
    return pl.pallas_call(
        functools.partial(_gdn_prep_kernel, n_qk_tiles=GDN_QK_DIM // tc),
        grid=(b, t // tt, GDN_CONV_DIM // tc),
        in_specs=[pl.BlockSpec((None, tt, tc), lambda bb, i, j: (bb, i, j)),
                  pl.BlockSpec((None, SUBLANES, tc), lambda bb, i, j: (bb, jnp.maximum(i * hb - 1, 0), j)),
                  pl.BlockSpec((None, SUBLANES, tc), lambda bb, i, j: (bb, 0, j)),
                  pl.BlockSpec((GDN_CONV_W, tc), lambda bb, i, j: (0, j))],
        out_specs=pl.BlockSpec((None, tt, tc), lambda bb, i, j: (bb, i, j)),
        out_shape=jax.ShapeDtypeStruct((b, t, GDN_CONV_DIM), F32),
        scratch_shapes=[pltpu.VMEM((SUBLANES + tt, tc), F32)],
        compiler_params=_params("parallel", "parallel", "parallel"), name="gdn_prep")(
            proj3, proj3, buf8, conv_w)


def _gdn_delta_kernel(q_ref, k_ref, v_ref, z_ref, gr_ref, br_ref, s0_ref, nw_ref,
                      o_ref, so_ref, s_scr, *, cb, hpb, t_valid, n_blocks):
    n = pl.program_id(2)
    r = cb * CHUNK
    dh = GDN_HEAD_DIM

    @pl.when(n == 0)
    def _():
        s_scr[...] = s0_ref[...]

    q = q_ref[...]
    k = k_ref[...]
    v = v_ref[...]
    grow = gr_ref[...]
    brow = br_ref[...]
    if t_valid is not None:
        okc = (n * r + lax.broadcasted_iota(jnp.int32, (r, 1), 0)) < t_valid
        okr = (n * r + lax.broadcasted_iota(jnp.int32, (1, r), 1)) < t_valid
        k = jnp.where(okc, k, 0.0)
        v = jnp.where(okc, v, 0.0)
        grow = jnp.where(okr, grow, 0.0)
        brow = jnp.where(okr, brow, 0.0)

    ri = lax.broadcasted_iota(jnp.int32, (r, r), 0)
    ci = lax.broadcasted_iota(jnp.int32, (r, r), 1)
    eye = jnp.where(ri == ci, 1.0, 0.0)
    lm = _chunk_tril(r)
    eyeb = eye.astype(BF16)
    g_all = _split3(jnp.concatenate([grow[qh] for qh in range(hpb)], axis=0))
    b_all = _split3(jnp.concatenate([brow[qh] for qh in range(hpb)], axis=0))
    gcum_col = sum(_dot_nt(lm, p) for p in g_all)
    gcum_row = sum(_dot_nt(p, lm) for p in g_all)
    bcol = sum(_dot_nt(eyeb, p) for p in b_all)
    tril = ((ri // CHUNK) == (ci // CHUNK)) & (ri >= ci)
    lvl_masks = []
    size = 1
    while size < CHUNK:
        lvl_masks.append(((ri // (2 * size)) == (ci // (2 * size)))
                         & ((ri // size) % 2 == 1) & ((ci // size) % 2 == 0))
        size *= 2
    if t_valid is not None and t_valid <= 1:
        lvl_masks = []

    nw = nw_ref[...]
    heads = [(qh, hh) for qh in range(hpb) for hh in range(2)]
    kq = [k[:, qh * dh:(qh + 1) * dh] for qh in range(hpb)]
    qq = [q[:, qh * dh:(qh + 1) * dh] for qh in range(hpb)]
    gram = [_dot_nt(kq[qh], kq[qh]) for qh in range(hpb)]
    qk = [_dot_nt(qq[qh], kq[qh]) for qh in range(hpb)]
    gc = [gcum_col[:, i:i + 1] for i in range(len(heads))]
    bc = [bcol[:, i:i + 1] for i in range(len(heads))]
    decay = [jnp.exp(jnp.where(tril, gc[i] - gcum_row[i:i + 1, :], NEG)) for i in range(len(heads))]
    mm_ = [(gram[qh] * bc[i] * decay[i]).astype(BF16) for i, (qh, hh) in enumerate(heads)]
    tinv = [eye.astype(BF16) for _ in heads]
    for lvl, msk in enumerate(lvl_masks):
        mskb = jnp.where(msk, 1.0, 0.0).astype(BF16)
        ml = [m * mskb for m in mm_]
        if lvl == 0:
            tinv = [t - m for t, m in zip(tinv, ml)]
        else:
            y = [_dot(m, t) for t, m in zip(tinv, ml)]
            tinv = [t - _dot(t, yy).astype(BF16) for t, yy in zip(tinv, y)]
    eg = [jnp.exp(g_) for g_ in gc]
    sol = [_dot(tinv[i], jnp.concatenate([v[:, (2 * qh + hh) * dh:(2 * qh + hh + 1) * dh] * bc[i],
                                          kq[qh] * (bc[i] * eg[i])], axis=1))
           for i, (qh, hh) in enumerate(heads)]
    asol = [_dot(qk[qh] * decay[i], sol[i]) for i, (qh, hh) in enumerate(heads)]
    o2 = [a[:, :dh] for a in asol]
    o1 = [qq[qh] * eg[i] - asol[i][:, dh:] for i, (qh, hh) in enumerate(heads)]
    glast = [[g_[(c + 1) * CHUNK - 1:(c + 1) * CHUNK, :] for c in range(cb)] for g_ in gc]
    kd = [kq[qh] * jnp.exp(jnp.concatenate([jnp.broadcast_to(gl, (CHUNK, 1)) for gl in glast[i]], axis=0) - gc[i])
          for i, (qh, hh) in enumerate(heads)]
    qw = [[_dot_tn(kd[i][c * CHUNK:(c + 1) * CHUNK], sol[i][c * CHUNK:(c + 1) * CHUNK]) for c in range(cb)]
          for i in range(len(heads))]
    s = [s_scr[2 * qh + hh] for qh, hh in heads]
    for c in range(cb):
        sl = slice(c * CHUNK, (c + 1) * CHUNK)
        for i, (qh, hh) in enumerate(heads):
            x = _dot(jnp.concatenate([qw[i][c][:, dh:], o1[i][sl]], axis=0), s[i])
            o = x[dh:] + o2[i][sl]
            s[i] = s[i] * jnp.exp(glast[i][c]) - x[:dh] + qw[i][c][:, :dh]
            col = slice((2 * qh + hh) * dh, (2 * qh + hh + 1) * dh)
            zc = z_ref[sl, col]
            o_ref[sl, col] = (_rms(o, nw) * _silu(zc)).astype(o_ref.dtype)
    for i, (qh, hh) in enumerate(heads):
        s_scr[2 * qh + hh] = s[i]

    @pl.when(n == n_blocks - 1)
    def _():
        so_ref[...] = s_scr[...]


def _gdn_delta(qkv3, proj3, grow, brow, s0, norm_w, out_dtype, cb, hpb, t_valid):
    b, t, _ = qkv3.shape
    r = cb * CHUNK
    assert t % r == 0
    nb = t // r
    dh = GDN_HEAD_DIM
    qw_, vw_ = hpb * dh, 2 * hpb * dh
    assert GDN_QK_HEADS % hpb == 0
    koff = GDN_QK_DIM // qw_
    voff = 2 * GDN_QK_DIM // vw_
    zoff = (proj3.shape[-1] - GDN_V_DIM) // vw_
    return pl.pallas_call(
        functools.partial(_gdn_delta_kernel, cb=cb, hpb=hpb, t_valid=t_valid, n_blocks=nb),
        grid=(b, GDN_QK_HEADS // hpb, nb),
        in_specs=[pl.BlockSpec((None, r, qw_), lambda bb, h, n: (bb, n, h)),
                  pl.BlockSpec((None, r, qw_), lambda bb, h, n: (bb, n, koff + h)),
                  pl.BlockSpec((None, r, vw_), lambda bb, h, n: (bb, n, voff + h)),
                  pl.BlockSpec((None, r, vw_), lambda bb, h, n: (bb, n, zoff + h)),
                  pl.BlockSpec((None, hpb, 2, r), lambda bb, h, n: (bb, h, 0, n)),
                  pl.BlockSpec((None, hpb, 2, r), lambda bb, h, n: (bb, h, 0, n)),
                  pl.BlockSpec((None, 2 * hpb, dh, dh), lambda bb, h, n: (bb, h, 0, 0)),
                  pl.BlockSpec((1, dh), lambda bb, h, n: (0, 0))],
        out_specs=[pl.BlockSpec((None, r, vw_), lambda bb, h, n: (bb, n, h)),
                   pl.BlockSpec((None, 2 * hpb, dh, dh), lambda bb, h, n: (bb, h, 0, 0))],
        out_shape=[jax.ShapeDtypeStruct((b, t, GDN_V_DIM), out_dtype),
                   jax.ShapeDtypeStruct(s0.shape, F32)],
        scratch_shapes=[pltpu.VMEM((2 * hpb, dh, dh), F32)],
        compiler_params=_params("parallel", "parallel", "arbitrary"), name="gdn_delta")(
            qkv3, qkv3, qkv3, proj3, grow, brow, s0, norm_w)


def _gla_kernel(q_ref, k_ref, v_ref, gate_ref, gk_ref, bgk_ref, s0_ref, nw_ref,
                o_ref, so_ref, s_scr, *, cb, t_valid, n_blocks):
    n = pl.program_id(2)
    r = cb * CHUNK
    dk, dv = GLA_DK, GLA_DV

    @pl.when(n == 0)
    def _():
        s_scr[...] = s0_ref[...]

    xg = gk_ref[...] + bgk_ref[...]
    g = (jnp.minimum(xg, 0.0) - jnp.log(1.0 + jnp.exp(-jnp.abs(xg)))) * (1.0 / GLA_GATE_NORMALIZER)
    q = q_ref[...] * GLA_DK ** -0.5
    k = k_ref[...]
    v = v_ref[...]
    if t_valid is not None:
        okc = (n * r + lax.broadcasted_iota(jnp.int32, (r, 1), 0)) < t_valid
        g = jnp.where(okc, g, 0.0)
        k = jnp.where(okc, k, 0.0)
        v = jnp.where(okc, v, 0.0)

    gparts = _split3(g)
    bcum = sum(_dot(_chunk_tril(r), p) for p in gparts)
    ones = jnp.ones((CHUNK, LANES), BF16)
    row = lax.broadcasted_iota(jnp.int32, (CHUNK, 1), 0)
    jrow = lax.broadcasted_iota(jnp.int32, (SUB, 1), 0)
    lane = lax.broadcasted_iota(jnp.int32, (SUB, CHUNK), 1)
    nw = nw_ref[...]
    s = s_scr[...]
    for c in range(cb):
        sl = slice(c * CHUNK, (c + 1) * CHUNK)
        qc, kc, vc, bc = q[sl], k[sl], v[sl], bcum[sl]
        blast = bc[CHUNK - 1:CHUNK, :]
        attn_t = jnp.zeros((CHUNK, CHUNK), F32)
        for bi in range(1, CHUNK // SUB):
            bref = bc[bi * SUB:bi * SUB + 1, :]
            k_i = jnp.where(row < bi * SUB, kc * jnp.exp(jnp.minimum(bref - bc, 0.0)), 0.0)
            q_i = jnp.where((row >= bi * SUB) & (row < (bi + 1) * SUB),
                            qc * jnp.exp(jnp.minimum(bc - bref, 0.0)), 0.0)
            attn_t = attn_t + _dot_nt(k_i, q_i)
        diag = []
        for bi in range(CHUNK // SUB):
            sb = slice(bi * SUB, (bi + 1) * SUB)
            qb, kb, bb = qc[sb], kc[sb], bc[sb]
            d = jnp.zeros((SUB, CHUNK), F32)
            for il in range(SUB):
                e = jnp.exp(jnp.minimum(bb[il:il + 1, :] - bb, 0.0))
                col = jnp.sum(kb * e * qb[il:il + 1, :], axis=-1, keepdims=True)
                col = jnp.where(jrow <= il, col, 0.0)
                d = jnp.where(lane == bi * SUB + il, col, d)
            diag.append(d)
        attn_t = attn_t + jnp.concatenate(diag, axis=0)
        o = _dot(qc * jnp.exp(bc), s) + _dot_tn(attn_t, vc)
        bl_col = sum(_dot_tn(p[sl], ones) for p in gparts)
        decay_col = jnp.concatenate([jnp.exp(bl_col)] * (dv // LANES), axis=1)
        s = s * decay_col + _dot_tn(kc * jnp.exp(blast - bc), vc)
        gt = gate_ref[sl, :]
        o_ref[sl, :] = (_rms(o, nw) * _silu(gt)).astype(o_ref.dtype)
    s_scr[...] = s

    @pl.when(n == n_blocks - 1)
    def _():
        so_ref[...] = s_scr[...]


def _gla_chunks(qkvg3, gk3, b_gk, s0, norm_w, out_dtype, cb, t_valid):
    b, t, _ = qkvg3.shape
    r = cb * CHUNK
    assert t % r == 0
    nb = t // r
    dk, dv = GLA_DK, GLA_DV
    koff = GLA_HEADS
    voff = 2 * GLA_HEADS * dk // dv
    goff = voff + GLA_HEADS
    return pl.pallas_call(
        functools.partial(_gla_kernel, cb=cb, t_valid=t_valid, n_blocks=nb),
        grid=(b, GLA_HEADS, nb),
        in_specs=[pl.BlockSpec((None, r, dk), lambda bb, h, n: (bb, n, h)),
                  pl.BlockSpec((None, r, dk), lambda bb, h, n: (bb, n, koff + h)),
                  pl.BlockSpec((None, r, dv), lambda bb, h, n: (bb, n, voff + h)),
                  pl.BlockSpec((None, r, dv), lambda bb, h, n: (bb, n, goff + h)),
                  pl.BlockSpec((None, r, dk), lambda bb, h, n: (bb, n, h)),
                  pl.BlockSpec((1, dk), lambda bb, h, n: (0, h)),
                  pl.BlockSpec((None, None, dk, dv), lambda bb, h, n: (bb, h, 0, 0)),
                  pl.BlockSpec((1, dv), lambda bb, h, n: (0, 0))],
        out_specs=[pl.BlockSpec((None, r, dv), lambda bb, h, n: (bb, n, h)),
                   pl.BlockSpec((None, None, dk, dv), lambda bb, h, n: (bb, h, 0, 0))],
        out_shape=[jax.ShapeDtypeStruct((b, t, GLA_HEADS * dv), out_dtype),
                   jax.ShapeDtypeStruct(s0.shape, F32)],
        scratch_shapes=[pltpu.VMEM((dk, dv), F32)],
        compiler_params=_params("parallel", "parallel", "arbitrary"), name="gla_chunks")(
            qkvg3, qkvg3, qkvg3, qkvg3, gk3, b_gk, s0, norm_w)


def _t5_bucket(n):
    n = np.asarray(n)
    max_exact = N_BUCKETS // 2
    nf = np.maximum(n, max_exact).astype(np.float32)
    large = max_exact + (np.log(nf / max_exact) / math.log(MAX_DISTANCE / max_exact)
                         * (N_BUCKETS - max_exact)).astype(np.int32)
    return np.where(n < max_exact, n, np.minimum(large, N_BUCKETS - 1)).astype(np.int32)


def _lambda(lam_ref, lam_init):
    lf = lam_ref[...]
    s1 = jnp.sum(lf[0:1] * lf[1:2], axis=-1, keepdims=True)
    s2 = jnp.sum(lf[2:3] * lf[3:4], axis=-1, keepdims=True)
    return jnp.exp(s1) - jnp.exp(s2) + lam_init


def _flash_kernel(qi_ref, ki_ref, q_ref, k_ref, v_ref, bd_ref, bs_ref, rb_ref, lam_ref, sub_ref, o_ref,
                  q_scr, m_scr, l_scr, acc_scr, bias_scr, *, lam_init):
    h = pl.program_id(0)
    step = pl.program_id(2)
    qi = qi_ref[step]
    ki = ki_ref[step]
    tb = q_ref.shape[0]
    dh = DIFF_HEAD_DIM

    @pl.when((pl.program_id(1) == 0) & (step == 0))
    def _():
        ri = lax.broadcasted_iota(jnp.int32, (tb, tb), 0)
        ci = lax.broadcasted_iota(jnp.int32, (tb, tb), 1)
        for mm in range(2):
            bd = jnp.zeros((tb, tb), F32)
            bs = jnp.zeros((tb, tb), F32)
            for bk in range(N_BUCKETS):
                val = rb_ref[bk, 2 * h + mm] * LOG2E
                bd = jnp.where(bd_ref[...] == bk, val, bd)
                bs = jnp.where(bs_ref[...] == bk, val, bs)
            bias_scr[mm, 0] = jnp.where(ri > ci, NEG, bd)
            bias_scr[mm, 1] = bs

    @pl.when(ki == 0)
    def _():
        q_scr[...] = (q_ref[...] * (dh ** -0.5 * LOG2E)).astype(q_scr.dtype)
        m_scr[...] = jnp.full_like(m_scr, NEG)
        l_scr[...] = jnp.zeros_like(l_scr)
        acc_scr[...] = jnp.zeros_like(acc_scr)

    def update(mm, s, shift, vt):
        m_prev = m_scr[mm]
        if shift.ndim == 0:
            m_new = jnp.maximum(m_prev, jnp.max(s, axis=0, keepdims=True) + shift)
            p = jnp.exp2(s - (m_new - shift))
        else:
            s = s + shift
            m_new = jnp.maximum(m_prev, jnp.max(s, axis=0, keepdims=True))
            p = jnp.exp2(s - m_new)
        alpha = jnp.exp2(m_prev - m_new)
        l_scr[mm] = alpha * l_scr[mm] + jnp.sum(p, axis=0, keepdims=True)
        acc_scr[mm] = alpha * acc_scr[mm] + _dot(vt, p)
        m_scr[mm] = m_new

    def both_maps(shift_of):
        rows = pl.ds(pl.multiple_of(ki * tb, tb), tb)
        vt = v_ref[rows, :].T.astype(BF16)
        s = [_dot_nt(k_ref[rows, mm * dh:(mm + 1) * dh], q_scr[:, mm * dh:(mm + 1) * dh]) for mm in range(2)]
        for mm in range(2):
            update(mm, s[mm], shift_of(mm), vt)

    @pl.when(ki == qi)
    def _():
        both_maps(lambda mm: bias_scr[mm, 0])

    @pl.when(ki == qi - 1)
    def _():
        both_maps(lambda mm: bias_scr[mm, 1])

    @pl.when(ki < qi - 1)
    def _():
        both_maps(lambda mm: rb_ref[N_BUCKETS - 1, 2 * h + mm] * LOG2E)

    @pl.when(ki == qi)
    def _():
        lam = _lambda(lam_ref, lam_init)
        out = acc_scr[0] / l_scr[0] - lam * (acc_scr[1] / l_scr[1])
        out = out * lax.rsqrt(jnp.mean(out * out, axis=0, keepdims=True) + NORM_EPS)
        o_ref[...] = (out * (sub_ref[...] * (1.0 - lam_init))).T.astype(o_ref.dtype)


def _flash_diff(qkv3, rel_bias, lambdas, subln_col, lam_init, out_dtype, tb):
    b, t, _ = qkv3.shape
    tb = min(tb, t)
    assert t % tb == 0 and tb >= MAX_DISTANCE
    nq = t // tb
    dd = 2 * DIFF_HEAD_DIM
    j = np.arange(tb)[:, None]
    i = np.arange(tb)[None, :]
    bkt_diag = jnp.asarray(_t5_bucket(np.maximum(i - j, 0)))
    bkt_sub = jnp.asarray(_t5_bucket(tb + i - j))
    koff = D_MODEL // dd
    pairs = [(qi, ki) for qi in range(nq) for ki in range(qi + 1)]
    qi_tbl = jnp.asarray([pq for pq, _ in pairs], jnp.int32)
    ki_tbl = jnp.asarray([pk for _, pk in pairs], jnp.int32)
    grid_spec = pltpu.PrefetchScalarGridSpec(
        num_scalar_prefetch=2, grid=(DIFF_HEADS, b, len(pairs)),
        in_specs=[pl.BlockSpec((None, tb, dd), lambda h, bb, s, qt, kt: (bb, qt[s], h)),
                  pl.BlockSpec((None, t, dd), lambda h, bb, s, qt, kt: (bb, 0, koff + h)),
                  pl.BlockSpec((None, t, dd), lambda h, bb, s, qt, kt: (bb, 0, 2 * koff + h)),
                  pl.BlockSpec((tb, tb), lambda h, bb, s, qt, kt: (0, 0)),
                  pl.BlockSpec((tb, tb), lambda h, bb, s, qt, kt: (0, 0)),
                  pl.BlockSpec(memory_space=pltpu.SMEM),
                  pl.BlockSpec((4, DIFF_HEAD_DIM), lambda h, bb, s, qt, kt: (0, 0)),
                  pl.BlockSpec((dd, 1), lambda h, bb, s, qt, kt: (0, 0))],
        out_specs=pl.BlockSpec((None, tb, dd), lambda h, bb, s, qt, kt: (bb, qt[s], h)),
        scratch_shapes=[pltpu.VMEM((tb, dd), BF16),
                        pltpu.VMEM((2, 1, tb), F32), pltpu.VMEM((2, 1, tb), F32),
                        pltpu.VMEM((2, dd, tb), F32), pltpu.VMEM((2, 2, tb, tb), F32)])
    return pl.pallas_call(
        functools.partial(_flash_kernel, lam_init=lam_init), grid_spec=grid_spec,
        out_shape=jax.ShapeDtypeStruct((b, t, D_MODEL), out_dtype),
        compiler_params=_params("parallel", "arbitrary", "arbitrary"), name="flash_diff")(
            qi_tbl, ki_tbl, qkv3, qkv3, qkv3, bkt_diag, bkt_sub, rel_bias, lambdas, subln_col)


def _decode_kernel(pt_ref, q_ref, *refs, pp, n_steps, lam_init):
    kp_refs, vp_refs = refs[:pp], refs[pp:2 * pp]
    kn_ref, vn_ref, bkt_ref, tbl_ref, lam_ref, sub_ref, o_ref, m_scr, l_scr, acc_scr, bias_scr, p_scr, a_scr = refs[2 * pp:]
    step = pl.program_id(1)
    nh, dh = DIFF_HEADS, DIFF_HEAD_DIM
    ones = jnp.ones((dh, LANES), BF16)
    qs = q_ref[...] * (dh ** -0.5 * LOG2E)

    @pl.when(step == 0)
    def _():
        m_scr[...] = jnp.full_like(m_scr, NEG)
        l_scr[...] = jnp.zeros_like(l_scr)
        acc_scr[...] = jnp.zeros_like(acc_scr)
        bkt = bkt_ref[...]
        bias = jnp.zeros(bkt.shape, F32)
        for bk in range(N_BUCKETS):
            bias = jnp.where(bkt == bk, (tbl_ref[bk] * LOG2E)[None], bias)
        bias_scr[...] = bias

    def lane_sum(x):
        return _dot(x, ones)

    def accumulate(pr, shift_row, values):
        m_prev = m_scr[...]
        if pr.ndim == 3:
            m_new = jnp.maximum(m_prev, jnp.max(pr, axis=0) + shift_row)
            p = jnp.exp2(pr - (m_new - shift_row)[None])
            psum = jnp.sum(p, axis=0)
        else:
            m_new = jnp.maximum(m_prev, pr + shift_row)
            p = jnp.exp2(pr - (m_new - shift_row))
            psum = p
        alpha = jnp.exp2(m_prev - m_new)
        l_scr[...] = alpha * l_scr[...] + psum
        m_scr[...] = m_new
        a_scr[...] = alpha
        if pr.ndim == 3:
            p_scr[...] = p
        else:
            p_scr[0] = p
        for par in range(2):
            ae = a_scr[pl.ds(par, nh, stride=2), :]
            if pr.ndim == 3:
                pe = p_scr[:, pl.ds(par, nh, stride=2), :]
                pv = jnp.sum(jnp.concatenate([pe, pe], axis=-1) * values(), axis=0)
            else:
                pe = p_scr[0, pl.ds(par, nh, stride=2), :]
                pv = jnp.concatenate([pe, pe], axis=-1) * values()
            acc_scr[par] = jnp.concatenate([ae, ae], axis=-1) * acc_scr[par] + pv

    def page_logits(j):
        prod = kp_refs[j][...] * qs[None]
        return lane_sum(prod.reshape(PAGE_SIZE * 2 * nh, dh)).reshape(PAGE_SIZE, 2 * nh, LANES)

    far = tbl_ref[N_BUCKETS - 1] * LOG2E
    zero = jnp.zeros_like(far)
    for j in range(pp - 1):
        accumulate(page_logits(j), far, lambda j=j: vp_refs[j][...])

    @pl.when(step < n_steps - 1)
    def _():
        accumulate(page_logits(pp - 1), far, lambda: vp_refs[pp - 1][...])

    @pl.when(step == n_steps - 1)
    def _():
        accumulate(page_logits(pp - 1) + bias_scr[...], zero, lambda: vp_refs[pp - 1][...])
        accumulate(lane_sum(kn_ref[...] * qs) + tbl_ref[0] * LOG2E, zero, lambda: vn_ref[...])
        l_scr_v = l_scr[...]
        a_scr[...] = l_scr_v
        outs = []
        for par in range(2):
            le = a_scr[pl.ds(par, nh, stride=2), :]
            outs.append(acc_scr[par] / jnp.concatenate([le, le], axis=-1))
        out = outs[0] - _lambda(lam_ref, lam_init) * outs[1]
        o_ref[...] = _rms(out, sub_ref[...]) * (1.0 - lam_init)


def _decode_diff(q, k_new, v_new, cache_k, cache_v, layer, page_table, rel_bias, lambdas, subln, lam_init, pp):
    b, n_pages = page_table.shape
    nh, dh = DIFF_HEADS, DIFF_HEAD_DIM
    pp = min(pp, n_pages)
    assert n_pages % pp == 0 and PAGE_SIZE >= MAX_DISTANCE and dh == LANES
    n_steps = n_pages // pp
    bkt_last = jnp.asarray(np.broadcast_to(
        _t5_bucket(PAGE_SIZE - np.arange(PAGE_SIZE)).reshape(PAGE_SIZE, 1, 1), (PAGE_SIZE, 2 * nh, LANES)))
    tbl = jnp.broadcast_to(rel_bias[:, :, None], (N_BUCKETS, 2 * nh, LANES))
    page_spec = lambda j, width, lanes: pl.BlockSpec(
        (None, None, PAGE_SIZE, width, lanes), lambda bb, s, pt: (layer, pt[bb, s * pp + j], 0, 0, 0))
    full = lambda shape: pl.BlockSpec(shape, lambda bb, s, pt: (0,) * len(shape))
    per_seq = lambda shape: pl.BlockSpec((None,) + shape, lambda bb, s, pt: (bb,) + (0,) * len(shape))
    grid_spec = pltpu.PrefetchScalarGridSpec(
        num_scalar_prefetch=1, grid=(b, n_steps),
        in_specs=([per_seq((2 * nh, dh))]
                  + [page_spec(j, 2 * nh, dh) for j in range(pp)]
                  + [page_spec(j, nh, 2 * dh) for j in range(pp)]
                  + [per_seq((2 * nh, dh)), per_seq((nh, 2 * dh)),
                     full((PAGE_SIZE, 2 * nh, LANES)), full((N_BUCKETS, 2 * nh, LANES)),
                     full((4, dh)), full((1, 2 * dh))]),
        out_specs=per_seq((nh, 2 * dh)),
        scratch_shapes=[pltpu.VMEM((2 * nh, LANES), F32), pltpu.VMEM((2 * nh, LANES), F32),
                        pltpu.VMEM((2, nh, 2 * dh), F32), pltpu.VMEM((PAGE_SIZE, 2 * nh, LANES), F32),
                        pltpu.VMEM((PAGE_SIZE, 2 * nh, LANES), F32), pltpu.VMEM((2 * nh, LANES), F32)])
    return pl.pallas_call(
        functools.partial(_decode_kernel, pp=pp, n_steps=n_steps, lam_init=lam_init),
        grid_spec=grid_spec,
        out_shape=jax.ShapeDtypeStruct((b, nh, 2 * dh), F32),
        compiler_params=_params("parallel", "arbitrary"), name="decode_diff")(
            page_table, q, *([cache_k] * pp), *([cache_v] * pp), k_new, v_new, bkt_last, tbl, lambdas, subln)


class _Tiles:
    def __init__(self, prompt):
        self.prompt = prompt
        self.act_dtype = BF16 if prompt else F32
        self.tm = 1024 if prompt else SUBLANES
        self.tn = 1024
        self.tn_ff = 512
        self.tm_ff = 1024
        self.tm_res = 256 if prompt else SUBLANES
        self.tk_res = 512
        self.tt_prep = 256 if prompt else SUBLANES
        self.tc_prep = 1024
        self.cb = 4 if prompt else 1
        self.gdn_hpb = 4
        self.tb_attn = 512
        self.decode_pages = 8


def _group_trunk(x3, tiles, gdn_s0, gdn_c0, gla_s0, cache_k, cache_v, page_table, p, wb):
    b, t, d = x3.shape
    m = b * t
    x = x3.reshape(m, d)
    if not tiles.prompt:
        assert t == 1 and m % SUBLANES == 0
    t_pad = t if tiles.prompt else CHUNK
    t_valid = None if tiles.prompt else t
    nw = lambda i, jn: p['norm_w'][i, jn].reshape(1, d)
    adt = tiles.act_dtype

    def pad_time(a3, to):
        return a3 if a3.shape[1] == to else jnp.pad(a3, ((0, 0), (0, to - a3.shape[1]), (0, 0)))

    def big_matmul(a, name, layer):
        if tiles.prompt:
            return _matmul(a, wb[name, layer], layer, F32, tiles.tm, tiles.tn)
        out, wb[name, layer] = _matmul(a, p[name], layer, F32, tiles.tm, tiles.tn, emit_wb=True)
        return out

    def residual_matmul(a, name, layer, x_in, nw_post, nw_next):
        if tiles.prompt:
            return _matmul_residual(a, wb[name, layer], layer, x_in, nw_post, nw_next, adt,
                                    tiles.tm_res, tiles.tk_res)
        xo, h_next, wb[name, layer] = _matmul_residual(a, p[name], layer, x_in, nw_post, nw_next, adt,
                                                       tiles.tm_res, tiles.tk_res, emit_wb=True)
        return xo, h_next

    def ffn_up(a, layer):
        if tiles.prompt:
            return _swiglu_up(a, wb['ffn_gate', layer], wb['ffn_up', layer], layer, adt, tiles.tm_ff, tiles.tn_ff)
        act_, wb['ffn_gate', layer], wb['ffn_up', layer] = _swiglu_up(
            a, p['ffn_w_up'], p['ffn_w_up'], layer, adt, tiles.tm, tiles.tn_ff, emit_wb=True)
        return act_

    gdn_s, gdn_c, gla_s, att_k, att_v = [], [], [], [], []
    h = _rmsnorm(x, nw(0, 0), adt, tiles.tm)
    for i in range(DEPTH):
        j = i // N_MIXERS
        kind = i % N_MIXERS
        if kind == 0:
            beta, g = _gdn_gates(h, p['gdn_w_ba'], j, p['gdn_a_log'][j].reshape(-1, 1),
                                 p['gdn_dt_bias'][j].reshape(-1, 1), tiles.tm)
            buf0 = gdn_c0[j]
            nbuf = GDN_CONV_W - 1
            buf8 = jnp.pad(buf0, ((0, 0), (SUBLANES - nbuf, 0), (0, 0)))
            if tiles.prompt:
                proj, tail = _gdn_in_proj(h, wb['gdn_w_in', j], buf8, p['gdn_conv_w'][j], t, tiles.tm, tiles.tn)
                qkv3 = proj.reshape(b, t, -1)
                proj3p = _matmul(h, wb['gdn_w_in', j], j, F32, tiles.tm, tiles.tn,
                                 col_start=GDN_CONV_DIM, n_out=GDN_V_DIM).reshape(b, t, -1)
                new_buf = tail[:, SUBLANES - nbuf:, :]
            else:
                proj3 = big_matmul(h, 'gdn_w_in', j).reshape(b, t, -1)
                new_buf = jnp.concatenate([buf0[:, t:], proj3[..., :GDN_CONV_DIM]], axis=1)
                proj3p = pad_time(proj3, SUBLANES)
                qkv3 = pad_time(_gdn_prep(proj3p, buf8, p['gdn_conv_w'][j], tiles.tt_prep, tiles.tc_prep), t_pad)
                proj3p = pad_time(proj3p, t_pad)

            def heads(a):
                a = a.reshape(GDN_QK_HEADS, 2, b, t).transpose(2, 0, 1, 3)
                return a if t == t_pad else jnp.pad(a, ((0, 0), (0, 0), (0, 0), (0, t_pad - t)))
            o3, s_new = _gdn_delta(qkv3, proj3p, heads(g), heads(beta), gdn_s0[j],
                                   p['gdn_norm_w'][j].reshape(1, -1), adt, tiles.cb, tiles.gdn_hpb, t_valid)
            mix_in, w_out = o3[:, :t].reshape(m, -1), 'gdn_w_out'
            gdn_s.append(s_new)
            gdn_c.append(new_buf)
        elif kind == 1:
            qkvg = big_matmul(h, 'gla_w_qkvg', j)
            low = _matmul(h, p['gla_w_gk1'], j, F32, tiles.tm, tiles.tn)
            gk = _matmul(low, p['gla_w_gk2'], j, F32, tiles.tm, tiles.tn)
            o3, s_new = _gla_chunks(pad_time(qkvg.reshape(b, t, -1), t_pad), pad_time(gk.reshape(b, t, -1), t_pad),
                                    p['gla_b_gk'][j].reshape(1, -1), gla_s0[j],
                                    p['gla_norm_w'][j].reshape(1, -1), adt, tiles.cb, t_valid)
            mix_in, w_out = o3[:, :t].reshape(m, -1), 'gla_w_out'
            gla_s.append(s_new)
        else:
            lam_init = 0.8 - 0.6 * math.exp(-0.3 * i)
            qkv = big_matmul(h, 'diff_w_qkv', j)
            nh, dh = DIFF_HEADS, DIFF_HEAD_DIM
            k_new = qkv[:, D_MODEL:2 * D_MODEL].reshape(b, t, 2 * nh, dh)
            v_new = qkv[:, 2 * D_MODEL:].reshape(b, t, nh, 2 * dh)
            subln = p['diff_subln'][j].reshape(1, -1)
            if tiles.prompt:
                o3 = _flash_diff(qkv.reshape(b, t, -1), p['rel_bias'], p['diff_lambda'][j],
                                 subln.reshape(-1, 1), lam_init, adt, tiles.tb_attn)
                mix_in = o3.reshape(m, -1)
            else:
                o = _decode_diff(qkv[:, :D_MODEL].reshape(b, 2 * nh, dh), k_new.reshape(b, 2 * nh, dh),
                                 v_new.reshape(b, nh, 2 * dh), cache_k, cache_v, j, page_table,
                                 p['rel_bias'], p['diff_lambda'][j], subln, lam_init, tiles.decode_pages)
                mix_in = o.reshape(m, -1)
            w_out = 'diff_w_out'
            att_k.append(k_new)
            att_v.append(v_new)
        x, h = residual_matmul(mix_in, w_out, j, x, nw(i, 1), nw(i, 2))
        act = ffn_up(h, i)
        nxt = nw(i + 1, 0) if i + 1 < DEPTH else None
        x, h = residual_matmul(act, 'ffn_w_down', i, x, nw(i, 3), nxt)
    return (x.reshape(b, t, d), jnp.stack(gdn_s), jnp.stack(gdn_c), jnp.stack(gla_s),
            jnp.stack(att_k), jnp.stack(att_v))


def kernel(x_prompt, x_sample, state_gdn, state_gdn_conv, state_gla, cache_k, cache_v, page_table,
           norm_w, ffn_w_up, ffn_w_down, rel_bias,
           gdn_w_in, gdn_w_ba, gdn_conv_w, gdn_a_log, gdn_dt_bias, gdn_norm_w, gdn_w_out,
           gla_w_qkvg, gla_w_gk1, gla_w_gk2, gla_b_gk, gla_norm_w, gla_w_out,
           diff_w_qkv, diff_lambda, diff_subln, diff_w_out):
    p = dict(norm_w=norm_w, ffn_w_up=ffn_w_up, ffn_w_down=ffn_w_down, rel_bias=rel_bias,
             gdn_w_in=gdn_w_in, gdn_w_ba=gdn_w_ba, gdn_conv_w=gdn_conv_w, gdn_a_log=gdn_a_log,
             gdn_dt_bias=gdn_dt_bias, gdn_norm_w=gdn_norm_w, gdn_w_out=gdn_w_out,
             gla_w_qkvg=gla_w_qkvg, gla_w_gk1=gla_w_gk1, gla_w_gk2=gla_w_gk2, gla_b_gk=gla_b_gk,
             gla_norm_w=gla_norm_w, gla_w_out=gla_w_out,
             diff_w_qkv=diff_w_qkv, diff_lambda=diff_lambda, diff_subln=diff_subln, diff_w_out=diff_w_out)
    bp = x_prompt.shape[0]
    n_gdn, n_gla = state_gdn.shape[0], state_gla.shape[0]
    zeros_gdn = jnp.zeros((n_gdn, bp) + state_gdn.shape[2:], F32)
    zeros_conv = jnp.zeros((n_gdn, bp) + state_gdn_conv.shape[2:], F32)
    zeros_gla = jnp.zeros((n_gla, bp) + state_gla.shape[2:], F32)
    wb = {}
    outs_s = _group_trunk(x_sample, _Tiles(False), state_gdn, state_gdn_conv, state_gla,
                          cache_k, cache_v, page_table, p, wb)
    outs_p = _group_trunk(x_prompt, _Tiles(True), zeros_gdn, zeros_conv, zeros_gla, None, None, None, p, wb)
    return (outs_p[0], outs_s[0]) + outs_p[1:] + outs_s[1:]
```

```python
import functools
import math

import numpy as np
import jax
import jax.numpy as jnp
from jax import lax
from jax.experimental import pallas as pl
from jax.experimental.pallas import tpu as pltpu

F32 = jnp.float32
BF16 = jnp.bfloat16

D_MODEL = 2048
DEPTH = 4
PAGE_SIZE = 128
N_MIXERS = 3
GDN_HEAD_DIM = 128
GDN_QK_HEADS = D_MODEL // 128
GDN_V_HEADS = 2 * GDN_QK_HEADS
GDN_QK_DIM = GDN_QK_HEADS * GDN_HEAD_DIM
GDN_V_DIM = GDN_V_HEADS * GDN_HEAD_DIM
GDN_CONV_DIM = 2 * GDN_QK_DIM + GDN_V_DIM
GDN_CONV_W = 4
GLA_HEADS = 4
GLA_DK = D_MODEL // 2 // GLA_HEADS
GLA_DV = D_MODEL // GLA_HEADS
GLA_GATE_NORMALIZER = 16.0
DIFF_HEAD_DIM = 128
DIFF_HEADS = D_MODEL // (2 * DIFF_HEAD_DIM)
N_BUCKETS = 32
MAX_DISTANCE = 128
D_FF = ((8 * D_MODEL + 3 * 256 - 1) // (3 * 256)) * 256
NORM_EPS = 1e-6

LANES = 128
SUBLANES = 8
CHUNK = 64
SUB = 16
GDN_PROJ_SUBBLOCKS = 4
VMEM_LIMIT_BYTES = 48 * 1024 * 1024
VMEM_LIMIT_BIG_BYTES = 56 * 1024 * 1024
NEG = -1e30
LOG2E = math.log2(math.e)


def _params(*sem, vmem=VMEM_LIMIT_BYTES):
    return pltpu.CompilerParams(dimension_semantics=sem, vmem_limit_bytes=vmem)


def _dot(a, b):
    return jnp.dot(a.astype(BF16), b.astype(BF16), preferred_element_type=F32)


def _dot_nt(a, b):
    return lax.dot_general(a.astype(BF16), b.astype(BF16), (((1,), (1,)), ((), ())),
                           preferred_element_type=F32)


def _dot_tn(a, b):
    return lax.dot_general(a.astype(BF16), b.astype(BF16), (((0,), (0,)), ((), ())),
                           preferred_element_type=F32)


def _split3(x):
    hi = x.astype(BF16)
    r = x - hi.astype(F32)
    mid = r.astype(BF16)
    lo = (r - mid.astype(F32)).astype(BF16)
    return hi, mid, lo


def _sigmoid(x):
    return 0.5 * jnp.tanh(0.5 * x) + 0.5


def _silu(x):
    h = 0.5 * x
    return h + h * jnp.tanh(h)


def _softplus(x):
    return jnp.maximum(x, 0.0) + jnp.log(1.0 + jnp.exp(-jnp.abs(x)))


def _rms(x, w):
    return x * lax.rsqrt(jnp.mean(x * x, axis=-1, keepdims=True) + NORM_EPS) * w


def _chunk_tril(r):
    i = lax.broadcasted_iota(jnp.int32, (r, r), 0)
    j = lax.broadcasted_iota(jnp.int32, (r, r), 1)
    return jnp.where((j <= i) & ((i // CHUNK) == (j // CHUNK)), 1.0, 0.0).astype(BF16)


def _rmsnorm_kernel(x_ref, w_ref, o_ref):
    o_ref[...] = _rms(x_ref[...], w_ref[...]).astype(o_ref.dtype)


def _rmsnorm(x, w, out_dtype, tm):
    m, d = x.shape
    tm = min(tm, m)
    return pl.pallas_call(
        _rmsnorm_kernel, grid=(m // tm,),
        in_specs=[pl.BlockSpec((tm, d), lambda i: (i, 0)), pl.BlockSpec((1, d), lambda i: (0, 0))],
        out_specs=pl.BlockSpec((tm, d), lambda i: (i, 0)),
        out_shape=jax.ShapeDtypeStruct((m, d), out_dtype),
        compiler_params=_params("parallel"), name="rmsnorm")(x, w)


def _weight_spec(w, layer, block, index):
    if w.ndim == 2:
        return pl.BlockSpec(block, index)
    return pl.BlockSpec((None,) + block, lambda *g: (layer,) + index(*g))


def _mm_kernel(a_ref, w_ref, o_ref, *wb_ref):
    w = w_ref[...].astype(BF16)
    if wb_ref:
        wb_ref[0][...] = w
    o_ref[...] = _dot(a_ref[...], w).astype(o_ref.dtype)


def _matmul(a, w, layer, out_dtype, tm, tn, emit_wb=False, col_start=0, n_out=None):
    m, k = a.shape
    n = w.shape[-1] if n_out is None else n_out
    tm, tn = min(tm, m), min(tn, n)
    assert m % tm == 0 and n % tn == 0 and col_start % tn == 0 and (not emit_wb or (m == tm and n_out is None))
    c0 = col_start // tn
    out_shape = [jax.ShapeDtypeStruct((m, n), out_dtype)]
    out_specs = [pl.BlockSpec((tm, tn), lambda i, j: (i, j))]
    if emit_wb:
        out_shape.append(jax.ShapeDtypeStruct((k, n), BF16))
        out_specs.append(pl.BlockSpec((k, tn), lambda i, j: (0, j)))
    res = pl.pallas_call(
        _mm_kernel, grid=(m // tm, n // tn),
        in_specs=[pl.BlockSpec((tm, k), lambda i, j: (i, 0)),
                  _weight_spec(w, layer, (k, tn), lambda i, j: (0, j + c0))],
        out_specs=out_specs, out_shape=out_shape,
        compiler_params=_params("parallel", "parallel"), name="matmul")(a, w)
    return (res[0], res[1]) if emit_wb else res[0]


def _swiglu_kernel(a_ref, wg_ref, wu_ref, o_ref, *wb_refs):
    a = a_ref[...].astype(BF16)
    wg = wg_ref[...].astype(BF16)
    wu = wu_ref[...].astype(BF16)
    if wb_refs:
        wb_refs[0][...] = wg
        wb_refs[1][...] = wu
    g = _dot(a, wg)
    u = _dot(a, wu)
    o_ref[...] = (_silu(g) * u).astype(o_ref.dtype)


def _swiglu_up(a, w_gate, w_up, layer, out_dtype, tm, tn, emit_wb=False):
    m, k = a.shape
    stacked = w_up.ndim == 3
    f = w_up.shape[-1] // 2 if stacked else w_up.shape[-1]
    tm, tn = min(tm, m), min(tn, f)
    assert m % tm == 0 and f % tn == 0 and (not emit_wb or m == tm)
    nf = f // tn
    up_off = nf if stacked else 0
    out_shape = [jax.ShapeDtypeStruct((m, f), out_dtype)]
    out_specs = [pl.BlockSpec((tm, tn), lambda i, j: (i, j))]
    if emit_wb:
        out_shape += [jax.ShapeDtypeStruct((k, f), BF16)] * 2
        out_specs += [pl.BlockSpec((k, tn), lambda i, j: (0, j))] * 2
    res = pl.pallas_call(
        _swiglu_kernel, grid=(m // tm, nf),
        in_specs=[pl.BlockSpec((tm, k), lambda i, j: (i, 0)),
                  _weight_spec(w_gate, layer, (k, tn), lambda i, j: (0, j)),
                  _weight_spec(w_up, layer, (k, tn), lambda i, j: (0, j + up_off))],
        out_specs=out_specs, out_shape=out_shape,
        compiler_params=_params("parallel", "parallel"), name="swiglu_up")(a, w_gate, w_up)
    return tuple(res) if emit_wb else res[0]


def _residual_epilogue(mix, x_ref, nw1_ref, nw2_ref, xo_ref, h_ref):
    xn = x_ref[...] + _rms(mix, nw1_ref[...])
    xo_ref[...] = xn
    if h_ref:
        h_ref[0][...] = _rms(xn, nw2_ref[...]).astype(h_ref[0].dtype)


def _mm_res_kernel(a_ref, w_ref, x_ref, nw1_ref, nw2_ref, xo_ref, *rest, nk, emit_h, emit_wb):
    h_ref = rest[:1] if emit_h else ()
    kk = pl.program_id(1)
    w = w_ref[...].astype(BF16)
    if emit_wb:
        rest[-1][...] = w
    part = _dot(a_ref[...], w)

    @pl.when(kk == 0)
    def _():
        xo_ref[...] = part

    @pl.when(kk > 0)
    def _():
        xo_ref[...] += part

    @pl.when(kk == nk - 1)
    def _():
        _residual_epilogue(xo_ref[...], x_ref, nw1_ref, nw2_ref, xo_ref, h_ref)


def _mm_res_resident_kernel(a_ref, w_ref, x_ref, nw1_ref, nw2_ref, xo_ref, *h_ref):
    _residual_epilogue(_dot(a_ref[...], w_ref[...]), x_ref, nw1_ref, nw2_ref, xo_ref, h_ref)


def _matmul_residual(a, w, layer, x, nw_post, nw_next, h_dtype, tm, tk, emit_wb=False):
    m, k = a.shape
    n = w.shape[-1]
    tm = min(tm, m)
    assert m % tm == 0 and (not emit_wb or m == tm)
    emit_h = nw_next is not None
    if not emit_h:
        nw_next = nw_post
    resident = w.ndim == 2
    if resident:
        grid = (m // tm,)
        row = lambda i: (i, 0)
        fixed = lambda i: (0, 0)
        a_spec = pl.BlockSpec((tm, k), row)
        w_spec = pl.BlockSpec((k, n), fixed, pipeline_mode=pl.Buffered(1))
        x_spec = pl.BlockSpec((tm, n), row)
        body = _mm_res_resident_kernel
        sem = ("parallel",)
    else:
        tk = min(tk, k)
        assert k % tk == 0
        nk = k // tk
        grid = (m // tm, nk)
        row = lambda i, kk: (i, 0)
        fixed = lambda i, kk: (0, 0)
        a_spec = pl.BlockSpec((tm, tk), lambda i, kk: (i, kk))
        w_spec = pl.BlockSpec((None, tk, n), lambda i, kk: (layer, kk, 0))
        x_spec = pl.BlockSpec((tm, n), row, pipeline_mode=pl.Buffered(1))
        body = functools.partial(_mm_res_kernel, nk=nk, emit_h=emit_h, emit_wb=emit_wb)
        sem = ("parallel", "arbitrary")
    out_shape = [jax.ShapeDtypeStruct((m, n), F32)]
    out_specs = [pl.BlockSpec((tm, n), row)]
    if emit_h:
        out_shape.append(jax.ShapeDtypeStruct((m, n), h_dtype))
        out_specs.append(pl.BlockSpec((tm, n), row))
    if emit_wb:
        out_shape.append(jax.ShapeDtypeStruct((k, n), BF16))
        out_specs.append(pl.BlockSpec((tk, n), lambda i, kk: (kk, 0)))
    res = pl.pallas_call(
        body, grid=grid,
        in_specs=[a_spec, w_spec, x_spec, pl.BlockSpec((1, n), fixed), pl.BlockSpec((1, n), fixed)],
        out_specs=out_specs, out_shape=out_shape,
        compiler_params=_params(*sem, vmem=VMEM_LIMIT_BIG_BYTES), name="matmul_residual")(
            a, w, x, nw_post, nw_next)
    xo = res[0]
    h = res[1] if emit_h else None
    return (xo, h, res[-1]) if emit_wb else (xo, h)


def _gdn_gates_kernel(a_ref, w_ref, alog_ref, dtb_ref, beta_ref, g_ref):
    ba_t = lax.dot_general(w_ref[...].astype(BF16), a_ref[...].astype(BF16), (((0,), (1,)), ((), ())),
                           preferred_element_type=F32)
    nh = beta_ref.shape[0]
    beta_ref[...] = _sigmoid(ba_t[:nh])
    g_ref[...] = -jnp.exp(alog_ref[...]) * _softplus(ba_t[nh:] + dtb_ref[...])


def _gdn_gates(a, w_ba, layer, a_log_col, dt_bias_col, tm):
    m, k = a.shape
    nh = GDN_V_HEADS
    tm = min(tm, m)
    return pl.pallas_call(
        _gdn_gates_kernel, grid=(m // tm,),
        in_specs=[pl.BlockSpec((tm, k), lambda i: (i, 0)),
                  pl.BlockSpec((None, k, 2 * nh), lambda i: (layer, 0, 0)),
                  pl.BlockSpec((nh, 1), lambda i: (0, 0)),
                  pl.BlockSpec((nh, 1), lambda i: (0, 0))],
        out_specs=[pl.BlockSpec((nh, tm), lambda i: (0, i))] * 2,
        out_shape=[jax.ShapeDtypeStruct((nh, m), F32)] * 2,
        compiler_params=_params("parallel"), name="gdn_gates")(a, w_ba, a_log_col, dt_bias_col)


def _conv_silu_norm(x, halo, ext_scr, cw_ref, j, n_qk_tiles):
    tt, tc = x.shape
    ext_scr[:SUBLANES, :] = halo
    ext_scr[SUBLANES:, :] = x
    y = x * cw_ref[GDN_CONV_W - 1:GDN_CONV_W, :]
    for s in range(1, GDN_CONV_W):
        y = y + ext_scr[SUBLANES - s:SUBLANES - s + tt, :] * cw_ref[GDN_CONV_W - 1 - s:GDN_CONV_W - s, :]
    y = _silu(y)
    qscale = jnp.where(j < n_qk_tiles, GDN_HEAD_DIM ** -0.5, 1.0)
    pieces = []
    for hh in range(tc // GDN_HEAD_DIM):
        yh = y[:, hh * GDN_HEAD_DIM:(hh + 1) * GDN_HEAD_DIM]
        r = lax.rsqrt(jnp.sum(yh * yh, axis=-1, keepdims=True) + NORM_EPS) * qscale
        pieces.append(yh * jnp.where(j < 2 * n_qk_tiles, r, 1.0))
    return jnp.concatenate(pieces, axis=1)


def _gdn_prep_kernel(x_ref, prev_ref, buf_ref, cw_ref, o_ref, ext_scr, *, n_qk_tiles):
    i = pl.program_id(1)
    j = pl.program_id(2)
    halo = jnp.where(i == 0, buf_ref[...], prev_ref[...])
    o_ref[...] = _conv_silu_norm(x_ref[...], halo, ext_scr, cw_ref, j, n_qk_tiles)


def _gdn_in_proj_kernel(a_ref, w_ref, buf_ref, cw_ref, o_ref, tail_ref, ext_scr, halo_scr, *,
                        tiles_per_seq, n_qk_tiles):
    i = pl.program_id(0)
    j = pl.program_id(1)

    @pl.when((i == 0) & (j == 0))
    def _():
        halo_scr[...] = jnp.zeros_like(halo_scr)

    halo = jnp.where(i % tiles_per_seq == 0, buf_ref[...], halo_scr[j])
    w = w_ref[...]
    n_sub, rows = ext_scr.shape[0], ext_scr.shape[1] - SUBLANES
    for s in range(n_sub):
        sl = slice(s * rows, (s + 1) * rows)
        x = _dot(a_ref[sl, :], w)
        o_ref[sl, :] = _conv_silu_norm(x, halo, ext_scr.at[s], cw_ref, j, n_qk_tiles)
        halo = x[rows - SUBLANES:, :]
    halo_scr[j] = halo
    tail_ref[...] = halo


def _gdn_in_proj(a, w, buf8, conv_w, seq_len, tm, tn):
    m, k = a.shape
    n = GDN_CONV_DIM
    tm = min(tm, seq_len)
    assert w.ndim == 2 and seq_len % tm == 0 and GDN_QK_DIM % tn == 0
    assert tm % (GDN_PROJ_SUBBLOCKS * SUBLANES) == 0
    tps = seq_len // tm
    proj, tails = pl.pallas_call(
        functools.partial(_gdn_in_proj_kernel, tiles_per_seq=tps, n_qk_tiles=GDN_QK_DIM // tn),
        grid=(m // tm, n // tn),
        in_specs=[pl.BlockSpec((tm, k), lambda i, j: (i, 0)),
                  pl.BlockSpec((k, tn), lambda i, j: (0, j)),
                  pl.BlockSpec((None, SUBLANES, tn), lambda i, j: (i // tps, 0, j)),
                  pl.BlockSpec((GDN_CONV_W, tn), lambda i, j: (0, j))],
        out_specs=[pl.BlockSpec((tm, tn), lambda i, j: (i, j)),
                   pl.BlockSpec((None, SUBLANES, tn), lambda i, j: (i, 0, j))],
        out_shape=[jax.ShapeDtypeStruct((m, n), F32), jax.ShapeDtypeStruct((m // tm, SUBLANES, n), F32)],
        scratch_shapes=[pltpu.VMEM((GDN_PROJ_SUBBLOCKS, SUBLANES + tm // GDN_PROJ_SUBBLOCKS, tn), F32),
                        pltpu.VMEM((n // tn, SUBLANES, tn), F32)],
        compiler_params=_params("arbitrary", "arbitrary", vmem=VMEM_LIMIT_BIG_BYTES), name="gdn_in_proj")(
            a, w, buf8, conv_w)
    return proj, tails[tps - 1::tps]


def _gdn_prep(proj3, buf8, conv_w, tt, tc):
    b, t, _ = proj3.shape
    tt = min(tt, t)
    assert t % tt == 0 and tt % SUBLANES == 0 and GDN_QK_DIM % tc == 0
    hb = tt // SUBLANES
    return pl.pallas_call(
        functools.partial(_gdn_prep_kernel, n_qk_tiles=GDN_QK_DIM // tc),
        grid=(b, t // tt, GDN_CONV_DIM // tc),
        in_specs=[pl.BlockSpec((None, tt, tc), lambda bb, i, j: (bb, i, j)),
                  pl.BlockSpec((None, SUBLANES, tc), lambda bb, i, j: (bb, jnp.maximum(i * hb - 1, 0), j)),
                  pl.BlockSpec((None, SUBLANES, tc), lambda bb, i, j: (bb, 0, j)),
                  pl.BlockSpec((GDN_CONV_W, tc), lambda bb, i, j: (0, j))],
        out_specs=pl.BlockSpec((None, tt, tc), lambda bb, i, j: (bb, i, j)),
        out_shape=jax.ShapeDtypeStruct((b, t, GDN_CONV_DIM), F32),
        scratch_shapes=[pltpu.VMEM((SUBLANES + tt, tc), F32)],
        compiler_params=_params("parallel", "parallel", "parallel"), name="gdn_prep")(
            proj3, proj3, buf8, conv_w)


def _gdn_delta_kernel(q_ref, k_ref, v_ref, z_ref, gr_ref, br_ref, s0_ref, nw_ref,
                      o_ref, so_ref, s_scr, *, cb, hpb, t_valid, n_blocks):
    n = pl.program_id(2)
    r = cb * CHUNK
    dh = GDN_HEAD_DIM

    @pl.when(n == 0)
    def _():
        s_scr[...] = s0_ref[...]

    q = q_ref[...]
    k = k_ref[...]
    v = v_ref[...]
    grow = gr_ref[...]
    brow = br_ref[...]
    if t_valid is not None:
        okc = (n * r + lax.broadcasted_iota(jnp.int32, (r, 1), 0)) < t_valid
        okr = (n * r + lax.broadcasted_iota(jnp.int32, (1, r), 1)) < t_valid
        k = jnp.where(okc, k, 0.0)
        v = jnp.where(okc, v, 0.0)
        grow = jnp.where(okr, grow, 0.0)
        brow = jnp.where(okr, brow, 0.0)

    ri = lax.broadcasted_iota(jnp.int32, (r, r), 0)
    ci = lax.broadcasted_iota(jnp.int32, (r, r), 1)
    eye = jnp.where(ri == ci, 1.0, 0.0)
    lm = _chunk_tril(r)
    eyeb = eye.astype(BF16)
    g_all = _split3(jnp.concatenate([grow[qh] for qh in range(hpb)], axis=0))
    b_all = _split3(jnp.concatenate([brow[qh] for qh in range(hpb)], axis=0))
    gcum_col = sum(_dot_nt(lm, p) for p in g_all)
    gcum_row = sum(_dot_nt(p, lm) for p in g_all)
    bcol = sum(_dot_nt(eyeb, p) for p in b_all)
    tril = ((ri // CHUNK) == (ci // CHUNK)) & (ri >= ci)
    lvl_masks = []
    size = 1
    while size < CHUNK:
        lvl_masks.append(((ri // (2 * size)) == (ci // (2 * size)))
                         & ((ri // size) % 2 == 1) & ((ci // size) % 2 == 0))
        size *= 2
    if t_valid is not None and t_valid <= 1:
        lvl_masks = []

    nw = nw_ref[...]
    heads = [(qh, hh) for qh in range(hpb) for hh in range(2)]
    kq = [k[:, qh * dh:(qh + 1) * dh] for qh in range(hpb)]
    qq = [q[:, qh * dh:(qh + 1) * dh] for qh in range(hpb)]
    gram = [_dot_nt(kq[qh], kq[qh]) for qh in range(hpb)]
    qk = [_dot_nt(qq[qh], kq[qh]) for qh in range(hpb)]
    gc = [gcum_col[:, i:i + 1] for i in range(len(heads))]
    bc = [bcol[:, i:i + 1] for i in range(len(heads))]
    decay = [jnp.exp(jnp.where(tril, gc[i] - gcum_row[i:i + 1, :], NEG)) for i in range(len(heads))]
    mm_ = [(gram[qh] * bc[i] * decay[i]).astype(BF16) for i, (qh, hh) in enumerate(heads)]
    tinv = [eye.astype(BF16) for _ in heads]
    for lvl, msk in enumerate(lvl_masks):
        mskb = jnp.where(msk, 1.0, 0.0).astype(BF16)
        ml = [m * mskb for m in mm_]
        if lvl == 0:
            tinv = [t - m for t, m in zip(tinv, ml)]
        else:
            y = [_dot(m, t) for t, m in zip(tinv, ml)]
            tinv = [t - _dot(t, yy).astype(BF16) for t, yy in zip(tinv, y)]
    eg = [jnp.exp(g_) for g_ in gc]
    sol = [_dot(tinv[i], jnp.concatenate([v[:, (2 * qh + hh) * dh:(2 * qh + hh + 1) * dh] * bc[i],
                                          kq[qh] * (bc[i] * eg[i])], axis=1))
           for i, (qh, hh) in enumerate(heads)]
    asol = [_dot(qk[qh] * decay[i], sol[i]) for i, (qh, hh) in enumerate(heads)]
    o2 = [a[:, :dh] for a in asol]
    o1 = [qq[qh] * eg[i] - asol[i][:, dh:] for i, (qh, hh) in enumerate(heads)]
    glast = [[g_[(c + 1) * CHUNK - 1:(c + 1) * CHUNK, :] for c in range(cb)] for g_ in gc]
    kd = [kq[qh] * jnp.exp(jnp.concatenate([jnp.broadcast_to(gl, (CHUNK, 1)) for gl in glast[i]], axis=0) - gc[i])
          for i, (qh, hh) in enumerate(heads)]
    qw = [[_dot_tn(kd[i][c * CHUNK:(c + 1) * CHUNK], sol[i][c * CHUNK:(c + 1) * CHUNK]) for c in range(cb)]
          for i in range(len(heads))]
    s = [s_scr[2 * qh + hh] for qh, hh in heads]
    for c in range(cb):
        sl = slice(c * CHUNK, (c + 1) * CHUNK)
        for i, (qh, hh) in enumerate(heads):
            x = _dot(jnp.concatenate([qw[i][c][:, dh:], o1[i][sl]], axis=0), s[i])
            o = x[dh:] + o2[i][sl]
            s[i] = s[i] * jnp.exp(glast[i][c]) - x[:dh] + qw[i][c][:, :dh]
            col = slice((2 * qh + hh) * dh, (2 * qh + hh + 1) * dh)
            zc = z_ref[sl, col]
            o_ref[sl, col] = (_rms(o, nw) * _silu(zc)).astype(o_ref.dtype)
    for i, (qh, hh) in enumerate(heads):
        s_scr[2 * qh + hh] = s[i]

    @pl.when(n == n_blocks - 1)
    def _():
        so_ref[...] = s_scr[...]


def _gdn_delta(qkv3, proj3, grow, brow, s0, norm_w, out_dtype, cb, hpb, t_valid):
    b, t, _ = qkv3.shape
    r = cb * CHUNK
    assert t % r == 0
    nb = t // r
    dh = GDN_HEAD_DIM
    qw_, vw_ = hpb * dh, 2 * hpb * dh
    assert GDN_QK_HEADS % hpb == 0
    koff = GDN_QK_DIM // qw_
    voff = 2 * GDN_QK_DIM // vw_
    zoff = (proj3.shape[-1] - GDN_V_DIM) // vw_
    return pl.pallas_call(
        functools.partial(_gdn_delta_kernel, cb=cb, hpb=hpb, t_valid=t_valid, n_blocks=nb),
        grid=(b, GDN_QK_HEADS // hpb, nb),
        in_specs=[pl.BlockSpec((None, r, qw_), lambda bb, h, n: (bb, n, h)),
                  pl.BlockSpec((None, r, qw_), lambda bb, h, n: (bb, n, koff + h)),
                  pl.BlockSpec((None, r, vw_), lambda bb, h, n: (bb, n, voff + h)),
                  pl.BlockSpec((None, r, vw_), lambda bb, h, n: (bb, n, zoff + h)),
                  pl.BlockSpec((None, hpb, 2, r), lambda bb, h, n: (bb, h, 0, n)),
                  pl.BlockSpec((None, hpb, 2, r), lambda bb, h, n: (bb, h, 0, n)),
                  pl.BlockSpec((None, 2 * hpb, dh, dh), lambda bb, h, n: (bb, h, 0, 0)),
                  pl.BlockSpec((1, dh), lambda bb, h, n: (0, 0))],
        out_specs=[pl.BlockSpec((None, r, vw_), lambda bb, h, n: (bb, n, h)),
                   pl.BlockSpec((None, 2 * hpb, dh, dh), lambda bb, h, n: (bb, h, 0, 0))],
        out_shape=[jax.ShapeDtypeStruct((b, t, GDN_V_DIM), out_dtype),
                   jax.ShapeDtypeStruct(s0.shape, F32)],
        scratch_shapes=[pltpu.VMEM((2 * hpb, dh, dh), F32)],
        compiler_params=_params("parallel", "parallel", "arbitrary"), name="gdn_delta")(
            qkv3, qkv3, qkv3, proj3, grow, brow, s0, norm_w)


def _gla_kernel(q_ref, k_ref, v_ref, gate_ref, gk_ref, bgk_ref, s0_ref, nw_ref,
                o_ref, so_ref, s_scr, *, cb, t_valid, n_blocks):
    n = pl.program_id(2)
    r = cb * CHUNK
    dk, dv = GLA_DK, GLA_DV

    @pl.when(n == 0)
    def _():
        s_scr[...] = s0_ref[...]

    xg = gk_ref[...] + bgk_ref[...]
    g = (jnp.minimum(xg, 0.0) - jnp.log(1.0 + jnp.exp(-jnp.abs(xg)))) * (1.0 / GLA_GATE_NORMALIZER)
    q = q_ref[...] * GLA_DK ** -0.5
    k = k_ref[...]
    v = v_ref[...]
    if t_valid is not None:
        okc = (n * r + lax.broadcasted_iota(jnp.int32, (r, 1), 0)) < t_valid
        g = jnp.where(okc, g, 0.0)
        k = jnp.where(okc, k, 0.0)
        v = jnp.where(okc, v, 0.0)

    gparts = _split3(g)
    bcum = sum(_dot(_chunk_tril(r), p) for p in gparts)
    ones = jnp.ones((CHUNK, LANES), BF16)
    row = lax.broadcasted_iota(jnp.int32, (CHUNK, 1), 0)
    jrow = lax.broadcasted_iota(jnp.int32, (SUB, 1), 0)
    lane = lax.broadcasted_iota(jnp.int32, (SUB, CHUNK), 1)
    nw = nw_ref[...]
    s = s_scr[...]
    for c in range(cb):
        sl = slice(c * CHUNK, (c + 1) * CHUNK)
        qc, kc, vc, bc = q[sl], k[sl], v[sl], bcum[sl]
        blast = bc[CHUNK - 1:CHUNK, :]
        attn_t = jnp.zeros((CHUNK, CHUNK), F32)
        for bi in range(1, CHUNK // SUB):
            bref = bc[bi * SUB:bi * SUB + 1, :]
            k_i = jnp.where(row < bi * SUB, kc * jnp.exp(jnp.minimum(bref - bc, 0.0)), 0.0)
            q_i = jnp.where((row >= bi * SUB) & (row < (bi + 1) * SUB),
                            qc * jnp.exp(jnp.minimum(bc - bref, 0.0)), 0.0)
            attn_t = attn_t + _dot_nt(k_i, q_i)
        diag = []
        for bi in range(CHUNK // SUB):
            sb = slice(bi * SUB, (bi + 1) * SUB)
            qb, kb, bb = qc[sb], kc[sb], bc[sb]
            d = jnp.zeros((SUB, CHUNK), F32)
            for il in range(SUB):
                e = jnp.exp(jnp.minimum(bb[il:il + 1, :] - bb, 0.0))
                col = jnp.sum(kb * e * qb[il:il + 1, :], axis=-1, keepdims=True)
                col = jnp.where(jrow <= il, col, 0.0)
                d = jnp.where(lane == bi * SUB + il, col, d)
            diag.append(d)
        attn_t = attn_t + jnp.concatenate(diag, axis=0)
        o = _dot(qc * jnp.exp(bc), s) + _dot_tn(attn_t, vc)
        bl_col = sum(_dot_tn(p[sl], ones) for p in gparts)
        decay_col = jnp.concatenate([jnp.exp(bl_col)] * (dv // LANES), axis=1)
        s = s * decay_col + _dot_tn(kc * jnp.exp(blast - bc), vc)
        gt = gate_ref[sl, :]
        o_ref[sl, :] = (_rms(o, nw) * _silu(gt)).astype(o_ref.dtype)
    s_scr[...] = s

    @pl.when(n == n_blocks - 1)
    def _():
        so_ref[...] = s_scr[...]


def _gla_chunks(qkvg3, gk3, b_gk, s0, norm_w, out_dtype, cb, t_valid):
    b, t, _ = qkvg3.shape
    r = cb * CHUNK
    assert t % r == 0
    nb = t // r
    dk, dv = GLA_DK, GLA_DV
    koff = GLA_HEADS
    voff = 2 * GLA_HEADS * dk // dv
    goff = voff + GLA_HEADS
    return pl.pallas_call(
        functools.partial(_gla_kernel, cb=cb, t_valid=t_valid, n_blocks=nb),
        grid=(b, GLA_HEADS, nb),
        in_specs=[pl.BlockSpec((None, r, dk), lambda bb, h, n: (bb, n, h)),
                  pl.BlockSpec((None, r, dk), lambda bb, h, n: (bb, n, koff + h)),
                  pl.BlockSpec((None, r, dv), lambda bb, h, n: (bb, n, voff + h)),
                  pl.BlockSpec((None, r, dv), lambda bb, h, n: (bb, n, goff + h)),
                  pl.BlockSpec((None, r, dk), lambda bb, h, n: (bb, n, h)),
                  pl.BlockSpec((1, dk), lambda bb, h, n: (0, h)),
                  pl.BlockSpec((None, None, dk, dv), lambda bb, h, n: (bb, h, 0, 0)),
                  pl.BlockSpec((1, dv), lambda bb, h, n: (0, 0))],
        out_specs=[pl.BlockSpec((None, r, dv), lambda bb, h, n: (bb, n, h)),
                   pl.BlockSpec((None, None, dk, dv), lambda bb, h, n: (bb, h, 0, 0))],
        out_shape=[jax.ShapeDtypeStruct((b, t, GLA_HEADS * dv), out_dtype),
                   jax.ShapeDtypeStruct(s0.shape, F32)],
        scratch_shapes=[pltpu.VMEM((dk, dv), F32)],
        compiler_params=_params("parallel", "parallel", "arbitrary"), name="gla_chunks")(
            qkvg3, qkvg3, qkvg3, qkvg3, gk3, b_gk, s0, norm_w)


def _t5_bucket(n):
    n = np.asarray(n)
    max_exact = N_BUCKETS // 2
    nf = np.maximum(n, max_exact).astype(np.float32)
    large = max_exact + (np.log(nf / max_exact) / math.log(MAX_DISTANCE / max_exact)
                         * (N_BUCKETS - max_exact)).astype(np.int32)
    return np.where(n < max_exact, n, np.minimum(large, N_BUCKETS - 1)).astype(np.int32)


def _lambda(lam_ref, lam_init):
    lf = lam_ref[...]
    s1 = jnp.sum(lf[0:1] * lf[1:2], axis=-1, keepdims=True)
    s2 = jnp.sum(lf[2:3] * lf[3:4], axis=-1, keepdims=True)
    return jnp.exp(s1) - jnp.exp(s2) + lam_init


_FAR_FAR, _FAR_SUB, _SUB_DIAG, _DIAG = range(4)


def _flash_kernel(qi_ref, ki_ref, kind_ref, q_ref, k_ref, v_ref, bd_ref, bs_ref, rb_ref, lam_ref, sub_ref, o_ref,
                  q_scr, m_scr, l_scr, acc_scr, bias_scr, *, lam_init):
    h = pl.program_id(0)
    step = pl.program_id(2)
    ki = ki_ref[step]
    kind = kind_ref[step]
    tb = q_ref.shape[0]
    dh = DIFF_HEAD_DIM

    @pl.when((pl.program_id(1) == 0) & (step == 0))
    def _():
        ri = lax.broadcasted_iota(jnp.int32, (tb, tb), 0)
        ci = lax.broadcasted_iota(jnp.int32, (tb, tb), 1)
        for mm in range(2):
            bd = jnp.zeros((tb, tb), F32)
            bs = jnp.zeros((tb, tb), F32)
            for bk in range(N_BUCKETS):
                val = rb_ref[bk, 2 * h + mm] * LOG2E
                bd = jnp.where(bd_ref[...] == bk, val, bd)
                bs = jnp.where(bs_ref[...] == bk, val, bs)
            bias_scr[mm, 0] = jnp.where(ri > ci, NEG, bd)
            bias_scr[mm, 1] = bs

    @pl.when(ki == 0)
    def _():
        q_scr[...] = (q_ref[...] * (dh ** -0.5 * LOG2E)).astype(q_scr.dtype)
        m_scr[...] = jnp.full_like(m_scr, NEG)
        l_scr[...] = jnp.zeros_like(l_scr)
        acc_scr[...] = jnp.zeros_like(acc_scr)

    def update(mm, s, shift, vt):
        m_prev = m_scr[mm]
        if shift.ndim == 0:
            m_new = jnp.maximum(m_prev, jnp.max(s, axis=0, keepdims=True) + shift)
            p = jnp.exp2(s - (m_new - shift))
        else:
            s = s + shift
            m_new = jnp.maximum(m_prev, jnp.max(s, axis=0, keepdims=True))
            p = jnp.exp2(s - m_new)
        alpha = jnp.exp2(m_prev - m_new)
        l_scr[mm] = alpha * l_scr[mm] + jnp.sum(p, axis=0, keepdims=True)
        acc_scr[mm] = alpha * acc_scr[mm] + _dot(vt, p)
        m_scr[mm] = m_new

    far = lambda mm: rb_ref[N_BUCKETS - 1, 2 * h + mm] * LOG2E
    sub = lambda mm: bias_scr[mm, 1]
    diag = lambda mm: bias_scr[mm, 0]

    def both_maps(shifts):
        rows = [pl.ds(pl.multiple_of((ki + n) * tb, tb), tb) for n in range(len(shifts))]
        vt = jnp.concatenate([v_ref[r, :].T.astype(BF16) for r in rows], axis=1)
        s = [[_dot_nt(k_ref[r, mm * dh:(mm + 1) * dh], q_scr[:, mm * dh:(mm + 1) * dh]) for r in rows]
             for mm in range(2)]
        for mm in range(2):
            sh = [f(mm) for f in shifts]
            if all(x.ndim == 0 for x in sh):
                update(mm, jnp.concatenate(s[mm], axis=0), sh[0], vt)
            else:
                update(mm, jnp.concatenate([a + x for a, x in zip(s[mm], sh)], axis=0), jnp.float32(0.0), vt)

    @pl.when(kind == _FAR_FAR)
    def _():
        both_maps([far, far])

    @pl.when(kind == _FAR_SUB)
    def _():
        both_maps([far, sub])

    @pl.when(kind == _SUB_DIAG)
    def _():
        both_maps([sub, diag])

    @pl.when(kind == _DIAG)
    def _():
        both_maps([diag])

    @pl.when(kind >= _SUB_DIAG)
    def _():
        lam = _lambda(lam_ref, lam_init)
        out = acc_scr[0] / l_scr[0] - lam * (acc_scr[1] / l_scr[1])
        out = out * lax.rsqrt(jnp.mean(out * out, axis=0, keepdims=True) + NORM_EPS)
        o_ref[...] = (out * (sub_ref[...] * (1.0 - lam_init))).T.astype(o_ref.dtype)


def _flash_diff(qkv3, rel_bias, lambdas, subln_col, lam_init, out_dtype, tb):
    b, t, _ = qkv3.shape
    tb = min(tb, t)
    assert t % tb == 0 and tb >= MAX_DISTANCE
    nq = t // tb
    dd = 2 * DIFF_HEAD_DIM
    j = np.arange(tb)[:, None]
    i = np.arange(tb)[None, :]
    bkt_diag = jnp.asarray(_t5_bucket(np.maximum(i - j, 0)))
    bkt_sub = jnp.asarray(_t5_bucket(tb + i - j))
    koff = D_MODEL // dd
    kinds = {"FF": _FAR_FAR, "FS": _FAR_SUB, "SD": _SUB_DIAG, "D": _DIAG}
    pairs = []
    for qi in range(nq):
        tags = "F" * (qi - 1) + ("S" if qi >= 1 else "") + "D"
        for ki in range(0, qi + 1, 2):
            pairs.append((qi, ki, kinds[tags[ki:ki + 2]]))
    qi_tbl = jnp.asarray([pq for pq, _, _ in pairs], jnp.int32)
    ki_tbl = jnp.asarray([pk for _, pk, _ in pairs], jnp.int32)
    kind_tbl = jnp.asarray([kd for _, _, kd in pairs], jnp.int32)
    grid_spec = pltpu.PrefetchScalarGridSpec(
        num_scalar_prefetch=3, grid=(DIFF_HEADS, b, len(pairs)),
        in_specs=[pl.BlockSpec((None, tb, dd), lambda h, bb, s, qt, kt, kd: (bb, qt[s], h)),
                  pl.BlockSpec((None, t, dd), lambda h, bb, s, qt, kt, kd: (bb, 0, koff + h)),
                  pl.BlockSpec((None, t, dd), lambda h, bb, s, qt, kt, kd: (bb, 0, 2 * koff + h)),
                  pl.BlockSpec((tb, tb), lambda h, bb, s, qt, kt, kd: (0, 0)),
                  pl.BlockSpec((tb, tb), lambda h, bb, s, qt, kt, kd: (0, 0)),
                  pl.BlockSpec(memory_space=pltpu.SMEM),
                  pl.BlockSpec((4, DIFF_HEAD_DIM), lambda h, bb, s, qt, kt, kd: (0, 0)),
                  pl.BlockSpec((dd, 1), lambda h, bb, s, qt, kt, kd: (0, 0))],
        out_specs=pl.BlockSpec((None, tb, dd), lambda h, bb, s, qt, kt, kd: (bb, qt[s], h)),
        scratch_shapes=[pltpu.VMEM((tb, dd), BF16),
                        pltpu.VMEM((2, 1, tb), F32), pltpu.VMEM((2, 1, tb), F32),
                        pltpu.VMEM((2, dd, tb), F32), pltpu.VMEM((2, 2, tb, tb), F32)])
    return pl.pallas_call(
        functools.partial(_flash_kernel, lam_init=lam_init), grid_spec=grid_spec,
        out_shape=jax.ShapeDtypeStruct((b, t, D_MODEL), out_dtype),
        compiler_params=_params("parallel", "arbitrary", "arbitrary"), name="flash_diff")(
            qi_tbl, ki_tbl, kind_tbl, qkv3, qkv3, qkv3, bkt_diag, bkt_sub, rel_bias, lambdas, subln_col)


def _decode_kernel(pt_ref, q_ref, *refs, pp, n_steps, lam_init):
    kp_refs, vp_refs = refs[:pp], refs[pp:2 * pp]
    kn_ref, vn_ref, bkt_ref, tbl_ref, lam_ref, sub_ref, o_ref, m_scr, l_scr, acc_scr, bias_scr, p_scr, a_scr = refs[2 * pp:]
    step = pl.program_id(1)
    nh, dh = DIFF_HEADS, DIFF_HEAD_DIM
    ones = jnp.ones((dh, LANES), BF16)
    qs = q_ref[...] * (dh ** -0.5 * LOG2E)

    @pl.when(step == 0)
    def _():
        m_scr[...] = jnp.full_like(m_scr, NEG)
        l_scr[...] = jnp.zeros_like(l_scr)
        acc_scr[...] = jnp.zeros_like(acc_scr)
        bkt = bkt_ref[...]
        bias = jnp.zeros(bkt.shape, F32)
        for bk in range(N_BUCKETS):
            bias = jnp.where(bkt == bk, (tbl_ref[bk] * LOG2E)[None], bias)
        bias_scr[...] = bias

    def lane_sum(x):
        return _dot(x, ones)

    def accumulate(pr, shift_row, values):
        m_prev = m_scr[...]
        if pr.ndim == 3:
            m_new = jnp.maximum(m_prev, jnp.max(pr, axis=0) + shift_row)
            p = jnp.exp2(pr - (m_new - shift_row)[None])
            psum = jnp.sum(p, axis=0)
        else:
            m_new = jnp.maximum(m_prev, pr + shift_row)
            p = jnp.exp2(pr - (m_new - shift_row))
            psum = p
        alpha = jnp.exp2(m_prev - m_new)
        l_scr[...] = alpha * l_scr[...] + psum
        m_scr[...] = m_new
        a_scr[...] = alpha
        if pr.ndim == 3:
            p_scr[...] = p
        else:
            p_scr[0] = p
        for par in range(2):
            ae = a_scr[pl.ds(par, nh, stride=2), :]
            if pr.ndim == 3:
                pe = p_scr[:, pl.ds(par, nh, stride=2), :]
                pv = jnp.sum(jnp.concatenate([pe, pe], axis=-1) * values(), axis=0)
            else:
                pe = p_scr[0, pl.ds(par, nh, stride=2), :]
                pv = jnp.concatenate([pe, pe], axis=-1) * values()
            acc_scr[par] = jnp.concatenate([ae, ae], axis=-1) * acc_scr[par] + pv

    def page_logits(j):
        prod = kp_refs[j][...] * qs[None]
        return lane_sum(prod.reshape(PAGE_SIZE * 2 * nh, dh)).reshape(PAGE_SIZE, 2 * nh, LANES)

    far = tbl_ref[N_BUCKETS - 1] * LOG2E
    zero = jnp.zeros_like(far)
    for j in range(pp - 1):
        accumulate(page_logits(j), far, lambda j=j: vp_refs[j][...])

    @pl.when(step < n_steps - 1)
    def _():
        accumulate(page_logits(pp - 1), far, lambda: vp_refs[pp - 1][...])

    @pl.when(step == n_steps - 1)
    def _():
        accumulate(page_logits(pp - 1) + bias_scr[...], zero, lambda: vp_refs[pp - 1][...])
        accumulate(lane_sum(kn_ref[...] * qs) + tbl_ref[0] * LOG2E, zero, lambda: vn_ref[...])
        l_scr_v = l_scr[...]
        a_scr[...] = l_scr_v
        outs = []
        for par in range(2):
            le = a_scr[pl.ds(par, nh, stride=2), :]
            outs.append(acc_scr[par] / jnp.concatenate([le, le], axis=-1))
        out = outs[0] - _lambda(lam_ref, lam_init) * outs[1]
        o_ref[...] = _rms(out, sub_ref[...]) * (1.0 - lam_init)


def _decode_diff(q, k_new, v_new, cache_k, cache_v, layer, page_table, rel_bias, lambdas, subln, lam_init, pp):
    b, n_pages = page_table.shape
    nh, dh = DIFF_HEADS, DIFF_HEAD_DIM
    pp = min(pp, n_pages)
    assert n_pages % pp == 0 and PAGE_SIZE >= MAX_DISTANCE and dh == LANES
    n_steps = n_pages // pp
    bkt_last = jnp.asarray(np.broadcast_to(
        _t5_bucket(PAGE_SIZE - np.arange(PAGE_SIZE)).reshape(PAGE_SIZE, 1, 1), (PAGE_SIZE, 2 * nh, LANES)))
    tbl = jnp.broadcast_to(rel_bias[:, :, None], (N_BUCKETS, 2 * nh, LANES))
    page_spec = lambda j, width, lanes: pl.BlockSpec(
        (None, None, PAGE_SIZE, width, lanes), lambda bb, s, pt: (layer, pt[bb, s * pp + j], 0, 0, 0))
    full = lambda shape: pl.BlockSpec(shape, lambda bb, s, pt: (0,) * len(shape))
    per_seq = lambda shape: pl.BlockSpec((None,) + shape, lambda bb, s, pt: (bb,) + (0,) * len(shape))
    grid_spec = pltpu.PrefetchScalarGridSpec(
        num_scalar_prefetch=1, grid=(b, n_steps),
        in_specs=([per_seq((2 * nh, dh))]
                  + [page_spec(j, 2 * nh, dh) for j in range(pp)]
                  + [page_spec(j, nh, 2 * dh) for j in range(pp)]
                  + [per_seq((2 * nh, dh)), per_seq((nh, 2 * dh)),
                     full((PAGE_SIZE, 2 * nh, LANES)), full((N_BUCKETS, 2 * nh, LANES)),
                     full((4, dh)), full((1, 2 * dh))]),
        out_specs=per_seq((nh, 2 * dh)),
        scratch_shapes=[pltpu.VMEM((2 * nh, LANES), F32), pltpu.VMEM((2 * nh, LANES), F32),
                        pltpu.VMEM((2, nh, 2 * dh), F32), pltpu.VMEM((PAGE_SIZE, 2 * nh, LANES), F32),
                        pltpu.VMEM((PAGE_SIZE, 2 * nh, LANES), F32), pltpu.VMEM((2 * nh, LANES), F32)])
    return pl.pallas_call(
        functools.partial(_decode_kernel, pp=pp, n_steps=n_steps, lam_init=lam_init),
        grid_spec=grid_spec,
        out_shape=jax.ShapeDtypeStruct((b, nh, 2 * dh), F32),
        compiler_params=_params("parallel", "arbitrary"), name="decode_diff")(
            page_table, q, *([cache_k] * pp), *([cache_v] * pp), k_new, v_new, bkt_last, tbl, lambdas, subln)


class _Tiles:
    def __init__(self, prompt):
        self.prompt = prompt
        self.act_dtype = BF16 if prompt else F32
        self.tm = 1024 if prompt else SUBLANES
        self.tn = 1024
        self.tn_ff = 512
        self.tm_ff = 1024
        self.tm_res = 256 if prompt else SUBLANES
        self.tk_res = 512
        self.tt_prep = 256 if prompt else SUBLANES
        self.tc_prep = 1024
        self.cb = 4 if prompt else 1
        self.gdn_hpb = 4
        self.tb_attn = 512
        self.decode_pages = 8


def _group_trunk(x3, tiles, gdn_s0, gdn_c0, gla_s0, cache_k, cache_v, page_table, p, wb):
    b, t, d = x3.shape
    m = b * t
    x = x3.reshape(m, d)
    if not tiles.prompt:
        assert t == 1 and m % SUBLANES == 0
    t_pad = t if tiles.prompt else CHUNK
    t_valid = None if tiles.prompt else t
    nw = lambda i, jn: p['norm_w'][i, jn].reshape(1, d)
    adt = tiles.act_dtype

    def pad_time(a3, to):
        return a3 if a3.shape[1] == to else jnp.pad(a3, ((0, 0), (0, to - a3.shape[1]), (0, 0)))

    def big_matmul(a, name, layer):
        if tiles.prompt:
            return _matmul(a, wb[name, layer], layer, F32, tiles.tm, tiles.tn)
        out, wb[name, layer] = _matmul(a, p[name], layer, F32, tiles.tm, tiles.tn, emit_wb=True)
        return out

    def residual_matmul(a, name, layer, x_in, nw_post, nw_next):
        if tiles.prompt:
            return _matmul_residual(a, wb[name, layer], layer, x_in, nw_post, nw_next, adt,
                                    tiles.tm_res, tiles.tk_res)
        xo, h_next, wb[name, layer] = _matmul_residual(a, p[name], layer, x_in, nw_post, nw_next, adt,
                                                       tiles.tm_res, tiles.tk_res, emit_wb=True)
        return xo, h_next

    def ffn_up(a, layer):
        if tiles.prompt:
            return _swiglu_up(a, wb['ffn_gate', layer], wb['ffn_up', layer], layer, adt, tiles.tm_ff, tiles.tn_ff)
        act_, wb['ffn_gate', layer], wb['ffn_up', layer] = _swiglu_up(
            a, p['ffn_w_up'], p['ffn_w_up'], layer, adt, tiles.tm, tiles.tn_ff, emit_wb=True)
        return act_

    gdn_s, gdn_c, gla_s, att_k, att_v = [], [], [], [], []
    h = _rmsnorm(x, nw(0, 0), adt, tiles.tm)
    for i in range(DEPTH):
        j = i // N_MIXERS
        kind = i % N_MIXERS
        if kind == 0:
            beta, g = _gdn_gates(h, p['gdn_w_ba'], j, p['gdn_a_log'][j].reshape(-1, 1),
                                 p['gdn_dt_bias'][j].reshape(-1, 1), tiles.tm)
            buf0 = gdn_c0[j]
            nbuf = GDN_CONV_W - 1
            buf8 = jnp.pad(buf0, ((0, 0), (SUBLANES - nbuf, 0), (0, 0)))
            if tiles.prompt:
                proj, tail = _gdn_in_proj(h, wb['gdn_w_in', j], buf8, p['gdn_conv_w'][j], t, tiles.tm, tiles.tn)
                qkv3 = proj.reshape(b, t, -1)
                proj3p = _matmul(h, wb['gdn_w_in', j], j, F32, tiles.tm, tiles.tn,
                                 col_start=GDN_CONV_DIM, n_out=GDN_V_DIM).reshape(b, t, -1)
                new_buf = tail[:, SUBLANES - nbuf:, :]
            else:
                proj3 = big_matmul(h, 'gdn_w_in', j).reshape(b, t, -1)
                new_buf = jnp.concatenate([buf0[:, t:], proj3[..., :GDN_CONV_DIM]], axis=1)
                proj3p = pad_time(proj3, SUBLANES)
                qkv3 = pad_time(_gdn_prep(proj3p, buf8, p['gdn_conv_w'][j], tiles.tt_prep, tiles.tc_prep), t_pad)
                proj3p = pad_time(proj3p, t_pad)

            def heads(a):
                a = a.reshape(GDN_QK_HEADS, 2, b, t).transpose(2, 0, 1, 3)
                return a if t == t_pad else jnp.pad(a, ((0, 0), (0, 0), (0, 0), (0, t_pad - t)))
            o3, s_new = _gdn_delta(qkv3, proj3p, heads(g), heads(beta), gdn_s0[j],
                                   p['gdn_norm_w'][j].reshape(1, -1), adt, tiles.cb, tiles.gdn_hpb, t_valid)
            mix_in, w_out = o3[:, :t].reshape(m, -1), 'gdn_w_out'
            gdn_s.append(s_new)
            gdn_c.append(new_buf)
        elif kind == 1:
            qkvg = big_matmul(h, 'gla_w_qkvg', j)
            low = _matmul(h, p['gla_w_gk1'], j, F32, tiles.tm, tiles.tn)
            gk = _matmul(low, p['gla_w_gk2'], j, F32, tiles.tm, tiles.tn)
            o3, s_new = _gla_chunks(pad_time(qkvg.reshape(b, t, -1), t_pad), pad_time(gk.reshape(b, t, -1), t_pad),
                                    p['gla_b_gk'][j].reshape(1, -1), gla_s0[j],
                                    p['gla_norm_w'][j].reshape(1, -1), adt, tiles.cb, t_valid)
            mix_in, w_out = o3[:, :t].reshape(m, -1), 'gla_w_out'
            gla_s.append(s_new)
        else:
            lam_init = 0.8 - 0.6 * math.exp(-0.3 * i)
            qkv = big_matmul(h, 'diff_w_qkv', j)
            nh, dh = DIFF_HEADS, DIFF_HEAD_DIM
            k_new = qkv[:, D_MODEL:2 * D_MODEL].reshape(b, t, 2 * nh, dh)
            v_new = qkv[:, 2 * D_MODEL:].reshape(b, t, nh, 2 * dh)
            subln = p['diff_subln'][j].reshape(1, -1)
            if tiles.prompt:
                o3 = _flash_diff(qkv.reshape(b, t, -1), p['rel_bias'], p['diff_lambda'][j],
                                 subln.reshape(-1, 1), lam_init, adt, tiles.tb_attn)
                mix_in = o3.reshape(m, -1)
            else:
                o = _decode_diff(qkv[:, :D_MODEL].reshape(b, 2 * nh, dh), k_new.reshape(b, 2 * nh, dh),
                                 v_new.reshape(b, nh, 2 * dh), cache_k, cache_v, j, page_table,
                                 p['rel_bias'], p['diff_lambda'][j], subln, lam_init, tiles.decode_pages)
                mix_in = o.reshape(m, -1)
            w_out = 'diff_w_out'
            att_k.append(k_new)
            att_v.append(v_new)
        x, h = residual_matmul(mix_in, w_out, j, x, nw(i, 1), nw(i, 2))
        act = ffn_up(h, i)
        nxt = nw(i + 1, 0) if i + 1 < DEPTH else None
        x, h = residual_matmul(act, 'ffn_w_down', i, x, nw(i, 3), nxt)
    return (x.reshape(b, t, d), jnp.stack(gdn_s), jnp.stack(gdn_c), jnp.stack(gla_s),
            jnp.stack(att_k), jnp.stack(att_v))


def kernel(x_prompt, x_sample, state_gdn, state_gdn_conv, state_gla, cache_k, cache_v, page_table,
           norm_w, ffn_w_up, ffn_w_down, rel_bias,
           gdn_w_in, gdn_w_ba, gdn_conv_w, gdn_a_log, gdn_dt_bias, gdn_norm_w, gdn_w_out,
           gla_w_qkvg, gla_w_gk1, gla_w_gk2, gla_b_gk, gla_norm_w, gla_w_out,
           diff_w_qkv, diff_lambda, diff_subln, diff_w_out):
    p = dict(norm_w=norm_w, ffn_w_up=ffn_w_up, ffn_w_down=ffn_w_down, rel_bias=rel_bias,
             gdn_w_in=gdn_w_in, gdn_w_ba=gdn_w_ba, gdn_conv_w=gdn_conv_w, gdn_a_log=gdn_a_log,
             gdn_dt_bias=gdn_dt_bias, gdn_norm_w=gdn_norm_w, gdn_w_out=gdn_w_out,
             gla_w_qkvg=gla_w_qkvg, gla_w_gk1=gla_w_gk1, gla_w_gk2=gla_w_gk2, gla_b_gk=gla_b_gk,
             gla_norm_w=gla_norm_w, gla_w_out=gla_w_out,
             diff_w_qkv=diff_w_qkv, diff_lambda=diff_lambda, diff_subln=diff_subln, diff_w_out=diff_w_out)
    bp = x_prompt.shape[0]
    n_gdn, n_gla = state_gdn.shape[0], state_gla.shape[0]
    zeros_gdn = jnp.zeros((n_gdn, bp) + state_gdn.shape[2:], F32)
    zeros_conv = jnp.zeros((n_gdn, bp) + state_gdn_conv.shape[2:], F32)
    zeros_gla = jnp.zeros((n_gla, bp) + state_gla.shape[2:], F32)
    wb = {}
    outs_s = _group_trunk(x_sample, _Tiles(False), state_gdn, state_gdn_conv, state_gla,
                          cache_k, cache_v, page_table, p, wb)
    outs_p = _group_trunk(x_prompt, _Tiles(True), zeros_gdn, zeros_conv, zeros_gla, None, None, None, p, wb)
    return (outs_p[0], outs_s[0]) + outs_p[1:] + outs_s[1:]
```

```python
import functools
import math

import numpy as np
import jax
import jax.numpy as jnp
from jax import lax
from jax.experimental import pallas as pl
from jax.experimental.pallas import tpu as pltpu

F32 = jnp.float32
BF16 = jnp.bfloat16

D_MODEL = 2048
DEPTH = 4
PAGE_SIZE = 128
N_MIXERS = 3
GDN_HEAD_DIM = 128
GDN_QK_HEADS = D_MODEL // 128
GDN_V_HEADS = 2 * GDN_QK_HEADS
GDN_QK_DIM = GDN_QK_HEADS * GDN_HEAD_DIM
GDN_V_DIM = GDN_V_HEADS * GDN_HEAD_DIM
GDN_CONV_DIM = 2 * GDN_QK_DIM + GDN_V_DIM
GDN_CONV_W = 4
GLA_HEADS = 4
GLA_DK = D_MODEL // 2 // GLA_HEADS
GLA_DV = D_MODEL // GLA_HEADS
GLA_GATE_NORMALIZER = 16.0
DIFF_HEAD_DIM = 128
DIFF_HEADS = D_MODEL // (2 * DIFF_HEAD_DIM)
N_BUCKETS = 32
MAX_DISTANCE = 128
D_FF = ((8 * D_MODEL + 3 * 256 - 1) // (3 * 256)) * 256
NORM_EPS = 1e-6

LANES = 128
SUBLANES = 8
CHUNK = 64
SUB = 16
GDN_PROJ_SUBBLOCKS = 8
VMEM_LIMIT_BYTES = 48 * 1024 * 1024
VMEM_LIMIT_BIG_BYTES = 56 * 1024 * 1024
NEG = -1e30
LOG2E = math.log2(math.e)


def _params(*sem, vmem=VMEM_LIMIT_BYTES):
    return pltpu.CompilerParams(dimension_semantics=sem, vmem_limit_bytes=vmem)


def _dot(a, b):
    return jnp.dot(a.astype(BF16), b.astype(BF16), preferred_element_type=F32)


def _dot_nt(a, b):
    return lax.dot_general(a.astype(BF16), b.astype(BF16), (((1,), (1,)), ((), ())),
                           preferred_element_type=F32)


def _dot_tn(a, b):
    return lax.dot_general(a.astype(BF16), b.astype(BF16), (((0,), (0,)), ((), ())),
                           preferred_element_type=F32)


def _split3(x):
    hi = x.astype(BF16)
    r = x - hi.astype(F32)
    mid = r.astype(BF16)
    lo = (r - mid.astype(F32)).astype(BF16)
    return hi, mid, lo


def _sigmoid(x):
    return 0.5 * jnp.tanh(0.5 * x) + 0.5


def _silu(x):
    h = 0.5 * x
    return h + h * jnp.tanh(h)


def _softplus(x):
    return jnp.maximum(x, 0.0) + jnp.log(1.0 + jnp.exp(-jnp.abs(x)))


def _rms(x, w):
    return x * lax.rsqrt(jnp.mean(x * x, axis=-1, keepdims=True) + NORM_EPS) * w


def _chunk_tril(r):
    i = lax.broadcasted_iota(jnp.int32, (r, r), 0)
    j = lax.broadcasted_iota(jnp.int32, (r, r), 1)
    return jnp.where((j <= i) & ((i // CHUNK) == (j // CHUNK)), 1.0, 0.0).astype(BF16)


def _rmsnorm_kernel(x_ref, w_ref, o_ref):
    o_ref[...] = _rms(x_ref[...], w_ref[...]).astype(o_ref.dtype)


def _rmsnorm(x, w, out_dtype, tm):
    m, d = x.shape
    tm = min(tm, m)
    return pl.pallas_call(
        _rmsnorm_kernel, grid=(m // tm,),
        in_specs=[pl.BlockSpec((tm, d), lambda i: (i, 0)), pl.BlockSpec((1, d), lambda i: (0, 0))],
        out_specs=pl.BlockSpec((tm, d), lambda i: (i, 0)),
        out_shape=jax.ShapeDtypeStruct((m, d), out_dtype),
        compiler_params=_params("parallel"), name="rmsnorm")(x, w)


def _weight_spec(w, layer, block, index):
    if w.ndim == 2:
        return pl.BlockSpec(block, index)
    return pl.BlockSpec((None,) + block, lambda *g: (layer,) + index(*g))


def _mm_kernel(a_ref, w_ref, o_ref, *wb_ref):
    w = w_ref[...].astype(BF16)
    if wb_ref:
        wb_ref[0][...] = w
    o_ref[...] = _dot(a_ref[...], w).astype(o_ref.dtype)


def _matmul(a, w, layer, out_dtype, tm, tn, emit_wb=False, col_start=0, n_out=None):
    m, k = a.shape
    n = w.shape[-1] if n_out is None else n_out
    tm, tn = min(tm, m), min(tn, n)
    assert m % tm == 0 and n % tn == 0 and col_start % tn == 0 and (not emit_wb or (m == tm and n_out is None))
    c0 = col_start // tn
    out_shape = [jax.ShapeDtypeStruct((m, n), out_dtype)]
    out_specs = [pl.BlockSpec((tm, tn), lambda i, j: (i, j))]
    if emit_wb:
        out_shape.append(jax.ShapeDtypeStruct((k, n), BF16))
        out_specs.append(pl.BlockSpec((k, tn), lambda i, j: (0, j)))
    res = pl.pallas_call(
        _mm_kernel, grid=(m // tm, n // tn),
        in_specs=[pl.BlockSpec((tm, k), lambda i, j: (i, 0)),
                  _weight_spec(w, layer, (k, tn), lambda i, j: (0, j + c0))],
        out_specs=out_specs, out_shape=out_shape,
        compiler_params=_params("parallel", "parallel"), name="matmul")(a, w)
    return (res[0], res[1]) if emit_wb else res[0]


def _swiglu_kernel(a_ref, wg_ref, wu_ref, o_ref, *wb_refs):
    a = a_ref[...].astype(BF16)
    wg = wg_ref[...].astype(BF16)
    wu = wu_ref[...].astype(BF16)
    if wb_refs:
        wb_refs[0][...] = wg
        wb_refs[1][...] = wu
    g = _dot(a, wg)
    u = _dot(a, wu)
    o_ref[...] = (_silu(g) * u).astype(o_ref.dtype)


def _swiglu_up(a, w_gate, w_up, layer, out_dtype, tm, tn, emit_wb=False):
    m, k = a.shape
    stacked = w_up.ndim == 3
    f = w_up.shape[-1] // 2 if stacked else w_up.shape[-1]
    tm, tn = min(tm, m), min(tn, f)
    assert m % tm == 0 and f % tn == 0 and (not emit_wb or m == tm)
    nf = f // tn
    up_off = nf if stacked else 0
    out_shape = [jax.ShapeDtypeStruct((m, f), out_dtype)]
    out_specs = [pl.BlockSpec((tm, tn), lambda i, j: (i, j))]
    if emit_wb:
        out_shape += [jax.ShapeDtypeStruct((k, f), BF16)] * 2
        out_specs += [pl.BlockSpec((k, tn), lambda i, j: (0, j))] * 2
    res = pl.pallas_call(
        _swiglu_kernel, grid=(m // tm, nf),
        in_specs=[pl.BlockSpec((tm, k), lambda i, j: (i, 0)),
                  _weight_spec(w_gate, layer, (k, tn), lambda i, j: (0, j)),
                  _weight_spec(w_up, layer, (k, tn), lambda i, j: (0, j + up_off))],
        out_specs=out_specs, out_shape=out_shape,
        compiler_params=_params("parallel", "parallel"), name="swiglu_up")(a, w_gate, w_up)
    return tuple(res) if emit_wb else res[0]


def _residual_epilogue(mix, x_ref, nw1_ref, nw2_ref, xo_ref, h_ref):
    xn = x_ref[...] + _rms(mix, nw1_ref[...])
    xo_ref[...] = xn
    if h_ref:
        h_ref[0][...] = _rms(xn, nw2_ref[...]).astype(h_ref[0].dtype)


def _mm_res_kernel(a_ref, w_ref, x_ref, nw1_ref, nw2_ref, xo_ref, *rest, nk, emit_h, emit_wb):
    h_ref = rest[:1] if emit_h else ()
    kk = pl.program_id(1)
    w = w_ref[...].astype(BF16)
    if emit_wb:
        rest[-1][...] = w
    part = _dot(a_ref[...], w)

    @pl.when(kk == 0)
    def _():
        xo_ref[...] = part

    @pl.when(kk > 0)
    def _():
        xo_ref[...] += part

    @pl.when(kk == nk - 1)
    def _():
        _residual_epilogue(xo_ref[...], x_ref, nw1_ref, nw2_ref, xo_ref, h_ref)


def _mm_res_resident_kernel(a_ref, w_ref, x_ref, nw1_ref, nw2_ref, xo_ref, *h_ref):
    _residual_epilogue(_dot(a_ref[...], w_ref[...]), x_ref, nw1_ref, nw2_ref, xo_ref, h_ref)


def _matmul_residual(a, w, layer, x, nw_post, nw_next, h_dtype, tm, tk, emit_wb=False):
    m, k = a.shape
    n = w.shape[-1]
    tm = min(tm, m)
    assert m % tm == 0 and (not emit_wb or m == tm)
    emit_h = nw_next is not None
    if not emit_h:
        nw_next = nw_post
    resident = w.ndim == 2
    if resident:
        grid = (m // tm,)
        row = lambda i: (i, 0)
        fixed = lambda i: (0, 0)
        a_spec = pl.BlockSpec((tm, k), row)
        w_spec = pl.BlockSpec((k, n), fixed, pipeline_mode=pl.Buffered(1))
        x_spec = pl.BlockSpec((tm, n), row)
        body = _mm_res_resident_kernel
        sem = ("parallel",)
    else:
        tk = min(tk, k)
        assert k % tk == 0
        nk = k // tk
        grid = (m // tm, nk)
        row = lambda i, kk: (i, 0)
        fixed = lambda i, kk: (0, 0)
        a_spec = pl.BlockSpec((tm, tk), lambda i, kk: (i, kk))
        w_spec = pl.BlockSpec((None, tk, n), lambda i, kk: (layer, kk, 0))
        x_spec = pl.BlockSpec((tm, n), row, pipeline_mode=pl.Buffered(1))
        body = functools.partial(_mm_res_kernel, nk=nk, emit_h=emit_h, emit_wb=emit_wb)
        sem = ("parallel", "arbitrary")
    out_shape = [jax.ShapeDtypeStruct((m, n), F32)]
    out_specs = [pl.BlockSpec((tm, n), row)]
    if emit_h:
        out_shape.append(jax.ShapeDtypeStruct((m, n), h_dtype))
        out_specs.append(pl.BlockSpec((tm, n), row))
    if emit_wb:
        out_shape.append(jax.ShapeDtypeStruct((k, n), BF16))
        out_specs.append(pl.BlockSpec((tk, n), lambda i, kk: (kk, 0)))
    res = pl.pallas_call(
        body, grid=grid,
        in_specs=[a_spec, w_spec, x_spec, pl.BlockSpec((1, n), fixed), pl.BlockSpec((1, n), fixed)],
        out_specs=out_specs, out_shape=out_shape,
        compiler_params=_params(*sem, vmem=VMEM_LIMIT_BIG_BYTES), name="matmul_residual")(
            a, w, x, nw_post, nw_next)
    xo = res[0]
    h = res[1] if emit_h else None
    return (xo, h, res[-1]) if emit_wb else (xo, h)


def _gdn_gates_kernel(a_ref, w_ref, alog_ref, dtb_ref, beta_ref, g_ref):
    ba_t = lax.dot_general(w_ref[...].astype(BF16), a_ref[...].astype(BF16), (((0,), (1,)), ((), ())),
                           preferred_element_type=F32)
    nh = beta_ref.shape[0]
    beta_ref[...] = _sigmoid(ba_t[:nh])
    g_ref[...] = -jnp.exp(alog_ref[...]) * _softplus(ba_t[nh:] + dtb_ref[...])


def _gdn_gates(a, w_ba, layer, a_log_col, dt_bias_col, tm):
    m, k = a.shape
    nh = GDN_V_HEADS
    tm = min(tm, m)
    return pl.pallas_call(
        _gdn_gates_kernel, grid=(m // tm,),
        in_specs=[pl.BlockSpec((tm, k), lambda i: (i, 0)),
                  pl.BlockSpec((None, k, 2 * nh), lambda i: (layer, 0, 0)),
                  pl.BlockSpec((nh, 1), lambda i: (0, 0)),
                  pl.BlockSpec((nh, 1), lambda i: (0, 0))],
        out_specs=[pl.BlockSpec((nh, tm), lambda i: (0, i))] * 2,
        out_shape=[jax.ShapeDtypeStruct((nh, m), F32)] * 2,
        compiler_params=_params("parallel"), name="gdn_gates")(a, w_ba, a_log_col, dt_bias_col)


def _conv_silu_norm(x, halo, ext_scr, cw_ref, j, n_qk_tiles):
    tt, tc = x.shape
    ext_scr[:SUBLANES, :] = halo
    ext_scr[SUBLANES:, :] = x
    y = x * cw_ref[GDN_CONV_W - 1:GDN_CONV_W, :]
    for s in range(1, GDN_CONV_W):
        y = y + ext_scr[SUBLANES - s:SUBLANES - s + tt, :] * cw_ref[GDN_CONV_W - 1 - s:GDN_CONV_W - s, :]
    y = _silu(y)
    qscale = jnp.where(j < n_qk_tiles, GDN_HEAD_DIM ** -0.5, 1.0)
    pieces = []
    for hh in range(tc // GDN_HEAD_DIM):
        yh = y[:, hh * GDN_HEAD_DIM:(hh + 1) * GDN_HEAD_DIM]
        r = lax.rsqrt(jnp.sum(yh * yh, axis=-1, keepdims=True) + NORM_EPS) * qscale
        pieces.append(yh * jnp.where(j < 2 * n_qk_tiles, r, 1.0))
    return jnp.concatenate(pieces, axis=1)


def _gdn_prep_kernel(x_ref, prev_ref, buf_ref, cw_ref, o_ref, ext_scr, *, n_qk_tiles):
    i = pl.program_id(1)
    j = pl.program_id(2)
    halo = jnp.where(i == 0, buf_ref[...], prev_ref[...])
    o_ref[...] = _conv_silu_norm(x_ref[...], halo, ext_scr, cw_ref, j, n_qk_tiles)


def _gdn_in_proj_kernel(a_ref, w_ref, buf_ref, cw_ref, o_ref, tail_ref, ext_scr, halo_scr, *,
                        tiles_per_seq, n_qk_tiles):
    i = pl.program_id(0)
    j = pl.program_id(1)

    @pl.when((i == 0) & (j == 0))
    def _():
        halo_scr[...] = jnp.zeros_like(halo_scr)

    halo = jnp.where(i % tiles_per_seq == 0, buf_ref[...], halo_scr[j])
    w = w_ref[...]
    n_sub, rows = ext_scr.shape[0], ext_scr.shape[1] - SUBLANES
    for s in range(n_sub):
        sl = slice(s * rows, (s + 1) * rows)
        x = _dot(a_ref[sl, :], w)
        o_ref[sl, :] = _conv_silu_norm(x, halo, ext_scr.at[s], cw_ref, j, n_qk_tiles)
        halo = x[rows - SUBLANES:, :]
    halo_scr[j] = halo
    tail_ref[...] = halo


def _gdn_in_proj(a, w, buf8, conv_w, seq_len, tm, tn):
    m, k = a.shape
    n = GDN_CONV_DIM
    tm = min(tm, seq_len)
    assert w.ndim == 2 and seq_len % tm == 0 and GDN_QK_DIM % tn == 0
    assert tm % (GDN_PROJ_SUBBLOCKS * SUBLANES) == 0
    tps = seq_len // tm
    proj, tails = pl.pallas_call(
        functools.partial(_gdn_in_proj_kernel, tiles_per_seq=tps, n_qk_tiles=GDN_QK_DIM // tn),
        grid=(m // tm, n // tn),
        in_specs=[pl.BlockSpec((tm, k), lambda i, j: (i, 0)),
                  pl.BlockSpec((k, tn), lambda i, j: (0, j)),
                  pl.BlockSpec((None, SUBLANES, tn), lambda i, j: (i // tps, 0, j)),
                  pl.BlockSpec((GDN_CONV_W, tn), lambda i, j: (0, j))],
        out_specs=[pl.BlockSpec((tm, tn), lambda i, j: (i, j)),
                   pl.BlockSpec((None, SUBLANES, tn), lambda i, j: (i, 0, j))],
        out_shape=[jax.ShapeDtypeStruct((m, n), F32), jax.ShapeDtypeStruct((m // tm, SUBLANES, n), F32)],
        scratch_shapes=[pltpu.VMEM((GDN_PROJ_SUBBLOCKS, SUBLANES + tm // GDN_PROJ_SUBBLOCKS, tn), F32),
                        pltpu.VMEM((n // tn, SUBLANES, tn), F32)],
        compiler_params=_params("arbitrary", "arbitrary", vmem=VMEM_LIMIT_BIG_BYTES), name="gdn_in_proj")(
            a, w, buf8, conv_w)
    return proj, tails[tps - 1::tps]


def _gdn_prep(proj3, buf8, conv_w, tt, tc):
    b, t, _ = proj3.shape
    tt = min(tt, t)
    assert t % tt == 0 and tt % SUBLANES == 0 and GDN_QK_DIM % tc == 0
    hb = tt // SUBLANES
    return pl.pallas_call(
        functools.partial(_gdn_prep_kernel, n_qk_tiles=GDN_QK_DIM // tc),
        grid=(b, t // tt, GDN_CONV_DIM // tc),
        in_specs=[pl.BlockSpec((None, tt, tc), lambda bb, i, j: (bb, i, j)),
                  pl.BlockSpec((None, SUBLANES, tc), lambda bb, i, j: (bb, jnp.maximum(i * hb - 1, 0), j)),
                  pl.BlockSpec((None, SUBLANES, tc), lambda bb, i, j: (bb, 0, j)),
                  pl.BlockSpec((GDN_CONV_W, tc), lambda bb, i, j: (0, j))],
        out_specs=pl.BlockSpec((None, tt, tc), lambda bb, i, j: (bb, i, j)),
        out_shape=jax.ShapeDtypeStruct((b, t, GDN_CONV_DIM), F32),
        scratch_shapes=[pltpu.VMEM((SUBLANES + tt, tc), F32)],
        compiler_params=_params("parallel", "parallel", "parallel"), name="gdn_prep")(
            proj3, proj3, buf8, conv_w)


def _gdn_delta_kernel(q_ref, k_ref, v_ref, z_ref, gr_ref, br_ref, s0_ref, nw_ref,
                      o_ref, so_ref, s_scr, *, cb, hpb, t_valid, n_blocks):
    n = pl.program_id(2)
    r = cb * CHUNK
    dh = GDN_HEAD_DIM

    @pl.when(n == 0)
    def _():
        s_scr[...] = s0_ref[...]

    q = q_ref[...]
    k = k_ref[...]
    v = v_ref[...]
    grow = gr_ref[...]
    brow = br_ref[...]
    if t_valid is not None:
        okc = (n * r + lax.broadcasted_iota(jnp.int32, (r, 1), 0)) < t_valid
        okr = (n * r + lax.broadcasted_iota(jnp.int32, (1, r), 1)) < t_valid
        k = jnp.where(okc, k, 0.0)
        v = jnp.where(okc, v, 0.0)
        grow = jnp.where(okr, grow, 0.0)
        brow = jnp.where(okr, brow, 0.0)

    ri = lax.broadcasted_iota(jnp.int32, (r, r), 0)
    ci = lax.broadcasted_iota(jnp.int32, (r, r), 1)
    eye = jnp.where(ri == ci, 1.0, 0.0)
    lm = _chunk_tril(r)
    eyeb = eye.astype(BF16)
    g_all = _split3(jnp.concatenate([grow[qh] for qh in range(hpb)], axis=0))
    b_all = _split3(jnp.concatenate([brow[qh] for qh in range(hpb)], axis=0))
    gcum_col = sum(_dot_nt(lm, p) for p in g_all)
    gcum_row = sum(_dot_nt(p, lm) for p in g_all)
    bcol = sum(_dot_nt(eyeb, p) for p in b_all)
    tril = ((ri // CHUNK) == (ci // CHUNK)) & (ri >= ci)
    lvl_masks = []
    size = 1
    while size < CHUNK:
        lvl_masks.append(((ri // (2 * size)) == (ci // (2 * size)))
                         & ((ri // size) % 2 == 1) & ((ci // size) % 2 == 0))
        size *= 2
    if t_valid is not None and t_valid <= 1:
        lvl_masks = []

    nw = nw_ref[...]
    heads = [(qh, hh) for qh in range(hpb) for hh in range(2)]
    kq = [k[:, qh * dh:(qh + 1) * dh] for qh in range(hpb)]
    qq = [q[:, qh * dh:(qh + 1) * dh] for qh in range(hpb)]
    gram = [_dot_nt(kq[qh], kq[qh]) for qh in range(hpb)]
    qk = [_dot_nt(qq[qh], kq[qh]) for qh in range(hpb)]
    gc = [gcum_col[:, i:i + 1] for i in range(len(heads))]
    bc = [bcol[:, i:i + 1] for i in range(len(heads))]
    decay = [jnp.exp(jnp.where(tril, gc[i] - gcum_row[i:i + 1, :], NEG)) for i in range(len(heads))]
    mm_ = [(gram[qh] * bc[i] * decay[i]).astype(BF16) for i, (qh, hh) in enumerate(heads)]
    tinv = [eye.astype(BF16) for _ in heads]
    for lvl, msk in enumerate(lvl_masks):
        mskb = jnp.where(msk, 1.0, 0.0).astype(BF16)
        ml = [m * mskb for m in mm_]
        if lvl == 0:
            tinv = [t - m for t, m in zip(tinv, ml)]
        else:
            y = [_dot(m, t) for t, m in zip(tinv, ml)]
            tinv = [t - _dot(t, yy).astype(BF16) for t, yy in zip(tinv, y)]
    eg = [jnp.exp(g_) for g_ in gc]
    sol = [_dot(tinv[i], jnp.concatenate([v[:, (2 * qh + hh) * dh:(2 * qh + hh + 1) * dh] * bc[i],
                                          kq[qh] * (bc[i] * eg[i])], axis=1))
           for i, (qh, hh) in enumerate(heads)]
    asol = [_dot(qk[qh] * decay[i], sol[i]) for i, (qh, hh) in enumerate(heads)]
    o2 = [a[:, :dh] for a in asol]
    o1 = [qq[qh] * eg[i] - asol[i][:, dh:] for i, (qh, hh) in enumerate(heads)]
    glast = [[g_[(c + 1) * CHUNK - 1:(c + 1) * CHUNK, :] for c in range(cb)] for g_ in gc]
    kd = [kq[qh] * jnp.exp(jnp.concatenate([jnp.broadcast_to(gl, (CHUNK, 1)) for gl in glast[i]], axis=0) - gc[i])
          for i, (qh, hh) in enumerate(heads)]
    qw = [[_dot_tn(kd[i][c * CHUNK:(c + 1) * CHUNK], sol[i][c * CHUNK:(c + 1) * CHUNK]) for c in range(cb)]
          for i in range(len(heads))]
    s = [s_scr[2 * qh + hh] for qh, hh in heads]
    for c in range(cb):
        sl = slice(c * CHUNK, (c + 1) * CHUNK)
        for i, (qh, hh) in enumerate(heads):
            x = _dot(jnp.concatenate([qw[i][c][:, dh:], o1[i][sl]], axis=0), s[i])
            o = x[dh:] + o2[i][sl]
            s[i] = s[i] * jnp.exp(glast[i][c]) - x[:dh] + qw[i][c][:, :dh]
            col = slice((2 * qh + hh) * dh, (2 * qh + hh + 1) * dh)
            zc = z_ref[sl, col]
            o_ref[sl, col] = (_rms(o, nw) * _silu(zc)).astype(o_ref.dtype)
    for i, (qh, hh) in enumerate(heads):
        s_scr[2 * qh + hh] = s[i]

    @pl.when(n == n_blocks - 1)
    def _():
        so_ref[...] = s_scr[...]


def _gdn_delta(qkv3, proj3, grow, brow, s0, norm_w, out_dtype, cb, hpb, t_valid):
    b, t, _ = qkv3.shape
    r = cb * CHUNK
    assert t % r == 0
    nb = t // r
    dh = GDN_HEAD_DIM
    qw_, vw_ = hpb * dh, 2 * hpb * dh
    assert GDN_QK_HEADS % hpb == 0
    koff = GDN_QK_DIM // qw_
    voff = 2 * GDN_QK_DIM // vw_
    zoff = (proj3.shape[-1] - GDN_V_DIM) // vw_
    return pl.pallas_call(
        functools.partial(_gdn_delta_kernel, cb=cb, hpb=hpb, t_valid=t_valid, n_blocks=nb),
        grid=(b, GDN_QK_HEADS // hpb, nb),
        in_specs=[pl.BlockSpec((None, r, qw_), lambda bb, h, n: (bb, n, h)),
                  pl.BlockSpec((None, r, qw_), lambda bb, h, n: (bb, n, koff + h)),
                  pl.BlockSpec((None, r, vw_), lambda bb, h, n: (bb, n, voff + h)),
                  pl.BlockSpec((None, r, vw_), lambda bb, h, n: (bb, n, zoff + h)),
                  pl.BlockSpec((None, hpb, 2, r), lambda bb, h, n: (bb, h, 0, n)),
                  pl.BlockSpec((None, hpb, 2, r), lambda bb, h, n: (bb, h, 0, n)),
                  pl.BlockSpec((None, 2 * hpb, dh, dh), lambda bb, h, n: (bb, h, 0, 0)),
                  pl.BlockSpec((1, dh), lambda bb, h, n: (0, 0))],
        out_specs=[pl.BlockSpec((None, r, vw_), lambda bb, h, n: (bb, n, h)),
                   pl.BlockSpec((None, 2 * hpb, dh, dh), lambda bb, h, n: (bb, h, 0, 0))],
        out_shape=[jax.ShapeDtypeStruct((b, t, GDN_V_DIM), out_dtype),
                   jax.ShapeDtypeStruct(s0.shape, F32)],
        scratch_shapes=[pltpu.VMEM((2 * hpb, dh, dh), F32)],
        compiler_params=_params("parallel", "parallel", "arbitrary"), name="gdn_delta")(
            qkv3, qkv3, qkv3, proj3, grow, brow, s0, norm_w)


def _gla_kernel(q_ref, k_ref, v_ref, gate_ref, gk_ref, bgk_ref, s0_ref, nw_ref,
                o_ref, so_ref, s_scr, *, cb, t_valid, n_blocks):
    n = pl.program_id(2)
    r = cb * CHUNK
    dk, dv = GLA_DK, GLA_DV

    @pl.when(n == 0)
    def _():
        s_scr[...] = s0_ref[...]

    xg = gk_ref[...] + bgk_ref[...]
    g = (jnp.minimum(xg, 0.0) - jnp.log(1.0 + jnp.exp(-jnp.abs(xg)))) * (1.0 / GLA_GATE_NORMALIZER)
    q = q_ref[...] * GLA_DK ** -0.5
    k = k_ref[...]
    v = v_ref[...]
    if t_valid is not None:
        okc = (n * r + lax.broadcasted_iota(jnp.int32, (r, 1), 0)) < t_valid
        g = jnp.where(okc, g, 0.0)
        k = jnp.where(okc, k, 0.0)
        v = jnp.where(okc, v, 0.0)

    gparts = _split3(g)
    bcum = sum(_dot(_chunk_tril(r), p) for p in gparts)
    ones = jnp.ones((CHUNK, LANES), BF16)
    row = lax.broadcasted_iota(jnp.int32, (CHUNK, 1), 0)
    jrow = lax.broadcasted_iota(jnp.int32, (SUB, 1), 0)
    lane = lax.broadcasted_iota(jnp.int32, (SUB, CHUNK), 1)
    nw = nw_ref[...]
    s = s_scr[...]
    for c in range(cb):
        sl = slice(c * CHUNK, (c + 1) * CHUNK)
        qc, kc, vc, bc = q[sl], k[sl], v[sl], bcum[sl]
        blast = bc[CHUNK - 1:CHUNK, :]
        attn_t = jnp.zeros((CHUNK, CHUNK), F32)
        for bi in range(1, CHUNK // SUB):
            bref = bc[bi * SUB:bi * SUB + 1, :]
            k_i = jnp.where(row < bi * SUB, kc * jnp.exp(jnp.minimum(bref - bc, 0.0)), 0.0)
            q_i = jnp.where((row >= bi * SUB) & (row < (bi + 1) * SUB),
                            qc * jnp.exp(jnp.minimum(bc - bref, 0.0)), 0.0)
            attn_t = attn_t + _dot_nt(k_i, q_i)
        diag = []
        for bi in range(CHUNK // SUB):
            sb = slice(bi * SUB, (bi + 1) * SUB)
            qb, kb, bb = qc[sb], kc[sb], bc[sb]
            d = jnp.zeros((SUB, CHUNK), F32)
            for il in range(SUB):
                e = jnp.exp(jnp.minimum(bb[il:il + 1, :] - bb, 0.0))
                col = jnp.sum(kb * e * qb[il:il + 1, :], axis=-1, keepdims=True)
                col = jnp.where(jrow <= il, col, 0.0)
                d = jnp.where(lane == bi * SUB + il, col, d)
            diag.append(d)
        attn_t = attn_t + jnp.concatenate(diag, axis=0)
        o = _dot(qc * jnp.exp(bc), s) + _dot_tn(attn_t, vc)
        bl_col = sum(_dot_tn(p[sl], ones) for p in gparts)
        decay_col = jnp.concatenate([jnp.exp(bl_col)] * (dv // LANES), axis=1)
        s = s * decay_col + _dot_tn(kc * jnp.exp(blast - bc), vc)
        gt = gate_ref[sl, :]
        o_ref[sl, :] = (_rms(o, nw) * _silu(gt)).astype(o_ref.dtype)
    s_scr[...] = s

    @pl.when(n == n_blocks - 1)
    def _():
        so_ref[...] = s_scr[...]


def _gla_chunks(qkvg3, gk3, b_gk, s0, norm_w, out_dtype, cb, t_valid):
    b, t, _ = qkvg3.shape
    r = cb * CHUNK
    assert t % r == 0
    nb = t // r
    dk, dv = GLA_DK, GLA_DV
    koff = GLA_HEADS
    voff = 2 * GLA_HEADS * dk // dv
    goff = voff + GLA_HEADS
    return pl.pallas_call(
        functools.partial(_gla_kernel, cb=cb, t_valid=t_valid, n_blocks=nb),
        grid=(b, GLA_HEADS, nb),
        in_specs=[pl.BlockSpec((None, r, dk), lambda bb, h, n: (bb, n, h)),
                  pl.BlockSpec((None, r, dk), lambda bb, h, n: (bb, n, koff + h)),
                  pl.BlockSpec((None, r, dv), lambda bb, h, n: (bb, n, voff + h)),
                  pl.BlockSpec((None, r, dv), lambda bb, h, n: (bb, n, goff + h)),
                  pl.BlockSpec((None, r, dk), lambda bb, h, n: (bb, n, h)),
                  pl.BlockSpec((1, dk), lambda bb, h, n: (0, h)),
                  pl.BlockSpec((None, None, dk, dv), lambda bb, h, n: (bb, h, 0, 0)),
                  pl.BlockSpec((1, dv), lambda bb, h, n: (0, 0))],
        out_specs=[pl.BlockSpec((None, r, dv), lambda bb, h, n: (bb, n, h)),
                   pl.BlockSpec((None, None, dk, dv), lambda bb, h, n: (bb, h, 0, 0))],
        out_shape=[jax.ShapeDtypeStruct((b, t, GLA_HEADS * dv), out_dtype),
                   jax.ShapeDtypeStruct(s0.shape, F32)],
        scratch_shapes=[pltpu.VMEM((dk, dv), F32)],
        compiler_params=_params("parallel", "parallel", "arbitrary"), name="gla_chunks")(
            qkvg3, qkvg3, qkvg3, qkvg3, gk3, b_gk, s0, norm_w)


def _t5_bucket(n):
    n = np.asarray(n)
    max_exact = N_BUCKETS // 2
    nf = np.maximum(n, max_exact).astype(np.float32)
    large = max_exact + (np.log(nf / max_exact) / math.log(MAX_DISTANCE / max_exact)
                         * (N_BUCKETS - max_exact)).astype(np.int32)
    return np.where(n < max_exact, n, np.minimum(large, N_BUCKETS - 1)).astype(np.int32)


def _lambda(lam_ref, lam_init):
    lf = lam_ref[...]
    s1 = jnp.sum(lf[0:1] * lf[1:2], axis=-1, keepdims=True)
    s2 = jnp.sum(lf[2:3] * lf[3:4], axis=-1, keepdims=True)
    return jnp.exp(s1) - jnp.exp(s2) + lam_init


_FAR_FAR, _FAR_SUB, _SUB_DIAG, _DIAG = range(4)


def _flash_kernel(qi_ref, ki_ref, kind_ref, q_ref, k_ref, v_ref, bd_ref, bs_ref, rb_ref, lam_ref, sub_ref, o_ref,
                  q_scr, m_scr, l_scr, acc_scr, bias_scr, *, lam_init):
    h = pl.program_id(0)
    step = pl.program_id(2)
    ki = ki_ref[step]
    kind = kind_ref[step]
    tb = q_ref.shape[0]
    dh = DIFF_HEAD_DIM

    @pl.when((pl.program_id(1) == 0) & (step == 0))
    def _():
        ri = lax.broadcasted_iota(jnp.int32, (tb, tb), 0)
        ci = lax.broadcasted_iota(jnp.int32, (tb, tb), 1)
        for mm in range(2):
            bd = jnp.zeros((tb, tb), F32)
            bs = jnp.zeros((tb, tb), F32)
            for bk in range(N_BUCKETS):
                val = rb_ref[bk, 2 * h + mm] * LOG2E
                bd = jnp.where(bd_ref[...] == bk, val, bd)
                bs = jnp.where(bs_ref[...] == bk, val, bs)
            bias_scr[mm, 0] = jnp.where(ri > ci, NEG, bd)
            bias_scr[mm, 1] = bs

    @pl.when(ki == 0)
    def _():
        q_scr[...] = (q_ref[...] * (dh ** -0.5 * LOG2E)).astype(q_scr.dtype)
        m_scr[...] = jnp.full_like(m_scr, NEG)
        l_scr[...] = jnp.zeros_like(l_scr)
        acc_scr[...] = jnp.zeros_like(acc_scr)

    def update(mm, s, shift, vt):
        m_prev = m_scr[mm]
        if shift.ndim == 0:
            m_new = jnp.maximum(m_prev, jnp.max(s, axis=0, keepdims=True) + shift)
            p = jnp.exp2(s - (m_new - shift))
        else:
            s = s + shift
            m_new = jnp.maximum(m_prev, jnp.max(s, axis=0, keepdims=True))
            p = jnp.exp2(s - m_new)
        alpha = jnp.exp2(m_prev - m_new)
        l_scr[mm] = alpha * l_scr[mm] + jnp.sum(p, axis=0, keepdims=True)
        acc_scr[mm] = alpha * acc_scr[mm] + _dot(vt, p)
        m_scr[mm] = m_new

    far = lambda mm: rb_ref[N_BUCKETS - 1, 2 * h + mm] * LOG2E
    sub = lambda mm: bias_scr[mm, 1]
    diag = lambda mm: bias_scr[mm, 0]

    def both_maps(shifts):
        rows = [pl.ds(pl.multiple_of((ki + n) * tb, tb), tb) for n in range(len(shifts))]
        vt = jnp.concatenate([v_ref[r, :].T.astype(BF16) for r in rows], axis=1)
        s = [[_dot_nt(k_ref[r, mm * dh:(mm + 1) * dh], q_scr[:, mm * dh:(mm + 1) * dh]) for r in rows]
             for mm in range(2)]
        for mm in range(2):
            sh = [f(mm) for f in shifts]
            if all(x.ndim == 0 for x in sh):
                update(mm, jnp.concatenate(s[mm], axis=0), sh[0], vt)
            else:
                update(mm, jnp.concatenate([a + x for a, x in zip(s[mm], sh)], axis=0), jnp.float32(0.0), vt)

    @pl.when(kind == _FAR_FAR)
    def _():
        both_maps([far, far])

    @pl.when(kind == _FAR_SUB)
    def _():
        both_maps([far, sub])

    @pl.when(kind == _SUB_DIAG)
    def _():
        both_maps([sub, diag])

    @pl.when(kind == _DIAG)
    def _():
        both_maps([diag])

    @pl.when(kind >= _SUB_DIAG)
    def _():
        lam = _lambda(lam_ref, lam_init)
        out = acc_scr[0] / l_scr[0] - lam * (acc_scr[1] / l_scr[1])
        out = out * lax.rsqrt(jnp.mean(out * out, axis=0, keepdims=True) + NORM_EPS)
        o_ref[...] = (out * (sub_ref[...] * (1.0 - lam_init))).T.astype(o_ref.dtype)


def _flash_diff(qkv3, rel_bias, lambdas, subln_col, lam_init, out_dtype, tb):
    b, t, _ = qkv3.shape
    tb = min(tb, t)
    assert t % tb == 0 and tb >= MAX_DISTANCE
    nq = t // tb
    dd = 2 * DIFF_HEAD_DIM
    j = np.arange(tb)[:, None]
    i = np.arange(tb)[None, :]
    bkt_diag = jnp.asarray(_t5_bucket(np.maximum(i - j, 0)))
    bkt_sub = jnp.asarray(_t5_bucket(tb + i - j))
    koff = D_MODEL // dd
    kinds = {"FF": _FAR_FAR, "FS": _FAR_SUB, "SD": _SUB_DIAG, "D": _DIAG}
    pairs = []
    for qi in range(nq):
        tags = "F" * (qi - 1) + ("S" if qi >= 1 else "") + "D"
        for ki in range(0, qi + 1, 2):
            pairs.append((qi, ki, kinds[tags[ki:ki + 2]]))
    qi_tbl = jnp.asarray([pq for pq, _, _ in pairs], jnp.int32)
    ki_tbl = jnp.asarray([pk for _, pk, _ in pairs], jnp.int32)
    kind_tbl = jnp.asarray([kd for _, _, kd in pairs], jnp.int32)
    grid_spec = pltpu.PrefetchScalarGridSpec(
        num_scalar_prefetch=3, grid=(DIFF_HEADS, b, len(pairs)),
        in_specs=[pl.BlockSpec((None, tb, dd), lambda h, bb, s, qt, kt, kd: (bb, qt[s], h)),
                  pl.BlockSpec((None, t, dd), lambda h, bb, s, qt, kt, kd: (bb, 0, koff + h)),
                  pl.BlockSpec((None, t, dd), lambda h, bb, s, qt, kt, kd: (bb, 0, 2 * koff + h)),
                  pl.BlockSpec((tb, tb), lambda h, bb, s, qt, kt, kd: (0, 0)),
                  pl.BlockSpec((tb, tb), lambda h, bb, s, qt, kt, kd: (0, 0)),
                  pl.BlockSpec(memory_space=pltpu.SMEM),
                  pl.BlockSpec((4, DIFF_HEAD_DIM), lambda h, bb, s, qt, kt, kd: (0, 0)),
                  pl.BlockSpec((dd, 1), lambda h, bb, s, qt, kt, kd: (0, 0))],
        out_specs=pl.BlockSpec((None, tb, dd), lambda h, bb, s, qt, kt, kd: (bb, qt[s], h)),
        scratch_shapes=[pltpu.VMEM((tb, dd), BF16),
                        pltpu.VMEM((2, 1, tb), F32), pltpu.VMEM((2, 1, tb), F32),
                        pltpu.VMEM((2, dd, tb), F32), pltpu.VMEM((2, 2, tb, tb), F32)])
    return pl.pallas_call(
        functools.partial(_flash_kernel, lam_init=lam_init), grid_spec=grid_spec,
        out_shape=jax.ShapeDtypeStruct((b, t, D_MODEL), out_dtype),
        compiler_params=_params("parallel", "arbitrary", "arbitrary"), name="flash_diff")(
            qi_tbl, ki_tbl, kind_tbl, qkv3, qkv3, qkv3, bkt_diag, bkt_sub, rel_bias, lambdas, subln_col)


def _decode_kernel(pt_ref, q_ref, *refs, pp, n_steps, lam_init):
    kp_refs, vp_refs = refs[:pp], refs[pp:2 * pp]
    kn_ref, vn_ref, bkt_ref, tbl_ref, lam_ref, sub_ref, o_ref, m_scr, l_scr, acc_scr, bias_scr, p_scr, a_scr = refs[2 * pp:]
    step = pl.program_id(1)
    nh, dh = DIFF_HEADS, DIFF_HEAD_DIM
    ones = jnp.ones((dh, LANES), BF16)
    qs = q_ref[...] * (dh ** -0.5 * LOG2E)

    @pl.when(step == 0)
    def _():
        m_scr[...] = jnp.full_like(m_scr, NEG)
        l_scr[...] = jnp.zeros_like(l_scr)
        acc_scr[...] = jnp.zeros_like(acc_scr)
        bkt = bkt_ref[...]
        bias = jnp.zeros(bkt.shape, F32)
        for bk in range(N_BUCKETS):
            bias = jnp.where(bkt == bk, (tbl_ref[bk] * LOG2E)[None], bias)
        bias_scr[...] = bias

    def lane_sum(x):
        return _dot(x, ones)

    def accumulate(pr, shift_row, values):
        m_prev = m_scr[...]
        if pr.ndim == 3:
            m_new = jnp.maximum(m_prev, jnp.max(pr, axis=0) + shift_row)
            p = jnp.exp2(pr - (m_new - shift_row)[None])
            psum = jnp.sum(p, axis=0)
        else:
            m_new = jnp.maximum(m_prev, pr + shift_row)
            p = jnp.exp2(pr - (m_new - shift_row))
            psum = p
        alpha = jnp.exp2(m_prev - m_new)
        l_scr[...] = alpha * l_scr[...] + psum
        m_scr[...] = m_new
        a_scr[...] = alpha
        if pr.ndim == 3:
            p_scr[...] = p
        else:
            p_scr[0] = p
        for par in range(2):
            ae = a_scr[pl.ds(par, nh, stride=2), :]
            if pr.ndim == 3:
                pe = p_scr[:, pl.ds(par, nh, stride=2), :]
                pv = jnp.sum(jnp.concatenate([pe, pe], axis=-1) * values(), axis=0)
            else:
                pe = p_scr[0, pl.ds(par, nh, stride=2), :]
                pv = jnp.concatenate([pe, pe], axis=-1) * values()
            acc_scr[par] = jnp.concatenate([ae, ae], axis=-1) * acc_scr[par] + pv

    def page_logits(j):
        prod = kp_refs[j][...] * qs[None]
        return lane_sum(prod.reshape(PAGE_SIZE * 2 * nh, dh)).reshape(PAGE_SIZE, 2 * nh, LANES)

    far = tbl_ref[N_BUCKETS - 1] * LOG2E
    zero = jnp.zeros_like(far)
    for j in range(pp - 1):
        accumulate(page_logits(j), far, lambda j=j: vp_refs[j][...])

    @pl.when(step < n_steps - 1)
    def _():
        accumulate(page_logits(pp - 1), far, lambda: vp_refs[pp - 1][...])

    @pl.when(step == n_steps - 1)
    def _():
        accumulate(page_logits(pp - 1) + bias_scr[...], zero, lambda: vp_refs[pp - 1][...])
        accumulate(lane_sum(kn_ref[...] * qs) + tbl_ref[0] * LOG2E, zero, lambda: vn_ref[...])
        l_scr_v = l_scr[...]
        a_scr[...] = l_scr_v
        outs = []
        for par in range(2):
            le = a_scr[pl.ds(par, nh, stride=2), :]
            outs.append(acc_scr[par] / jnp.concatenate([le, le], axis=-1))
        out = outs[0] - _lambda(lam_ref, lam_init) * outs[1]
        o_ref[...] = _rms(out, sub_ref[...]) * (1.0 - lam_init)


def _decode_diff(q, k_new, v_new, cache_k, cache_v, layer, page_table, rel_bias, lambdas, subln, lam_init, pp):
    b, n_pages = page_table.shape
    nh, dh = DIFF_HEADS, DIFF_HEAD_DIM
    pp = min(pp, n_pages)
    assert n_pages % pp == 0 and PAGE_SIZE >= MAX_DISTANCE and dh == LANES
    n_steps = n_pages // pp
    bkt_last = jnp.asarray(np.broadcast_to(
        _t5_bucket(PAGE_SIZE - np.arange(PAGE_SIZE)).reshape(PAGE_SIZE, 1, 1), (PAGE_SIZE, 2 * nh, LANES)))
    tbl = jnp.broadcast_to(rel_bias[:, :, None], (N_BUCKETS, 2 * nh, LANES))
    page_spec = lambda j, width, lanes: pl.BlockSpec(
        (None, None, PAGE_SIZE, width, lanes), lambda bb, s, pt: (layer, pt[bb, s * pp + j], 0, 0, 0))
    full = lambda shape: pl.BlockSpec(shape, lambda bb, s, pt: (0,) * len(shape))
    per_seq = lambda shape: pl.BlockSpec((None,) + shape, lambda bb, s, pt: (bb,) + (0,) * len(shape))
    grid_spec = pltpu.PrefetchScalarGridSpec(
        num_scalar_prefetch=1, grid=(b, n_steps),
        in_specs=([per_seq((2 * nh, dh))]
                  + [page_spec(j, 2 * nh, dh) for j in range(pp)]
                  + [page_spec(j, nh, 2 * dh) for j in range(pp)]
                  + [per_seq((2 * nh, dh)), per_seq((nh, 2 * dh)),
                     full((PAGE_SIZE, 2 * nh, LANES)), full((N_BUCKETS, 2 * nh, LANES)),
                     full((4, dh)), full((1, 2 * dh))]),
        out_specs=per_seq((nh, 2 * dh)),
        scratch_shapes=[pltpu.VMEM((2 * nh, LANES), F32), pltpu.VMEM((2 * nh, LANES), F32),
                        pltpu.VMEM((2, nh, 2 * dh), F32), pltpu.VMEM((PAGE_SIZE, 2 * nh, LANES), F32),
                        pltpu.VMEM((PAGE_SIZE, 2 * nh, LANES), F32), pltpu.VMEM((2 * nh, LANES), F32)])
    return pl.pallas_call(
        functools.partial(_decode_kernel, pp=pp, n_steps=n_steps, lam_init=lam_init),
        grid_spec=grid_spec,
        out_shape=jax.ShapeDtypeStruct((b, nh, 2 * dh), F32),
        compiler_params=_params("parallel", "arbitrary"), name="decode_diff")(
            page_table, q, *([cache_k] * pp), *([cache_v] * pp), k_new, v_new, bkt_last, tbl, lambdas, subln)


class _Tiles:
    def __init__(self, prompt):
        self.prompt = prompt
        self.act_dtype = BF16 if prompt else F32
        self.tm = 1024 if prompt else SUBLANES
        self.tn = 1024
        self.tn_ff = 512
        self.tm_ff = 1024
        self.tm_res = 256 if prompt else SUBLANES
        self.tk_res = 512
        self.tt_prep = 256 if prompt else SUBLANES
        self.tc_prep = 1024
        self.cb = 4 if prompt else 1
        self.gdn_hpb = 8 if prompt else 4
        self.tb_attn = 512
        self.decode_pages = 8


def _group_trunk(x3, tiles, gdn_s0, gdn_c0, gla_s0, cache_k, cache_v, page_table, p, wb):
    b, t, d = x3.shape
    m = b * t
    x = x3.reshape(m, d)
    if not tiles.prompt:
        assert t == 1 and m % SUBLANES == 0
    t_pad = t if tiles.prompt else CHUNK
    t_valid = None if tiles.prompt else t
    nw = lambda i, jn: p['norm_w'][i, jn].reshape(1, d)
    adt = tiles.act_dtype

    def pad_time(a3, to):
        return a3 if a3.shape[1] == to else jnp.pad(a3, ((0, 0), (0, to - a3.shape[1]), (0, 0)))

    def big_matmul(a, name, layer):
        if tiles.prompt:
            return _matmul(a, wb[name, layer], layer, F32, tiles.tm, tiles.tn)
        out, wb[name, layer] = _matmul(a, p[name], layer, F32, tiles.tm, tiles.tn, emit_wb=True)
        return out

    def residual_matmul(a, name, layer, x_in, nw_post, nw_next):
        if tiles.prompt:
            return _matmul_residual(a, wb[name, layer], layer, x_in, nw_post, nw_next, adt,
                                    tiles.tm_res, tiles.tk_res)
        xo, h_next, wb[name, layer] = _matmul_residual(a, p[name], layer, x_in, nw_post, nw_next, adt,
                                                       tiles.tm_res, tiles.tk_res, emit_wb=True)
        return xo, h_next

    def ffn_up(a, layer):
        if tiles.prompt:
            return _swiglu_up(a, wb['ffn_gate', layer], wb['ffn_up', layer], layer, adt, tiles.tm_ff, tiles.tn_ff)
        act_, wb['ffn_gate', layer], wb['ffn_up', layer] = _swiglu_up(
            a, p['ffn_w_up'], p['ffn_w_up'], layer, adt, tiles.tm, tiles.tn_ff, emit_wb=True)
        return act_

    gdn_s, gdn_c, gla_s, att_k, att_v = [], [], [], [], []
    h = _rmsnorm(x, nw(0, 0), adt, tiles.tm)
    for i in range(DEPTH):
        j = i // N_MIXERS
        kind = i % N_MIXERS
        if kind == 0:
            beta, g = _gdn_gates(h, p['gdn_w_ba'], j, p['gdn_a_log'][j].reshape(-1, 1),
                                 p['gdn_dt_bias'][j].reshape(-1, 1), tiles.tm)
            buf0 = gdn_c0[j]
            nbuf = GDN_CONV_W - 1
            buf8 = jnp.pad(buf0, ((0, 0), (SUBLANES - nbuf, 0), (0, 0)))
            if tiles.prompt:
                proj, tail = _gdn_in_proj(h, wb['gdn_w_in', j], buf8, p['gdn_conv_w'][j], t, tiles.tm, tiles.tn)
                qkv3 = proj.reshape(b, t, -1)
                proj3p = _matmul(h, wb['gdn_w_in', j], j, F32, tiles.tm, tiles.tn,
                                 col_start=GDN_CONV_DIM, n_out=GDN_V_DIM).reshape(b, t, -1)
                new_buf = tail[:, SUBLANES - nbuf:, :]
            else:
                proj3 = big_matmul(h, 'gdn_w_in', j).reshape(b, t, -1)
                new_buf = jnp.concatenate([buf0[:, t:], proj3[..., :GDN_CONV_DIM]], axis=1)
                proj3p = pad_time(proj3, SUBLANES)
                qkv3 = pad_time(_gdn_prep(proj3p, buf8, p['gdn_conv_w'][j], tiles.tt_prep, tiles.tc_prep), t_pad)
                proj3p = pad_time(proj3p, t_pad)

            def heads(a):
                a = a.reshape(GDN_QK_HEADS, 2, b, t).transpose(2, 0, 1, 3)
                return a if t == t_pad else jnp.pad(a, ((0, 0), (0, 0), (0, 0), (0, t_pad - t)))
            o3, s_new = _gdn_delta(qkv3, proj3p, heads(g), heads(beta), gdn_s0[j],
                                   p['gdn_norm_w'][j].reshape(1, -1), adt, tiles.cb, tiles.gdn_hpb, t_valid)
            mix_in, w_out = o3[:, :t].reshape(m, -1), 'gdn_w_out'
            gdn_s.append(s_new)
            gdn_c.append(new_buf)
        elif kind == 1:
            qkvg = big_matmul(h, 'gla_w_qkvg', j)
            low = _matmul(h, p['gla_w_gk1'], j, F32, tiles.tm, tiles.tn)
            gk = _matmul(low, p['gla_w_gk2'], j, F32, tiles.tm, tiles.tn)
            o3, s_new = _gla_chunks(pad_time(qkvg.reshape(b, t, -1), t_pad), pad_time(gk.reshape(b, t, -1), t_pad),
                                    p['gla_b_gk'][j].reshape(1, -1), gla_s0[j],
                                    p['gla_norm_w'][j].reshape(1, -1), adt, tiles.cb, t_valid)
            mix_in, w_out = o3[:, :t].reshape(m, -1), 'gla_w_out'
            gla_s.append(s_new)
        else:
            lam_init = 0.8 - 0.6 * math.exp(-0.3 * i)
            qkv = big_matmul(h, 'diff_w_qkv', j)
            nh, dh = DIFF_HEADS, DIFF_HEAD_DIM
            k_new = qkv[:, D_MODEL:2 * D_MODEL].reshape(b, t, 2 * nh, dh)
            v_new = qkv[:, 2 * D_MODEL:].reshape(b, t, nh, 2 * dh)
            subln = p['diff_subln'][j].reshape(1, -1)
            if tiles.prompt:
                o3 = _flash_diff(qkv.reshape(b, t, -1), p['rel_bias'], p['diff_lambda'][j],
                                 subln.reshape(-1, 1), lam_init, adt, tiles.tb_attn)
                mix_in = o3.reshape(m, -1)
            else:
                o = _decode_diff(qkv[:, :D_MODEL].reshape(b, 2 * nh, dh), k_new.reshape(b, 2 * nh, dh),
                                 v_new.reshape(b, nh, 2 * dh), cache_k, cache_v, j, page_table,
                                 p['rel_bias'], p['diff_lambda'][j], subln, lam_init, tiles.decode_pages)
                mix_in = o.reshape(m, -1)
            w_out = 'diff_w_out'
            att_k.append(k_new)
            att_v.append(v_new)
        x, h = residual_matmul(mix_in, w_out, j, x, nw(i, 1), nw(i, 2))
        act = ffn_up(h, i)
        nxt = nw(i + 1, 0) if i + 1 < DEPTH else None
        x, h = residual_matmul(act, 'ffn_w_down', i, x, nw(i, 3), nxt)
    return (x.reshape(b, t, d), jnp.stack(gdn_s), jnp.stack(gdn_c), jnp.stack(gla_s),
            jnp.stack(att_k), jnp.stack(att_v))


def kernel(x_prompt, x_sample, state_gdn, state_gdn_conv, state_gla, cache_k, cache_v, page_table,
           norm_w, ffn_w_up, ffn_w_down, rel_bias,
           gdn_w_in, gdn_w_ba, gdn_conv_w, gdn_a_log, gdn_dt_bias, gdn_norm_w, gdn_w_out,
           gla_w_qkvg, gla_w_gk1, gla_w_gk2, gla_b_gk, gla_norm_w, gla_w_out,
           diff_w_qkv, diff_lambda, diff_subln, diff_w_out):
    p = dict(norm_w=norm_w, ffn_w_up=ffn_w_up, ffn_w_down=ffn_w_down, rel_bias=rel_bias,
             gdn_w_in=gdn_w_in, gdn_w_ba=gdn_w_ba, gdn_conv_w=gdn_conv_w, gdn_a_log=gdn_a_log,
             gdn_dt_bias=gdn_dt_bias, gdn_norm_w=gdn_norm_w, gdn_w_out=gdn_w_out,
             gla_w_qkvg=gla_w_qkvg, gla_w_gk1=gla_w_gk1, gla_w_gk2=gla_w_gk2, gla_b_gk=gla_b_gk,
             gla_norm_w=gla_norm_w, gla_w_out=gla_w_out,
             diff_w_qkv=diff_w_qkv, diff_lambda=diff_lambda, diff_subln=diff_subln, diff_w_out=diff_w_out)
    bp = x_prompt.shape[0]
    n_gdn, n_gla = state_gdn.shape[0], state_gla.shape[0]
    zeros_gdn = jnp.zeros((n_gdn, bp) + state_gdn.shape[2:], F32)
    zeros_conv = jnp.zeros((n_gdn, bp) + state_gdn_conv.shape[2:], F32)
    zeros_gla = jnp.zeros((n_gla, bp) + state_gla.shape[2:], F32)
    wb = {}
    outs_s = _group_trunk(x_sample, _Tiles(False), state_gdn, state_gdn_conv, state_gla,
                          cache_k, cache_v, page_table, p, wb)
    outs_p = _group_trunk(x_prompt, _Tiles(True), zeros_gdn, zeros_conv, zeros_gla, None, None, None, p, wb)
    return (outs_p[0], outs_s[0]) + outs_p[1:] + outs_s[1:]
```

```python
import functools
import math

import numpy as np
import jax
import jax.numpy as jnp
from jax import lax
from jax.experimental import pallas as pl
from jax.experimental.pallas import tpu as pltpu

F32 = jnp.float32
BF16 = jnp.bfloat16

D_MODEL = 2048
DEPTH = 4
PAGE_SIZE = 128
N_MIXERS = 3
GDN_HEAD_DIM = 128
GDN_QK_HEADS = D_MODEL // 128
GDN_V_HEADS = 2 * GDN_QK_HEADS
GDN_QK_DIM = GDN_QK_HEADS * GDN_HEAD_DIM
GDN_V_DIM = GDN_V_HEADS * GDN_HEAD_DIM
GDN_CONV_DIM = 2 * GDN_QK_DIM + GDN_V_DIM
GDN_CONV_W = 4
GLA_HEADS = 4
GLA_DK = D_MODEL // 2 // GLA_HEADS
GLA_DV = D_MODEL // GLA_HEADS
GLA_GATE_NORMALIZER = 16.0
DIFF_HEAD_DIM = 128
DIFF_HEADS = D_MODEL // (2 * DIFF_HEAD_DIM)
N_BUCKETS = 32
MAX_DISTANCE = 128
D_FF = ((8 * D_MODEL + 3 * 256 - 1) // (3 * 256)) * 256
NORM_EPS = 1e-6

LANES = 128
SUBLANES = 8
CHUNK = 64
SUB = 16
GDN_PROJ_SUBBLOCKS = 8
VMEM_LIMIT_BYTES = 48 * 1024 * 1024
VMEM_LIMIT_BIG_BYTES = 56 * 1024 * 1024
NEG = -1e30
LOG2E = math.log2(math.e)


def _params(*sem, vmem=VMEM_LIMIT_BYTES):
    return pltpu.CompilerParams(dimension_semantics=sem, vmem_limit_bytes=vmem)


def _dot(a, b):
    return jnp.dot(a.astype(BF16), b.astype(BF16), preferred_element_type=F32)


def _dot_nt(a, b):
    return lax.dot_general(a.astype(BF16), b.astype(BF16), (((1,), (1,)), ((), ())),
                           preferred_element_type=F32)


def _dot_tn(a, b):
    return lax.dot_general(a.astype(BF16), b.astype(BF16), (((0,), (0,)), ((), ())),
                           preferred_element_type=F32)


def _split3(x):
    hi = x.astype(BF16)
    r = x - hi.astype(F32)
    mid = r.astype(BF16)
    lo = (r - mid.astype(F32)).astype(BF16)
    return hi, mid, lo


def _sigmoid(x):
    return 0.5 * jnp.tanh(0.5 * x) + 0.5


def _silu(x):
    h = 0.5 * x
    return h + h * jnp.tanh(h)


def _softplus(x):
    return jnp.maximum(x, 0.0) + jnp.log(1.0 + jnp.exp(-jnp.abs(x)))


def _rms(x, w):
    return x * lax.rsqrt(jnp.mean(x * x, axis=-1, keepdims=True) + NORM_EPS) * w


def _chunk_tril(r):
    i = lax.broadcasted_iota(jnp.int32, (r, r), 0)
    j = lax.broadcasted_iota(jnp.int32, (r, r), 1)
    return jnp.where((j <= i) & ((i // CHUNK) == (j // CHUNK)), 1.0, 0.0).astype(BF16)


def _rmsnorm_kernel(x_ref, w_ref, o_ref):
    o_ref[...] = _rms(x_ref[...], w_ref[...]).astype(o_ref.dtype)


def _rmsnorm(x, w, out_dtype, tm):
    m, d = x.shape
    tm = min(tm, m)
    return pl.pallas_call(
        _rmsnorm_kernel, grid=(m // tm,),
        in_specs=[pl.BlockSpec((tm, d), lambda i: (i, 0)), pl.BlockSpec((1, d), lambda i: (0, 0))],
        out_specs=pl.BlockSpec((tm, d), lambda i: (i, 0)),
        out_shape=jax.ShapeDtypeStruct((m, d), out_dtype),
        compiler_params=_params("parallel"), name="rmsnorm")(x, w)


def _weight_spec(w, layer, block, index):
    if w.ndim == 2:
        return pl.BlockSpec(block, index)
    return pl.BlockSpec((None,) + block, lambda *g: (layer,) + index(*g))


def _mm_kernel(a_ref, w_ref, o_ref, *wb_ref):
    w = w_ref[...].astype(BF16)
    if wb_ref:
        wb_ref[0][...] = w
    o_ref[...] = _dot(a_ref[...], w).astype(o_ref.dtype)


def _matmul(a, w, layer, out_dtype, tm, tn, emit_wb=False, col_start=0, n_out=None):
    m, k = a.shape
    n = w.shape[-1] if n_out is None else n_out
    tm, tn = min(tm, m), min(tn, n)
    assert m % tm == 0 and n % tn == 0 and col_start % tn == 0 and (not emit_wb or (m == tm and n_out is None))
    c0 = col_start // tn
    out_shape = [jax.ShapeDtypeStruct((m, n), out_dtype)]
    out_specs = [pl.BlockSpec((tm, tn), lambda i, j: (i, j))]
    if emit_wb:
        out_shape.append(jax.ShapeDtypeStruct((k, n), BF16))
        out_specs.append(pl.BlockSpec((k, tn), lambda i, j: (0, j)))
    res = pl.pallas_call(
        _mm_kernel, grid=(m // tm, n // tn),
        in_specs=[pl.BlockSpec((tm, k), lambda i, j: (i, 0)),
                  _weight_spec(w, layer, (k, tn), lambda i, j: (0, j + c0))],
        out_specs=out_specs, out_shape=out_shape,
        compiler_params=_params("parallel", "parallel"), name="matmul")(a, w)
    return (res[0], res[1]) if emit_wb else res[0]


def _swiglu_kernel(a_ref, wg_ref, wu_ref, o_ref, *wb_refs):
    a = a_ref[...].astype(BF16)
    wg = wg_ref[...].astype(BF16)
    wu = wu_ref[...].astype(BF16)
    if wb_refs:
        wb_refs[0][...] = wg
        wb_refs[1][...] = wu
    g = _dot(a, wg)
    u = _dot(a, wu)
    o_ref[...] = (_silu(g) * u).astype(o_ref.dtype)


def _swiglu_up(a, w_gate, w_up, layer, out_dtype, tm, tn, emit_wb=False):
    m, k = a.shape
    stacked = w_up.ndim == 3
    f = w_up.shape[-1] // 2 if stacked else w_up.shape[-1]
    tm, tn = min(tm, m), min(tn, f)
    assert m % tm == 0 and f % tn == 0 and (not emit_wb or m == tm)
    nf = f // tn
    up_off = nf if stacked else 0
    out_shape = [jax.ShapeDtypeStruct((m, f), out_dtype)]
    out_specs = [pl.BlockSpec((tm, tn), lambda i, j: (i, j))]
    if emit_wb:
        out_shape += [jax.ShapeDtypeStruct((k, f), BF16)] * 2
        out_specs += [pl.BlockSpec((k, tn), lambda i, j: (0, j))] * 2
    res = pl.pallas_call(
        _swiglu_kernel, grid=(m // tm, nf),
        in_specs=[pl.BlockSpec((tm, k), lambda i, j: (i, 0)),
                  _weight_spec(w_gate, layer, (k, tn), lambda i, j: (0, j)),
                  _weight_spec(w_up, layer, (k, tn), lambda i, j: (0, j + up_off))],
        out_specs=out_specs, out_shape=out_shape,
        compiler_params=_params("parallel", "parallel"), name="swiglu_up")(a, w_gate, w_up)
    return tuple(res) if emit_wb else res[0]


def _residual_epilogue(mix, x_ref, nw1_ref, nw2_ref, xo_ref, h_ref):
    xn = x_ref[...] + _rms(mix, nw1_ref[...])
    xo_ref[...] = xn
    if h_ref:
        h_ref[0][...] = _rms(xn, nw2_ref[...]).astype(h_ref[0].dtype)


def _mm_res_kernel(a_ref, w_ref, x_ref, nw1_ref, nw2_ref, xo_ref, *rest, nk, emit_h, emit_wb):
    h_ref = rest[:1] if emit_h else ()
    kk = pl.program_id(1)
    w = w_ref[...].astype(BF16)
    if emit_wb:
        rest[-1][...] = w
    part = _dot(a_ref[...], w)

    @pl.when(kk == 0)
    def _():
        xo_ref[...] = part

    @pl.when(kk > 0)
    def _():
        xo_ref[...] += part

    @pl.when(kk == nk - 1)
    def _():
        _residual_epilogue(xo_ref[...], x_ref, nw1_ref, nw2_ref, xo_ref, h_ref)


def _mm_res_resident_kernel(a_ref, w_ref, x_ref, nw1_ref, nw2_ref, xo_ref, *h_ref):
    _residual_epilogue(_dot(a_ref[...], w_ref[...]), x_ref, nw1_ref, nw2_ref, xo_ref, h_ref)


def _matmul_residual(a, w, layer, x, nw_post, nw_next, h_dtype, tm, tk, emit_wb=False):
    m, k = a.shape
    n = w.shape[-1]
    tm = min(tm, m)
    assert m % tm == 0 and (not emit_wb or m == tm)
    emit_h = nw_next is not None
    if not emit_h:
        nw_next = nw_post
    resident = w.ndim == 2
    if resident:
        grid = (m // tm,)
        row = lambda i: (i, 0)
        fixed = lambda i: (0, 0)
        a_spec = pl.BlockSpec((tm, k), row)
        w_spec = pl.BlockSpec((k, n), fixed, pipeline_mode=pl.Buffered(1))
        x_spec = pl.BlockSpec((tm, n), row)
        body = _mm_res_resident_kernel
        sem = ("parallel",)
    else:
        tk = min(tk, k)
        assert k % tk == 0
        nk = k // tk
        grid = (m // tm, nk)
        row = lambda i, kk: (i, 0)
        fixed = lambda i, kk: (0, 0)
        a_spec = pl.BlockSpec((tm, tk), lambda i, kk: (i, kk))
        w_spec = pl.BlockSpec((None, tk, n), lambda i, kk: (layer, kk, 0))
        x_spec = pl.BlockSpec((tm, n), row, pipeline_mode=pl.Buffered(1))
        body = functools.partial(_mm_res_kernel, nk=nk, emit_h=emit_h, emit_wb=emit_wb)
        sem = ("parallel", "arbitrary")
    out_shape = [jax.ShapeDtypeStruct((m, n), F32)]
    out_specs = [pl.BlockSpec((tm, n), row)]
    if emit_h:
        out_shape.append(jax.ShapeDtypeStruct((m, n), h_dtype))
        out_specs.append(pl.BlockSpec((tm, n), row))
    if emit_wb:
        out_shape.append(jax.ShapeDtypeStruct((k, n), BF16))
        out_specs.append(pl.BlockSpec((tk, n), lambda i, kk: (kk, 0)))
    res = pl.pallas_call(
        body, grid=grid,
        in_specs=[a_spec, w_spec, x_spec, pl.BlockSpec((1, n), fixed), pl.BlockSpec((1, n), fixed)],
        out_specs=out_specs, out_shape=out_shape,
        compiler_params=_params(*sem, vmem=VMEM_LIMIT_BIG_BYTES), name="matmul_residual")(
            a, w, x, nw_post, nw_next)
    xo = res[0]
    h = res[1] if emit_h else None
    return (xo, h, res[-1]) if emit_wb else (xo, h)


def _gdn_gates_kernel(a_ref, w_ref, alog_ref, dtb_ref, beta_ref, g_ref):
    ba_t = lax.dot_general(w_ref[...].astype(BF16), a_ref[...].astype(BF16), (((0,), (1,)), ((), ())),
                           preferred_element_type=F32)
    nh = beta_ref.shape[0]
    beta_ref[...] = _sigmoid(ba_t[:nh])
    g_ref[...] = -jnp.exp(alog_ref[...]) * _softplus(ba_t[nh:] + dtb_ref[...])


def _gdn_gates(a, w_ba, layer, a_log_col, dt_bias_col, tm):
    m, k = a.shape
    nh = GDN_V_HEADS
    tm = min(tm, m)
    return pl.pallas_call(
        _gdn_gates_kernel, grid=(m // tm,),
        in_specs=[pl.BlockSpec((tm, k), lambda i: (i, 0)),
                  pl.BlockSpec((None, k, 2 * nh), lambda i: (layer, 0, 0)),
                  pl.BlockSpec((nh, 1), lambda i: (0, 0)),
                  pl.BlockSpec((nh, 1), lambda i: (0, 0))],
        out_specs=[pl.BlockSpec((nh, tm), lambda i: (0, i))] * 2,
        out_shape=[jax.ShapeDtypeStruct((nh, m), F32)] * 2,
        compiler_params=_params("parallel"), name="gdn_gates")(a, w_ba, a_log_col, dt_bias_col)


def _conv_silu_norm(x, halo, ext_scr, cw_ref, j, n_qk_tiles):
    tt, tc = x.shape
    ext_scr[:SUBLANES, :] = halo
    ext_scr[SUBLANES:, :] = x
    y = x * cw_ref[GDN_CONV_W - 1:GDN_CONV_W, :]
    for s in range(1, GDN_CONV_W):
        y = y + ext_scr[SUBLANES - s:SUBLANES - s + tt, :] * cw_ref[GDN_CONV_W - 1 - s:GDN_CONV_W - s, :]
    y = _silu(y)
    qscale = jnp.where(j < n_qk_tiles, GDN_HEAD_DIM ** -0.5, 1.0)
    pieces = []
    for hh in range(tc // GDN_HEAD_DIM):
        yh = y[:, hh * GDN_HEAD_DIM:(hh + 1) * GDN_HEAD_DIM]
        r = lax.rsqrt(jnp.sum(yh * yh, axis=-1, keepdims=True) + NORM_EPS) * qscale
        pieces.append(yh * jnp.where(j < 2 * n_qk_tiles, r, 1.0))
    return jnp.concatenate(pieces, axis=1)


def _gdn_prep_kernel(x_ref, prev_ref, buf_ref, cw_ref, o_ref, ext_scr, *, n_qk_tiles):
    i = pl.program_id(1)
    j = pl.program_id(2)
    halo = jnp.where(i == 0, buf_ref[...], prev_ref[...])
    o_ref[...] = _conv_silu_norm(x_ref[...], halo, ext_scr, cw_ref, j, n_qk_tiles)


def _gdn_in_proj_kernel(a_ref, w_ref, buf_ref, cw_ref, o_ref, tail_ref, ext_scr, halo_scr, *,
                        tiles_per_seq, n_qk_tiles):
    i = pl.program_id(0)
    j = pl.program_id(1)

    @pl.when((i == 0) & (j == 0))
    def _():
        halo_scr[...] = jnp.zeros_like(halo_scr)

    halo = jnp.where(i % tiles_per_seq == 0, buf_ref[...], halo_scr[j])
    w = w_ref[...]
    n_sub, rows = ext_scr.shape[0], ext_scr.shape[1] - SUBLANES
    for s in range(n_sub):
        sl = slice(s * rows, (s + 1) * rows)
        x = _dot(a_ref[sl, :], w)
        o_ref[sl, :] = _conv_silu_norm(x, halo, ext_scr.at[s], cw_ref, j, n_qk_tiles)
        halo = x[rows - SUBLANES:, :]
    halo_scr[j] = halo
    tail_ref[...] = halo


def _gdn_in_proj(a, w, buf8, conv_w, seq_len, tm, tn):
    m, k = a.shape
    n = GDN_CONV_DIM
    tm = min(tm, seq_len)
    assert w.ndim == 2 and seq_len % tm == 0 and GDN_QK_DIM % tn == 0
    assert tm % (GDN_PROJ_SUBBLOCKS * SUBLANES) == 0
    tps = seq_len // tm
    proj, tails = pl.pallas_call(
        functools.partial(_gdn_in_proj_kernel, tiles_per_seq=tps, n_qk_tiles=GDN_QK_DIM // tn),
        grid=(m // tm, n // tn),
        in_specs=[pl.BlockSpec((tm, k), lambda i, j: (i, 0)),
                  pl.BlockSpec((k, tn), lambda i, j: (0, j)),
                  pl.BlockSpec((None, SUBLANES, tn), lambda i, j: (i // tps, 0, j)),
                  pl.BlockSpec((GDN_CONV_W, tn), lambda i, j: (0, j))],
        out_specs=[pl.BlockSpec((tm, tn), lambda i, j: (i, j)),
                   pl.BlockSpec((None, SUBLANES, tn), lambda i, j: (i, 0, j))],
        out_shape=[jax.ShapeDtypeStruct((m, n), F32), jax.ShapeDtypeStruct((m // tm, SUBLANES, n), F32)],
        scratch_shapes=[pltpu.VMEM((GDN_PROJ_SUBBLOCKS, SUBLANES + tm // GDN_PROJ_SUBBLOCKS, tn), F32),
                        pltpu.VMEM((n // tn, SUBLANES, tn), F32)],
        compiler_params=_params("arbitrary", "arbitrary", vmem=VMEM_LIMIT_BIG_BYTES), name="gdn_in_proj")(
            a, w, buf8, conv_w)
    return proj, tails[tps - 1::tps]


def _gdn_prep(proj3, buf8, conv_w, tt, tc):
    b, t, _ = proj3.shape
    tt = min(tt, t)
    assert t % tt == 0 and tt % SUBLANES == 0 and GDN_QK_DIM % tc == 0
    hb = tt // SUBLANES
    return pl.pallas_call(
        functools.partial(_gdn_prep_kernel, n_qk_tiles=GDN_QK_DIM // tc),
        grid=(b, t // tt, GDN_CONV_DIM // tc),
        in_specs=[pl.BlockSpec((None, tt, tc), lambda bb, i, j: (bb, i, j)),
                  pl.BlockSpec((None, SUBLANES, tc), lambda bb, i, j: (bb, jnp.maximum(i * hb - 1, 0), j)),
                  pl.BlockSpec((None, SUBLANES, tc), lambda bb, i, j: (bb, 0, j)),
                  pl.BlockSpec((GDN_CONV_W, tc), lambda bb, i, j: (0, j))],
        out_specs=pl.BlockSpec((None, tt, tc), lambda bb, i, j: (bb, i, j)),
        out_shape=jax.ShapeDtypeStruct((b, t, GDN_CONV_DIM), F32),
        scratch_shapes=[pltpu.VMEM((SUBLANES + tt, tc), F32)],
        compiler_params=_params("parallel", "parallel", "parallel"), name="gdn_prep")(
            proj3, proj3, buf8, conv_w)


def _gdn_delta_kernel(q_ref, k_ref, v_ref, z_ref, gr_ref, br_ref, s0_ref, nw_ref,
                      o_ref, so_ref, s_scr, *, cb, hpb, t_valid, n_blocks):
    n = pl.program_id(2)
    r = cb * CHUNK
    dh = GDN_HEAD_DIM

    @pl.when(n == 0)
    def _():
        s_scr[...] = s0_ref[...]

    q = q_ref[...]
    k = k_ref[...]
    v = v_ref[...]
    grow = gr_ref[...]
    brow = br_ref[...]
    if t_valid is not None:
        okc = (n * r + lax.broadcasted_iota(jnp.int32, (r, 1), 0)) < t_valid
        okr = (n * r + lax.broadcasted_iota(jnp.int32, (1, r), 1)) < t_valid
        k = jnp.where(okc, k, 0.0)
        v = jnp.where(okc, v, 0.0)
        grow = jnp.where(okr, grow, 0.0)
        brow = jnp.where(okr, brow, 0.0)

    ri = lax.broadcasted_iota(jnp.int32, (r, r), 0)
    ci = lax.broadcasted_iota(jnp.int32, (r, r), 1)
    eye = jnp.where(ri == ci, 1.0, 0.0)
    lm = _chunk_tril(r)
    eyeb = eye.astype(BF16)
    g_all = _split3(jnp.concatenate([grow[qh] for qh in range(hpb)], axis=0))
    b_all = _split3(jnp.concatenate([brow[qh] for qh in range(hpb)], axis=0))
    gcum_col = sum(_dot_nt(lm, p) for p in g_all)
    gcum_row = sum(_dot_nt(p, lm) for p in g_all)
    bcol = sum(_dot_nt(eyeb, p) for p in b_all)
    tril = ((ri // CHUNK) == (ci // CHUNK)) & (ri >= ci)
    lvl_masks = []
    size = 1
    while size < CHUNK:
        lvl_masks.append(((ri // (2 * size)) == (ci // (2 * size)))
                         & ((ri // size) % 2 == 1) & ((ci // size) % 2 == 0))
        size *= 2
    if t_valid is not None and t_valid <= 1:
        lvl_masks = []

    nw = nw_ref[...]
    heads = [(qh, hh) for qh in range(hpb) for hh in range(2)]
    kq = [k[:, qh * dh:(qh + 1) * dh] for qh in range(hpb)]
    qq = [q[:, qh * dh:(qh + 1) * dh] for qh in range(hpb)]
    gram = [_dot_nt(kq[qh], kq[qh]) for qh in range(hpb)]
    qk = [_dot_nt(qq[qh], kq[qh]) for qh in range(hpb)]
    gc = [gcum_col[:, i:i + 1] for i in range(len(heads))]
    bc = [bcol[:, i:i + 1] for i in range(len(heads))]
    decay = [jnp.exp(jnp.where(tril, gc[i] - gcum_row[i:i + 1, :], NEG)) for i in range(len(heads))]
    mm_ = [(gram[qh] * bc[i] * decay[i]).astype(BF16) for i, (qh, hh) in enumerate(heads)]
    tinv = [eye.astype(BF16) for _ in heads]
    for lvl, msk in enumerate(lvl_masks):
        mskb = jnp.where(msk, 1.0, 0.0).astype(BF16)
        ml = [m * mskb for m in mm_]
        if lvl == 0:
            tinv = [t - m for t, m in zip(tinv, ml)]
        else:
            y = [_dot(m, t) for t, m in zip(tinv, ml)]
            tinv = [t - _dot(t, yy).astype(BF16) for t, yy in zip(tinv, y)]
    eg = [jnp.exp(g_) for g_ in gc]
    sol = [_dot(tinv[i], jnp.concatenate([v[:, (2 * qh + hh) * dh:(2 * qh + hh + 1) * dh] * bc[i],
                                          kq[qh] * (bc[i] * eg[i])], axis=1))
           for i, (qh, hh) in enumerate(heads)]
    asol = [_dot(qk[qh] * decay[i], sol[i]) for i, (qh, hh) in enumerate(heads)]
    o2 = [a[:, :dh] for a in asol]
    o1 = [qq[qh] * eg[i] - asol[i][:, dh:] for i, (qh, hh) in enumerate(heads)]
    glast = [[g_[(c + 1) * CHUNK - 1:(c + 1) * CHUNK, :] for c in range(cb)] for g_ in gc]
    kd = [kq[qh] * jnp.exp(jnp.concatenate([jnp.broadcast_to(gl, (CHUNK, 1)) for gl in glast[i]], axis=0) - gc[i])
          for i, (qh, hh) in enumerate(heads)]
    qw = [[_dot_tn(kd[i][c * CHUNK:(c + 1) * CHUNK], sol[i][c * CHUNK:(c + 1) * CHUNK]) for c in range(cb)]
          for i in range(len(heads))]
    s = [s_scr[2 * qh + hh] for qh, hh in heads]
    for c in range(cb):
        sl = slice(c * CHUNK, (c + 1) * CHUNK)
        for i, (qh, hh) in enumerate(heads):
            x = _dot(jnp.concatenate([qw[i][c][:, dh:], o1[i][sl]], axis=0), s[i])
            o = x[dh:] + o2[i][sl]
            s[i] = s[i] * jnp.exp(glast[i][c]) - x[:dh] + qw[i][c][:, :dh]
            col = slice((2 * qh + hh) * dh, (2 * qh + hh + 1) * dh)
            zc = z_ref[sl, col]
            o_ref[sl, col] = (_rms(o, nw) * _silu(zc)).astype(o_ref.dtype)
    for i, (qh, hh) in enumerate(heads):
        s_scr[2 * qh + hh] = s[i]

    @pl.when(n == n_blocks - 1)
    def _():
        so_ref[...] = s_scr[...]


def _gdn_delta(qkv3, proj3, grow, brow, s0, norm_w, out_dtype, cb, hpb, t_valid):
    b, t, _ = qkv3.shape
    r = cb * CHUNK
    assert t % r == 0
    nb = t // r
    dh = GDN_HEAD_DIM
    qw_, vw_ = hpb * dh, 2 * hpb * dh
    assert GDN_QK_HEADS % hpb == 0
    koff = GDN_QK_DIM // qw_
    voff = 2 * GDN_QK_DIM // vw_
    zoff = (proj3.shape[-1] - GDN_V_DIM) // vw_
    return pl.pallas_call(
        functools.partial(_gdn_delta_kernel, cb=cb, hpb=hpb, t_valid=t_valid, n_blocks=nb),
        grid=(b, GDN_QK_HEADS // hpb, nb),
        in_specs=[pl.BlockSpec((None, r, qw_), lambda bb, h, n: (bb, n, h)),
                  pl.BlockSpec((None, r, qw_), lambda bb, h, n: (bb, n, koff + h)),
                  pl.BlockSpec((None, r, vw_), lambda bb, h, n: (bb, n, voff + h)),
                  pl.BlockSpec((None, r, vw_), lambda bb, h, n: (bb, n, zoff + h)),
                  pl.BlockSpec((None, hpb, 2, r), lambda bb, h, n: (bb, h, 0, n)),
                  pl.BlockSpec((None, hpb, 2, r), lambda bb, h, n: (bb, h, 0, n)),
                  pl.BlockSpec((None, 2 * hpb, dh, dh), lambda bb, h, n: (bb, h, 0, 0)),
                  pl.BlockSpec((1, dh), lambda bb, h, n: (0, 0))],
        out_specs=[pl.BlockSpec((None, r, vw_), lambda bb, h, n: (bb, n, h)),
                   pl.BlockSpec((None, 2 * hpb, dh, dh), lambda bb, h, n: (bb, h, 0, 0))],
        out_shape=[jax.ShapeDtypeStruct((b, t, GDN_V_DIM), out_dtype),
                   jax.ShapeDtypeStruct(s0.shape, F32)],
        scratch_shapes=[pltpu.VMEM((2 * hpb, dh, dh), F32)],
        compiler_params=_params("parallel", "parallel", "arbitrary"), name="gdn_delta")(
            qkv3, qkv3, qkv3, proj3, grow, brow, s0, norm_w)


def _gla_kernel(q_ref, k_ref, v_ref, gate_ref, gk_ref, bgk_ref, s0_ref, nw_ref,
                o_ref, so_ref, s_scr, *, cb, t_valid, n_blocks):
    n = pl.program_id(2)
    r = cb * CHUNK
    dk, dv = GLA_DK, GLA_DV

    @pl.when(n == 0)
    def _():
        s_scr[...] = s0_ref[...]

    xg = gk_ref[...] + bgk_ref[...]
    g = (jnp.minimum(xg, 0.0) - jnp.log(1.0 + jnp.exp(-jnp.abs(xg)))) * (1.0 / GLA_GATE_NORMALIZER)
    q = q_ref[...] * GLA_DK ** -0.5
    k = k_ref[...]
    v = v_ref[...]
    if t_valid is not None:
        okc = (n * r + lax.broadcasted_iota(jnp.int32, (r, 1), 0)) < t_valid
        g = jnp.where(okc, g, 0.0)
        k = jnp.where(okc, k, 0.0)
        v = jnp.where(okc, v, 0.0)

    gparts = _split3(g)
    bcum = sum(_dot(_chunk_tril(r), p) for p in gparts)
    ones = jnp.ones((CHUNK, LANES), BF16)
    row = lax.broadcasted_iota(jnp.int32, (CHUNK, 1), 0)
    jrow = lax.broadcasted_iota(jnp.int32, (SUB, 1), 0)
    lane = lax.broadcasted_iota(jnp.int32, (SUB, CHUNK), 1)
    nw = nw_ref[...]
    s = s_scr[...]
    for c in range(cb):
        sl = slice(c * CHUNK, (c + 1) * CHUNK)
        qc, kc, vc, bc = q[sl], k[sl], v[sl], bcum[sl]
        blast = bc[CHUNK - 1:CHUNK, :]
        attn_t = jnp.zeros((CHUNK, CHUNK), F32)
        for bi in range(1, CHUNK // SUB):
            bref = bc[bi * SUB:bi * SUB + 1, :]
            k_i = jnp.where(row < bi * SUB, kc * jnp.exp(jnp.minimum(bref - bc, 0.0)), 0.0)
            q_i = jnp.where((row >= bi * SUB) & (row < (bi + 1) * SUB),
                            qc * jnp.exp(jnp.minimum(bc - bref, 0.0)), 0.0)
            attn_t = attn_t + _dot_nt(k_i, q_i)
        diag = []
        for bi in range(CHUNK // SUB):
            sb = slice(bi * SUB, (bi + 1) * SUB)
            qb, kb, bb = qc[sb], kc[sb], bc[sb]
            d = jnp.zeros((SUB, CHUNK), F32)
            for il in range(SUB):
                e = jnp.exp(jnp.minimum(bb[il:il + 1, :] - bb, 0.0))
                col = jnp.sum(kb * e * qb[il:il + 1, :], axis=-1, keepdims=True)
                col = jnp.where(jrow <= il, col, 0.0)
                d = jnp.where(lane == bi * SUB + il, col, d)
            diag.append(d)
        attn_t = attn_t + jnp.concatenate(diag, axis=0)
        o = _dot(qc * jnp.exp(bc), s) + _dot_tn(attn_t, vc)
        bl_col = sum(_dot_tn(p[sl], ones) for p in gparts)
        decay_col = jnp.concatenate([jnp.exp(bl_col)] * (dv // LANES), axis=1)
        s = s * decay_col + _dot_tn(kc * jnp.exp(blast - bc), vc)
        gt = gate_ref[sl, :]
        o_ref[sl, :] = (_rms(o, nw) * _silu(gt)).astype(o_ref.dtype)
    s_scr[...] = s

    @pl.when(n == n_blocks - 1)
    def _():
        so_ref[...] = s_scr[...]


def _gla_chunks(qkvg3, gk3, b_gk, s0, norm_w, out_dtype, cb, t_valid):
    b, t, _ = qkvg3.shape
    r = cb * CHUNK
    assert t % r == 0
    nb = t // r
    dk, dv = GLA_DK, GLA_DV
    koff = GLA_HEADS
    voff = 2 * GLA_HEADS * dk // dv
    goff = voff + GLA_HEADS
    return pl.pallas_call(
        functools.partial(_gla_kernel, cb=cb, t_valid=t_valid, n_blocks=nb),
        grid=(b, GLA_HEADS, nb),
        in_specs=[pl.BlockSpec((None, r, dk), lambda bb, h, n: (bb, n, h)),
                  pl.BlockSpec((None, r, dk), lambda bb, h, n: (bb, n, koff + h)),
                  pl.BlockSpec((None, r, dv), lambda bb, h, n: (bb, n, voff + h)),
                  pl.BlockSpec((None, r, dv), lambda bb, h, n: (bb, n, goff + h)),
                  pl.BlockSpec((None, r, dk), lambda bb, h, n: (bb, n, h)),
                  pl.BlockSpec((1, dk), lambda bb, h, n: (0, h)),
                  pl.BlockSpec((None, None, dk, dv), lambda bb, h, n: (bb, h, 0, 0)),
                  pl.BlockSpec((1, dv), lambda bb, h, n: (0, 0))],
        out_specs=[pl.BlockSpec((None, r, dv), lambda bb, h, n: (bb, n, h)),
                   pl.BlockSpec((None, None, dk, dv), lambda bb, h, n: (bb, h, 0, 0))],
        out_shape=[jax.ShapeDtypeStruct((b, t, GLA_HEADS * dv), out_dtype),
                   jax.ShapeDtypeStruct(s0.shape, F32)],
        scratch_shapes=[pltpu.VMEM((dk, dv), F32)],
        compiler_params=_params("parallel", "parallel", "arbitrary"), name="gla_chunks")(
            qkvg3, qkvg3, qkvg3, qkvg3, gk3, b_gk, s0, norm_w)


def _t5_bucket(n):
    n = np.asarray(n)
    max_exact = N_BUCKETS // 2
    nf = np.maximum(n, max_exact).astype(np.float32)
    large = max_exact + (np.log(nf / max_exact) / math.log(MAX_DISTANCE / max_exact)
                         * (N_BUCKETS - max_exact)).astype(np.int32)
    return np.where(n < max_exact, n, np.minimum(large, N_BUCKETS - 1)).astype(np.int32)


def _lambda(lam_ref, lam_init):
    lf = lam_ref[...]
    s1 = jnp.sum(lf[0:1] * lf[1:2], axis=-1, keepdims=True)
    s2 = jnp.sum(lf[2:3] * lf[3:4], axis=-1, keepdims=True)
    return jnp.exp(s1) - jnp.exp(s2) + lam_init


_FAR_FAR, _FAR_SUB, _SUB_DIAG, _DIAG = range(4)


def _flash_kernel(qi_ref, ki_ref, kind_ref, q_ref, k_ref, v_ref, bd_ref, bs_ref, rb_ref, lam_ref, sub_ref, o_ref,
                  q_scr, m_scr, l_scr, acc_scr, bias_scr, *, lam_init):
    h = pl.program_id(0)
    step = pl.program_id(2)
    ki = ki_ref[step]
    kind = kind_ref[step]
    tb = q_ref.shape[0]
    dh = DIFF_HEAD_DIM

    @pl.when((pl.program_id(1) == 0) & (step == 0))
    def _():
        ri = lax.broadcasted_iota(jnp.int32, (tb, tb), 0)
        ci = lax.broadcasted_iota(jnp.int32, (tb, tb), 1)
        for mm in range(2):
            bd = jnp.zeros((tb, tb), F32)
            bs = jnp.zeros((tb, tb), F32)
            for bk in range(N_BUCKETS):
                val = rb_ref[bk, 2 * h + mm] * LOG2E
                bd = jnp.where(bd_ref[...] == bk, val, bd)
                bs = jnp.where(bs_ref[...] == bk, val, bs)
            bias_scr[mm, 0] = jnp.where(ri > ci, NEG, bd)
            bias_scr[mm, 1] = bs

    @pl.when(ki == 0)
    def _():
        q_scr[...] = (q_ref[...] * (dh ** -0.5 * LOG2E)).astype(q_scr.dtype)
        m_scr[...] = jnp.full_like(m_scr, NEG)
        l_scr[...] = jnp.zeros_like(l_scr)
        acc_scr[...] = jnp.zeros_like(acc_scr)

    def update(mm, s, shift, vt):
        m_prev = m_scr[mm]
        if shift.ndim == 0:
            m_new = jnp.maximum(m_prev, jnp.max(s, axis=0, keepdims=True) + shift)
            p = jnp.exp2(s - (m_new - shift))
        else:
            s = s + shift
            m_new = jnp.maximum(m_prev, jnp.max(s, axis=0, keepdims=True))
            p = jnp.exp2(s - m_new)
        alpha = jnp.exp2(m_prev - m_new)
        l_scr[mm] = alpha * l_scr[mm] + jnp.sum(p, axis=0, keepdims=True)
        acc_scr[mm] = alpha * acc_scr[mm] + _dot(vt, p)
        m_scr[mm] = m_new

    far = lambda mm: rb_ref[N_BUCKETS - 1, 2 * h + mm] * LOG2E
    sub = lambda mm: bias_scr[mm, 1]
    diag = lambda mm: bias_scr[mm, 0]

    def both_maps(shifts):
        rows = [pl.ds(pl.multiple_of((ki + n) * tb, tb), tb) for n in range(len(shifts))]
        vt = jnp.concatenate([v_ref[r, :].T.astype(BF16) for r in rows], axis=1)
        s = [[_dot_nt(k_ref[r, mm * dh:(mm + 1) * dh], q_scr[:, mm * dh:(mm + 1) * dh]) for r in rows]
             for mm in range(2)]
        for mm in range(2):
            sh = [f(mm) for f in shifts]
            if all(x.ndim == 0 for x in sh):
                update(mm, jnp.concatenate(s[mm], axis=0), sh[0], vt)
            else:
                update(mm, jnp.concatenate([a + x for a, x in zip(s[mm], sh)], axis=0), jnp.float32(0.0), vt)

    @pl.when(kind == _FAR_FAR)
    def _():
        both_maps([far, far])

    @pl.when(kind == _FAR_SUB)
    def _():
        both_maps([far, sub])

    @pl.when(kind == _SUB_DIAG)
    def _():
        both_maps([sub, diag])

    @pl.when(kind == _DIAG)
    def _():
        both_maps([diag])

    @pl.when(kind >= _SUB_DIAG)
    def _():
        lam = _lambda(lam_ref, lam_init)
        out = acc_scr[0] / l_scr[0] - lam * (acc_scr[1] / l_scr[1])
        out = out * lax.rsqrt(jnp.mean(out * out, axis=0, keepdims=True) + NORM_EPS)
        o_ref[...] = (out * (sub_ref[...] * (1.0 - lam_init))).T.astype(o_ref.dtype)


def _flash_diff(qkv3, rel_bias, lambdas, subln_col, lam_init, out_dtype, tb):
    b, t, _ = qkv3.shape
    tb = min(tb, t)
    assert t % tb == 0 and tb >= MAX_DISTANCE
    nq = t // tb
    dd = 2 * DIFF_HEAD_DIM
    j = np.arange(tb)[:, None]
    i = np.arange(tb)[None, :]
    bkt_diag = jnp.asarray(_t5_bucket(np.maximum(i - j, 0)))
    bkt_sub = jnp.asarray(_t5_bucket(tb + i - j))
    koff = D_MODEL // dd
    kinds = {"FF": _FAR_FAR, "FS": _FAR_SUB, "SD": _SUB_DIAG, "D": _DIAG}
    pairs = []
    for qi in range(nq):
        tags = "F" * (qi - 1) + ("S" if qi >= 1 else "") + "D"
        for ki in range(0, qi + 1, 2):
            pairs.append((qi, ki, kinds[tags[ki:ki + 2]]))
    qi_tbl = jnp.asarray([pq for pq, _, _ in pairs], jnp.int32)
    ki_tbl = jnp.asarray([pk for _, pk, _ in pairs], jnp.int32)
    kind_tbl = jnp.asarray([kd for _, _, kd in pairs], jnp.int32)
    grid_spec = pltpu.PrefetchScalarGridSpec(
        num_scalar_prefetch=3, grid=(DIFF_HEADS, b, len(pairs)),
        in_specs=[pl.BlockSpec((None, tb, dd), lambda h, bb, s, qt, kt, kd: (bb, qt[s], h)),
                  pl.BlockSpec((None, t, dd), lambda h, bb, s, qt, kt, kd: (bb, 0, koff + h)),
                  pl.BlockSpec((None, t, dd), lambda h, bb, s, qt, kt, kd: (bb, 0, 2 * koff + h)),
                  pl.BlockSpec((tb, tb), lambda h, bb, s, qt, kt, kd: (0, 0)),
                  pl.BlockSpec((tb, tb), lambda h, bb, s, qt, kt, kd: (0, 0)),
                  pl.BlockSpec(memory_space=pltpu.SMEM),
                  pl.BlockSpec((4, DIFF_HEAD_DIM), lambda h, bb, s, qt, kt, kd: (0, 0)),
                  pl.BlockSpec((dd, 1), lambda h, bb, s, qt, kt, kd: (0, 0))],
        out_specs=pl.BlockSpec((None, tb, dd), lambda h, bb, s, qt, kt, kd: (bb, qt[s], h)),
        scratch_shapes=[pltpu.VMEM((tb, dd), BF16),
                        pltpu.VMEM((2, 1, tb), F32), pltpu.VMEM((2, 1, tb), F32),
                        pltpu.VMEM((2, dd, tb), F32), pltpu.VMEM((2, 2, tb, tb), F32)])
    return pl.pallas_call(
        functools.partial(_flash_kernel, lam_init=lam_init), grid_spec=grid_spec,
        out_shape=jax.ShapeDtypeStruct((b, t, D_MODEL), out_dtype),
        compiler_params=_params("parallel", "arbitrary", "arbitrary"), name="flash_diff")(
            qi_tbl, ki_tbl, kind_tbl, qkv3, qkv3, qkv3, bkt_diag, bkt_sub, rel_bias, lambdas, subln_col)


def _decode_kernel(pt_ref, q_ref, *refs, pp, n_steps, lam_init):
    kp_refs, vp_refs = refs[:pp], refs[pp:2 * pp]
    kn_ref, vn_ref, bkt_ref, tbl_ref, lam_ref, sub_ref, o_ref, m_scr, l_scr, acc_scr, bias_scr, p_scr, a_scr = refs[2 * pp:]
    step = pl.program_id(1)
    nh, dh = DIFF_HEADS, DIFF_HEAD_DIM
    ones = jnp.ones((dh, LANES), BF16)
    qs = q_ref[...] * (dh ** -0.5 * LOG2E)

    @pl.when(step == 0)
    def _():
        m_scr[...] = jnp.full_like(m_scr, NEG)
        l_scr[...] = jnp.zeros_like(l_scr)
        acc_scr[...] = jnp.zeros_like(acc_scr)
        bkt = bkt_ref[...]
        bias = jnp.zeros(bkt.shape, F32)
        for bk in range(N_BUCKETS):
            bias = jnp.where(bkt == bk, (tbl_ref[bk] * LOG2E)[None], bias)
        bias_scr[...] = bias

    def lane_sum(x):
        return _dot(x, ones)

    def accumulate(pr, shift_row, values):
        m_prev = m_scr[...]
        if pr.ndim == 3:
            m_new = jnp.maximum(m_prev, jnp.max(pr, axis=0) + shift_row)
            p = jnp.exp2(pr - (m_new - shift_row)[None])
            psum = jnp.sum(p, axis=0)
        else:
            m_new = jnp.maximum(m_prev, pr + shift_row)
            p = jnp.exp2(pr - (m_new - shift_row))
            psum = p
        alpha = jnp.exp2(m_prev - m_new)
        l_scr[...] = alpha * l_scr[...] + psum
        m_scr[...] = m_new
        a_scr[...] = alpha
        if pr.ndim == 3:
            p_scr[...] = p
        else:
            p_scr[0] = p
        for par in range(2):
            ae = a_scr[pl.ds(par, nh, stride=2), :]
            if pr.ndim == 3:
                pe = p_scr[:, pl.ds(par, nh, stride=2), :]
                pv = jnp.sum(jnp.concatenate([pe, pe], axis=-1) * values(), axis=0)
            else:
                pe = p_scr[0, pl.ds(par, nh, stride=2), :]
                pv = jnp.concatenate([pe, pe], axis=-1) * values()
            acc_scr[par] = jnp.concatenate([ae, ae], axis=-1) * acc_scr[par] + pv

    def page_logits(j):
        prod = kp_refs[j][...] * qs[None]
        return lane_sum(prod.reshape(PAGE_SIZE * 2 * nh, dh)).reshape(PAGE_SIZE, 2 * nh, LANES)

    far = tbl_ref[N_BUCKETS - 1] * LOG2E
    zero = jnp.zeros_like(far)
    logits = [page_logits(j) for j in range(pp)]
    shifts = [far] * (pp - 1) + [zero]
    logits[pp - 1] = logits[pp - 1] + jnp.where(step == n_steps - 1, bias_scr[...], far[None])
    m_prev = m_scr[...]
    m_new = m_prev
    for pr, sh in zip(logits, shifts):
        m_new = jnp.maximum(m_new, jnp.max(pr, axis=0) + sh)
    alpha = jnp.exp2(m_prev - m_new)
    a_scr[...] = alpha
    psum = jnp.zeros_like(m_new)
    pvs = [jnp.zeros(acc_scr.shape[1:], F32) for _ in range(2)]
    for j, (pr, sh) in enumerate(zip(logits, shifts)):
        p = jnp.exp2(pr - (m_new - sh)[None])
        psum = psum + jnp.sum(p, axis=0)
        p_scr[...] = p
        for par in range(2):
            pe = p_scr[:, pl.ds(par, nh, stride=2), :]
            pvs[par] = pvs[par] + jnp.sum(jnp.concatenate([pe, pe], axis=-1) * vp_refs[j][...], axis=0)
    l_scr[...] = alpha * l_scr[...] + psum
    m_scr[...] = m_new
    for par in range(2):
        ae = a_scr[pl.ds(par, nh, stride=2), :]
        acc_scr[par] = jnp.concatenate([ae, ae], axis=-1) * acc_scr[par] + pvs[par]

    @pl.when(step == n_steps - 1)
    def _():
        accumulate(lane_sum(kn_ref[...] * qs) + tbl_ref[0] * LOG2E, zero, lambda: vn_ref[...])
        l_scr_v = l_scr[...]
        a_scr[...] = l_scr_v
        outs = []
        for par in range(2):
            le = a_scr[pl.ds(par, nh, stride=2), :]
            outs.append(acc_scr[par] / jnp.concatenate([le, le], axis=-1))
        out = outs[0] - _lambda(lam_ref, lam_init) * outs[1]
        o_ref[...] = _rms(out, sub_ref[...]) * (1.0 - lam_init)


def _decode_diff(q, k_new, v_new, cache_k, cache_v, layer, page_table, rel_bias, lambdas, subln, lam_init, pp):
    b, n_pages = page_table.shape
    nh, dh = DIFF_HEADS, DIFF_HEAD_DIM
    pp = min(pp, n_pages)
    assert n_pages % pp == 0 and PAGE_SIZE >= MAX_DISTANCE and dh == LANES
    n_steps = n_pages // pp
    bkt_last = jnp.asarray(np.broadcast_to(
        _t5_bucket(PAGE_SIZE - np.arange(PAGE_SIZE)).reshape(PAGE_SIZE, 1, 1), (PAGE_SIZE, 2 * nh, LANES)))
    tbl = jnp.broadcast_to(rel_bias[:, :, None], (N_BUCKETS, 2 * nh, LANES))
    page_spec = lambda j, width, lanes: pl.BlockSpec(
        (None, None, PAGE_SIZE, width, lanes), lambda bb, s, pt: (layer, pt[bb, s * pp + j], 0, 0, 0))
    full = lambda shape: pl.BlockSpec(shape, lambda bb, s, pt: (0,) * len(shape))
    per_seq = lambda shape: pl.BlockSpec((None,) + shape, lambda bb, s, pt: (bb,) + (0,) * len(shape))
    grid_spec = pltpu.PrefetchScalarGridSpec(
        num_scalar_prefetch=1, grid=(b, n_steps),
        in_specs=([per_seq((2 * nh, dh))]
                  + [page_spec(j, 2 * nh, dh) for j in range(pp)]
                  + [page_spec(j, nh, 2 * dh) for j in range(pp)]
                  + [per_seq((2 * nh, dh)), per_seq((nh, 2 * dh)),
                     full((PAGE_SIZE, 2 * nh, LANES)), full((N_BUCKETS, 2 * nh, LANES)),
                     full((4, dh)), full((1, 2 * dh))]),
        out_specs=per_seq((nh, 2 * dh)),
        scratch_shapes=[pltpu.VMEM((2 * nh, LANES), F32), pltpu.VMEM((2 * nh, LANES), F32),
                        pltpu.VMEM((2, nh, 2 * dh), F32), pltpu.VMEM((PAGE_SIZE, 2 * nh, LANES), F32),
                        pltpu.VMEM((PAGE_SIZE, 2 * nh, LANES), F32), pltpu.VMEM((2 * nh, LANES), F32)])
    return pl.pallas_call(
        functools.partial(_decode_kernel, pp=pp, n_steps=n_steps, lam_init=lam_init),
        grid_spec=grid_spec,
        out_shape=jax.ShapeDtypeStruct((b, nh, 2 * dh), F32),
        compiler_params=_params("parallel", "arbitrary"), name="decode_diff")(
            page_table, q, *([cache_k] * pp), *([cache_v] * pp), k_new, v_new, bkt_last, tbl, lambdas, subln)


class _Tiles:
    def __init__(self, prompt):
        self.prompt = prompt
        self.act_dtype = BF16 if prompt else F32
        self.tm = 1024 if prompt else SUBLANES
        self.tn = 1024
        self.tn_ff = 512
        self.tm_ff = 1024
        self.tm_res = 256 if prompt else SUBLANES
        self.tk_res = 512
        self.tt_prep = 256 if prompt else SUBLANES
        self.tc_prep = 1024
        self.cb = 4 if prompt else 1
        self.gdn_hpb = 8 if prompt else 4
        self.tb_attn = 512
        self.decode_pages = 8


def _group_trunk(x3, tiles, gdn_s0, gdn_c0, gla_s0, cache_k, cache_v, page_table, p, wb):
    b, t, d = x3.shape
    m = b * t
    x = x3.reshape(m, d)
    if not tiles.prompt:
        assert t == 1 and m % SUBLANES == 0
    t_pad = t if tiles.prompt else CHUNK
    t_valid = None if tiles.prompt else t
    nw = lambda i, jn: p['norm_w'][i, jn].reshape(1, d)
    adt = tiles.act_dtype

    def pad_time(a3, to):
        return a3 if a3.shape[1] == to else jnp.pad(a3, ((0, 0), (0, to - a3.shape[1]), (0, 0)))

    def big_matmul(a, name, layer):
        if tiles.prompt:
            return _matmul(a, wb[name, layer], layer, F32, tiles.tm, tiles.tn)
        out, wb[name, layer] = _matmul(a, p[name], layer, F32, tiles.tm, tiles.tn, emit_wb=True)
        return out

    def residual_matmul(a, name, layer, x_in, nw_post, nw_next):
        if tiles.prompt:
            return _matmul_residual(a, wb[name, layer], layer, x_in, nw_post, nw_next, adt,
                                    tiles.tm_res, tiles.tk_res)
        xo, h_next, wb[name, layer] = _matmul_residual(a, p[name], layer, x_in, nw_post, nw_next, adt,
                                                       tiles.tm_res, tiles.tk_res, emit_wb=True)
        return xo, h_next

    def ffn_up(a, layer):
        if tiles.prompt:
            return _swiglu_up(a, wb['ffn_gate', layer], wb['ffn_up', layer], layer, adt, tiles.tm_ff, tiles.tn_ff)
        act_, wb['ffn_gate', layer], wb['ffn_up', layer] = _swiglu_up(
            a, p['ffn_w_up'], p['ffn_w_up'], layer, adt, tiles.tm, tiles.tn_ff, emit_wb=True)
        return act_

    gdn_s, gdn_c, gla_s, att_k, att_v = [], [], [], [], []
    h = _rmsnorm(x, nw(0, 0), adt, tiles.tm)
    for i in range(DEPTH):
        j = i // N_MIXERS
        kind = i % N_MIXERS
        if kind == 0:
            beta, g = _gdn_gates(h, p['gdn_w_ba'], j, p['gdn_a_log'][j].reshape(-1, 1),
                                 p['gdn_dt_bias'][j].reshape(-1, 1), tiles.tm)
            buf0 = gdn_c0[j]
            nbuf = GDN_CONV_W - 1
            buf8 = jnp.pad(buf0, ((0, 0), (SUBLANES - nbuf, 0), (0, 0)))
            if tiles.prompt:
                proj, tail = _gdn_in_proj(h, wb['gdn_w_in', j], buf8, p['gdn_conv_w'][j], t, tiles.tm, tiles.tn)
                qkv3 = proj.reshape(b, t, -1)
                proj3p = _matmul(h, wb['gdn_w_in', j], j, F32, tiles.tm, tiles.tn,
                                 col_start=GDN_CONV_DIM, n_out=GDN_V_DIM).reshape(b, t, -1)
                new_buf = tail[:, SUBLANES - nbuf:, :]
            else:
                proj3 = big_matmul(h, 'gdn_w_in', j).reshape(b, t, -1)
                new_buf = jnp.concatenate([buf0[:, t:], proj3[..., :GDN_CONV_DIM]], axis=1)
                proj3p = pad_time(proj3, SUBLANES)
                qkv3 = pad_time(_gdn_prep(proj3p, buf8, p['gdn_conv_w'][j], tiles.tt_prep, tiles.tc_prep), t_pad)
                proj3p = pad_time(proj3p, t_pad)

            def heads(a):
                a = a.reshape(GDN_QK_HEADS, 2, b, t).transpose(2, 0, 1, 3)
                return a if t == t_pad else jnp.pad(a, ((0, 0), (0, 0), (0, 0), (0, t_pad - t)))
            o3, s_new = _gdn_delta(qkv3, proj3p, heads(g), heads(beta), gdn_s0[j],
                                   p['gdn_norm_w'][j].reshape(1, -1), adt, tiles.cb, tiles.gdn_hpb, t_valid)
            mix_in, w_out = o3[:, :t].reshape(m, -1), 'gdn_w_out'
            gdn_s.append(s_new)
            gdn_c.append(new_buf)
        elif kind == 1:
            qkvg = big_matmul(h, 'gla_w_qkvg', j)
            low = _matmul(h, p['gla_w_gk1'], j, F32, tiles.tm, tiles.tn)
            gk = _matmul(low, p['gla_w_gk2'], j, F32, tiles.tm, tiles.tn)
            o3, s_new = _gla_chunks(pad_time(qkvg.reshape(b, t, -1), t_pad), pad_time(gk.reshape(b, t, -1), t_pad),
                                    p['gla_b_gk'][j].reshape(1, -1), gla_s0[j],
                                    p['gla_norm_w'][j].reshape(1, -1), adt, tiles.cb, t_valid)
            mix_in, w_out = o3[:, :t].reshape(m, -1), 'gla_w_out'
            gla_s.append(s_new)
        else:
            lam_init = 0.8 - 0.6 * math.exp(-0.3 * i)
            qkv = big_matmul(h, 'diff_w_qkv', j)
            nh, dh = DIFF_HEADS, DIFF_HEAD_DIM
            k_new = qkv[:, D_MODEL:2 * D_MODEL].reshape(b, t, 2 * nh, dh)
            v_new = qkv[:, 2 * D_MODEL:].reshape(b, t, nh, 2 * dh)
            subln = p['diff_subln'][j].reshape(1, -1)
            if tiles.prompt:
                o3 = _flash_diff(qkv.reshape(b, t, -1), p['rel_bias'], p['diff_lambda'][j],
                                 subln.reshape(-1, 1), lam_init, adt, tiles.tb_attn)
                mix_in = o3.reshape(m, -1)
            else:
                o = _decode_diff(qkv[:, :D_MODEL].reshape(b, 2 * nh, dh), k_new.reshape(b, 2 * nh, dh),
                                 v_new.reshape(b, nh, 2 * dh), cache_k, cache_v, j, page_table,
                                 p['rel_bias'], p['diff_lambda'][j], subln, lam_init, tiles.decode_pages)
                mix_in = o.reshape(m, -1)
            w_out = 'diff_w_out'
            att_k.append(k_new)
            att_v.append(v_new)
        x, h = residual_matmul(mix_in, w_out, j, x, nw(i, 1), nw(i, 2))
        act = ffn_up(h, i)
        nxt = nw(i + 1, 0) if i + 1 < DEPTH else None
        x, h = residual_matmul(act, 'ffn_w_down', i, x, nw(i, 3), nxt)
    return (x.reshape(b, t, d), jnp.stack(gdn_s), jnp.stack(gdn_c), jnp.stack(gla_s),
            jnp.stack(att_k), jnp.stack(att_v))


def kernel(x_prompt, x_sample, state_gdn, state_gdn_conv, state_gla, cache_k, cache_v, page_table,
           norm_w, ffn_w_up, ffn_w_down, rel_bias,
           gdn_w_in, gdn_w_ba, gdn_conv_w, gdn_a_log, gdn_dt_bias, gdn_norm_w, gdn_w_out,
           gla_w_qkvg, gla_w_gk1, gla_w_gk2, gla_b_gk, gla_norm_w, gla_w_out,
           diff_w_qkv, diff_lambda, diff_subln, diff_w_out):
    p = dict(norm_w=norm_w, ffn_w_up=ffn_w_up, ffn_w_down=ffn_w_down, rel_bias=rel_bias,
             gdn_w_in=gdn_w_in, gdn_w_ba=gdn_w_ba, gdn_conv_w=gdn_conv_w, gdn_a_log=gdn_a_log,
             gdn_dt_bias=gdn_dt_bias, gdn_norm_w=gdn_norm_w, gdn_w_out=gdn_w_out,
             gla_w_qkvg=gla_w_qkvg, gla_w_gk1=gla_w_gk1, gla_w_gk2=gla_w_gk2, gla_b_gk=gla_b_gk,
             gla_norm_w=gla_norm_w, gla_w_out=gla_w_out,
             diff_w_qkv=diff_w_qkv, diff_lambda=diff_lambda, diff_subln=diff_subln, diff_w_out=diff_w_out)
    bp = x_prompt.shape[0]
    n_gdn, n_gla = state_gdn.shape[0], state_gla.shape[0]
    zeros_gdn = jnp.zeros((n_gdn, bp) + state_gdn.shape[2:], F32)
    zeros_conv = jnp.zeros((n_gdn, bp) + state_gdn_conv.shape[2:], F32)
    zeros_gla = jnp.zeros((n_gla, bp) + state_gla.shape[2:], F32)
    wb = {}
    outs_s = _group_trunk(x_sample, _Tiles(False), state_gdn, state_gdn_conv, state_gla,
                          cache_k, cache_v, page_table, p, wb)
    outs_p = _group_trunk(x_prompt, _Tiles(True), zeros_gdn, zeros_conv, zeros_gla, None, None, None, p, wb)
    return (outs_p[0], outs_s[0]) + outs_p[1:] + outs_s[1:]
```
